```python
import jax, jax.numpy as jnp
from jax import lax
import numpy as np

D_MODEL = 1024
BATCH = 16
SEQ = 256
DEPTH = 2
DEC_BATCH = 4
DEC_SEQ = 4096
PAST_LEN = 512

GRID_W = 64
N_EVEN = (DEPTH + 1) // 2
N_ODD = DEPTH // 2
D_FF = 4 * D_MODEL
EPS = 1e-6
SGU_CHUNK = 128
SGU_GROUPS = 4
SGU_WIDTH = D_MODEL // 2
SGU_GD = SGU_WIDTH // SGU_GROUPS
GLA_HEADS = 4
GLA_DV = (D_MODEL // 2) // GLA_HEADS
GLA_DK = GLA_DV // 2
GLA_RANK = 16
GLA_NORMALIZER = 16.0
GLA_CHUNK = 64
EV_SPLITS = (SGU_WIDTH,
             2 * SGU_WIDTH,
             2 * SGU_WIDTH + GLA_HEADS * GLA_DK,
             2 * SGU_WIDTH + 2 * GLA_HEADS * GLA_DK,
             2 * SGU_WIDTH + 2 * GLA_HEADS * GLA_DK + GLA_HEADS * GLA_DV,
             2 * SGU_WIDTH + 2 * GLA_HEADS * GLA_DK + 2 * GLA_HEADS * GLA_DV,
             2 * SGU_WIDTH + 2 * GLA_HEADS * GLA_DK + 2 * GLA_HEADS * GLA_DV + GLA_RANK)
EV_IN = 2 * SGU_WIDTH + 2 * GLA_HEADS * GLA_DK + 2 * GLA_HEADS * GLA_DV + 2 * GLA_RANK
EV_OUT = SGU_WIDTH + GLA_HEADS * GLA_DV
D_RNN = D_MODEL
RG_BLOCKS = 16
RG_BS = D_RNN // RG_BLOCKS
RG_CONV = 4
RG_PAD_L = 2
RG_PAD_R = 1
RG_C = 8.0

kernel_name = "hybrid_sgu_gla_rglru_diffusion_step"


def rmsnorm(x, g):
    xf = x.astype(jnp.float32)
    y = xf * lax.rsqrt(jnp.mean(xf * xf, axis=-1, keepdims=True) + EPS)
    return (y * g.astype(jnp.float32)).astype(x.dtype)


def layernorm(x, g, b):
    xf = x.astype(jnp.float32)
    mu = jnp.mean(xf, axis=-1, keepdims=True)
    xc = xf - mu
    y = xc * lax.rsqrt(jnp.mean(xc * xc, axis=-1, keepdims=True) + EPS)
    return (y * g.astype(jnp.float32) + b.astype(jnp.float32)).astype(x.dtype)


def grid_pos_embed(L, dtype):
    rows = L // GRID_W
    n = D_MODEL // 4
    omega = 1.0 / (10000.0 ** (jnp.arange(n, dtype=jnp.float32) / n))
    r = jnp.broadcast_to(jnp.arange(rows, dtype=jnp.float32)[:, None], (rows, GRID_W)).reshape(L)
    cc = jnp.broadcast_to(jnp.arange(GRID_W, dtype=jnp.float32)[None, :], (rows, GRID_W)).reshape(L)
    ar = r[:, None] * omega
    ac = cc[:, None] * omega
    return jnp.concatenate([jnp.sin(ar), jnp.cos(ar), jnp.sin(ac), jnp.cos(ac)], axis=-1).astype(dtype)


def gla_chunked(q, k, v, log_a, s0):
    B, H, L, DK = q.shape
    DV = v.shape[-1]
    n = L // GLA_CHUNK
    q, k, v, log_a = (t.reshape(B, H, n, GLA_CHUNK, t.shape[-1]) for t in (q, k, v, log_a))
    b = jnp.cumsum(log_a, axis=3)
    b_last = b[:, :, :, -1:, :]
    q_e = q * jnp.exp(b)
    k_e = k * jnp.exp(-b)
    k_d = k * jnp.exp(b_last - b)
    mask = jnp.tril(jnp.ones((GLA_CHUNK, GLA_CHUNK), dtype=bool))
    att = jnp.where(mask, jnp.einsum('bhnid,bhnjd->bhnij', q_e, k_e), 0.0)
    o_intra = jnp.einsum('bhnij,bhnjv->bhniv', att, v)
    ds = jnp.einsum('bhnjd,bhnjv->bhndv', k_d, v)
    decay = jnp.exp(b_last[:, :, :, 0, :])

    def step(s, inp):
        dec, d = inp
        return dec[..., None] * s + d, s

    s_final, s_in = lax.scan(step, s0, (jnp.moveaxis(decay, 2, 0), jnp.moveaxis(ds, 2, 0)))
    s_in = jnp.moveaxis(s_in, 0, 2)
    o_inter = jnp.einsum('bhnid,bhndv->bhniv', q_e, s_in)
    return (o_intra + o_inter).reshape(B, H, L, DV), s_final


def linear_scan(a, b, h0, reverse):
    idx = -1 if reverse else 0
    b = b.at[:, idx].add(a[:, idx] * h0)

    def combine(left, right):
        a_l, b_l = left
        a_r, b_r = right
        return a_l * a_r, a_r * b_l + b_r

    _, hs = lax.associative_scan(combine, (a, b), reverse=reverse, axis=1)
    final = hs[:, 0] if reverse else hs[:, -1]
    return hs, final


def even_mixer(hin, s0, w_in, w_out, ln_g, ln_b, ws, bs, gw2, gb, gnorm):
    B, L, _ = hin.shape
    u, v, q, k, vv, g, lr_f, lr_b = jnp.split(hin @ w_in, EV_SPLITS, axis=-1)
    u = jax.nn.gelu(u)
    v = layernorm(jax.nn.gelu(v), ln_g, ln_b)
    n = L // SGU_CHUNK
    v = v.reshape(B, n, SGU_CHUNK, SGU_GROUPS, SGU_GD)
    sv = jnp.einsum('gpq,bnqgd->bnpgd', ws, v) + bs.T[None, None, :, :, None]
    out_a = u * sv.reshape(B, L, SGU_WIDTH)
    def heads(t):
        return t.reshape(B, L, GLA_HEADS, -1).transpose(0, 2, 1, 3).astype(jnp.float32)
    qh = heads(q) * (GLA_DK ** -0.5)
    kh = heads(k)
    vh = heads(vv)
    la_f = jax.nn.log_sigmoid(heads(lr_f @ gw2[0] + gb[0])) / GLA_NORMALIZER
    la_b = jax.nn.log_sigmoid(heads(lr_b @ gw2[1] + gb[1])) / GLA_NORMALIZER
    s0 = s0.astype(jnp.float32)
    o_f, sf = gla_chunked(qh, kh, vh, la_f, s0[:, 0])
    rev = lambda t: jnp.flip(t, axis=2)
    o_b, sb = gla_chunked(rev(qh), rev(kh), rev(vh), rev(la_b), s0[:, 1])
    o = o_f + rev(o_b)
    o = o * lax.rsqrt(jnp.mean(o * o, axis=-1, keepdims=True) + EPS) \
        * gnorm.reshape(GLA_HEADS, 1, GLA_DV).astype(jnp.float32)
    o = o.transpose(0, 2, 1, 3).reshape(B, L, GLA_HEADS * GLA_DV).astype(hin.dtype) * jax.nn.silu(g)
    y = jnp.concatenate([out_a, o], axis=-1) @ w_out
    return y, jnp.stack([sf, sb], axis=1)


def odd_mixer(hin, s0, w_in, conv_w, conv_b, wa, ba, wx, bx, lam, w_out):
    B, L, _ = hin.shape
    xb, gbr = jnp.split(hin @ w_in, 2, axis=-1)
    xc = lax.conv_general_dilated(xb, conv_w[:, None, :].astype(xb.dtype), (1,), [(RG_PAD_L, RG_PAD_R)],
                                  dimension_numbers=('NWC', 'WIO', 'NWC'),
                                  feature_group_count=D_RNN) + conv_b
    xf = xc.astype(jnp.float32)
    xblk = xf.reshape(B, L, RG_BLOCKS, RG_BS)
    s0 = s0.astype(jnp.float32)

    def direction(d, reverse):
        r = jax.nn.sigmoid(jnp.einsum('blhi,hij->blhj', xblk, wa[d]).reshape(B, L, D_RNN) + ba[d])
        i = jax.nn.sigmoid(jnp.einsum('blhi,hij->blhj', xblk, wx[d]).reshape(B, L, D_RNN) + bx[d])
        log_a = -RG_C * r * jax.nn.softplus(-lam[d].astype(jnp.float32))
        a = jnp.exp(log_a)
        bt = jnp.sqrt(-jnp.expm1(2.0 * log_a)) * (i * xf)
        return linear_scan(a, bt, s0[:, d], reverse)

    h_f, sf = direction(0, False)
    h_b, sb = direction(1, True)
    y = ((h_f + h_b).astype(hin.dtype) * jax.nn.gelu(gbr)) @ w_out
    return y, jnp.stack([sf, sb], axis=1)


def setup_inputs(seed: int = 0) -> dict:
    key = jax.random.key(seed)
    ks = jax.random.split(key, 32)

    def nrm(k, shape, scale):
        return jax.random.normal(k, shape, jnp.float32) * scale

    D = D_MODEL
    rg_u = jax.random.uniform(ks[29], (N_ODD, 2, D_RNN), jnp.float32, 0.9, 0.999)
    rg_p = rg_u ** (1.0 / RG_C)
    return {
        'x_prompt': nrm(ks[0], (BATCH, SEQ, D), 1.0),
        'x_sample': nrm(ks[1], (DEC_BATCH, DEC_SEQ, D), 1.0),
        'c': nrm(ks[2], (DEC_BATCH, D), 1.0),
        'state_gla': nrm(ks[3], (DEC_BATCH, N_EVEN, 2, GLA_HEADS, GLA_DK, GLA_DV), 1.0),
        'state_rglru': nrm(ks[4], (DEC_BATCH, N_ODD, 2, D_RNN), 1.0),
        'c_ctx': nrm(ks[5], (D,), 1.0),
        'mod_w': nrm(ks[6], (DEPTH, D, 6 * D), D ** -0.5),
        'mod_b': nrm(ks[7], (DEPTH, 6 * D), 0.02),
        'norm_g': 1.0 + nrm(ks[8], (DEPTH, 4, D), 0.02),
        'mlp_w1': nrm(ks[9], (DEPTH, D, D_FF), D ** -0.5),
        'mlp_b1': nrm(ks[10], (DEPTH, D_FF), 0.02),
        'mlp_w2': nrm(ks[11], (DEPTH, D_FF, D), D_FF ** -0.5),
        'mlp_b2': nrm(ks[12], (DEPTH, D), 0.02),
        'ev_w_in': nrm(ks[13], (N_EVEN, D, EV_IN), D ** -0.5),
        'ev_w_out': nrm(ks[14], (N_EVEN, EV_OUT, D), EV_OUT ** -0.5),
        'sgu_ln_g': 1.0 + nrm(ks[15], (N_EVEN, SGU_WIDTH), 0.02),
        'sgu_ln_b': nrm(ks[16], (N_EVEN, SGU_WIDTH), 0.02),
        'sgu_ws': nrm(ks[17], (N_EVEN, SGU_GROUPS, SGU_CHUNK, SGU_CHUNK), SGU_CHUNK ** -0.5),
        'sgu_bs': nrm(ks[18], (N_EVEN, SGU_GROUPS, SGU_CHUNK), 0.02),
        'gla_gate_w2': nrm(ks[19], (N_EVEN, 2, GLA_RANK, GLA_HEADS * GLA_DK), GLA_RANK ** -0.5),
        'gla_gate_b': nrm(ks[20], (N_EVEN, 2, GLA_HEADS * GLA_DK), 0.02),
        'gla_norm_g': 1.0 + nrm(ks[21], (N_EVEN, GLA_HEADS * GLA_DV), 0.02),
        'rg_w_in': nrm(ks[22], (N_ODD, D, 2 * D_RNN), D ** -0.5),
        'rg_conv_w': nrm(ks[23], (N_ODD, RG_CONV, D_RNN), RG_CONV ** -0.5),
        'rg_conv_b': nrm(ks[24], (N_ODD, D_RNN), 0.02),
        'rg_wa': nrm(ks[25], (N_ODD, 2, RG_BLOCKS, RG_BS, RG_BS), RG_BS ** -0.5),
        'rg_ba': nrm(ks[26], (N_ODD, 2, D_RNN), 0.02),
        'rg_wx': nrm(ks[27], (N_ODD, 2, RG_BLOCKS, RG_BS, RG_BS), RG_BS ** -0.5),
        'rg_bx': nrm(ks[28], (N_ODD, 2, D_RNN), 0.02),
        'rg_L': jnp.log(rg_p) - jnp.log1p(-rg_p),
        'rg_w_out': nrm(ks[30], (N_ODD, D_RNN, D), D_RNN ** -0.5),
    }


def reference(x_prompt, x_sample, c, state_gla, state_rglru, c_ctx, mod_w, mod_b, norm_g,
              mlp_w1, mlp_b1, mlp_w2, mlp_b2, ev_w_in, ev_w_out, sgu_ln_g, sgu_ln_b, sgu_ws, sgu_bs,
              gla_gate_w2, gla_gate_b, gla_norm_g, rg_w_in, rg_conv_w, rg_conv_b, rg_wa, rg_ba,
              rg_wx, rg_bx, rg_L, rg_w_out):
    def trunk(x, cond, gla_init, rg_init):
        gla_fin = []
        rg_fin = []
        for i in range(DEPTH):
            mod = (jax.nn.silu(cond) @ mod_w[i] + mod_b[i])[:, None, :]
            sh1, sc1, g1, sh2, sc2, g2 = jnp.split(mod, 6, axis=-1)
            hmix = rmsnorm(x, norm_g[i, 0]) * (1.0 + sc1) + sh1
            if i % 2 == 0:
                e = i // 2
                y, s = even_mixer(hmix, gla_init[:, e], ev_w_in[e], ev_w_out[e], sgu_ln_g[e], sgu_ln_b[e],
                                  sgu_ws[e], sgu_bs[e], gla_gate_w2[e], gla_gate_b[e], gla_norm_g[e])
                gla_fin.append(s)
            else:
                o = i // 2
                y, s = odd_mixer(hmix, rg_init[:, o], rg_w_in[o], rg_conv_w[o], rg_conv_b[o], rg_wa[o],
                                 rg_ba[o], rg_wx[o], rg_bx[o], rg_L[o], rg_w_out[o])
                rg_fin.append(s)
            x = x + g1 * rmsnorm(y, norm_g[i, 1])
            hff = rmsnorm(x, norm_g[i, 2]) * (1.0 + sc2) + sh2
            f = jnp.square(jax.nn.relu(hff @ mlp_w1[i] + mlp_b1[i])) @ mlp_w2[i] + mlp_b2[i]
            x = x + g2 * rmsnorm(f, norm_g[i, 3])
        return x, jnp.stack(gla_fin, axis=1), jnp.stack(rg_fin, axis=1)

    nb = x_prompt.shape[0]
    gla_zero = jnp.zeros((nb, N_EVEN, 2, GLA_HEADS, GLA_DK, GLA_DV), jnp.float32)
    rg_zero = jnp.zeros((nb, N_ODD, 2, D_RNN), jnp.float32)
    y_prompt, new_state_gla, new_state_rglru = trunk(x_prompt, c_ctx[None, :], gla_zero, rg_zero)
    xs = x_sample + grid_pos_embed(x_sample.shape[1], x_sample.dtype)[None]
    y_sample, _, _ = trunk(xs, c, state_gla, state_rglru)
    return (y_prompt, y_sample, new_state_gla, new_state_rglru)
```

```python
import functools

import jax
import jax.numpy as jnp
from jax import lax
from jax.experimental import pallas as pl
from jax.experimental.pallas import tpu as pltpu

D = 1024
BATCH = 16
SEQ = 256
DEPTH = 2
DEC_BATCH = 4
DEC_SEQ = 4096
GRID_W = 64
D_FF = 4 * D
EPS = 1e-6
SGU_CHUNK = 128
SGU_GROUPS = 4
SGU_WIDTH = D // 2
GLA_HEADS = 4
GLA_DV = 128
GLA_DK = 64
GLA_RANK = 16
GLA_NORMALIZER = 16.0
GLA_CHUNK = 64
QK = GLA_HEADS * GLA_DK
VW = GLA_HEADS * GLA_DV
EV_MAIN = 2 * SGU_WIDTH + 2 * QK + 2 * VW
D_RNN = D
RG_BLOCKS = 16
RG_BS = D_RNN // RG_BLOCKS
RG_C = 8.0
LANES = 128
SUBLANES = 8
SLAB = 256

TM = 256
NTP = BATCH * SEQ // TM
TPS = DEC_SEQ // TM
NTS = DEC_BATCH * TPS
NT = NTP + NTS
NSEQ = BATCH + DEC_BATCH
NCOND = 8
HALO = 16
EV_MID = 2 * VW + VW + 2 * QK + VW + QK
OD_MID = 3 * D_RNN
VMEM_LIMIT = 56 * 1024 * 1024

F32 = jnp.float32
BF16 = jnp.bfloat16


def _tile_info(t):
    is_p = t < NTP
    ts = jnp.maximum(t - NTP, 0)
    sq = ts // TPS
    within = ts % TPS
    cidx = jnp.where(is_p, 0, 1 + sq)
    first = jnp.logical_or(is_p, within == 0)
    last = jnp.logical_or(is_p, within == TPS - 1)
    seq = jnp.where(is_p, t, BATCH + sq)
    return is_p, within, cidx, first, last, seq


def _seq_of(t):
    return jnp.where(t < NTP, t, BATCH + jnp.maximum(t - NTP, 0) // TPS)


def _rms(x, g):
    return x * lax.rsqrt(jnp.mean(x * x, axis=-1, keepdims=True) + EPS) * g


def _dot(a, b):
    return jnp.dot(a, b, preferred_element_type=F32)


def _dot_nt(a, b):
    return lax.dot_general(a, b, (((1,), (1,)), ((), ())), preferred_element_type=F32)


def _dot_tn(a, b):
    return lax.dot_general(a, b, (((0,), (0,)), ((), ())), preferred_element_type=F32)


def _split3(x):
    hi = x.astype(BF16)
    r1 = x - hi.astype(F32)
    mid = r1.astype(BF16)
    lo = (r1 - mid.astype(F32)).astype(BF16)
    return hi, mid, lo


def _dot_exact_lhs(m, parts):
    return _dot(m, parts[0]) + _dot(m, parts[1]) + _dot(m, parts[2])


def _log_sigmoid(x):
    return jnp.minimum(x, 0.0) - jnp.log1p(jnp.exp(-jnp.abs(x)))


def _softplus(x):
    return jnp.maximum(x, 0.0) + jnp.log1p(jnp.exp(-jnp.abs(x)))


def _mod_rows(mod_ref, cidx):
    m = mod_ref[pl.ds(cidx, 1), :]
    return [m[:, j * D:(j + 1) * D] for j in range(6)]


def _load_x0(is_p, within, xp_ref, xs_ref, rtab_ref, ctab_ref):
    rows_per_tile = TM // GRID_W
    r0 = within * rows_per_tile
    posr = jnp.concatenate(
        [jnp.broadcast_to(rtab_ref[pl.ds(r0 + j, 1), :], (GRID_W, D // 2)) for j in range(rows_per_tile)],
        axis=0)
    posc = jnp.concatenate([ctab_ref[...]] * rows_per_tile, axis=0)
    pos = jnp.concatenate([posr, posc], axis=1)
    return jnp.where(is_p, xp_ref[...], xs_ref[...] + pos)


def _mod_kernel(cond_ref, w_ref, b_ref, o_ref):
    c = cond_ref[...]
    sc = (c * jax.nn.sigmoid(c)).astype(BF16)
    o_ref[0] = _dot(sc, w_ref[0].astype(BF16)) + b_ref[0]


def _modulation(cond8, mod_w, mod_b):
    nb = 6 * D // D
    return pl.pallas_call(
        _mod_kernel,
        out_shape=jax.ShapeDtypeStruct((DEPTH, NCOND, 6 * D), F32),
        grid=(DEPTH, nb),
        in_specs=[
            pl.BlockSpec((NCOND, D), lambda l, j: (0, 0)),
            pl.BlockSpec((1, D, D), lambda l, j: (l, 0, j)),
            pl.BlockSpec((1, 1, D), lambda l, j: (l, 0, j)),
        ],
        out_specs=pl.BlockSpec((1, NCOND, D), lambda l, j: (l, 0, j)),
        compiler_params=pltpu.CompilerParams(
            dimension_semantics=("arbitrary", "arbitrary"), vmem_limit_bytes=VMEM_LIMIT),
        name="modulation",
    )(cond8, mod_w, mod_b.reshape(DEPTH, 1, 6 * D))


def _gla_tile(qs, k, v_bf, la, p_ref, reverse):
    n = TM // GLA_CHUNK
    ri = lax.broadcasted_iota(jnp.int32, (TM, TM), 0)
    ci = lax.broadcasted_iota(jnp.int32, (TM, TM), 1)
    same = (ri // GLA_CHUNK) == (ci // GLA_CHUNK)
    order = (ci >= ri) if reverse else (ci <= ri)
    cum_m = jnp.where(jnp.logical_and(same, order), 1.0, 0.0).astype(BF16)
    tot_m = jnp.where(same, 1.0, 0.0).astype(BF16)
    parts = _split3(la)
    b = _dot_exact_lhs(cum_m, parts)
    btot = _dot_exact_lhs(tot_m, parts)
    qe = qs * jnp.exp(b)
    ke = (k * jnp.exp(-b)).astype(BF16)
    kd = (k * jnp.exp(btot - b)).astype(BF16)
    dec = jnp.exp(btot)
    lane = lax.broadcasted_iota(jnp.int32, (TM, QK), 1) % LANES
    qm = (jnp.where(lane < GLA_DK, qe, 0.0).astype(BF16),
          jnp.where(lane >= GLA_DK, qe, 0.0).astype(BF16))
    cr = lax.broadcasted_iota(jnp.int32, (GLA_CHUNK, GLA_CHUNK), 0)
    cc = lax.broadcasted_iota(jnp.int32, (GLA_CHUNK, GLA_CHUNK), 1)
    cmask = (cc >= cr) if reverse else (cc <= cr)
    rows = [None] * n
    for c in (reversed(range(n)) if reverse else range(n)):
        rs = slice(c * GLA_CHUNK, (c + 1) * GLA_CHUNK)
        outs = []
        for h in range(GLA_HEADS):
            ls = slice((h // 2) * LANES, (h // 2 + 1) * LANES)
            qmh = qm[h % 2][rs, ls]
            att = jnp.where(cmask, _dot_nt(qmh, ke[rs, ls]), 0.0).astype(BF16)
            vh = v_bf[rs, h * GLA_DV:(h + 1) * GLA_DV]
            p = p_ref[h]
            outs.append(_dot(att, vh) + _dot_nt(qmh, p.astype(BF16)))
            p_ref[h] = p * dec[c * GLA_CHUNK:c * GLA_CHUNK + 1, ls] + _dot_tn(vh, kd[rs, ls])
        rows[c] = jnp.concatenate(outs, axis=1)
    return jnp.concatenate(rows, axis=0)


def _gla_state_load(s_ref, p_scr):
    z = jnp.zeros((GLA_DK, GLA_DV), F32)
    for h in range(GLA_HEADS):
        s = s_ref[0, 0, h]
        p_scr[h] = (jnp.concatenate([s, z], axis=0) if h % 2 == 0 else jnp.concatenate([z, s], axis=0)).T


def _gla_state_store(p_scr, s_ref):
    for h in range(GLA_HEADS):
        s_ref[0, 0, h] = p_scr[h].T[(h % 2) * GLA_DK:(h % 2 + 1) * GLA_DK, :]


def _ev1_kernel(xp_ref, xs_ref, rtab_ref, ctab_ref, mod_ref, ng_ref, win_ref, wlr_ref, gmat_ref, gb_ref,
                lng_ref, lnb_ref, ws_ref, bst_ref, p0_ref, mid_ref, pfin_ref, p_scr):
    t = pl.program_id(0)
    is_p, within, cidx, first, last, _ = _tile_info(t)
    x = _load_x0(is_p, within, xp_ref, xs_ref, rtab_ref, ctab_ref)
    sh1, sc1, _, _, _, _ = _mod_rows(mod_ref, cidx)
    hm = (_rms(x, ng_ref[0:1, :]) * (1.0 + sc1) + sh1).astype(BF16)
    proj = _dot(hm, win_ref[...])
    lr = _dot(hm, wlr_ref[...]).astype(BF16)
    la_all = _log_sigmoid(_dot(lr, gmat_ref[...]) + gb_ref[...]) * (1.0 / GLA_NORMALIZER)

    u = jax.nn.gelu(proj[:, 0:SGU_WIDTH])
    vg = jax.nn.gelu(proj[:, SGU_WIDTH:2 * SGU_WIDTH])
    mu = jnp.mean(vg, axis=-1, keepdims=True)
    vc = vg - mu
    vn = (vc * lax.rsqrt(jnp.mean(vc * vc, axis=-1, keepdims=True) + EPS) * lng_ref[...] + lnb_ref[...]).astype(BF16)
    gd = SGU_WIDTH // SGU_GROUPS
    nch = TM // SGU_CHUNK
    sv_cols = []
    for g in range(SGU_GROUPS):
        vcat = jnp.concatenate(
            [vn[c * SGU_CHUNK:(c + 1) * SGU_CHUNK, g * gd:(g + 1) * gd] for c in range(nch)], axis=1)
        sg = _dot(ws_ref[g], vcat) + bst_ref[:, g:g + 1]
        sv_cols.append(jnp.concatenate([sg[:, c * gd:(c + 1) * gd] for c in range(nch)], axis=0))
    out_a = u * jnp.concatenate(sv_cols, axis=1)

    o0 = 2 * SGU_WIDTH
    qs = proj[:, o0:o0 + QK] * (GLA_DK ** -0.5)
    k = proj[:, o0 + QK:o0 + 2 * QK]
    v = proj[:, o0 + 2 * QK:o0 + 2 * QK + VW]
    g = proj[:, o0 + 2 * QK + VW:o0 + 2 * QK + 2 * VW]

    @pl.when(first)
    def _():
        _gla_state_load(p0_ref, p_scr)

    o_f = _gla_tile(qs, k, v.astype(BF16), la_all[:, 0:QK], p_scr, reverse=False)

    @pl.when(last)
    def _():
        _gla_state_store(p_scr, pfin_ref)

    mid_ref[:, 0:VW] = out_a
    mid_ref[:, VW:2 * VW] = o_f
    mid_ref[:, 2 * VW:3 * VW] = g * jax.nn.sigmoid(g)
    c0 = 3 * VW
    mid_ref[:, c0:c0 + QK] = qs
    mid_ref[:, c0 + QK:c0 + 2 * QK] = k
    mid_ref[:, c0 + 2 * QK:c0 + 2 * QK + VW] = v
    mid_ref[:, c0 + 2 * QK + VW:c0 + 3 * QK + VW] = la_all[:, QK:2 * QK]


def _const_spec(shape):
    nd = len(shape)
    return pl.BlockSpec(shape, lambda i, _nd=nd: (0,) * _nd, pipeline_mode=pl.Buffered(1))


def _xp_spec(rev):
    if rev:
        return pl.BlockSpec((TM, D), lambda i: (jnp.minimum(NT - 1 - i, NTP - 1), 0))
    return pl.BlockSpec((TM, D), lambda i: (jnp.minimum(i, NTP - 1), 0))


def _xs_spec(rev):
    if rev:
        return pl.BlockSpec((TM, D), lambda i: (jnp.maximum(NT - 1 - i - NTP, 0), 0))
    return pl.BlockSpec((TM, D), lambda i: (jnp.maximum(i - NTP, 0), 0))


def _tile_spec(width, rev=False):
    if rev:
        return pl.BlockSpec((TM, width), lambda i: (NT - 1 - i, 0))
    return pl.BlockSpec((TM, width), lambda i: (i, 0))


def _params():
    return pltpu.CompilerParams(dimension_semantics=("arbitrary",), vmem_limit_bytes=VMEM_LIMIT)


def _even_forward(xp, xs, rtab, ctab, mod, ng, w):
    state_blk = (1, 1, GLA_HEADS, GLA_DK, GLA_DV)
    return pl.pallas_call(
        _ev1_kernel,
        out_shape=(jax.ShapeDtypeStruct((NT * TM, EV_MID), F32),
                   jax.ShapeDtypeStruct((NSEQ, 1) + state_blk[2:], F32)),
        grid=(NT,),
        in_specs=[
            _xp_spec(False), _xs_spec(False),
            _const_spec((GRID_W, D // 2)), _const_spec((GRID_W, D // 2)),
            _const_spec((NCOND, 6 * D)), _const_spec((4, D)),
            _const_spec((D, EV_MAIN)), _const_spec((D, LANES)), _const_spec((LANES, 2 * QK)),
            _const_spec((1, 2 * QK)),
            _const_spec((1, SGU_WIDTH)), _const_spec((1, SGU_WIDTH)),
            _const_spec((SGU_GROUPS, SGU_CHUNK, SGU_CHUNK)), _const_spec((SGU_CHUNK, SGU_GROUPS)),
            pl.BlockSpec(state_blk, lambda i: (_seq_of(i), 0, 0, 0, 0)),
        ],
        out_specs=(_tile_spec(EV_MID),
                   pl.BlockSpec(state_blk, lambda i: (_seq_of(i), 0, 0, 0, 0))),
        scratch_shapes=[pltpu.VMEM((GLA_HEADS, GLA_DV, LANES), F32)],
        compiler_params=_params(),
        name="even_forward",
    )(xp, xs, rtab, ctab, mod, ng, w["win"], w["wlr"], w["gmat"], w["gb"], w["lng"], w["lnb"],
      w["ws"], w["bst"], w["p0"])


def _ev2_kernel(xp_ref, xs_ref, rtab_ref, ctab_ref, mod_ref, ng_ref, mid_ref, gn_ref, wout_ref, p0_ref,
                x1_ref, pfin_ref, p_scr):
    t = NT - 1 - pl.program_id(0)
    is_p, within, cidx, first, last, _ = _tile_info(t)
    x = _load_x0(is_p, within, xp_ref, xs_ref, rtab_ref, ctab_ref)
    _, _, g1, _, _, _ = _mod_rows(mod_ref, cidx)
    out_a = mid_ref[:, 0:VW]
    o_f = mid_ref[:, VW:2 * VW]
    sg = mid_ref[:, 2 * VW:3 * VW]
    c0 = 3 * VW
    qs = mid_ref[:, c0:c0 + QK]
    k = mid_ref[:, c0 + QK:c0 + 2 * QK]
    v = mid_ref[:, c0 + 2 * QK:c0 + 2 * QK + VW]
    la_b = mid_ref[:, c0 + 2 * QK + VW:c0 + 3 * QK + VW]

    @pl.when(last)
    def _():
        _gla_state_load(p0_ref, p_scr)

    o = o_f + _gla_tile(qs, k, v.astype(BF16), la_b, p_scr, reverse=True)

    @pl.when(first)
    def _():
        _gla_state_store(p_scr, pfin_ref)

    heads = []
    for h in range(GLA_HEADS):
        oh = o[:, h * GLA_DV:(h + 1) * GLA_DV]
        heads.append(oh * lax.rsqrt(jnp.mean(oh * oh, axis=-1, keepdims=True) + EPS))
    on = jnp.concatenate(heads, axis=1) * gn_ref[...] * sg
    cat = jnp.concatenate([out_a, on], axis=1).astype(BF16)
    y = _dot(cat, wout_ref[...])
    x1_ref[...] = x + g1 * _rms(y, ng_ref[1:2, :])


def _even_reverse(xp, xs, rtab, ctab, mod, ng, mid, w):
    state_blk = (1, 1, GLA_HEADS, GLA_DK, GLA_DV)
    return pl.pallas_call(
        _ev2_kernel,
        out_shape=(jax.ShapeDtypeStruct((NT * TM, D), F32),
                   jax.ShapeDtypeStruct((NSEQ, 1) + state_blk[2:], F32)),
        grid=(NT,),
        in_specs=[
            _xp_spec(True), _xs_spec(True),
            _const_spec((GRID_W, D // 2)), _const_spec((GRID_W, D // 2)),
            _const_spec((NCOND, 6 * D)), _const_spec((4, D)),
            _tile_spec(EV_MID, rev=True),
            _const_spec((1, VW)), _const_spec((2 * VW, D)),
            pl.BlockSpec(state_blk, lambda i: (_seq_of(NT - 1 - i), 1, 0, 0, 0)),
        ],
        out_specs=(_tile_spec(D, rev=True),
                   pl.BlockSpec(state_blk, lambda i: (_seq_of(NT - 1 - i), 0, 0, 0, 0))),
        scratch_shapes=[pltpu.VMEM((GLA_HEADS, GLA_DV, LANES), F32)],
        compiler_params=_params(),
        name="even_reverse",
    )(xp, xs, rtab, ctab, mod, ng, mid, w["gn"], w["wout"], w["p0"])


FF_CHUNK = 1024


def _mlp_body(x, mod_ref, ng_ref, w1_ref, b1_ref, w2_ref, b2_ref, cidx):
    _, _, _, sh2, sc2, g2 = _mod_rows(mod_ref, cidx)
    hff = (_rms(x, ng_ref[2:3, :]) * (1.0 + sc2) + sh2).astype(BF16)
    acc = jnp.zeros((TM, D), F32)
    for j in range(D_FF // FF_CHUNK):
        cs = slice(j * FF_CHUNK, (j + 1) * FF_CHUNK)
        h = _dot(hff, w1_ref[:, cs]) + b1_ref[:, cs]
        h = jnp.square(jnp.maximum(h, 0.0)).astype(BF16)
        acc = acc + _dot(h, w2_ref[cs, :])
    f = acc + b2_ref[...]
    return x + g2 * _rms(f, ng_ref[3:4, :])


def _mlp_kernel(x_ref, mod_ref, ng_ref, w1_ref, b1_ref, w2_ref, b2_ref, o_ref):
    _, _, cidx, _, _, _ = _tile_info(pl.program_id(0))
    o_ref[...] = _mlp_body(x_ref[...], mod_ref, ng_ref, w1_ref, b1_ref, w2_ref, b2_ref, cidx)


def _mlp_final_kernel(x_ref, mod_ref, ng_ref, w1_ref, b1_ref, w2_ref, b2_ref, op_ref, os_ref):
    t = pl.program_id(0)
    is_p, _, cidx, _, _, _ = _tile_info(t)
    y = _mlp_body(x_ref[...], mod_ref, ng_ref, w1_ref, b1_ref, w2_ref, b2_ref, cidx)

    @pl.when(is_p)
    def _():
        op_ref[...] = y

    @pl.when(jnp.logical_not(is_p))
    def _():
        os_ref[...] = y


def _mlp(x, mod, ng, w1, b1, w2, b2, final):
    in_specs = [
        _tile_spec(D),
        _const_spec((NCOND, 6 * D)), _const_spec((4, D)),
        _const_spec((D, D_FF)), _const_spec((1, D_FF)), _const_spec((D_FF, D)), _const_spec((1, D)),
    ]
    if final:
        out_shape = (jax.ShapeDtypeStruct((NTP * TM, D), F32), jax.ShapeDtypeStruct((NTS * TM, D), F32))
        out_specs = (_xp_spec(False), _xs_spec(False))
        body, name = _mlp_final_kernel, "mlp_final"
    else:
        out_shape = jax.ShapeDtypeStruct((NT * TM, D), F32)
        out_specs = _tile_spec(D)
        body, name = _mlp_kernel, "mlp"
    return pl.pallas_call(
        body, out_shape=out_shape, grid=(NT,), in_specs=in_specs, out_specs=out_specs,
        compiler_params=_params(), name=name,
    )(x, mod, ng, w1, b1, w2, b2)


def _rg_gates(xc, wg_ref, ba_ref, bx_ref, lam_ref, a_scr, b_scr):
    xcb = xc.astype(BF16)
    sp = _softplus(-lam_ref[...])
    for s in range(D_RNN // SLAB):
        cs = slice(s * SLAB, (s + 1) * SLAB)
        pre = _dot(xcb[:, cs], wg_ref[s])
        r = jax.nn.sigmoid(pre[:, 0:SLAB] + ba_ref[:, cs])
        i = jax.nn.sigmoid(pre[:, SLAB:2 * SLAB] + bx_ref[:, cs])
        log_a = (-RG_C) * r * sp[:, cs]
        a = jnp.exp(log_a)
        a_scr[:, cs] = a
        b_scr[:, cs] = jnp.sqrt(-jnp.tanh(log_a) * (1.0 + a * a)) * (i * xc[:, cs])


def _rg_scan(a_scr, b_scr, h_scr, h0, reverse):
    row = lax.broadcasted_iota(jnp.int32, (SUBLANES, D_RNN), 0)
    ngroups = TM // SUBLANES

    def body(j, hin):
        grp = (ngroups - 1 - j) if reverse else j
        r0 = pl.multiple_of(grp * SUBLANES, SUBLANES)
        a = a_scr[pl.ds(r0, SUBLANES), :]
        b = b_scr[pl.ds(r0, SUBLANES), :]
        for s in (1, 2, 4):
            shift = (SUBLANES - s) if reverse else s
            valid = (row < SUBLANES - s) if reverse else (row >= s)
            a_s = pltpu.roll(a, shift, 0)
            b_s = pltpu.roll(b, shift, 0)
            b = jnp.where(valid, a * b_s + b, b)
            a = jnp.where(valid, a * a_s, a)
        h = a * hin + b
        h_scr[pl.ds(r0, SUBLANES), :] = h
        return h[0:1, :] if reverse else h[SUBLANES - 1:SUBLANES, :]

    return lax.fori_loop(0, ngroups, body, h0)


def _od1_kernel(x_ref, xn_ref, mod_ref, ng_ref, win_ref, cw_ref, cb_ref, wg_ref, ba_ref, bx_ref, lam_ref,
                s0_ref, mid_ref, sfin_ref, xb_scr, a_scr, b_scr, h_scr, hc_scr):
    t = pl.program_id(0)
    _, _, cidx, first, last, _ = _tile_info(t)
    sh1, sc1, _, _, _, _ = _mod_rows(mod_ref, cidx)
    xe = jnp.concatenate([x_ref[...], xn_ref[...]], axis=0)
    hm = (_rms(xe, ng_ref[0:1, :]) * (1.0 + sc1) + sh1).astype(BF16)
    proj = _dot(hm, win_ref[...])

    @pl.when(first)
    def _():
        xb_scr[0:SUBLANES, :] = jnp.zeros((SUBLANES, D_RNN), F32)
        hc_scr[...] = s0_ref[0, 0]

    xb_scr[SUBLANES:SUBLANES + TM, :] = proj[0:TM, 0:D_RNN]
    xb_scr[SUBLANES + TM:SUBLANES + TM + HALO, :] = jnp.where(last, 0.0, proj[TM:TM + HALO, 0:D_RNN])
    xc = cb_ref[...]
    for j in range(4):
        xc = xc + cw_ref[j:j + 1, :] * xb_scr[SUBLANES - 2 + j:SUBLANES - 2 + j + TM, :]
    xb_scr[0:SUBLANES, :] = xb_scr[TM:TM + SUBLANES, :]

    _rg_gates(xc, wg_ref, ba_ref, bx_ref, lam_ref, a_scr, b_scr)
    hc_scr[...] = _rg_scan(a_scr, b_scr, h_scr, hc_scr[...], reverse=False)

    @pl.when(last)
    def _():
        sfin_ref[0, 0] = hc_scr[...]

    mid_ref[:, 0:D_RNN] = h_scr[...]
    mid_ref[:, D_RNN:2 * D_RNN] = jax.nn.gelu(proj[0:TM, D_RNN:2 * D_RNN])
    mid_ref[:, 2 * D_RNN:3 * D_RNN] = xc


def _odd_forward(x, mod, ng, w):
    nb16 = TM // HALO
    return pl.pallas_call(
        _od1_kernel,
        out_shape=(jax.ShapeDtypeStruct((NT * TM, OD_MID), F32),
                   jax.ShapeDtypeStruct((NSEQ, 1, 1, D_RNN), F32)),
        grid=(NT,),
        in_specs=[
            _tile_spec(D),
            pl.BlockSpec((HALO, D), lambda i: (jnp.minimum(i + 1, NT - 1) * nb16, 0)),
            _const_spec((NCOND, 6 * D)), _const_spec((4, D)),
            _const_spec((D, 2 * D_RNN)), _const_spec((4, D_RNN)), _const_spec((1, D_RNN)),
            _const_spec((D_RNN // SLAB, SLAB, 2 * SLAB)),
            _const_spec((1, D_RNN)), _const_spec((1, D_RNN)), _const_spec((1, D_RNN)),
            pl.BlockSpec((1, 1, 1, D_RNN), lambda i: (_seq_of(i), 0, 0, 0)),
        ],
        out_specs=(_tile_spec(OD_MID),
                   pl.BlockSpec((1, 1, 1, D_RNN), lambda i: (_seq_of(i), 0, 0, 0))),
        scratch_shapes=[
            pltpu.VMEM((SUBLANES + TM + HALO, D_RNN), F32),
            pltpu.VMEM((TM, D_RNN), F32), pltpu.VMEM((TM, D_RNN), F32), pltpu.VMEM((TM, D_RNN), F32),
            pltpu.VMEM((1, D_RNN), F32),
        ],
        compiler_params=_params(),
        name="odd_forward",
    )(x, x, mod, ng, w["win"], w["cw"], w["cb"], w["wg"][0], w["ba"][0:1], w["bx"][0:1], w["lam"][0:1],
      w["s0"])


def _od2_kernel(x_ref, mod_ref, ng_ref, mid_ref, wg_ref, ba_ref, bx_ref, lam_ref, wout_ref, s0_ref,
                x1_ref, sfin_ref, a_scr, b_scr, h_scr, hc_scr):
    t = NT - 1 - pl.program_id(0)
    _, _, cidx, first, last, _ = _tile_info(t)
    _, _, g1, _, _, _ = _mod_rows(mod_ref, cidx)

    @pl.when(last)
    def _():
        hc_scr[...] = s0_ref[0, 0]

    _rg_gates(mid_ref[:, 2 * D_RNN:3 * D_RNN], wg_ref, ba_ref, bx_ref, lam_ref, a_scr, b_scr)
    hc_scr[...] = _rg_scan(a_scr, b_scr, h_scr, hc_scr[...], reverse=True)

    @pl.when(first)
    def _():
        sfin_ref[0, 0] = hc_scr[...]

    hsum = mid_ref[:, 0:D_RNN] + h_scr[...]
    y = _dot((hsum * mid_ref[:, D_RNN:2 * D_RNN]).astype(BF16), wout_ref[...])
    x1_ref[...] = x_ref[...] + g1 * _rms(y, ng_ref[1:2, :])


def _odd_reverse(x, mod, ng, mid, w):
    return pl.pallas_call(
        _od2_kernel,
        out_shape=(jax.ShapeDtypeStruct((NT * TM, D), F32),
                   jax.ShapeDtypeStruct((NSEQ, 1, 1, D_RNN), F32)),
        grid=(NT,),
        in_specs=[
            _tile_spec(D, rev=True),
            _const_spec((NCOND, 6 * D)), _const_spec((4, D)),
            _tile_spec(OD_MID, rev=True),
            _const_spec((D_RNN // SLAB, SLAB, 2 * SLAB)),
            _const_spec((1, D_RNN)), _const_spec((1, D_RNN)), _const_spec((1, D_RNN)),
            _const_spec((D_RNN, D)),
            pl.BlockSpec((1, 1, 1, D_RNN), lambda i: (_seq_of(NT - 1 - i), 1, 0, 0)),
        ],
        out_specs=(_tile_spec(D, rev=True),
                   pl.BlockSpec((1, 1, 1, D_RNN), lambda i: (_seq_of(NT - 1 - i), 0, 0, 0))),
        scratch_shapes=[
            pltpu.VMEM((TM, D_RNN), F32), pltpu.VMEM((TM, D_RNN), F32), pltpu.VMEM((TM, D_RNN), F32),
            pltpu.VMEM((1, D_RNN), F32),
        ],
        compiler_params=_params(),
        name="odd_reverse",
    )(x, mod, ng, mid, w["wg"][1], w["ba"][1:2], w["bx"][1:2], w["lam"][1:2], w["wout"], w["s0"])


def _pos_tables():
    n = D // 4
    omega = 1.0 / (10000.0 ** (jnp.arange(n, dtype=F32) / n))
    idx = jnp.arange(GRID_W, dtype=F32)[:, None] * omega
    tab = jnp.concatenate([jnp.sin(idx), jnp.cos(idx)], axis=-1)
    return tab, tab


def _block_diag_slabs(w):
    per = SLAB // RG_BS
    w = w.reshape(D_RNN // SLAB, per, RG_BS, RG_BS)
    eye = jnp.eye(per, dtype=w.dtype)
    return jnp.einsum("spij,pq->spiqj", w, eye).reshape(D_RNN // SLAB, SLAB, SLAB)


def kernel(x_prompt, x_sample, c, state_gla, state_rglru, c_ctx, mod_w, mod_b, norm_g, mlp_w1, mlp_b1, mlp_w2,
           mlp_b2, ev_w_in, ev_w_out, sgu_ln_g, sgu_ln_b, sgu_ws, sgu_bs, gla_gate_w2, gla_gate_b, gla_norm_g,
           rg_w_in, rg_conv_w, rg_conv_b, rg_wa, rg_ba, rg_wx, rg_bx, rg_L, rg_w_out):
    assert x_prompt.shape == (BATCH, SEQ, D) and x_sample.shape == (DEC_BATCH, DEC_SEQ, D)
    assert SEQ == TM and DEC_SEQ % TM == 0 and DEPTH == 2
    xp = x_prompt.reshape(NTP * TM, D)
    xs = x_sample.reshape(NTS * TM, D)
    cond8 = jnp.concatenate([c_ctx[None, :], c, jnp.zeros((NCOND - 1 - DEC_BATCH, D), F32)], axis=0)
    mods = _modulation(cond8, mod_w, mod_b)
    rtab, ctab = _pos_tables()

    gmat = jnp.zeros((LANES, 2 * QK), F32)
    gmat = gmat.at[0:GLA_RANK, 0:QK].set(gla_gate_w2[0, 0])
    gmat = gmat.at[GLA_RANK:2 * GLA_RANK, QK:2 * QK].set(gla_gate_w2[0, 1])
    s0_gla = jnp.concatenate([jnp.zeros((BATCH,) + state_gla.shape[2:], F32), state_gla[:, 0]], axis=0)
    ev = {
        "win": ev_w_in[0, :, 0:EV_MAIN].astype(BF16),
        "wlr": jnp.pad(ev_w_in[0, :, EV_MAIN:], ((0, 0), (0, LANES - 2 * GLA_RANK))).astype(BF16),
        "gmat": gmat.astype(BF16),
        "gb": gla_gate_b[0].reshape(1, 2 * QK),
        "lng": sgu_ln_g[0].reshape(1, SGU_WIDTH),
        "lnb": sgu_ln_b[0].reshape(1, SGU_WIDTH),
        "ws": sgu_ws[0].astype(BF16),
        "bst": sgu_bs[0].T,
        "gn": gla_norm_g[0].reshape(1, VW),
        "wout": ev_w_out[0].astype(BF16),
        "p0": s0_gla,
    }
    mid, pf = _even_forward(xp, xs, rtab, ctab, mods[0], norm_g[0], ev)
    x1, pb = _even_reverse(xp, xs, rtab, ctab, mods[0], norm_g[0], mid, ev)
    x2 = _mlp(x1, mods[0], norm_g[0], mlp_w1[0].astype(BF16), mlp_b1[0].reshape(1, D_FF),
              mlp_w2[0].astype(BF16), mlp_b2[0].reshape(1, D), final=False)
    new_gla = jnp.stack([pf[:BATCH, 0], pb[:BATCH, 0]], axis=1)[:, None]

    s0_rg = jnp.concatenate([jnp.zeros((BATCH, 2, D_RNN), F32), state_rglru[:, 0]], axis=0)
    od = {
        "win": rg_w_in[0].astype(BF16),
        "cw": rg_conv_w[0],
        "cb": rg_conv_b[0].reshape(1, D_RNN),
        "wg": jnp.stack([
            jnp.concatenate([_block_diag_slabs(rg_wa[0, d]), _block_diag_slabs(rg_wx[0, d])], axis=-1)
            for d in range(2)], axis=0).astype(BF16),
        "ba": rg_ba[0], "bx": rg_bx[0], "lam": rg_L[0],
        "wout": rg_w_out[0].astype(BF16),
        "s0": s0_rg.reshape(NSEQ, 2, 1, D_RNN),
    }
    mid1, sf = _odd_forward(x2, mods[1], norm_g[1], od)
    x3, sb = _odd_reverse(x2, mods[1], norm_g[1], mid1, od)
    y_prompt, y_sample = _mlp(x3, mods[1], norm_g[1], mlp_w1[1].astype(BF16), mlp_b1[1].reshape(1, D_FF),
                              mlp_w2[1].astype(BF16), mlp_b2[1].reshape(1, D), final=True)
    new_rg = jnp.stack([sf[:BATCH, 0, 0], sb[:BATCH, 0, 0]], axis=1)[:, None]
    return (y_prompt.reshape(BATCH, SEQ, D), y_sample.reshape(DEC_BATCH, DEC_SEQ, D), new_gla, new_rg)
```

```python
import functools

import jax
import jax.numpy as jnp
from jax import lax
from jax.experimental import pallas as pl
from jax.experimental.pallas import tpu as pltpu

D = 1024
BATCH = 16
SEQ = 256
DEPTH = 2
DEC_BATCH = 4
DEC_SEQ = 4096
GRID_W = 64
D_FF = 4 * D
EPS = 1e-6
SGU_CHUNK = 128
SGU_GROUPS = 4
SGU_WIDTH = D // 2
GLA_HEADS = 4
GLA_DV = 128
GLA_DK = 64
GLA_RANK = 16
GLA_NORMALIZER = 16.0
GLA_CHUNK = 64
QK = GLA_HEADS * GLA_DK
VW = GLA_HEADS * GLA_DV
EV_MAIN = 2 * SGU_WIDTH + 2 * QK + 2 * VW
D_RNN = D
RG_BLOCKS = 16
RG_BS = D_RNN // RG_BLOCKS
RG_C = 8.0
LANES = 128
SUBLANES = 8
SLAB = 256

TM = 256
NTP = BATCH * SEQ // TM
TPS = DEC_SEQ // TM
NTS = DEC_BATCH * TPS
NT = NTP + NTS
NSEQ = BATCH + DEC_BATCH
NCOND = 8
HALO = 16
EV_MID = 2 * VW + VW + 2 * QK + VW + QK
OD_MID = 3 * D_RNN
VMEM_LIMIT = 56 * 1024 * 1024

F32 = jnp.float32
BF16 = jnp.bfloat16


def _tile_info(t):
    is_p = t < NTP
    ts = jnp.maximum(t - NTP, 0)
    sq = ts // TPS
    within = ts % TPS
    cidx = jnp.where(is_p, 0, 1 + sq)
    first = jnp.logical_or(is_p, within == 0)
    last = jnp.logical_or(is_p, within == TPS - 1)
    seq = jnp.where(is_p, t, BATCH + sq)
    return is_p, within, cidx, first, last, seq


def _seq_of(t):
    return jnp.where(t < NTP, t, BATCH + jnp.maximum(t - NTP, 0) // TPS)


def _rms(x, g):
    return x * lax.rsqrt(jnp.mean(x * x, axis=-1, keepdims=True) + EPS) * g


def _dot(a, b):
    return jnp.dot(a, b, preferred_element_type=F32)


def _dot_nt(a, b):
    return lax.dot_general(a, b, (((1,), (1,)), ((), ())), preferred_element_type=F32)


def _dot_tn(a, b):
    return lax.dot_general(a, b, (((0,), (0,)), ((), ())), preferred_element_type=F32)


def _split3(x):
    hi = x.astype(BF16)
    r1 = x - hi.astype(F32)
    mid = r1.astype(BF16)
    lo = (r1 - mid.astype(F32)).astype(BF16)
    return hi, mid, lo


def _dot_exact_lhs(m, parts):
    return _dot(m, parts[0]) + _dot(m, parts[1]) + _dot(m, parts[2])


def _log_sigmoid(x):
    return jnp.minimum(x, 0.0) - jnp.log1p(jnp.exp(-jnp.abs(x)))


def _softplus(x):
    return jnp.maximum(x, 0.0) + jnp.log1p(jnp.exp(-jnp.abs(x)))


def _mod_rows(mod_ref, cidx):
    m = mod_ref[pl.ds(cidx, 1), :]
    return [m[:, j * D:(j + 1) * D] for j in range(6)]


def _load_x0(is_p, within, xp_ref, xs_ref, rtab_ref, ctab_ref):
    rows_per_tile = TM // GRID_W
    r0 = within * rows_per_tile
    posr = jnp.concatenate(
        [jnp.broadcast_to(rtab_ref[pl.ds(r0 + j, 1), :], (GRID_W, D // 2)) for j in range(rows_per_tile)],
        axis=0)
    posc = jnp.concatenate([ctab_ref[...]] * rows_per_tile, axis=0)
    pos = jnp.concatenate([posr, posc], axis=1)
    return jnp.where(is_p, xp_ref[...], xs_ref[...] + pos)


def _mod_kernel(cond_ref, w_ref, b_ref, o_ref):
    c = cond_ref[...]
    sc = (c * jax.nn.sigmoid(c)).astype(BF16)
    o_ref[0] = _dot(sc, w_ref[0].astype(BF16)) + b_ref[0]


def _modulation(cond8, mod_w, mod_b):
    nb = 6 * D // D
    return pl.pallas_call(
        _mod_kernel,
        out_shape=jax.ShapeDtypeStruct((DEPTH, NCOND, 6 * D), F32),
        grid=(DEPTH, nb),
        in_specs=[
            pl.BlockSpec((NCOND, D), lambda l, j: (0, 0)),
            pl.BlockSpec((1, D, D), lambda l, j: (l, 0, j)),
            pl.BlockSpec((1, 1, D), lambda l, j: (l, 0, j)),
        ],
        out_specs=pl.BlockSpec((1, NCOND, D), lambda l, j: (l, 0, j)),
        compiler_params=pltpu.CompilerParams(
            dimension_semantics=("arbitrary", "arbitrary"), vmem_limit_bytes=VMEM_LIMIT),
        name="modulation",
    )(cond8, mod_w, mod_b.reshape(DEPTH, 1, 6 * D))


def _gla_tile(qs, k, v_bf, la, p_ref, reverse):
    n = TM // GLA_CHUNK
    ri = lax.broadcasted_iota(jnp.int32, (TM, TM), 0)
    ci = lax.broadcasted_iota(jnp.int32, (TM, TM), 1)
    same = (ri // GLA_CHUNK) == (ci // GLA_CHUNK)
    order = (ci >= ri) if reverse else (ci <= ri)
    cum_m = jnp.where(jnp.logical_and(same, order), 1.0, 0.0).astype(BF16)
    tot_m = jnp.where(same, 1.0, 0.0).astype(BF16)
    parts = _split3(la)
    b = _dot_exact_lhs(cum_m, parts)
    btot = _dot_exact_lhs(tot_m, parts)
    qe = qs * jnp.exp(b)
    ke = (k * jnp.exp(-b)).astype(BF16)
    kd = (k * jnp.exp(btot - b)).astype(BF16)
    dec = jnp.exp(btot)
    lane = lax.broadcasted_iota(jnp.int32, (TM, QK), 1) % LANES
    qm = (jnp.where(lane < GLA_DK, qe, 0.0).astype(BF16),
          jnp.where(lane >= GLA_DK, qe, 0.0).astype(BF16))
    cr = lax.broadcasted_iota(jnp.int32, (GLA_CHUNK, GLA_CHUNK), 0)
    cc = lax.broadcasted_iota(jnp.int32, (GLA_CHUNK, GLA_CHUNK), 1)
    cmask = (cc >= cr) if reverse else (cc <= cr)
    rows = [None] * n
    for c in (reversed(range(n)) if reverse else range(n)):
        rs = slice(c * GLA_CHUNK, (c + 1) * GLA_CHUNK)
        outs = []
        for h in range(GLA_HEADS):
            ls = slice((h // 2) * LANES, (h // 2 + 1) * LANES)
            qmh = qm[h % 2][rs, ls]
            att = jnp.where(cmask, _dot_nt(qmh, ke[rs, ls]), 0.0).astype(BF16)
            vh = v_bf[rs, h * GLA_DV:(h + 1) * GLA_DV]
            p = p_ref[h]
            outs.append(_dot(att, vh) + _dot_nt(qmh, p.astype(BF16)))
            p_ref[h] = p * dec[c * GLA_CHUNK:c * GLA_CHUNK + 1, ls] + _dot_tn(vh, kd[rs, ls])
        rows[c] = jnp.concatenate(outs, axis=1)
    return jnp.concatenate(rows, axis=0)


def _gla_state_load(s_ref, p_scr):
    z = jnp.zeros((GLA_DK, GLA_DV), F32)
    for h in range(GLA_HEADS):
        s = s_ref[0, 0, h]
        p_scr[h] = (jnp.concatenate([s, z], axis=0) if h % 2 == 0 else jnp.concatenate([z, s], axis=0)).T


def _gla_state_store(p_scr, s_ref):
    for h in range(GLA_HEADS):
        s_ref[0, 0, h] = p_scr[h].T[(h % 2) * GLA_DK:(h % 2 + 1) * GLA_DK, :]


def _ev1_kernel(xp_ref, xs_ref, rtab_ref, ctab_ref, mod_ref, ng_ref, win_ref, wlr_ref, gmat_ref, gb_ref,
                lng_ref, lnb_ref, ws_ref, bst_ref, p0_ref, mid_ref, pfin_ref, p_scr):
    t = pl.program_id(0)
    is_p, within, cidx, first, last, _ = _tile_info(t)

    @pl.when(first)
    def _():
        _gla_state_load(p0_ref, p_scr)

    x = _load_x0(is_p, within, xp_ref, xs_ref, rtab_ref, ctab_ref)
    sh1, sc1, _, _, _, _ = _mod_rows(mod_ref, cidx)
    hm = (_rms(x, ng_ref[0:1, :]) * (1.0 + sc1) + sh1).astype(BF16)
    proj = _dot(hm, win_ref[...])
    lr = _dot(hm, wlr_ref[...]).astype(BF16)
    la_all = _log_sigmoid(_dot(lr, gmat_ref[...]) + gb_ref[...]) * (1.0 / GLA_NORMALIZER)

    u = jax.nn.gelu(proj[:, 0:SGU_WIDTH])
    vg = jax.nn.gelu(proj[:, SGU_WIDTH:2 * SGU_WIDTH])
    mu = jnp.mean(vg, axis=-1, keepdims=True)
    vc = vg - mu
    vn = (vc * lax.rsqrt(jnp.mean(vc * vc, axis=-1, keepdims=True) + EPS) * lng_ref[...] + lnb_ref[...]).astype(BF16)
    gd = SGU_WIDTH // SGU_GROUPS
    nch = TM // SGU_CHUNK
    sv_cols = []
    for g in range(SGU_GROUPS):
        vcat = jnp.concatenate(
            [vn[c * SGU_CHUNK:(c + 1) * SGU_CHUNK, g * gd:(g + 1) * gd] for c in range(nch)], axis=1)
        sg = _dot(ws_ref[g], vcat) + bst_ref[:, g:g + 1]
        sv_cols.append(jnp.concatenate([sg[:, c * gd:(c + 1) * gd] for c in range(nch)], axis=0))
    out_a = u * jnp.concatenate(sv_cols, axis=1)

    o0 = 2 * SGU_WIDTH
    qs = proj[:, o0:o0 + QK] * (GLA_DK ** -0.5)
    k = proj[:, o0 + QK:o0 + 2 * QK]
    v = proj[:, o0 + 2 * QK:o0 + 2 * QK + VW]
    g = proj[:, o0 + 2 * QK + VW:o0 + 2 * QK + 2 * VW]

    o_f = _gla_tile(qs, k, v.astype(BF16), la_all[:, 0:QK], p_scr, reverse=False)
    mid_ref[:, 0:VW] = out_a
    mid_ref[:, VW:2 * VW] = o_f
    mid_ref[:, 2 * VW:3 * VW] = g * jax.nn.sigmoid(g)
    c0 = 3 * VW
    mid_ref[:, c0:c0 + QK] = qs
    mid_ref[:, c0 + QK:c0 + 2 * QK] = k
    mid_ref[:, c0 + 2 * QK:c0 + 2 * QK + VW] = v
    mid_ref[:, c0 + 2 * QK + VW:c0 + 3 * QK + VW] = la_all[:, QK:2 * QK]

    @pl.when(last)
    def _():
        _gla_state_store(p_scr, pfin_ref)


def _const_spec(shape):
    nd = len(shape)
    return pl.BlockSpec(shape, lambda i, _nd=nd: (0,) * _nd, pipeline_mode=pl.Buffered(1))


def _xp_spec(rev):
    if rev:
        return pl.BlockSpec((TM, D), lambda i: (jnp.minimum(NT - 1 - i, NTP - 1), 0))
    return pl.BlockSpec((TM, D), lambda i: (jnp.minimum(i, NTP - 1), 0))


def _xs_spec(rev):
    if rev:
        return pl.BlockSpec((TM, D), lambda i: (jnp.maximum(NT - 1 - i - NTP, 0), 0))
    return pl.BlockSpec((TM, D), lambda i: (jnp.maximum(i - NTP, 0), 0))


def _tile_spec(width, rev=False):
    if rev:
        return pl.BlockSpec((TM, width), lambda i: (NT - 1 - i, 0))
    return pl.BlockSpec((TM, width), lambda i: (i, 0))


def _params():
    return pltpu.CompilerParams(dimension_semantics=("arbitrary",), vmem_limit_bytes=VMEM_LIMIT)


def _even_forward(xp, xs, rtab, ctab, mod, ng, w):
    state_blk = (1, 1, GLA_HEADS, GLA_DK, GLA_DV)
    return pl.pallas_call(
        _ev1_kernel,
        out_shape=(jax.ShapeDtypeStruct((NT * TM, EV_MID), F32),
                   jax.ShapeDtypeStruct((NSEQ, 1) + state_blk[2:], F32)),
        grid=(NT,),
        in_specs=[
            _xp_spec(False), _xs_spec(False),
            _const_spec((GRID_W, D // 2)), _const_spec((GRID_W, D // 2)),
            _const_spec((NCOND, 6 * D)), _const_spec((4, D)),
            _const_spec((D, EV_MAIN)), _const_spec((D, LANES)), _const_spec((LANES, 2 * QK)),
            _const_spec((1, 2 * QK)),
            _const_spec((1, SGU_WIDTH)), _const_spec((1, SGU_WIDTH)),
            _const_spec((SGU_GROUPS, SGU_CHUNK, SGU_CHUNK)), _const_spec((SGU_CHUNK, SGU_GROUPS)),
            pl.BlockSpec(state_blk, lambda i: (_seq_of(i), 0, 0, 0, 0)),
        ],
        out_specs=(_tile_spec(EV_MID),
                   pl.BlockSpec(state_blk, lambda i: (_seq_of(i), 0, 0, 0, 0))),
        scratch_shapes=[pltpu.VMEM((GLA_HEADS, GLA_DV, LANES), F32)],
        compiler_params=_params(),
        name="even_forward",
    )(xp, xs, rtab, ctab, mod, ng, w["win"], w["wlr"], w["gmat"], w["gb"], w["lng"], w["lnb"],
      w["ws"], w["bst"], w["p0"])


def _ev2_kernel(xp_ref, xs_ref, rtab_ref, ctab_ref, mod_ref, ng_ref, mid_ref, gn_ref, wout_ref, p0_ref,
                x1_ref, pfin_ref, p_scr):
    t = NT - 1 - pl.program_id(0)
    is_p, within, cidx, first, last, _ = _tile_info(t)

    @pl.when(last)
    def _():
        _gla_state_load(p0_ref, p_scr)

    x = _load_x0(is_p, within, xp_ref, xs_ref, rtab_ref, ctab_ref)
    _, _, g1, _, _, _ = _mod_rows(mod_ref, cidx)
    out_a = mid_ref[:, 0:VW]
    o_f = mid_ref[:, VW:2 * VW]
    sg = mid_ref[:, 2 * VW:3 * VW]
    c0 = 3 * VW
    qs = mid_ref[:, c0:c0 + QK]
    k = mid_ref[:, c0 + QK:c0 + 2 * QK]
    v = mid_ref[:, c0 + 2 * QK:c0 + 2 * QK + VW]
    la_b = mid_ref[:, c0 + 2 * QK + VW:c0 + 3 * QK + VW]

    o = o_f + _gla_tile(qs, k, v.astype(BF16), la_b, p_scr, reverse=True)
    heads = []
    for h in range(GLA_HEADS):
        oh = o[:, h * GLA_DV:(h + 1) * GLA_DV]
        heads.append(oh * lax.rsqrt(jnp.mean(oh * oh, axis=-1, keepdims=True) + EPS))
    on = jnp.concatenate(heads, axis=1) * gn_ref[...] * sg
    cat = jnp.concatenate([out_a, on], axis=1).astype(BF16)
    y = _dot(cat, wout_ref[...])
    x1_ref[...] = x + g1 * _rms(y, ng_ref[1:2, :])

    @pl.when(first)
    def _():
        _gla_state_store(p_scr, pfin_ref)


def _even_reverse(xp, xs, rtab, ctab, mod, ng, mid, w):
    state_blk = (1, 1, GLA_HEADS, GLA_DK, GLA_DV)
    return pl.pallas_call(
        _ev2_kernel,
        out_shape=(jax.ShapeDtypeStruct((NT * TM, D), F32),
                   jax.ShapeDtypeStruct((NSEQ, 1) + state_blk[2:], F32)),
        grid=(NT,),
        in_specs=[
            _xp_spec(True), _xs_spec(True),
            _const_spec((GRID_W, D // 2)), _const_spec((GRID_W, D // 2)),
            _const_spec((NCOND, 6 * D)), _const_spec((4, D)),
            _tile_spec(EV_MID, rev=True),
            _const_spec((1, VW)), _const_spec((2 * VW, D)),
            pl.BlockSpec(state_blk, lambda i: (_seq_of(NT - 1 - i), 1, 0, 0, 0)),
        ],
        out_specs=(_tile_spec(D, rev=True),
                   pl.BlockSpec(state_blk, lambda i: (_seq_of(NT - 1 - i), 0, 0, 0, 0))),
        scratch_shapes=[pltpu.VMEM((GLA_HEADS, GLA_DV, LANES), F32)],
        compiler_params=_params(),
        name="even_reverse",
    )(xp, xs, rtab, ctab, mod, ng, mid, w["gn"], w["wout"], w["p0"])


FF_CHUNK = 1024
MLP_TILES = 2
TMM = MLP_TILES * TM


def _mlp_kernel(x_ref, mod_ref, ng_ref, w1_ref, b1_ref, w2_ref, b2_ref, o_ref, *, tile0):
    _, _, cidx, _, _, _ = _tile_info(tile0 + pl.program_id(0) * MLP_TILES)
    _, _, _, sh2, sc2, g2 = _mod_rows(mod_ref, cidx)
    nch = D_FF // FF_CHUNK

    def rows(t):
        return slice(t * TM, (t + 1) * TM)

    def pre(t):
        return (_rms(x_ref[rows(t), :], ng_ref[2:3, :]) * (1.0 + sc2) + sh2).astype(BF16)

    def post(t, acc):
        o_ref[rows(t), :] = x_ref[rows(t), :] + g2 * _rms(acc, ng_ref[3:4, :])

    hff = pre(0)
    prev = None
    for t in range(MLP_TILES):
        acc = b2_ref[...]
        nxt = None
        for j in range(nch):
            cs = slice(j * FF_CHUNK, (j + 1) * FF_CHUNK)
            h = _dot(hff, w1_ref[:, cs]) + b1_ref[:, cs]
            h = jnp.square(jnp.maximum(h, 0.0)).astype(BF16)
            acc = acc + _dot(h, w2_ref[cs, :])
            if j == 0 and t + 1 < MLP_TILES:
                nxt = pre(t + 1)
            if j == 1 and prev is not None:
                post(t - 1, prev)
        prev, hff = acc, nxt
    post(MLP_TILES - 1, prev)


def _mlp(x, mod, ng, w1, b1, w2, b2, tile0, ntiles):
    assert tile0 % MLP_TILES == 0 and ntiles % MLP_TILES == 0 and TPS % MLP_TILES == 0 and NTP % MLP_TILES == 0
    b0 = tile0 // MLP_TILES
    return pl.pallas_call(
        functools.partial(_mlp_kernel, tile0=tile0),
        out_shape=jax.ShapeDtypeStruct((ntiles * TM, D), F32),
        grid=(ntiles // MLP_TILES,),
        in_specs=[
            pl.BlockSpec((TMM, D), lambda i: (b0 + i, 0)),
            _const_spec((NCOND, 6 * D)), _const_spec((4, D)),
            _const_spec((D, D_FF)), _const_spec((1, D_FF)), _const_spec((D_FF, D)), _const_spec((1, D)),
        ],
        out_specs=pl.BlockSpec((TMM, D), lambda i: (i, 0)),
        compiler_params=_params(), name="mlp",
    )(x, mod, ng, w1, b1, w2, b2)


def _rg_gates(xc, wg_ref, ba_ref, bx_ref, lam_ref, a_scr, b_scr):
    xcb = xc.astype(BF16)
    sp = _softplus(-lam_ref[...])
    for s in range(D_RNN // SLAB):
        cs = slice(s * SLAB, (s + 1) * SLAB)
        pre = _dot(xcb[:, cs], wg_ref[s])
        r = jax.nn.sigmoid(pre[:, 0:SLAB] + ba_ref[:, cs])
        i = jax.nn.sigmoid(pre[:, SLAB:2 * SLAB] + bx_ref[:, cs])
        log_a = (-RG_C) * r * sp[:, cs]
        a = jnp.exp(log_a)
        a_scr[:, cs] = a
        b_scr[:, cs] = jnp.sqrt(-jnp.tanh(log_a) * (1.0 + a * a)) * (i * xc[:, cs])


def _rg_scan(a_scr, b_scr, h_scr, h0, reverse):
    row = lax.broadcasted_iota(jnp.int32, (SUBLANES, D_RNN), 0)
    ngroups = TM // SUBLANES

    def body(j, hin):
        grp = (ngroups - 1 - j) if reverse else j
        r0 = pl.multiple_of(grp * SUBLANES, SUBLANES)
        a = a_scr[pl.ds(r0, SUBLANES), :]
        b = b_scr[pl.ds(r0, SUBLANES), :]
        for s in (1, 2, 4):
            shift = (SUBLANES - s) if reverse else s
            valid = (row < SUBLANES - s) if reverse else (row >= s)
            a_s = pltpu.roll(a, shift, 0)
            b_s = pltpu.roll(b, shift, 0)
            b = jnp.where(valid, a * b_s + b, b)
            a = jnp.where(valid, a * a_s, a)
        h = a * hin + b
        h_scr[pl.ds(r0, SUBLANES), :] = h
        return h[0:1, :] if reverse else h[SUBLANES - 1:SUBLANES, :]

    return lax.fori_loop(0, ngroups, body, h0)


def _od1_kernel(x_ref, xn_ref, mod_ref, ng_ref, win_ref, cw_ref, cb_ref, wg_ref, ba_ref, bx_ref, lam_ref,
                s0_ref, mid_ref, sfin_ref, xb_scr, a_scr, b_scr, h_scr, hc_scr):
    t = pl.program_id(0)
    _, _, cidx, first, last, _ = _tile_info(t)

    @pl.when(first)
    def _():
        xb_scr[0:SUBLANES, :] = jnp.zeros((SUBLANES, D_RNN), F32)
        hc_scr[...] = s0_ref[0, 0]

    sh1, sc1, _, _, _, _ = _mod_rows(mod_ref, cidx)
    xe = jnp.concatenate([x_ref[...], xn_ref[...]], axis=0)
    hm = (_rms(xe, ng_ref[0:1, :]) * (1.0 + sc1) + sh1).astype(BF16)
    proj = _dot(hm, win_ref[...])

    xb_scr[SUBLANES:SUBLANES + TM, :] = proj[0:TM, 0:D_RNN]
    xb_scr[SUBLANES + TM:SUBLANES + TM + HALO, :] = jnp.where(last, 0.0, proj[TM:TM + HALO, 0:D_RNN])
    xc = cb_ref[...]
    for j in range(4):
        xc = xc + cw_ref[j:j + 1, :] * xb_scr[SUBLANES - 2 + j:SUBLANES - 2 + j + TM, :]
    xb_scr[0:SUBLANES, :] = xb_scr[TM:TM + SUBLANES, :]

    _rg_gates(xc, wg_ref, ba_ref, bx_ref, lam_ref, a_scr, b_scr)
    hc_scr[...] = _rg_scan(a_scr, b_scr, h_scr, hc_scr[...], reverse=False)

    mid_ref[:, 0:D_RNN] = h_scr[...]
    mid_ref[:, D_RNN:2 * D_RNN] = jax.nn.gelu(proj[0:TM, D_RNN:2 * D_RNN])
    mid_ref[:, 2 * D_RNN:3 * D_RNN] = xc

    @pl.when(last)
    def _():
        sfin_ref[0, 0] = hc_scr[...]


def _odd_forward(x, mod, ng, w):
    nb16 = TM // HALO
    return pl.pallas_call(
        _od1_kernel,
        out_shape=(jax.ShapeDtypeStruct((NT * TM, OD_MID), F32),
                   jax.ShapeDtypeStruct((NSEQ, 1, 1, D_RNN), F32)),
        grid=(NT,),
        in_specs=[
            _tile_spec(D),
            pl.BlockSpec((HALO, D), lambda i: (jnp.minimum(i + 1, NT - 1) * nb16, 0)),
            _const_spec((NCOND, 6 * D)), _const_spec((4, D)),
            _const_spec((D, 2 * D_RNN)), _const_spec((4, D_RNN)), _const_spec((1, D_RNN)),
            _const_spec((D_RNN // SLAB, SLAB, 2 * SLAB)),
            _const_spec((1, D_RNN)), _const_spec((1, D_RNN)), _const_spec((1, D_RNN)),
            pl.BlockSpec((1, 1, 1, D_RNN), lambda i: (_seq_of(i), 0, 0, 0)),
        ],
        out_specs=(_tile_spec(OD_MID),
                   pl.BlockSpec((1, 1, 1, D_RNN), lambda i: (_seq_of(i), 0, 0, 0))),
        scratch_shapes=[
            pltpu.VMEM((SUBLANES + TM + HALO, D_RNN), F32),
            pltpu.VMEM((TM, D_RNN), F32), pltpu.VMEM((TM, D_RNN), F32), pltpu.VMEM((TM, D_RNN), F32),
            pltpu.VMEM((1, D_RNN), F32),
        ],
        compiler_params=_params(),
        name="odd_forward",
    )(x, x, mod, ng, w["win"], w["cw"], w["cb"], w["wg"][0], w["ba"][0:1], w["bx"][0:1], w["lam"][0:1],
      w["s0"])


def _od2_kernel(x_ref, mod_ref, ng_ref, mid_ref, wg_ref, ba_ref, bx_ref, lam_ref, wout_ref, s0_ref,
                x1_ref, sfin_ref, a_scr, b_scr, h_scr, hc_scr):
    t = NT - 1 - pl.program_id(0)
    _, _, cidx, first, last, _ = _tile_info(t)
    _, _, g1, _, _, _ = _mod_rows(mod_ref, cidx)

    @pl.when(last)
    def _():
        hc_scr[...] = s0_ref[0, 0]

    _rg_gates(mid_ref[:, 2 * D_RNN:3 * D_RNN], wg_ref, ba_ref, bx_ref, lam_ref, a_scr, b_scr)
    hc_scr[...] = _rg_scan(a_scr, b_scr, h_scr, hc_scr[...], reverse=True)

    hsum = mid_ref[:, 0:D_RNN] + h_scr[...]
    y = _dot((hsum * mid_ref[:, D_RNN:2 * D_RNN]).astype(BF16), wout_ref[...])
    x1_ref[...] = x_ref[...] + g1 * _rms(y, ng_ref[1:2, :])

    @pl.when(first)
    def _():
        sfin_ref[0, 0] = hc_scr[...]


def _odd_reverse(x, mod, ng, mid, w):
    return pl.pallas_call(
        _od2_kernel,
        out_shape=(jax.ShapeDtypeStruct((NT * TM, D), F32),
                   jax.ShapeDtypeStruct((NSEQ, 1, 1, D_RNN), F32)),
        grid=(NT,),
        in_specs=[
            _tile_spec(D, rev=True),
            _const_spec((NCOND, 6 * D)), _const_spec((4, D)),
            _tile_spec(OD_MID, rev=True),
            _const_spec((D_RNN // SLAB, SLAB, 2 * SLAB)),
            _const_spec((1, D_RNN)), _const_spec((1, D_RNN)), _const_spec((1, D_RNN)),
            _const_spec((D_RNN, D)),
            pl.BlockSpec((1, 1, 1, D_RNN), lambda i: (_seq_of(NT - 1 - i), 1, 0, 0)),
        ],
        out_specs=(_tile_spec(D, rev=True),
                   pl.BlockSpec((1, 1, 1, D_RNN), lambda i: (_seq_of(NT - 1 - i), 0, 0, 0))),
        scratch_shapes=[
            pltpu.VMEM((TM, D_RNN), F32), pltpu.VMEM((TM, D_RNN), F32), pltpu.VMEM((TM, D_RNN), F32),
            pltpu.VMEM((1, D_RNN), F32),
        ],
        compiler_params=_params(),
        name="odd_reverse",
    )(x, mod, ng, mid, w["wg"][1], w["ba"][1:2], w["bx"][1:2], w["lam"][1:2], w["wout"], w["s0"])


def _pos_tables():
    n = D // 4
    omega = 1.0 / (10000.0 ** (jnp.arange(n, dtype=F32) / n))
    idx = jnp.arange(GRID_W, dtype=F32)[:, None] * omega
    tab = jnp.concatenate([jnp.sin(idx), jnp.cos(idx)], axis=-1)
    return tab, tab


def _block_diag_slabs(w):
    per = SLAB // RG_BS
    w = w.reshape(D_RNN // SLAB, per, RG_BS, RG_BS)
    eye = jnp.eye(per, dtype=w.dtype)
    return jnp.einsum("spij,pq->spiqj", w, eye).reshape(D_RNN // SLAB, SLAB, SLAB)


def kernel(x_prompt, x_sample, c, state_gla, state_rglru, c_ctx, mod_w, mod_b, norm_g, mlp_w1, mlp_b1, mlp_w2,
           mlp_b2, ev_w_in, ev_w_out, sgu_ln_g, sgu_ln_b, sgu_ws, sgu_bs, gla_gate_w2, gla_gate_b, gla_norm_g,
           rg_w_in, rg_conv_w, rg_conv_b, rg_wa, rg_ba, rg_wx, rg_bx, rg_L, rg_w_out):
    assert x_prompt.shape == (BATCH, SEQ, D) and x_sample.shape == (DEC_BATCH, DEC_SEQ, D)
    assert SEQ == TM and DEC_SEQ % TM == 0 and DEPTH == 2
    xp = x_prompt.reshape(NTP * TM, D)
    xs = x_sample.reshape(NTS * TM, D)
    cond8 = jnp.concatenate([c_ctx[None, :], c, jnp.zeros((NCOND - 1 - DEC_BATCH, D), F32)], axis=0)
    mods = _modulation(cond8, mod_w, mod_b)
    rtab, ctab = _pos_tables()

    gmat = jnp.zeros((LANES, 2 * QK), F32)
    gmat = gmat.at[0:GLA_RANK, 0:QK].set(gla_gate_w2[0, 0])
    gmat = gmat.at[GLA_RANK:2 * GLA_RANK, QK:2 * QK].set(gla_gate_w2[0, 1])
    s0_gla = jnp.concatenate([jnp.zeros((BATCH,) + state_gla.shape[2:], F32), state_gla[:, 0]], axis=0)
    ev = {
        "win": ev_w_in[0, :, 0:EV_MAIN].astype(BF16),
        "wlr": jnp.pad(ev_w_in[0, :, EV_MAIN:], ((0, 0), (0, LANES - 2 * GLA_RANK))).astype(BF16),
        "gmat": gmat.astype(BF16),
        "gb": gla_gate_b[0].reshape(1, 2 * QK),
        "lng": sgu_ln_g[0].reshape(1, SGU_WIDTH),
        "lnb": sgu_ln_b[0].reshape(1, SGU_WIDTH),
        "ws": sgu_ws[0].astype(BF16),
        "bst": sgu_bs[0].T,
        "gn": gla_norm_g[0].reshape(1, VW),
        "wout": ev_w_out[0].astype(BF16),
        "p0": s0_gla,
    }
    mid, pf = _even_forward(xp, xs, rtab, ctab, mods[0], norm_g[0], ev)
    x1, pb = _even_reverse(xp, xs, rtab, ctab, mods[0], norm_g[0], mid, ev)
    x2 = _mlp(x1, mods[0], norm_g[0], mlp_w1[0].astype(BF16), mlp_b1[0].reshape(1, D_FF),
              mlp_w2[0].astype(BF16), mlp_b2[0].reshape(1, D), 0, NT)
    new_gla = jnp.stack([pf[:BATCH, 0], pb[:BATCH, 0]], axis=1)[:, None]

    s0_rg = jnp.concatenate([jnp.zeros((BATCH, 2, D_RNN), F32), state_rglru[:, 0]], axis=0)
    od = {
        "win": rg_w_in[0].astype(BF16),
        "cw": rg_conv_w[0],
        "cb": rg_conv_b[0].reshape(1, D_RNN),
        "wg": jnp.stack([
            jnp.concatenate([_block_diag_slabs(rg_wa[0, d]), _block_diag_slabs(rg_wx[0, d])], axis=-1)
            for d in range(2)], axis=0).astype(BF16),
        "ba": rg_ba[0], "bx": rg_bx[0], "lam": rg_L[0],
        "wout": rg_w_out[0].astype(BF16),
        "s0": s0_rg.reshape(NSEQ, 2, 1, D_RNN),
    }
    mid1, sf = _odd_forward(x2, mods[1], norm_g[1], od)
    x3, sb = _odd_reverse(x2, mods[1], norm_g[1], mid1, od)
    mlp1 = (mods[1], norm_g[1], mlp_w1[1].astype(BF16), mlp_b1[1].reshape(1, D_FF),
            mlp_w2[1].astype(BF16), mlp_b2[1].reshape(1, D))
    y_prompt = _mlp(x3, *mlp1, 0, NTP)
    y_sample = _mlp(x3, *mlp1, NTP, NTS)
    new_rg = jnp.stack([sf[:BATCH, 0, 0], sb[:BATCH, 0, 0]], axis=1)[:, None]
    return (y_prompt.reshape(BATCH, SEQ, D), y_sample.reshape(DEC_BATCH, DEC_SEQ, D), new_gla, new_rg)
```

```python
import functools

import jax
import jax.numpy as jnp
from jax import lax
from jax.experimental import pallas as pl
from jax.experimental.pallas import tpu as pltpu

D = 1024
BATCH = 16
SEQ = 256
DEPTH = 2
DEC_BATCH = 4
DEC_SEQ = 4096
GRID_W = 64
D_FF = 4 * D
EPS = 1e-6
SGU_CHUNK = 128
SGU_GROUPS = 4
SGU_WIDTH = D // 2
GLA_HEADS = 4
GLA_DV = 128
GLA_DK = 64
GLA_RANK = 16
GLA_NORMALIZER = 16.0
GLA_CHUNK = 64
QK = GLA_HEADS * GLA_DK
VW = GLA_HEADS * GLA_DV
EV_MAIN = 2 * SGU_WIDTH + 2 * QK + 2 * VW
D_RNN = D
RG_BLOCKS = 16
RG_BS = D_RNN // RG_BLOCKS
RG_C = 8.0
LANES = 128
SUBLANES = 8
SLAB = 256

TM = 256
NTP = BATCH * SEQ // TM
TPS = DEC_SEQ // TM
NTS = DEC_BATCH * TPS
NT = NTP + NTS
NSEQ = BATCH + DEC_BATCH
NCOND = 8
HALO = 16
EV_MID = 2 * VW + VW + 2 * QK + VW + QK
OD_MID = 3 * D_RNN
VMEM_LIMIT = 56 * 1024 * 1024

F32 = jnp.float32
BF16 = jnp.bfloat16


def _tile_info(t):
    is_p = t < NTP
    ts = jnp.maximum(t - NTP, 0)
    sq = ts // TPS
    within = ts % TPS
    cidx = jnp.where(is_p, 0, 1 + sq)
    first = jnp.logical_or(is_p, within == 0)
    last = jnp.logical_or(is_p, within == TPS - 1)
    seq = jnp.where(is_p, t, BATCH + sq)
    return is_p, within, cidx, first, last, seq


def _seq_of(t):
    return jnp.where(t < NTP, t, BATCH + jnp.maximum(t - NTP, 0) // TPS)


def _rms(x, g):
    return x * lax.rsqrt(jnp.mean(x * x, axis=-1, keepdims=True) + EPS) * g


def _dot(a, b):
    return jnp.dot(a, b, preferred_element_type=F32)


def _dot_nt(a, b):
    return lax.dot_general(a, b, (((1,), (1,)), ((), ())), preferred_element_type=F32)


def _dot_tn(a, b):
    return lax.dot_general(a, b, (((0,), (0,)), ((), ())), preferred_element_type=F32)


def _split3(x):
    hi = x.astype(BF16)
    r1 = x - hi.astype(F32)
    mid = r1.astype(BF16)
    lo = (r1 - mid.astype(F32)).astype(BF16)
    return hi, mid, lo


def _dot_exact_lhs(m, parts):
    return _dot(m, parts[0]) + _dot(m, parts[1]) + _dot(m, parts[2])


def _log_sigmoid(x):
    return jnp.minimum(x, 0.0) - jnp.log1p(jnp.exp(-jnp.abs(x)))


def _softplus(x):
    return jnp.maximum(x, 0.0) + jnp.log1p(jnp.exp(-jnp.abs(x)))


def _mod_rows(mod_ref, cidx):
    m = mod_ref[pl.ds(cidx, 1), :]
    return [m[:, j * D:(j + 1) * D] for j in range(6)]


def _load_x0(is_p, within, xp_ref, xs_ref, rtab_ref, ctab_ref):
    rows_per_tile = TM // GRID_W
    r0 = within * rows_per_tile
    posr = jnp.concatenate(
        [jnp.broadcast_to(rtab_ref[pl.ds(r0 + j, 1), :], (GRID_W, D // 2)) for j in range(rows_per_tile)],
        axis=0)
    posc = jnp.concatenate([ctab_ref[...]] * rows_per_tile, axis=0)
    pos = jnp.concatenate([posr, posc], axis=1)
    return jnp.where(is_p, xp_ref[...], xs_ref[...] + pos)


def _mod_kernel(cond_ref, w_ref, b_ref, o_ref):
    c = cond_ref[...]
    sc = (c * jax.nn.sigmoid(c)).astype(BF16)
    o_ref[0] = _dot(sc, w_ref[0].astype(BF16)) + b_ref[0]


def _modulation(cond8, mod_w, mod_b):
    nb = 6 * D // D
    return pl.pallas_call(
        _mod_kernel,
        out_shape=jax.ShapeDtypeStruct((DEPTH, NCOND, 6 * D), F32),
        grid=(DEPTH, nb),
        in_specs=[
            pl.BlockSpec((NCOND, D), lambda l, j: (0, 0)),
            pl.BlockSpec((1, D, D), lambda l, j: (l, 0, j)),
            pl.BlockSpec((1, 1, D), lambda l, j: (l, 0, j)),
        ],
        out_specs=pl.BlockSpec((1, NCOND, D), lambda l, j: (l, 0, j)),
        compiler_params=pltpu.CompilerParams(
            dimension_semantics=("arbitrary", "arbitrary"), vmem_limit_bytes=VMEM_LIMIT),
        name="modulation",
    )(cond8, mod_w, mod_b.reshape(DEPTH, 1, 6 * D))


def _gla_tile(qs, k, v_bf, la, p_ref, reverse):
    n = TM // GLA_CHUNK
    ri = lax.broadcasted_iota(jnp.int32, (TM, TM), 0)
    ci = lax.broadcasted_iota(jnp.int32, (TM, TM), 1)
    same = (ri // GLA_CHUNK) == (ci // GLA_CHUNK)
    order = (ci >= ri) if reverse else (ci <= ri)
    cum_m = jnp.where(jnp.logical_and(same, order), 1.0, 0.0).astype(BF16)
    tot_m = jnp.where(same, 1.0, 0.0).astype(BF16)
    parts = _split3(la)
    b = _dot_exact_lhs(cum_m, parts)
    btot = _dot_exact_lhs(tot_m, parts)
    qe = qs * jnp.exp(b)
    ke = (k * jnp.exp(-b)).astype(BF16)
    kd = (k * jnp.exp(btot - b)).astype(BF16)
    dec = jnp.exp(btot)
    lane = lax.broadcasted_iota(jnp.int32, (TM, QK), 1) % LANES
    qm = (jnp.where(lane < GLA_DK, qe, 0.0).astype(BF16),
          jnp.where(lane >= GLA_DK, qe, 0.0).astype(BF16))
    cr = lax.broadcasted_iota(jnp.int32, (GLA_CHUNK, GLA_CHUNK), 0)
    cc = lax.broadcasted_iota(jnp.int32, (GLA_CHUNK, GLA_CHUNK), 1)
    cmask = (cc >= cr) if reverse else (cc <= cr)
    rows = [None] * n
    for c in (reversed(range(n)) if reverse else range(n)):
        rs = slice(c * GLA_CHUNK, (c + 1) * GLA_CHUNK)
        outs = []
        for h in range(GLA_HEADS):
            ls = slice((h // 2) * LANES, (h // 2 + 1) * LANES)
            qmh = qm[h % 2][rs, ls]
            att = jnp.where(cmask, _dot_nt(qmh, ke[rs, ls]), 0.0).astype(BF16)
            vh = v_bf[rs, h * GLA_DV:(h + 1) * GLA_DV]
            p = p_ref[h]
            outs.append(_dot(att, vh) + _dot_nt(qmh, p.astype(BF16)))
            p_ref[h] = p * dec[c * GLA_CHUNK:c * GLA_CHUNK + 1, ls] + _dot_tn(vh, kd[rs, ls])
        rows[c] = jnp.concatenate(outs, axis=1)
    return jnp.concatenate(rows, axis=0)


def _gla_state_load(s_ref, p_scr):
    z = jnp.zeros((GLA_DK, GLA_DV), F32)
    for h in range(GLA_HEADS):
        s = s_ref[0, 0, h]
        p_scr[h] = (jnp.concatenate([s, z], axis=0) if h % 2 == 0 else jnp.concatenate([z, s], axis=0)).T


def _gla_state_store(p_scr, s_ref):
    for h in range(GLA_HEADS):
        s_ref[0, 0, h] = p_scr[h].T[(h % 2) * GLA_DK:(h % 2 + 1) * GLA_DK, :]


def _ev1_kernel(xp_ref, xs_ref, rtab_ref, ctab_ref, mod_ref, ng_ref, win_ref, wlr_ref, gmat_ref, gb_ref,
                lng_ref, lnb_ref, ws_ref, bst_ref, p0_ref, mid_ref, pfin_ref, p_scr):
    t = pl.program_id(0)
    is_p, within, cidx, first, last, _ = _tile_info(t)

    @pl.when(first)
    def _():
        _gla_state_load(p0_ref, p_scr)

    x = _load_x0(is_p, within, xp_ref, xs_ref, rtab_ref, ctab_ref)
    sh1, sc1, _, _, _, _ = _mod_rows(mod_ref, cidx)
    hm = (_rms(x, ng_ref[0:1, :]) * (1.0 + sc1) + sh1).astype(BF16)
    proj = _dot(hm, win_ref[...])
    lr = _dot(hm, wlr_ref[...]).astype(BF16)
    la_all = _log_sigmoid(_dot(lr, gmat_ref[...]) + gb_ref[...]) * (1.0 / GLA_NORMALIZER)

    u = jax.nn.gelu(proj[:, 0:SGU_WIDTH])
    vg = jax.nn.gelu(proj[:, SGU_WIDTH:2 * SGU_WIDTH])
    mu = jnp.mean(vg, axis=-1, keepdims=True)
    vc = vg - mu
    vn = (vc * lax.rsqrt(jnp.mean(vc * vc, axis=-1, keepdims=True) + EPS) * lng_ref[...] + lnb_ref[...]).astype(BF16)
    gd = SGU_WIDTH // SGU_GROUPS
    nch = TM // SGU_CHUNK
    sv_cols = []
    for g in range(SGU_GROUPS):
        vcat = jnp.concatenate(
            [vn[c * SGU_CHUNK:(c + 1) * SGU_CHUNK, g * gd:(g + 1) * gd] for c in range(nch)], axis=1)
        sg = _dot(ws_ref[g], vcat) + bst_ref[:, g:g + 1]
        sv_cols.append(jnp.concatenate([sg[:, c * gd:(c + 1) * gd] for c in range(nch)], axis=0))
    out_a = u * jnp.concatenate(sv_cols, axis=1)

    o0 = 2 * SGU_WIDTH
    qs = proj[:, o0:o0 + QK] * (GLA_DK ** -0.5)
    k = proj[:, o0 + QK:o0 + 2 * QK]
    v = proj[:, o0 + 2 * QK:o0 + 2 * QK + VW]
    g = proj[:, o0 + 2 * QK + VW:o0 + 2 * QK + 2 * VW]

    o_f = _gla_tile(qs, k, v.astype(BF16), la_all[:, 0:QK], p_scr, reverse=False)
    mid_ref[:, 0:VW] = out_a
    mid_ref[:, VW:2 * VW] = o_f
    mid_ref[:, 2 * VW:3 * VW] = g * jax.nn.sigmoid(g)
    c0 = 3 * VW
    mid_ref[:, c0:c0 + QK] = qs
    mid_ref[:, c0 + QK:c0 + 2 * QK] = k
    mid_ref[:, c0 + 2 * QK:c0 + 2 * QK + VW] = v
    mid_ref[:, c0 + 2 * QK + VW:c0 + 3 * QK + VW] = la_all[:, QK:2 * QK]

    @pl.when(last)
    def _():
        _gla_state_store(p_scr, pfin_ref)


def _const_spec(shape):
    nd = len(shape)
    return pl.BlockSpec(shape, lambda i, _nd=nd: (0,) * _nd, pipeline_mode=pl.Buffered(1))


def _xp_spec(rev):
    if rev:
        return pl.BlockSpec((TM, D), lambda i: (jnp.minimum(NT - 1 - i, NTP - 1), 0))
    return pl.BlockSpec((TM, D), lambda i: (jnp.minimum(i, NTP - 1), 0))


def _xs_spec(rev):
    if rev:
        return pl.BlockSpec((TM, D), lambda i: (jnp.maximum(NT - 1 - i - NTP, 0), 0))
    return pl.BlockSpec((TM, D), lambda i: (jnp.maximum(i - NTP, 0), 0))


def _tile_spec(width, rev=False):
    if rev:
        return pl.BlockSpec((TM, width), lambda i: (NT - 1 - i, 0))
    return pl.BlockSpec((TM, width), lambda i: (i, 0))


def _params():
    return pltpu.CompilerParams(dimension_semantics=("arbitrary",), vmem_limit_bytes=VMEM_LIMIT)


def _even_forward(xp, xs, rtab, ctab, mod, ng, w):
    state_blk = (1, 1, GLA_HEADS, GLA_DK, GLA_DV)
    return pl.pallas_call(
        _ev1_kernel,
        out_shape=(jax.ShapeDtypeStruct((NT * TM, EV_MID), F32),
                   jax.ShapeDtypeStruct((NSEQ, 1) + state_blk[2:], F32)),
        grid=(NT,),
        in_specs=[
            _xp_spec(False), _xs_spec(False),
            _const_spec((GRID_W, D // 2)), _const_spec((GRID_W, D // 2)),
            _const_spec((NCOND, 6 * D)), _const_spec((4, D)),
            _const_spec((D, EV_MAIN)), _const_spec((D, LANES)), _const_spec((LANES, 2 * QK)),
            _const_spec((1, 2 * QK)),
            _const_spec((1, SGU_WIDTH)), _const_spec((1, SGU_WIDTH)),
            _const_spec((SGU_GROUPS, SGU_CHUNK, SGU_CHUNK)), _const_spec((SGU_CHUNK, SGU_GROUPS)),
            pl.BlockSpec(state_blk, lambda i: (_seq_of(i), 0, 0, 0, 0)),
        ],
        out_specs=(_tile_spec(EV_MID),
                   pl.BlockSpec(state_blk, lambda i: (_seq_of(i), 0, 0, 0, 0))),
        scratch_shapes=[pltpu.VMEM((GLA_HEADS, GLA_DV, LANES), F32)],
        compiler_params=_params(),
        name="even_forward",
    )(xp, xs, rtab, ctab, mod, ng, w["win"], w["wlr"], w["gmat"], w["gb"], w["lng"], w["lnb"],
      w["ws"], w["bst"], w["p0"])


def _ev2_kernel(xp_ref, xs_ref, rtab_ref, ctab_ref, mod_ref, ng_ref, mid_ref, gn_ref, wout_ref, p0_ref,
                x1_ref, pfin_ref, p_scr):
    t = NT - 1 - pl.program_id(0)
    is_p, within, cidx, first, last, _ = _tile_info(t)

    @pl.when(last)
    def _():
        _gla_state_load(p0_ref, p_scr)

    x = _load_x0(is_p, within, xp_ref, xs_ref, rtab_ref, ctab_ref)
    _, _, g1, _, _, _ = _mod_rows(mod_ref, cidx)
    out_a = mid_ref[:, 0:VW]
    o_f = mid_ref[:, VW:2 * VW]
    sg = mid_ref[:, 2 * VW:3 * VW]
    c0 = 3 * VW
    qs = mid_ref[:, c0:c0 + QK]
    k = mid_ref[:, c0 + QK:c0 + 2 * QK]
    v = mid_ref[:, c0 + 2 * QK:c0 + 2 * QK + VW]
    la_b = mid_ref[:, c0 + 2 * QK + VW:c0 + 3 * QK + VW]

    o = o_f + _gla_tile(qs, k, v.astype(BF16), la_b, p_scr, reverse=True)
    heads = []
    for h in range(GLA_HEADS):
        oh = o[:, h * GLA_DV:(h + 1) * GLA_DV]
        heads.append(oh * lax.rsqrt(jnp.mean(oh * oh, axis=-1, keepdims=True) + EPS))
    on = jnp.concatenate(heads, axis=1) * gn_ref[...] * sg
    cat = jnp.concatenate([out_a, on], axis=1).astype(BF16)
    y = _dot(cat, wout_ref[...])
    x1_ref[...] = x + g1 * _rms(y, ng_ref[1:2, :])

    @pl.when(first)
    def _():
        _gla_state_store(p_scr, pfin_ref)


def _even_reverse(xp, xs, rtab, ctab, mod, ng, mid, w):
    state_blk = (1, 1, GLA_HEADS, GLA_DK, GLA_DV)
    return pl.pallas_call(
        _ev2_kernel,
        out_shape=(jax.ShapeDtypeStruct((NT * TM, D), F32),
                   jax.ShapeDtypeStruct((NSEQ, 1) + state_blk[2:], F32)),
        grid=(NT,),
        in_specs=[
            _xp_spec(True), _xs_spec(True),
            _const_spec((GRID_W, D // 2)), _const_spec((GRID_W, D // 2)),
            _const_spec((NCOND, 6 * D)), _const_spec((4, D)),
            _tile_spec(EV_MID, rev=True),
            _const_spec((1, VW)), _const_spec((2 * VW, D)),
            pl.BlockSpec(state_blk, lambda i: (_seq_of(NT - 1 - i), 1, 0, 0, 0)),
        ],
        out_specs=(_tile_spec(D, rev=True),
                   pl.BlockSpec(state_blk, lambda i: (_seq_of(NT - 1 - i), 0, 0, 0, 0))),
        scratch_shapes=[pltpu.VMEM((GLA_HEADS, GLA_DV, LANES), F32)],
        compiler_params=_params(),
        name="even_reverse",
    )(xp, xs, rtab, ctab, mod, ng, mid, w["gn"], w["wout"], w["p0"])


FF_CHUNK = 1024
MLP_TILES = 2
TMM = MLP_TILES * TM


def _mlp_kernel(x_ref, mod_ref, ng_ref, w1_ref, b1_ref, w2_ref, b2_ref, o_ref, *, tile0):
    _, _, cidx, _, _, _ = _tile_info(tile0 + pl.program_id(0) * MLP_TILES)
    _, _, _, sh2, sc2, g2 = _mod_rows(mod_ref, cidx)
    nch = D_FF // FF_CHUNK

    def rows(t):
        return slice(t * TM, (t + 1) * TM)

    def pre(t):
        return (_rms(x_ref[rows(t), :], ng_ref[2:3, :]) * (1.0 + sc2) + sh2).astype(BF16)

    def post(t, acc):
        o_ref[rows(t), :] = x_ref[rows(t), :] + g2 * _rms(acc, ng_ref[3:4, :])

    hff = pre(0)
    prev = None
    for t in range(MLP_TILES):
        acc = b2_ref[...]
        nxt = None
        for j in range(nch):
            cs = slice(j * FF_CHUNK, (j + 1) * FF_CHUNK)
            h = _dot(hff, w1_ref[:, cs]) + b1_ref[:, cs]
            h = jnp.square(jnp.maximum(h, 0.0)).astype(BF16)
            acc = acc + _dot(h, w2_ref[cs, :])
            if j == 0 and t + 1 < MLP_TILES:
                nxt = pre(t + 1)
            if j == 1 and prev is not None:
                post(t - 1, prev)
        prev, hff = acc, nxt
    post(MLP_TILES - 1, prev)


def _mlp(x, mod, ng, w1, b1, w2, b2, tile0, ntiles):
    assert tile0 % MLP_TILES == 0 and ntiles % MLP_TILES == 0 and TPS % MLP_TILES == 0 and NTP % MLP_TILES == 0
    b0 = tile0 // MLP_TILES
    return pl.pallas_call(
        functools.partial(_mlp_kernel, tile0=tile0),
        out_shape=jax.ShapeDtypeStruct((ntiles * TM, D), F32),
        grid=(ntiles // MLP_TILES,),
        in_specs=[
            pl.BlockSpec((TMM, D), lambda i: (b0 + i, 0)),
            _const_spec((NCOND, 6 * D)), _const_spec((4, D)),
            _const_spec((D, D_FF)), _const_spec((1, D_FF)), _const_spec((D_FF, D)), _const_spec((1, D)),
        ],
        out_specs=pl.BlockSpec((TMM, D), lambda i: (i, 0)),
        compiler_params=_params(), name="mlp",
    )(x, mod, ng, w1, b1, w2, b2)


def _rg_gates(xc, wg_ref, ba_ref, bx_ref, lam_ref, a_scr, b_scr):
    xcb = xc.astype(BF16)
    sp = _softplus(-lam_ref[...])
    for s in range(D_RNN // SLAB):
        cs = slice(s * SLAB, (s + 1) * SLAB)
        pre = _dot(xcb[:, cs], wg_ref[s])
        r = jax.nn.sigmoid(pre[:, 0:SLAB] + ba_ref[:, cs])
        i = jax.nn.sigmoid(pre[:, SLAB:2 * SLAB] + bx_ref[:, cs])
        log_a = (-RG_C) * r * sp[:, cs]
        a = jnp.exp(log_a)
        a_scr[:, cs] = a
        b_scr[:, cs] = jnp.sqrt(-jnp.tanh(log_a) * (1.0 + a * a)) * (i * xc[:, cs])


SEG = TM // SUBLANES


def _seg_perm(transpose):
    rr = lax.broadcasted_iota(jnp.int32, (TM, TM), 0)
    cc = lax.broadcasted_iota(jnp.int32, (TM, TM), 1)
    if transpose:
        rr, cc = cc, rr
    hit = jnp.logical_and(cc // SEG == rr % SUBLANES, cc % SEG == rr // SUBLANES)
    return jnp.where(hit, 1.0, 0.0).astype(BF16)


def _rg_scan(a_scr, b_scr, h_scr, c_scr, h0, reverse):
    def body(i, carry):
        hh, cc = carry
        grp = (SEG - 1 - i) if reverse else i
        r0 = pl.multiple_of(grp * SUBLANES, SUBLANES)
        a = a_scr[pl.ds(r0, SUBLANES), :]
        hh = a * hh + b_scr[pl.ds(r0, SUBLANES), :]
        cc = a * cc
        h_scr[pl.ds(r0, SUBLANES), :] = hh
        c_scr[pl.ds(r0, SUBLANES), :] = cc
        return hh, cc

    init = (jnp.zeros((SUBLANES, D_RNN), F32), jnp.ones((SUBLANES, D_RNN), F32))
    b, a = lax.fori_loop(0, SEG, body, init, unroll=4)
    row = lax.broadcasted_iota(jnp.int32, (SUBLANES, D_RNN), 0)
    for s in (1, 2, 4):
        shift = (SUBLANES - s) if reverse else s
        valid = (row < SUBLANES - s) if reverse else (row >= s)
        a_s = pltpu.roll(a, shift, 0)
        b_s = pltpu.roll(b, shift, 0)
        b = jnp.where(valid, a * b_s + b, b)
        a = jnp.where(valid, a * a_s, a)
    after = a * h0 + b
    edge = (row == SUBLANES - 1) if reverse else (row == 0)
    enter = jnp.where(edge, h0, pltpu.roll(after, (SUBLANES - 1) if reverse else 1, 0))
    h = h_scr[...] + c_scr[...] * pltpu.repeat(enter, SEG, axis=0)
    out = after[0:1, :] if reverse else after[SUBLANES - 1:SUBLANES, :]
    return h, out


def _od1_kernel(x_ref, xn_ref, mod_ref, ng_ref, win_ref, cw_ref, cb_ref, wg_ref, ba_ref, bx_ref, lam_ref,
                s0_ref, mid_ref, sfin_ref, xb_scr, a_scr, b_scr, h_scr, c_scr, hc_scr, tail_scr):
    t = pl.program_id(0)
    _, _, cidx, first, last, _ = _tile_info(t)

    @pl.when(first)
    def _():
        tail_scr[...] = jnp.zeros((2 * SUBLANES, D_RNN), F32)
        hc_scr[...] = s0_ref[0, 0]

    sh1, sc1, _, _, _, _ = _mod_rows(mod_ref, cidx)
    xe = jnp.concatenate([x_ref[...], xn_ref[...]], axis=0)
    hm = (_rms(xe, ng_ref[0:1, :]) * (1.0 + sc1) + sh1).astype(BF16)
    hmp = _dot(_seg_perm(False), hm[0:TM, :]).astype(BF16)
    proj = _dot(jnp.concatenate([hmp, hm[TM:TM + HALO, :]], axis=0), win_ref[...])

    xb = proj[0:TM, 0:D_RNN]
    row = lax.broadcasted_iota(jnp.int32, (SUBLANES, D_RNN), 0)
    g30 = xb[TM - 2 * SUBLANES:TM - SUBLANES, :]
    g31 = xb[TM - SUBLANES:TM, :]
    nxt = jnp.where(last, 0.0, proj[TM:TM + 1, 0:D_RNN])
    xb_scr[0:SUBLANES, :] = jnp.where(row == 0, pltpu.roll(tail_scr[0:SUBLANES, :], 1, 0), pltpu.roll(g30, 1, 0))
    xb_scr[SUBLANES:2 * SUBLANES, :] = jnp.where(
        row == 0, pltpu.roll(tail_scr[SUBLANES:2 * SUBLANES, :], 1, 0), pltpu.roll(g31, 1, 0))
    xb_scr[2 * SUBLANES:2 * SUBLANES + TM, :] = xb
    xb_scr[2 * SUBLANES + TM:3 * SUBLANES + TM, :] = jnp.where(
        row == SUBLANES - 1, nxt, pltpu.roll(xb[0:SUBLANES, :], SUBLANES - 1, 0))
    tail_scr[0:SUBLANES, :] = g30
    tail_scr[SUBLANES:2 * SUBLANES, :] = g31
    xc = cb_ref[...]
    for j in range(4):
        xc = xc + cw_ref[j:j + 1, :] * xb_scr[j * SUBLANES:j * SUBLANES + TM, :]

    _rg_gates(xc, wg_ref, ba_ref, bx_ref, lam_ref, a_scr, b_scr)
    h, hc_scr[...] = _rg_scan(a_scr, b_scr, h_scr, c_scr, hc_scr[...], reverse=False)

    mid_ref[:, 0:D_RNN] = h
    mid_ref[:, D_RNN:2 * D_RNN] = jax.nn.gelu(proj[0:TM, D_RNN:2 * D_RNN])
    mid_ref[:, 2 * D_RNN:3 * D_RNN] = xc

    @pl.when(last)
    def _():
        sfin_ref[0, 0] = hc_scr[...]


def _odd_forward(x, mod, ng, w):
    nb16 = TM // HALO
    return pl.pallas_call(
        _od1_kernel,
        out_shape=(jax.ShapeDtypeStruct((NT * TM, OD_MID), F32),
                   jax.ShapeDtypeStruct((NSEQ, 1, 1, D_RNN), F32)),
        grid=(NT,),
        in_specs=[
            _tile_spec(D),
            pl.BlockSpec((HALO, D), lambda i: (jnp.minimum(i + 1, NT - 1) * nb16, 0)),
            _const_spec((NCOND, 6 * D)), _const_spec((4, D)),
            _const_spec((D, 2 * D_RNN)), _const_spec((4, D_RNN)), _const_spec((1, D_RNN)),
            _const_spec((D_RNN // SLAB, SLAB, 2 * SLAB)),
            _const_spec((1, D_RNN)), _const_spec((1, D_RNN)), _const_spec((1, D_RNN)),
            pl.BlockSpec((1, 1, 1, D_RNN), lambda i: (_seq_of(i), 0, 0, 0)),
        ],
        out_specs=(_tile_spec(OD_MID),
                   pl.BlockSpec((1, 1, 1, D_RNN), lambda i: (_seq_of(i), 0, 0, 0))),
        scratch_shapes=[
            pltpu.VMEM((3 * SUBLANES + TM, D_RNN), F32),
            pltpu.VMEM((TM, D_RNN), F32), pltpu.VMEM((TM, D_RNN), F32), pltpu.VMEM((TM, D_RNN), F32),
            pltpu.VMEM((TM, D_RNN), F32),
            pltpu.VMEM((1, D_RNN), F32), pltpu.VMEM((2 * SUBLANES, D_RNN), F32),
        ],
        compiler_params=_params(),
        name="odd_forward",
    )(x, x, mod, ng, w["win"], w["cw"], w["cb"], w["wg"][0], w["ba"][0:1], w["bx"][0:1], w["lam"][0:1],
      w["s0"])


def _od2_kernel(x_ref, mod_ref, ng_ref, mid_ref, wg_ref, ba_ref, bx_ref, lam_ref, wout_ref, s0_ref,
                x1_ref, sfin_ref, a_scr, b_scr, h_scr, c_scr, hc_scr):
    t = NT - 1 - pl.program_id(0)
    _, _, cidx, first, last, _ = _tile_info(t)
    _, _, g1, _, _, _ = _mod_rows(mod_ref, cidx)

    @pl.when(last)
    def _():
        hc_scr[...] = s0_ref[0, 0]

    _rg_gates(mid_ref[:, 2 * D_RNN:3 * D_RNN], wg_ref, ba_ref, bx_ref, lam_ref, a_scr, b_scr)
    h_b, hc_scr[...] = _rg_scan(a_scr, b_scr, h_scr, c_scr, hc_scr[...], reverse=True)

    zp = ((mid_ref[:, 0:D_RNN] + h_b) * mid_ref[:, D_RNN:2 * D_RNN]).astype(BF16)
    z = _dot(_seg_perm(True), zp).astype(BF16)
    y = _dot(z, wout_ref[...])
    x1_ref[...] = x_ref[...] + g1 * _rms(y, ng_ref[1:2, :])

    @pl.when(first)
    def _():
        sfin_ref[0, 0] = hc_scr[...]


def _odd_reverse(x, mod, ng, mid, w):
    return pl.pallas_call(
        _od2_kernel,
        out_shape=(jax.ShapeDtypeStruct((NT * TM, D), F32),
                   jax.ShapeDtypeStruct((NSEQ, 1, 1, D_RNN), F32)),
        grid=(NT,),
        in_specs=[
            _tile_spec(D, rev=True),
            _const_spec((NCOND, 6 * D)), _const_spec((4, D)),
            _tile_spec(OD_MID, rev=True),
            _const_spec((D_RNN // SLAB, SLAB, 2 * SLAB)),
            _const_spec((1, D_RNN)), _const_spec((1, D_RNN)), _const_spec((1, D_RNN)),
            _const_spec((D_RNN, D)),
            pl.BlockSpec((1, 1, 1, D_RNN), lambda i: (_seq_of(NT - 1 - i), 1, 0, 0)),
        ],
        out_specs=(_tile_spec(D, rev=True),
                   pl.BlockSpec((1, 1, 1, D_RNN), lambda i: (_seq_of(NT - 1 - i), 0, 0, 0))),
        scratch_shapes=[
            pltpu.VMEM((TM, D_RNN), F32), pltpu.VMEM((TM, D_RNN), F32), pltpu.VMEM((TM, D_RNN), F32),
            pltpu.VMEM((TM, D_RNN), F32),
            pltpu.VMEM((1, D_RNN), F32),
        ],
        compiler_params=_params(),
        name="odd_reverse",
    )(x, mod, ng, mid, w["wg"][1], w["ba"][1:2], w["bx"][1:2], w["lam"][1:2], w["wout"], w["s0"])


def _pos_tables():
    n = D // 4
    omega = 1.0 / (10000.0 ** (jnp.arange(n, dtype=F32) / n))
    idx = jnp.arange(GRID_W, dtype=F32)[:, None] * omega
    tab = jnp.concatenate([jnp.sin(idx), jnp.cos(idx)], axis=-1)
    return tab, tab


def _block_diag_slabs(w):
    per = SLAB // RG_BS
    w = w.reshape(D_RNN // SLAB, per, RG_BS, RG_BS)
    eye = jnp.eye(per, dtype=w.dtype)
    return jnp.einsum("spij,pq->spiqj", w, eye).reshape(D_RNN // SLAB, SLAB, SLAB)


def kernel(x_prompt, x_sample, c, state_gla, state_rglru, c_ctx, mod_w, mod_b, norm_g, mlp_w1, mlp_b1, mlp_w2,
           mlp_b2, ev_w_in, ev_w_out, sgu_ln_g, sgu_ln_b, sgu_ws, sgu_bs, gla_gate_w2, gla_gate_b, gla_norm_g,
           rg_w_in, rg_conv_w, rg_conv_b, rg_wa, rg_ba, rg_wx, rg_bx, rg_L, rg_w_out):
    assert x_prompt.shape == (BATCH, SEQ, D) and x_sample.shape == (DEC_BATCH, DEC_SEQ, D)
    assert SEQ == TM and DEC_SEQ % TM == 0 and DEPTH == 2
    xp = x_prompt.reshape(NTP * TM, D)
    xs = x_sample.reshape(NTS * TM, D)
    cond8 = jnp.concatenate([c_ctx[None, :], c, jnp.zeros((NCOND - 1 - DEC_BATCH, D), F32)], axis=0)
    mods = _modulation(cond8, mod_w, mod_b)
    rtab, ctab = _pos_tables()

    gmat = jnp.zeros((LANES, 2 * QK), F32)
    gmat = gmat.at[0:GLA_RANK, 0:QK].set(gla_gate_w2[0, 0])
    gmat = gmat.at[GLA_RANK:2 * GLA_RANK, QK:2 * QK].set(gla_gate_w2[0, 1])
    s0_gla = jnp.concatenate([jnp.zeros((BATCH,) + state_gla.shape[2:], F32), state_gla[:, 0]], axis=0)
    ev = {
        "win": ev_w_in[0, :, 0:EV_MAIN].astype(BF16),
        "wlr": jnp.pad(ev_w_in[0, :, EV_MAIN:], ((0, 0), (0, LANES - 2 * GLA_RANK))).astype(BF16),
        "gmat": gmat.astype(BF16),
        "gb": gla_gate_b[0].reshape(1, 2 * QK),
        "lng": sgu_ln_g[0].reshape(1, SGU_WIDTH),
        "lnb": sgu_ln_b[0].reshape(1, SGU_WIDTH),
        "ws": sgu_ws[0].astype(BF16),
        "bst": sgu_bs[0].T,
        "gn": gla_norm_g[0].reshape(1, VW),
        "wout": ev_w_out[0].astype(BF16),
        "p0": s0_gla,
    }
    mid, pf = _even_forward(xp, xs, rtab, ctab, mods[0], norm_g[0], ev)
    x1, pb = _even_reverse(xp, xs, rtab, ctab, mods[0], norm_g[0], mid, ev)
    x2 = _mlp(x1, mods[0], norm_g[0], mlp_w1[0].astype(BF16), mlp_b1[0].reshape(1, D_FF),
              mlp_w2[0].astype(BF16), mlp_b2[0].reshape(1, D), 0, NT)
    new_gla = jnp.stack([pf[:BATCH, 0], pb[:BATCH, 0]], axis=1)[:, None]

    s0_rg = jnp.concatenate([jnp.zeros((BATCH, 2, D_RNN), F32), state_rglru[:, 0]], axis=0)
    od = {
        "win": rg_w_in[0].astype(BF16),
        "cw": rg_conv_w[0],
        "cb": rg_conv_b[0].reshape(1, D_RNN),
        "wg": jnp.stack([
            jnp.concatenate([_block_diag_slabs(rg_wa[0, d]), _block_diag_slabs(rg_wx[0, d])], axis=-1)
            for d in range(2)], axis=0).astype(BF16),
        "ba": rg_ba[0], "bx": rg_bx[0], "lam": rg_L[0],
        "wout": rg_w_out[0].astype(BF16),
        "s0": s0_rg.reshape(NSEQ, 2, 1, D_RNN),
    }
    mid1, sf = _odd_forward(x2, mods[1], norm_g[1], od)
    x3, sb = _odd_reverse(x2, mods[1], norm_g[1], mid1, od)
    mlp1 = (mods[1], norm_g[1], mlp_w1[1].astype(BF16), mlp_b1[1].reshape(1, D_FF),
            mlp_w2[1].astype(BF16), mlp_b2[1].reshape(1, D))
    y_prompt = _mlp(x3, *mlp1, 0, NTP)
    y_sample = _mlp(x3, *mlp1, NTP, NTS)
    new_rg = jnp.stack([sf[:BATCH, 0, 0], sb[:BATCH, 0, 0]], axis=1)[:, None]
    return (y_prompt.reshape(BATCH, SEQ, D), y_sample.reshape(DEC_BATCH, DEC_SEQ, D), new_gla, new_rg)
```

```python
import functools

import jax
import jax.numpy as jnp
from jax import lax
from jax.experimental import pallas as pl
from jax.experimental.pallas import tpu as pltpu

D = 1024
BATCH = 16
SEQ = 256
DEPTH = 2
DEC_BATCH = 4
DEC_SEQ = 4096
GRID_W = 64
D_FF = 4 * D
EPS = 1e-6
SGU_CHUNK = 128
SGU_GROUPS = 4
SGU_WIDTH = D // 2
GLA_HEADS = 4
GLA_DV = 128
GLA_DK = 64
GLA_RANK = 16
GLA_NORMALIZER = 16.0
GLA_CHUNK = 64
QK = GLA_HEADS * GLA_DK
VW = GLA_HEADS * GLA_DV
EV_MAIN = 2 * SGU_WIDTH + 2 * QK + 2 * VW
D_RNN = D
RG_BLOCKS = 16
RG_BS = D_RNN // RG_BLOCKS
RG_C = 8.0
LANES = 128
SUBLANES = 8
SLAB = 256

TM = 256
NTP = BATCH * SEQ // TM
TPS = DEC_SEQ // TM
NTS = DEC_BATCH * TPS
NT = NTP + NTS
NSEQ = BATCH + DEC_BATCH
NCOND = 8
HALO = 16
EV_MID = 2 * VW + VW + 2 * QK + VW + QK
OD_MID = 3 * D_RNN
VMEM_LIMIT = 56 * 1024 * 1024

F32 = jnp.float32
BF16 = jnp.bfloat16


def _tile_info(t):
    is_p = t < NTP
    ts = jnp.maximum(t - NTP, 0)
    sq = ts // TPS
    within = ts % TPS
    cidx = jnp.where(is_p, 0, 1 + sq)
    first = jnp.logical_or(is_p, within == 0)
    last = jnp.logical_or(is_p, within == TPS - 1)
    seq = jnp.where(is_p, t, BATCH + sq)
    return is_p, within, cidx, first, last, seq


def _seq_of(t):
    return jnp.where(t < NTP, t, BATCH + jnp.maximum(t - NTP, 0) // TPS)


def _rms(x, g):
    return x * lax.rsqrt(jnp.mean(x * x, axis=-1, keepdims=True) + EPS) * g


def _dot(a, b):
    return jnp.dot(a, b, preferred_element_type=F32)


def _dot_nt(a, b):
    return lax.dot_general(a, b, (((1,), (1,)), ((), ())), preferred_element_type=F32)


def _dot_tn(a, b):
    return lax.dot_general(a, b, (((0,), (0,)), ((), ())), preferred_element_type=F32)


def _split3(x):
    hi = x.astype(BF16)
    r1 = x - hi.astype(F32)
    mid = r1.astype(BF16)
    lo = (r1 - mid.astype(F32)).astype(BF16)
    return hi, mid, lo


def _dot_exact_lhs(m, parts):
    return _dot(m, parts[0]) + _dot(m, parts[1]) + _dot(m, parts[2])


def _log_sigmoid(x):
    return jnp.minimum(x, 0.0) - jnp.log1p(jnp.exp(-jnp.abs(x)))


def _softplus(x):
    return jnp.maximum(x, 0.0) + jnp.log1p(jnp.exp(-jnp.abs(x)))


def _mod_rows(mod_ref, cidx):
    m = mod_ref[pl.ds(cidx, 1), :]
    return [m[:, j * D:(j + 1) * D] for j in range(6)]


def _load_x0(is_p, within, xp_ref, xs_ref, rtab_ref, ctab_ref):
    rows_per_tile = TM // GRID_W
    r0 = within * rows_per_tile
    posr = jnp.concatenate(
        [jnp.broadcast_to(rtab_ref[pl.ds(r0 + j, 1), :], (GRID_W, D // 2)) for j in range(rows_per_tile)],
        axis=0)
    posc = jnp.concatenate([ctab_ref[...]] * rows_per_tile, axis=0)
    pos = jnp.concatenate([posr, posc], axis=1)
    return jnp.where(is_p, xp_ref[...], xs_ref[...] + pos)


def _mod_kernel(cond_ref, w_ref, b_ref, o_ref):
    c = cond_ref[...]
    sc = (c * jax.nn.sigmoid(c)).astype(BF16)
    o_ref[0] = _dot(sc, w_ref[0].astype(BF16)) + b_ref[0]


def _modulation(cond8, mod_w, mod_b):
    nb = 6 * D // D
    return pl.pallas_call(
        _mod_kernel,
        out_shape=jax.ShapeDtypeStruct((DEPTH, NCOND, 6 * D), F32),
        grid=(DEPTH, nb),
        in_specs=[
            pl.BlockSpec((NCOND, D), lambda l, j: (0, 0)),
            pl.BlockSpec((1, D, D), lambda l, j: (l, 0, j)),
            pl.BlockSpec((1, 1, D), lambda l, j: (l, 0, j)),
        ],
        out_specs=pl.BlockSpec((1, NCOND, D), lambda l, j: (l, 0, j)),
        compiler_params=pltpu.CompilerParams(
            dimension_semantics=("arbitrary", "arbitrary"), vmem_limit_bytes=VMEM_LIMIT),
        name="modulation",
    )(cond8, mod_w, mod_b.reshape(DEPTH, 1, 6 * D))


def _gla_tile(qs, k, v_bf, la, p_scr, reverse):
    n = TM // GLA_CHUNK
    ri = lax.broadcasted_iota(jnp.int32, (TM, TM), 0)
    ci = lax.broadcasted_iota(jnp.int32, (TM, TM), 1)
    same = (ri // GLA_CHUNK) == (ci // GLA_CHUNK)
    order = (ci >= ri) if reverse else (ci <= ri)
    cum_m = jnp.where(jnp.logical_and(same, order), 1.0, 0.0).astype(BF16)
    b = _dot_exact_lhs(cum_m, _split3(la))
    ends = [c * GLA_CHUNK if reverse else (c + 1) * GLA_CHUNK - 1 for c in range(n)]
    btot = jnp.concatenate([jnp.broadcast_to(b[e:e + 1, :], (GLA_CHUNK, QK)) for e in ends], axis=0)
    qe = (qs * jnp.exp(b)).astype(BF16)
    ke = k * jnp.exp(-b)
    kd = (k * jnp.exp(btot - b)).astype(BF16)
    dec = jnp.exp(btot)
    lane = lax.broadcasted_iota(jnp.int32, (TM, QK), 1) % LANES
    ke_h = (jnp.where(lane < GLA_DK, ke, 0.0).astype(BF16), jnp.where(lane >= GLA_DK, ke, 0.0).astype(BF16))
    lane_v = lax.broadcasted_iota(jnp.int32, (TM, 2 * GLA_DV), 1)
    cr = lax.broadcasted_iota(jnp.int32, (GLA_CHUNK, 2 * GLA_CHUNK), 0)
    cc = lax.broadcasted_iota(jnp.int32, (GLA_CHUNK, 2 * GLA_CHUNK), 1) % GLA_CHUNK
    cmask = (cc >= cr) if reverse else (cc <= cr)
    br = lax.broadcasted_iota(jnp.int32, (2 * GLA_DK, 2 * GLA_DV), 0)
    bc = lax.broadcasted_iota(jnp.int32, (2 * GLA_DK, 2 * GLA_DV), 1)
    diag = (br < GLA_DK) == (bc < GLA_DV)
    half = TM // 2
    rows = [[None] * (GLA_HEADS // 2) for _ in range(n)]
    for p in range(GLA_HEADS // 2):
        ls = slice(p * LANES, (p + 1) * LANES)
        vp = v_bf[:, p * 2 * GLA_DV:(p + 1) * 2 * GLA_DV]
        vl = jnp.where(lane_v < GLA_DV, vp, jnp.zeros_like(vp))
        vr = jnp.where(lane_v >= GLA_DV, vp, jnp.zeros_like(vp))
        dec_t = (dec[0:half, ls].T, dec[half:TM, ls].T)
        s = p_scr[p]
        for c in (reversed(range(n)) if reverse else range(n)):
            rs = slice(c * GLA_CHUNK, (c + 1) * GLA_CHUNK)
            qc = qe[rs, ls]
            kk = jnp.concatenate([ke_h[0][rs, ls], ke_h[1][rs, ls]], axis=0)
            att = jnp.where(cmask, _dot_nt(qc, kk), 0.0).astype(BF16)
            rhs = jnp.concatenate([vl[rs, :], vr[rs, :], s.astype(BF16)], axis=0)
            rows[c][p] = _dot(jnp.concatenate([att, qc], axis=1), rhs)
            col = (c % 2) * GLA_CHUNK
            dcol = dec_t[(c * GLA_CHUNK) // half][:, col:col + 1]
            s = dcol * s + jnp.where(diag, _dot_tn(kd[rs, ls], vp[rs, :]), 0.0)
        p_scr[p] = s
    return jnp.concatenate([jnp.concatenate(r, axis=1) for r in rows], axis=0)


def _gla_state_load(s_ref, p_scr):
    p_scr[...] = jnp.zeros(p_scr.shape, F32)
    for h in range(GLA_HEADS):
        r0, c0 = (h % 2) * GLA_DK, (h % 2) * GLA_DV
        p_scr[h // 2, r0:r0 + GLA_DK, c0:c0 + GLA_DV] = s_ref[0, 0, h]


def _gla_state_store(p_scr, s_ref):
    for h in range(GLA_HEADS):
        r0, c0 = (h % 2) * GLA_DK, (h % 2) * GLA_DV
        s_ref[0, 0, h] = p_scr[h // 2, r0:r0 + GLA_DK, c0:c0 + GLA_DV]


def _ev1_kernel(xp_ref, xs_ref, rtab_ref, ctab_ref, mod_ref, ng_ref, win_ref, wlr_ref, gmat_ref, gb_ref,
                lng_ref, lnb_ref, ws_ref, bst_ref, p0_ref, mid_ref, pfin_ref, p_scr):
    t = pl.program_id(0)
    is_p, within, cidx, first, last, _ = _tile_info(t)

    @pl.when(first)
    def _():
        _gla_state_load(p0_ref, p_scr)

    x = _load_x0(is_p, within, xp_ref, xs_ref, rtab_ref, ctab_ref)
    sh1, sc1, _, _, _, _ = _mod_rows(mod_ref, cidx)
    hm = (_rms(x, ng_ref[0:1, :]) * (1.0 + sc1) + sh1).astype(BF16)
    proj = _dot(hm, win_ref[...])
    lr = _dot(hm, wlr_ref[...]).astype(BF16)
    la_all = _log_sigmoid(_dot(lr, gmat_ref[...]) + gb_ref[...]) * (1.0 / GLA_NORMALIZER)

    u = jax.nn.gelu(proj[:, 0:SGU_WIDTH])
    vg = jax.nn.gelu(proj[:, SGU_WIDTH:2 * SGU_WIDTH])
    mu = jnp.mean(vg, axis=-1, keepdims=True)
    vc = vg - mu
    vn = (vc * lax.rsqrt(jnp.mean(vc * vc, axis=-1, keepdims=True) + EPS) * lng_ref[...] + lnb_ref[...]).astype(BF16)
    gd = SGU_WIDTH // SGU_GROUPS
    nch = TM // SGU_CHUNK
    sv_cols = []
    for g in range(SGU_GROUPS):
        vcat = jnp.concatenate(
            [vn[c * SGU_CHUNK:(c + 1) * SGU_CHUNK, g * gd:(g + 1) * gd] for c in range(nch)], axis=1)
        sg = _dot(ws_ref[g], vcat) + bst_ref[:, g:g + 1]
        sv_cols.append(jnp.concatenate([sg[:, c * gd:(c + 1) * gd] for c in range(nch)], axis=0))
    out_a = u * jnp.concatenate(sv_cols, axis=1)

    o0 = 2 * SGU_WIDTH
    qs = proj[:, o0:o0 + QK] * (GLA_DK ** -0.5)
    k = proj[:, o0 + QK:o0 + 2 * QK]
    v = proj[:, o0 + 2 * QK:o0 + 2 * QK + VW]
    g = proj[:, o0 + 2 * QK + VW:o0 + 2 * QK + 2 * VW]

    o_f = _gla_tile(qs, k, v.astype(BF16), la_all[:, 0:QK], p_scr, reverse=False)
    mid_ref[:, 0:VW] = out_a
    mid_ref[:, VW:2 * VW] = o_f
    mid_ref[:, 2 * VW:3 * VW] = g * jax.nn.sigmoid(g)
    c0 = 3 * VW
    mid_ref[:, c0:c0 + QK] = qs
    mid_ref[:, c0 + QK:c0 + 2 * QK] = k
    mid_ref[:, c0 + 2 * QK:c0 + 2 * QK + VW] = v
    mid_ref[:, c0 + 2 * QK + VW:c0 + 3 * QK + VW] = la_all[:, QK:2 * QK]

    @pl.when(last)
    def _():
        _gla_state_store(p_scr, pfin_ref)


def _const_spec(shape):
    nd = len(shape)
    return pl.BlockSpec(shape, lambda i, _nd=nd: (0,) * _nd, pipeline_mode=pl.Buffered(1))


def _xp_spec(rev):
    if rev:
        return pl.BlockSpec((TM, D), lambda i: (jnp.minimum(NT - 1 - i, NTP - 1), 0))
    return pl.BlockSpec((TM, D), lambda i: (jnp.minimum(i, NTP - 1), 0))


def _xs_spec(rev):
    if rev:
        return pl.BlockSpec((TM, D), lambda i: (jnp.maximum(NT - 1 - i - NTP, 0), 0))
    return pl.BlockSpec((TM, D), lambda i: (jnp.maximum(i - NTP, 0), 0))


def _tile_spec(width, rev=False):
    if rev:
        return pl.BlockSpec((TM, width), lambda i: (NT - 1 - i, 0))
    return pl.BlockSpec((TM, width), lambda i: (i, 0))


def _params():
    return pltpu.CompilerParams(dimension_semantics=("arbitrary",), vmem_limit_bytes=VMEM_LIMIT)


def _even_forward(xp, xs, rtab, ctab, mod, ng, w):
    state_blk = (1, 1, GLA_HEADS, GLA_DK, GLA_DV)
    return pl.pallas_call(
        _ev1_kernel,
        out_shape=(jax.ShapeDtypeStruct((NT * TM, EV_MID), F32),
                   jax.ShapeDtypeStruct((NSEQ, 1) + state_blk[2:], F32)),
        grid=(NT,),
        in_specs=[
            _xp_spec(False), _xs_spec(False),
            _const_spec((GRID_W, D // 2)), _const_spec((GRID_W, D // 2)),
            _const_spec((NCOND, 6 * D)), _const_spec((4, D)),
            _const_spec((D, EV_MAIN)), _const_spec((D, LANES)), _const_spec((LANES, 2 * QK)),
            _const_spec((1, 2 * QK)),
            _const_spec((1, SGU_WIDTH)), _const_spec((1, SGU_WIDTH)),
            _const_spec((SGU_GROUPS, SGU_CHUNK, SGU_CHUNK)), _const_spec((SGU_CHUNK, SGU_GROUPS)),
            pl.BlockSpec(state_blk, lambda i: (_seq_of(i), 0, 0, 0, 0)),
        ],
        out_specs=(_tile_spec(EV_MID),
                   pl.BlockSpec(state_blk, lambda i: (_seq_of(i), 0, 0, 0, 0))),
        scratch_shapes=[pltpu.VMEM((GLA_HEADS // 2, 2 * GLA_DK, 2 * GLA_DV), F32)],
        compiler_params=_params(),
        name="even_forward",
    )(xp, xs, rtab, ctab, mod, ng, w["win"], w["wlr"], w["gmat"], w["gb"], w["lng"], w["lnb"],
      w["ws"], w["bst"], w["p0"])


def _ev2_kernel(xp_ref, xs_ref, rtab_ref, ctab_ref, mod_ref, ng_ref, mid_ref, gn_ref, wout_ref, p0_ref,
                x1_ref, pfin_ref, p_scr):
    t = NT - 1 - pl.program_id(0)
    is_p, within, cidx, first, last, _ = _tile_info(t)

    @pl.when(last)
    def _():
        _gla_state_load(p0_ref, p_scr)

    x = _load_x0(is_p, within, xp_ref, xs_ref, rtab_ref, ctab_ref)
    _, _, g1, _, _, _ = _mod_rows(mod_ref, cidx)
    out_a = mid_ref[:, 0:VW]
    o_f = mid_ref[:, VW:2 * VW]
    sg = mid_ref[:, 2 * VW:3 * VW]
    c0 = 3 * VW
    qs = mid_ref[:, c0:c0 + QK]
    k = mid_ref[:, c0 + QK:c0 + 2 * QK]
    v = mid_ref[:, c0 + 2 * QK:c0 + 2 * QK + VW]
    la_b = mid_ref[:, c0 + 2 * QK + VW:c0 + 3 * QK + VW]

    o = o_f + _gla_tile(qs, k, v.astype(BF16), la_b, p_scr, reverse=True)
    heads = []
    for h in range(GLA_HEADS):
        oh = o[:, h * GLA_DV:(h + 1) * GLA_DV]
        heads.append(oh * lax.rsqrt(jnp.mean(oh * oh, axis=-1, keepdims=True) + EPS))
    on = jnp.concatenate(heads, axis=1) * gn_ref[...] * sg
    cat = jnp.concatenate([out_a, on], axis=1).astype(BF16)
    y = _dot(cat, wout_ref[...])
    x1_ref[...] = x + g1 * _rms(y, ng_ref[1:2, :])

    @pl.when(first)
    def _():
        _gla_state_store(p_scr, pfin_ref)


def _even_reverse(xp, xs, rtab, ctab, mod, ng, mid, w):
    state_blk = (1, 1, GLA_HEADS, GLA_DK, GLA_DV)
    return pl.pallas_call(
        _ev2_kernel,
        out_shape=(jax.ShapeDtypeStruct((NT * TM, D), F32),
                   jax.ShapeDtypeStruct((NSEQ, 1) + state_blk[2:], F32)),
        grid=(NT,),
        in_specs=[
            _xp_spec(True), _xs_spec(True),
            _const_spec((GRID_W, D // 2)), _const_spec((GRID_W, D // 2)),
            _const_spec((NCOND, 6 * D)), _const_spec((4, D)),
            _tile_spec(EV_MID, rev=True),
            _const_spec((1, VW)), _const_spec((2 * VW, D)),
            pl.BlockSpec(state_blk, lambda i: (_seq_of(NT - 1 - i), 1, 0, 0, 0)),
        ],
        out_specs=(_tile_spec(D, rev=True),
                   pl.BlockSpec(state_blk, lambda i: (_seq_of(NT - 1 - i), 0, 0, 0, 0))),
        scratch_shapes=[pltpu.VMEM((GLA_HEADS // 2, 2 * GLA_DK, 2 * GLA_DV), F32)],
        compiler_params=_params(),
        name="even_reverse",
    )(xp, xs, rtab, ctab, mod, ng, mid, w["gn"], w["wout"], w["p0"])


FF_CHUNK = 1024
MLP_TILES = 2
TMM = MLP_TILES * TM


def _mlp_kernel(x_ref, mod_ref, ng_ref, w1_ref, b1_ref, w2_ref, b2_ref, o_ref, *, tile0):
    _, _, cidx, _, _, _ = _tile_info(tile0 + pl.program_id(0) * MLP_TILES)
    _, _, _, sh2, sc2, g2 = _mod_rows(mod_ref, cidx)
    nch = D_FF // FF_CHUNK

    def rows(t):
        return slice(t * TM, (t + 1) * TM)

    def pre(t):
        return (_rms(x_ref[rows(t), :], ng_ref[2:3, :]) * (1.0 + sc2) + sh2).astype(BF16)

    def post(t, acc):
        o_ref[rows(t), :] = x_ref[rows(t), :] + g2 * _rms(acc, ng_ref[3:4, :])

    hff = pre(0)
    prev = None
    for t in range(MLP_TILES):
        acc = b2_ref[...]
        nxt = None
        for j in range(nch):
            cs = slice(j * FF_CHUNK, (j + 1) * FF_CHUNK)
            h = _dot(hff, w1_ref[:, cs]) + b1_ref[:, cs]
            h = jnp.square(jnp.maximum(h, 0.0)).astype(BF16)
            acc = acc + _dot(h, w2_ref[cs, :])
            if j == 0 and t + 1 < MLP_TILES:
                nxt = pre(t + 1)
            if j == 1 and prev is not None:
                post(t - 1, prev)
        prev, hff = acc, nxt
    post(MLP_TILES - 1, prev)


def _mlp(x, mod, ng, w1, b1, w2, b2, tile0, ntiles):
    assert tile0 % MLP_TILES == 0 and ntiles % MLP_TILES == 0 and TPS % MLP_TILES == 0 and NTP % MLP_TILES == 0
    b0 = tile0 // MLP_TILES
    return pl.pallas_call(
        functools.partial(_mlp_kernel, tile0=tile0),
        out_shape=jax.ShapeDtypeStruct((ntiles * TM, D), F32),
        grid=(ntiles // MLP_TILES,),
        in_specs=[
            pl.BlockSpec((TMM, D), lambda i: (b0 + i, 0)),
            _const_spec((NCOND, 6 * D)), _const_spec((4, D)),
            _const_spec((D, D_FF)), _const_spec((1, D_FF)), _const_spec((D_FF, D)), _const_spec((1, D)),
        ],
        out_specs=pl.BlockSpec((TMM, D), lambda i: (i, 0)),
        compiler_params=_params(), name="mlp",
    )(x, mod, ng, w1, b1, w2, b2)


def _rg_gates(xc, wg_ref, ba_ref, bx_ref, lam_ref, a_scr, b_scr):
    xcb = xc.astype(BF16)
    sp = _softplus(-lam_ref[...])
    for s in range(D_RNN // SLAB):
        cs = slice(s * SLAB, (s + 1) * SLAB)
        pre = _dot(xcb[:, cs], wg_ref[s])
        r = jax.nn.sigmoid(pre[:, 0:SLAB] + ba_ref[:, cs])
        i = jax.nn.sigmoid(pre[:, SLAB:2 * SLAB] + bx_ref[:, cs])
        log_a = (-RG_C) * r * sp[:, cs]
        a = jnp.exp(log_a)
        a_scr[:, cs] = a
        b_scr[:, cs] = jnp.sqrt(-jnp.tanh(log_a) * (1.0 + a * a)) * (i * xc[:, cs])


SEG = TM // SUBLANES


def _seg_perm(transpose):
    rr = lax.broadcasted_iota(jnp.int32, (TM, TM), 0)
    cc = lax.broadcasted_iota(jnp.int32, (TM, TM), 1)
    if transpose:
        rr, cc = cc, rr
    hit = jnp.logical_and(cc // SEG == rr % SUBLANES, cc % SEG == rr // SUBLANES)
    return jnp.where(hit, 1.0, 0.0).astype(BF16)


def _rg_scan(a_scr, b_scr, h_scr, c_scr, h0, reverse):
    def body(i, carry):
        hh, cc = carry
        grp = (SEG - 1 - i) if reverse else i
        r0 = pl.multiple_of(grp * SUBLANES, SUBLANES)
        a = a_scr[pl.ds(r0, SUBLANES), :]
        hh = a * hh + b_scr[pl.ds(r0, SUBLANES), :]
        cc = a * cc
        h_scr[pl.ds(r0, SUBLANES), :] = hh
        c_scr[pl.ds(r0, SUBLANES), :] = cc
        return hh, cc

    init = (jnp.zeros((SUBLANES, D_RNN), F32), jnp.ones((SUBLANES, D_RNN), F32))
    b, a = lax.fori_loop(0, SEG, body, init, unroll=4)
    row = lax.broadcasted_iota(jnp.int32, (SUBLANES, D_RNN), 0)
    for s in (1, 2, 4):
        shift = (SUBLANES - s) if reverse else s
        valid = (row < SUBLANES - s) if reverse else (row >= s)
        a_s = pltpu.roll(a, shift, 0)
        b_s = pltpu.roll(b, shift, 0)
        b = jnp.where(valid, a * b_s + b, b)
        a = jnp.where(valid, a * a_s, a)
    after = a * h0 + b
    edge = (row == SUBLANES - 1) if reverse else (row == 0)
    enter = jnp.where(edge, h0, pltpu.roll(after, (SUBLANES - 1) if reverse else 1, 0))
    h = h_scr[...] + c_scr[...] * jnp.concatenate([enter] * SEG, axis=0)
    out = after[0:1, :] if reverse else after[SUBLANES - 1:SUBLANES, :]
    return h, out


def _od1_kernel(x_ref, xn_ref, mod_ref, ng_ref, win_ref, cw_ref, cb_ref, wg_ref, ba_ref, bx_ref, lam_ref,
                s0_ref, mid_ref, sfin_ref, xb_scr, a_scr, b_scr, h_scr, c_scr, hc_scr, tail_scr):
    t = pl.program_id(0)
    _, _, cidx, first, last, _ = _tile_info(t)

    @pl.when(first)
    def _():
        tail_scr[...] = jnp.zeros((2 * SUBLANES, D_RNN), F32)
        hc_scr[...] = s0_ref[0, 0]

    sh1, sc1, _, _, _, _ = _mod_rows(mod_ref, cidx)
    xe = jnp.concatenate([x_ref[...], xn_ref[...]], axis=0)
    hm = (_rms(xe, ng_ref[0:1, :]) * (1.0 + sc1) + sh1).astype(BF16)
    hmp = _dot(_seg_perm(False), hm[0:TM, :]).astype(BF16)
    proj = _dot(jnp.concatenate([hmp, hm[TM:TM + HALO, :]], axis=0), win_ref[...])

    xb = proj[0:TM, 0:D_RNN]
    row = lax.broadcasted_iota(jnp.int32, (SUBLANES, D_RNN), 0)
    g30 = xb[TM - 2 * SUBLANES:TM - SUBLANES, :]
    g31 = xb[TM - SUBLANES:TM, :]
    nxt = jnp.where(last, 0.0, proj[TM:TM + 1, 0:D_RNN])
    xb_scr[0:SUBLANES, :] = jnp.where(row == 0, pltpu.roll(tail_scr[0:SUBLANES, :], 1, 0), pltpu.roll(g30, 1, 0))
    xb_scr[SUBLANES:2 * SUBLANES, :] = jnp.where(
        row == 0, pltpu.roll(tail_scr[SUBLANES:2 * SUBLANES, :], 1, 0), pltpu.roll(g31, 1, 0))
    xb_scr[2 * SUBLANES:2 * SUBLANES + TM, :] = xb
    xb_scr[2 * SUBLANES + TM:3 * SUBLANES + TM, :] = jnp.where(
        row == SUBLANES - 1, nxt, pltpu.roll(xb[0:SUBLANES, :], SUBLANES - 1, 0))
    tail_scr[0:SUBLANES, :] = g30
    tail_scr[SUBLANES:2 * SUBLANES, :] = g31
    xc = cb_ref[...]
    for j in range(4):
        xc = xc + cw_ref[j:j + 1, :] * xb_scr[j * SUBLANES:j * SUBLANES + TM, :]

    _rg_gates(xc, wg_ref, ba_ref, bx_ref, lam_ref, a_scr, b_scr)
    h, hc_scr[...] = _rg_scan(a_scr, b_scr, h_scr, c_scr, hc_scr[...], reverse=False)

    mid_ref[:, 0:D_RNN] = h
    mid_ref[:, D_RNN:2 * D_RNN] = jax.nn.gelu(proj[0:TM, D_RNN:2 * D_RNN])
    mid_ref[:, 2 * D_RNN:3 * D_RNN] = xc

    @pl.when(last)
    def _():
        sfin_ref[0, 0] = hc_scr[...]


def _odd_forward(x, mod, ng, w):
    nb16 = TM // HALO
    return pl.pallas_call(
        _od1_kernel,
        out_shape=(jax.ShapeDtypeStruct((NT * TM, OD_MID), F32),
                   jax.ShapeDtypeStruct((NSEQ, 1, 1, D_RNN), F32)),
        grid=(NT,),
        in_specs=[
            _tile_spec(D),
            pl.BlockSpec((HALO, D), lambda i: (jnp.minimum(i + 1, NT - 1) * nb16, 0)),
            _const_spec((NCOND, 6 * D)), _const_spec((4, D)),
            _const_spec((D, 2 * D_RNN)), _const_spec((4, D_RNN)), _const_spec((1, D_RNN)),
            _const_spec((D_RNN // SLAB, SLAB, 2 * SLAB)),
            _const_spec((1, D_RNN)), _const_spec((1, D_RNN)), _const_spec((1, D_RNN)),
            pl.BlockSpec((1, 1, 1, D_RNN), lambda i: (_seq_of(i), 0, 0, 0)),
        ],
        out_specs=(_tile_spec(OD_MID),
                   pl.BlockSpec((1, 1, 1, D_RNN), lambda i: (_seq_of(i), 0, 0, 0))),
        scratch_shapes=[
            pltpu.VMEM((3 * SUBLANES + TM, D_RNN), F32),
            pltpu.VMEM((TM, D_RNN), F32), pltpu.VMEM((TM, D_RNN), F32), pltpu.VMEM((TM, D_RNN), F32),
            pltpu.VMEM((TM, D_RNN), F32),
            pltpu.VMEM((1, D_RNN), F32), pltpu.VMEM((2 * SUBLANES, D_RNN), F32),
        ],
        compiler_params=_params(),
        name="odd_forward",
    )(x, x, mod, ng, w["win"], w["cw"], w["cb"], w["wg"][0], w["ba"][0:1], w["bx"][0:1], w["lam"][0:1],
      w["s0"])


def _od2_kernel(x_ref, mod_ref, ng_ref, mid_ref, wg_ref, ba_ref, bx_ref, lam_ref, wout_ref, s0_ref,
                x1_ref, sfin_ref, a_scr, b_scr, h_scr, c_scr, hc_scr):
    t = NT - 1 - pl.program_id(0)
    _, _, cidx, first, last, _ = _tile_info(t)
    _, _, g1, _, _, _ = _mod_rows(mod_ref, cidx)

    @pl.when(last)
    def _():
        hc_scr[...] = s0_ref[0, 0]

    _rg_gates(mid_ref[:, 2 * D_RNN:3 * D_RNN], wg_ref, ba_ref, bx_ref, lam_ref, a_scr, b_scr)
    h_b, hc_scr[...] = _rg_scan(a_scr, b_scr, h_scr, c_scr, hc_scr[...], reverse=True)

    zp = ((mid_ref[:, 0:D_RNN] + h_b) * mid_ref[:, D_RNN:2 * D_RNN]).astype(BF16)
    z = _dot(_seg_perm(True), zp).astype(BF16)
    y = _dot(z, wout_ref[...])
    x1_ref[...] = x_ref[...] + g1 * _rms(y, ng_ref[1:2, :])

    @pl.when(first)
    def _():
        sfin_ref[0, 0] = hc_scr[...]


def _odd_reverse(x, mod, ng, mid, w):
    return pl.pallas_call(
        _od2_kernel,
        out_shape=(jax.ShapeDtypeStruct((NT * TM, D), F32),
                   jax.ShapeDtypeStruct((NSEQ, 1, 1, D_RNN), F32)),
        grid=(NT,),
        in_specs=[
            _tile_spec(D, rev=True),
            _const_spec((NCOND, 6 * D)), _const_spec((4, D)),
            _tile_spec(OD_MID, rev=True),
            _const_spec((D_RNN // SLAB, SLAB, 2 * SLAB)),
            _const_spec((1, D_RNN)), _const_spec((1, D_RNN)), _const_spec((1, D_RNN)),
            _const_spec((D_RNN, D)),
            pl.BlockSpec((1, 1, 1, D_RNN), lambda i: (_seq_of(NT - 1 - i), 1, 0, 0)),
        ],
        out_specs=(_tile_spec(D, rev=True),
                   pl.BlockSpec((1, 1, 1, D_RNN), lambda i: (_seq_of(NT - 1 - i), 0, 0, 0))),
        scratch_shapes=[
            pltpu.VMEM((TM, D_RNN), F32), pltpu.VMEM((TM, D_RNN), F32), pltpu.VMEM((TM, D_RNN), F32),
            pltpu.VMEM((TM, D_RNN), F32),
            pltpu.VMEM((1, D_RNN), F32),
        ],
        compiler_params=_params(),
        name="odd_reverse",
    )(x, mod, ng, mid, w["wg"][1], w["ba"][1:2], w["bx"][1:2], w["lam"][1:2], w["wout"], w["s0"])


def _pos_tables():
    n = D // 4
    omega = 1.0 / (10000.0 ** (jnp.arange(n, dtype=F32) / n))
    idx = jnp.arange(GRID_W, dtype=F32)[:, None] * omega
    tab = jnp.concatenate([jnp.sin(idx), jnp.cos(idx)], axis=-1)
    return tab, tab


def _block_diag_slabs(w):
    per = SLAB // RG_BS
    w = w.reshape(D_RNN // SLAB, per, RG_BS, RG_BS)
    eye = jnp.eye(per, dtype=w.dtype)
    return jnp.einsum("spij,pq->spiqj", w, eye).reshape(D_RNN // SLAB, SLAB, SLAB)


def kernel(x_prompt, x_sample, c, state_gla, state_rglru, c_ctx, mod_w, mod_b, norm_g, mlp_w1, mlp_b1, mlp_w2,
           mlp_b2, ev_w_in, ev_w_out, sgu_ln_g, sgu_ln_b, sgu_ws, sgu_bs, gla_gate_w2, gla_gate_b, gla_norm_g,
           rg_w_in, rg_conv_w, rg_conv_b, rg_wa, rg_ba, rg_wx, rg_bx, rg_L, rg_w_out):
    assert x_prompt.shape == (BATCH, SEQ, D) and x_sample.shape == (DEC_BATCH, DEC_SEQ, D)
    assert SEQ == TM and DEC_SEQ % TM == 0 and DEPTH == 2
    xp = x_prompt.reshape(NTP * TM, D)
    xs = x_sample.reshape(NTS * TM, D)
    cond8 = jnp.concatenate([c_ctx[None, :], c, jnp.zeros((NCOND - 1 - DEC_BATCH, D), F32)], axis=0)
    mods = _modulation(cond8, mod_w, mod_b)
    rtab, ctab = _pos_tables()

    gmat = jnp.zeros((LANES, 2 * QK), F32)
    gmat = gmat.at[0:GLA_RANK, 0:QK].set(gla_gate_w2[0, 0])
    gmat = gmat.at[GLA_RANK:2 * GLA_RANK, QK:2 * QK].set(gla_gate_w2[0, 1])
    s0_gla = jnp.concatenate([jnp.zeros((BATCH,) + state_gla.shape[2:], F32), state_gla[:, 0]], axis=0)
    ev = {
        "win": ev_w_in[0, :, 0:EV_MAIN].astype(BF16),
        "wlr": jnp.pad(ev_w_in[0, :, EV_MAIN:], ((0, 0), (0, LANES - 2 * GLA_RANK))).astype(BF16),
        "gmat": gmat.astype(BF16),
        "gb": gla_gate_b[0].reshape(1, 2 * QK),
        "lng": sgu_ln_g[0].reshape(1, SGU_WIDTH),
        "lnb": sgu_ln_b[0].reshape(1, SGU_WIDTH),
        "ws": sgu_ws[0].astype(BF16),
        "bst": sgu_bs[0].T,
        "gn": gla_norm_g[0].reshape(1, VW),
        "wout": ev_w_out[0].astype(BF16),
        "p0": s0_gla,
    }
    mid, pf = _even_forward(xp, xs, rtab, ctab, mods[0], norm_g[0], ev)
    x1, pb = _even_reverse(xp, xs, rtab, ctab, mods[0], norm_g[0], mid, ev)
    x2 = _mlp(x1, mods[0], norm_g[0], mlp_w1[0].astype(BF16), mlp_b1[0].reshape(1, D_FF),
              mlp_w2[0].astype(BF16), mlp_b2[0].reshape(1, D), 0, NT)
    new_gla = jnp.stack([pf[:BATCH, 0], pb[:BATCH, 0]], axis=1)[:, None]

    s0_rg = jnp.concatenate([jnp.zeros((BATCH, 2, D_RNN), F32), state_rglru[:, 0]], axis=0)
    od = {
        "win": rg_w_in[0].astype(BF16),
        "cw": rg_conv_w[0],
        "cb": rg_conv_b[0].reshape(1, D_RNN),
        "wg": jnp.stack([
            jnp.concatenate([_block_diag_slabs(rg_wa[0, d]), _block_diag_slabs(rg_wx[0, d])], axis=-1)
            for d in range(2)], axis=0).astype(BF16),
        "ba": rg_ba[0], "bx": rg_bx[0], "lam": rg_L[0],
        "wout": rg_w_out[0].astype(BF16),
        "s0": s0_rg.reshape(NSEQ, 2, 1, D_RNN),
    }
    mid1, sf = _odd_forward(x2, mods[1], norm_g[1], od)
    x3, sb = _odd_reverse(x2, mods[1], norm_g[1], mid1, od)
    mlp1 = (mods[1], norm_g[1], mlp_w1[1].astype(BF16), mlp_b1[1].reshape(1, D_FF),
            mlp_w2[1].astype(BF16), mlp_b2[1].reshape(1, D))
    y_prompt = _mlp(x3, *mlp1, 0, NTP)
    y_sample = _mlp(x3, *mlp1, NTP, NTS)
    new_rg = jnp.stack([sf[:BATCH, 0, 0], sb[:BATCH, 0, 0]], axis=1)[:, None]
    return (y_prompt.reshape(BATCH, SEQ, D), y_sample.reshape(DEC_BATCH, DEC_SEQ, D), new_gla, new_rg)
```

```python
import functools

import jax
import jax.numpy as jnp
from jax import lax
from jax.experimental import pallas as pl
from jax.experimental.pallas import tpu as pltpu

D = 1024
BATCH = 16
SEQ = 256
DEPTH = 2
DEC_BATCH = 4
DEC_SEQ = 4096
GRID_W = 64
D_FF = 4 * D
EPS = 1e-6
SGU_CHUNK = 128
SGU_GROUPS = 4
SGU_WIDTH = D // 2
GLA_HEADS = 4
GLA_DV = 128
GLA_DK = 64
GLA_RANK = 16
GLA_NORMALIZER = 16.0
GLA_CHUNK = 64
QK = GLA_HEADS * GLA_DK
VW = GLA_HEADS * GLA_DV
EV_MAIN = 2 * SGU_WIDTH + 2 * QK + 2 * VW
D_RNN = D
RG_BLOCKS = 16
RG_BS = D_RNN // RG_BLOCKS
RG_C = 8.0
LANES = 128
SUBLANES = 8
SLAB = 256

TM = 256
NTP = BATCH * SEQ // TM
TPS = DEC_SEQ // TM
NTS = DEC_BATCH * TPS
NT = NTP + NTS
NSEQ = BATCH + DEC_BATCH
NCOND = 8
HALO = 16
EV_MID = 2 * VW + VW + 2 * QK + VW + QK
OD_MID = 3 * D_RNN
VMEM_LIMIT = 56 * 1024 * 1024

F32 = jnp.float32
BF16 = jnp.bfloat16


def _tile_info(t):
    is_p = t < NTP
    ts = jnp.maximum(t - NTP, 0)
    sq = ts // TPS
    within = ts % TPS
    cidx = jnp.where(is_p, 0, 1 + sq)
    first = jnp.logical_or(is_p, within == 0)
    last = jnp.logical_or(is_p, within == TPS - 1)
    seq = jnp.where(is_p, t, BATCH + sq)
    return is_p, within, cidx, first, last, seq


def _seq_of(t):
    return jnp.where(t < NTP, t, BATCH + jnp.maximum(t - NTP, 0) // TPS)


def _rms(x, g):
    return x * lax.rsqrt(jnp.mean(x * x, axis=-1, keepdims=True) + EPS) * g


def _dot(a, b):
    return jnp.dot(a, b, preferred_element_type=F32)


def _dot_nt(a, b):
    return lax.dot_general(a, b, (((1,), (1,)), ((), ())), preferred_element_type=F32)


def _dot_tn(a, b):
    return lax.dot_general(a, b, (((0,), (0,)), ((), ())), preferred_element_type=F32)


def _split3(x):
    hi = x.astype(BF16)
    r1 = x - hi.astype(F32)
    mid = r1.astype(BF16)
    lo = (r1 - mid.astype(F32)).astype(BF16)
    return hi, mid, lo


def _dot_exact_lhs(m, parts):
    return _dot(m, parts[0]) + _dot(m, parts[1]) + _dot(m, parts[2])


def _log_sigmoid(x):
    return jnp.minimum(x, 0.0) - jnp.log(1.0 + jnp.exp(-jnp.abs(x)))


def _softplus(x):
    return jnp.maximum(x, 0.0) + jnp.log(1.0 + jnp.exp(-jnp.abs(x)))


def _mod_rows(mod_ref, cidx):
    m = mod_ref[pl.ds(cidx, 1), :]
    return [m[:, j * D:(j + 1) * D] for j in range(6)]


def _load_x0(is_p, within, xp_ref, xs_ref, rtab_ref, ctab_ref):
    rows_per_tile = TM // GRID_W
    r0 = within * rows_per_tile
    posr = jnp.concatenate(
        [jnp.broadcast_to(rtab_ref[pl.ds(r0 + j, 1), :], (GRID_W, D // 2)) for j in range(rows_per_tile)],
        axis=0)
    posc = jnp.concatenate([ctab_ref[...]] * rows_per_tile, axis=0)
    pos = jnp.concatenate([posr, posc], axis=1)
    return jnp.where(is_p, xp_ref[...], xs_ref[...] + pos)


def _mod_kernel(cond_ref, w_ref, b_ref, o_ref):
    c = cond_ref[...]
    sc = (c * jax.nn.sigmoid(c)).astype(BF16)
    o_ref[0] = _dot(sc, w_ref[0].astype(BF16)) + b_ref[0]


def _modulation(cond8, mod_w, mod_b):
    nb = 6 * D // D
    return pl.pallas_call(
        _mod_kernel,
        out_shape=jax.ShapeDtypeStruct((DEPTH, NCOND, 6 * D), F32),
        grid=(DEPTH, nb),
        in_specs=[
            pl.BlockSpec((NCOND, D), lambda l, j: (0, 0)),
            pl.BlockSpec((1, D, D), lambda l, j: (l, 0, j)),
            pl.BlockSpec((1, 1, D), lambda l, j: (l, 0, j)),
        ],
        out_specs=pl.BlockSpec((1, NCOND, D), lambda l, j: (l, 0, j)),
        compiler_params=pltpu.CompilerParams(
            dimension_semantics=("arbitrary", "arbitrary"), vmem_limit_bytes=VMEM_LIMIT),
        name="modulation",
    )(cond8, mod_w, mod_b.reshape(DEPTH, 1, 6 * D))


def _gla_tile(qs, k, v_bf, la, p_scr, reverse):
    n = TM // GLA_CHUNK
    ri = lax.broadcasted_iota(jnp.int32, (TM, TM), 0)
    ci = lax.broadcasted_iota(jnp.int32, (TM, TM), 1)
    same = (ri // GLA_CHUNK) == (ci // GLA_CHUNK)
    order = (ci >= ri) if reverse else (ci <= ri)
    cum_m = jnp.where(jnp.logical_and(same, order), 1.0, 0.0).astype(BF16)
    b = _dot_exact_lhs(cum_m, _split3(la))
    ends = [c * GLA_CHUNK if reverse else (c + 1) * GLA_CHUNK - 1 for c in range(n)]
    btot = jnp.concatenate([jnp.broadcast_to(b[e:e + 1, :], (GLA_CHUNK, QK)) for e in ends], axis=0)
    qe = (qs * jnp.exp(b)).astype(BF16)
    ke = k * jnp.exp(-b)
    kd = (k * jnp.exp(btot - b)).astype(BF16)
    dec = jnp.exp(btot)
    lane = lax.broadcasted_iota(jnp.int32, (TM, QK), 1) % LANES
    ke_h = (jnp.where(lane < GLA_DK, ke, 0.0).astype(BF16), jnp.where(lane >= GLA_DK, ke, 0.0).astype(BF16))
    lane_v = lax.broadcasted_iota(jnp.int32, (TM, 2 * GLA_DV), 1)
    cr = lax.broadcasted_iota(jnp.int32, (GLA_CHUNK, 2 * GLA_CHUNK), 0)
    cc = lax.broadcasted_iota(jnp.int32, (GLA_CHUNK, 2 * GLA_CHUNK), 1) % GLA_CHUNK
    cmask = (cc >= cr) if reverse else (cc <= cr)
    br = lax.broadcasted_iota(jnp.int32, (2 * GLA_DK, 2 * GLA_DV), 0)
    bc = lax.broadcasted_iota(jnp.int32, (2 * GLA_DK, 2 * GLA_DV), 1)
    diag = (br < GLA_DK) == (bc < GLA_DV)
    half = TM // 2
    pairs = range(GLA_HEADS // 2)
    order_c = list(reversed(range(n))) if reverse else list(range(n))

    def rs(c):
        return slice(c * GLA_CHUNK, (c + 1) * GLA_CHUNK)

    def ls(p):
        return slice(p * LANES, (p + 1) * LANES)

    vps = [v_bf[:, p * 2 * GLA_DV:(p + 1) * 2 * GLA_DV] for p in pairs]
    att = [[None] * n for _ in pairs]
    ds = [[None] * n for _ in pairs]
    for p in pairs:
        for c in order_c:
            kk = jnp.concatenate([ke_h[0][rs(c), ls(p)], ke_h[1][rs(c), ls(p)]], axis=0)
            att[p][c] = jnp.where(cmask, _dot_nt(qe[rs(c), ls(p)], kk), 0.0).astype(BF16)
            ds[p][c] = jnp.where(diag, _dot_tn(kd[rs(c), ls(p)], vps[p][rs(c), :]), 0.0)
    rows = [[None] * len(pairs) for _ in range(n)]
    for p in pairs:
        vl = jnp.where(lane_v < GLA_DV, vps[p], jnp.zeros_like(vps[p]))
        vr = jnp.where(lane_v >= GLA_DV, vps[p], jnp.zeros_like(vps[p]))
        dec_t = (dec[0:half, ls(p)].T, dec[half:TM, ls(p)].T)
        s = p_scr[p]
        for c in order_c:
            rhs = jnp.concatenate([vl[rs(c), :], vr[rs(c), :], s.astype(BF16)], axis=0)
            rows[c][p] = _dot(jnp.concatenate([att[p][c], qe[rs(c), ls(p)]], axis=1), rhs)
            col = (c % 2) * GLA_CHUNK
            s = dec_t[(c * GLA_CHUNK) // half][:, col:col + 1] * s + ds[p][c]
        p_scr[p] = s
    return jnp.concatenate([jnp.concatenate(r, axis=1) for r in rows], axis=0)


def _gla_state_load(s_ref, p_scr):
    p_scr[...] = jnp.zeros(p_scr.shape, F32)
    for h in range(GLA_HEADS):
        r0, c0 = (h % 2) * GLA_DK, (h % 2) * GLA_DV
        p_scr[h // 2, r0:r0 + GLA_DK, c0:c0 + GLA_DV] = s_ref[0, 0, h]


def _gla_state_store(p_scr, s_ref):
    for h in range(GLA_HEADS):
        r0, c0 = (h % 2) * GLA_DK, (h % 2) * GLA_DV
        s_ref[0, 0, h] = p_scr[h // 2, r0:r0 + GLA_DK, c0:c0 + GLA_DV]


def _ev1_kernel(xp_ref, xs_ref, rtab_ref, ctab_ref, mod_ref, ng_ref, win_ref, wlr_ref, gmat_ref, gb_ref,
                lng_ref, lnb_ref, ws_ref, bst_ref, p0_ref, mid_ref, pfin_ref, p_scr):
    t = pl.program_id(0)
    is_p, within, cidx, first, last, _ = _tile_info(t)

    @pl.when(first)
    def _():
        _gla_state_load(p0_ref, p_scr)

    x = _load_x0(is_p, within, xp_ref, xs_ref, rtab_ref, ctab_ref)
    sh1, sc1, _, _, _, _ = _mod_rows(mod_ref, cidx)
    hm = (_rms(x, ng_ref[0:1, :]) * (1.0 + sc1) + sh1).astype(BF16)
    proj = _dot(hm, win_ref[...])
    lr = _dot(hm, wlr_ref[...]).astype(BF16)
    la_all = _log_sigmoid(_dot(lr, gmat_ref[...]) + gb_ref[...]) * (1.0 / GLA_NORMALIZER)

    u = jax.nn.gelu(proj[:, 0:SGU_WIDTH])
    vg = jax.nn.gelu(proj[:, SGU_WIDTH:2 * SGU_WIDTH])
    mu = jnp.mean(vg, axis=-1, keepdims=True)
    vc = vg - mu
    vn = (vc * lax.rsqrt(jnp.mean(vc * vc, axis=-1, keepdims=True) + EPS) * lng_ref[...] + lnb_ref[...]).astype(BF16)
    gd = SGU_WIDTH // SGU_GROUPS
    nch = TM // SGU_CHUNK
    sv_cols = []
    for g in range(SGU_GROUPS):
        vcat = jnp.concatenate(
            [vn[c * SGU_CHUNK:(c + 1) * SGU_CHUNK, g * gd:(g + 1) * gd] for c in range(nch)], axis=1)
        sg = _dot(ws_ref[g], vcat) + bst_ref[:, g:g + 1]
        sv_cols.append(jnp.concatenate([sg[:, c * gd:(c + 1) * gd] for c in range(nch)], axis=0))
    out_a = u * jnp.concatenate(sv_cols, axis=1)

    o0 = 2 * SGU_WIDTH
    qs = proj[:, o0:o0 + QK] * (GLA_DK ** -0.5)
    k = proj[:, o0 + QK:o0 + 2 * QK]
    v = proj[:, o0 + 2 * QK:o0 + 2 * QK + VW]
    g = proj[:, o0 + 2 * QK + VW:o0 + 2 * QK + 2 * VW]

    o_f = _gla_tile(qs, k, v.astype(BF16), la_all[:, 0:QK], p_scr, reverse=False)
    mid_ref[:, 0:VW] = out_a
    mid_ref[:, VW:2 * VW] = o_f
    mid_ref[:, 2 * VW:3 * VW] = g * jax.nn.sigmoid(g)
    c0 = 3 * VW
    mid_ref[:, c0:c0 + QK] = qs
    mid_ref[:, c0 + QK:c0 + 2 * QK] = k
    mid_ref[:, c0 + 2 * QK:c0 + 2 * QK + VW] = v
    mid_ref[:, c0 + 2 * QK + VW:c0 + 3 * QK + VW] = la_all[:, QK:2 * QK]

    @pl.when(last)
    def _():
        _gla_state_store(p_scr, pfin_ref)


def _const_spec(shape):
    nd = len(shape)
    return pl.BlockSpec(shape, lambda i, _nd=nd: (0,) * _nd, pipeline_mode=pl.Buffered(1))


def _xp_spec(rev):
    if rev:
        return pl.BlockSpec((TM, D), lambda i: (jnp.minimum(NT - 1 - i, NTP - 1), 0))
    return pl.BlockSpec((TM, D), lambda i: (jnp.minimum(i, NTP - 1), 0))


def _xs_spec(rev):
    if rev:
        return pl.BlockSpec((TM, D), lambda i: (jnp.maximum(NT - 1 - i - NTP, 0), 0))
    return pl.BlockSpec((TM, D), lambda i: (jnp.maximum(i - NTP, 0), 0))


def _tile_spec(width, rev=False):
    if rev:
        return pl.BlockSpec((TM, width), lambda i: (NT - 1 - i, 0))
    return pl.BlockSpec((TM, width), lambda i: (i, 0))


def _params():
    return pltpu.CompilerParams(dimension_semantics=("arbitrary",), vmem_limit_bytes=VMEM_LIMIT)


def _even_forward(xp, xs, rtab, ctab, mod, ng, w):
    state_blk = (1, 1, GLA_HEADS, GLA_DK, GLA_DV)
    return pl.pallas_call(
        _ev1_kernel,
        out_shape=(jax.ShapeDtypeStruct((NT * TM, EV_MID), F32),
                   jax.ShapeDtypeStruct((NSEQ, 1) + state_blk[2:], F32)),
        grid=(NT,),
        in_specs=[
            _xp_spec(False), _xs_spec(False),
            _const_spec((GRID_W, D // 2)), _const_spec((GRID_W, D // 2)),
            _const_spec((NCOND, 6 * D)), _const_spec((4, D)),
            _const_spec((D, EV_MAIN)), _const_spec((D, LANES)), _const_spec((LANES, 2 * QK)),
            _const_spec((1, 2 * QK)),
            _const_spec((1, SGU_WIDTH)), _const_spec((1, SGU_WIDTH)),
            _const_spec((SGU_GROUPS, SGU_CHUNK, SGU_CHUNK)), _const_spec((SGU_CHUNK, SGU_GROUPS)),
            pl.BlockSpec(state_blk, lambda i: (_seq_of(i), 0, 0, 0, 0)),
        ],
        out_specs=(_tile_spec(EV_MID),
                   pl.BlockSpec(state_blk, lambda i: (_seq_of(i), 0, 0, 0, 0))),
        scratch_shapes=[pltpu.VMEM((GLA_HEADS // 2, 2 * GLA_DK, 2 * GLA_DV), F32)],
        compiler_params=_params(),
        name="even_forward",
    )(xp, xs, rtab, ctab, mod, ng, w["win"], w["wlr"], w["gmat"], w["gb"], w["lng"], w["lnb"],
      w["ws"], w["bst"], w["p0"])


def _ev2_kernel(xp_ref, xs_ref, rtab_ref, ctab_ref, mod_ref, ng_ref, mid_ref, gn_ref, wout_ref, p0_ref,
                x1_ref, pfin_ref, p_scr):
    t = NT - 1 - pl.program_id(0)
    is_p, within, cidx, first, last, _ = _tile_info(t)

    @pl.when(last)
    def _():
        _gla_state_load(p0_ref, p_scr)

    x = _load_x0(is_p, within, xp_ref, xs_ref, rtab_ref, ctab_ref)
    _, _, g1, _, _, _ = _mod_rows(mod_ref, cidx)
    out_a = mid_ref[:, 0:VW]
    o_f = mid_ref[:, VW:2 * VW]
    sg = mid_ref[:, 2 * VW:3 * VW]
    c0 = 3 * VW
    qs = mid_ref[:, c0:c0 + QK]
    k = mid_ref[:, c0 + QK:c0 + 2 * QK]
    v = mid_ref[:, c0 + 2 * QK:c0 + 2 * QK + VW]
    la_b = mid_ref[:, c0 + 2 * QK + VW:c0 + 3 * QK + VW]

    o = o_f + _gla_tile(qs, k, v.astype(BF16), la_b, p_scr, reverse=True)
    heads = []
    for h in range(GLA_HEADS):
        oh = o[:, h * GLA_DV:(h + 1) * GLA_DV]
        heads.append(oh * lax.rsqrt(jnp.mean(oh * oh, axis=-1, keepdims=True) + EPS))
    on = jnp.concatenate(heads, axis=1) * gn_ref[...] * sg
    cat = jnp.concatenate([out_a, on], axis=1).astype(BF16)
    y = _dot(cat, wout_ref[...])
    x1_ref[...] = x + g1 * _rms(y, ng_ref[1:2, :])

    @pl.when(first)
    def _():
        _gla_state_store(p_scr, pfin_ref)


def _even_reverse(xp, xs, rtab, ctab, mod, ng, mid, w):
    state_blk = (1, 1, GLA_HEADS, GLA_DK, GLA_DV)
    return pl.pallas_call(
        _ev2_kernel,
        out_shape=(jax.ShapeDtypeStruct((NT * TM, D), F32),
                   jax.ShapeDtypeStruct((NSEQ, 1) + state_blk[2:], F32)),
        grid=(NT,),
        in_specs=[
            _xp_spec(True), _xs_spec(True),
            _const_spec((GRID_W, D // 2)), _const_spec((GRID_W, D // 2)),
            _const_spec((NCOND, 6 * D)), _const_spec((4, D)),
            _tile_spec(EV_MID, rev=True),
            _const_spec((1, VW)), _const_spec((2 * VW, D)),
            pl.BlockSpec(state_blk, lambda i: (_seq_of(NT - 1 - i), 1, 0, 0, 0)),
        ],
        out_specs=(_tile_spec(D, rev=True),
                   pl.BlockSpec(state_blk, lambda i: (_seq_of(NT - 1 - i), 0, 0, 0, 0))),
        scratch_shapes=[pltpu.VMEM((GLA_HEADS // 2, 2 * GLA_DK, 2 * GLA_DV), F32)],
        compiler_params=_params(),
        name="even_reverse",
    )(xp, xs, rtab, ctab, mod, ng, mid, w["gn"], w["wout"], w["p0"])


FF_CHUNK = 1024
MLP_TILES = 2
TMM = MLP_TILES * TM


def _mlp_kernel(x_ref, mod_ref, ng_ref, w1_ref, b1_ref, w2_ref, b2_ref, o_ref, *, tile0):
    _, _, cidx, _, _, _ = _tile_info(tile0 + pl.program_id(0) * MLP_TILES)
    _, _, _, sh2, sc2, g2 = _mod_rows(mod_ref, cidx)
    x = x_ref[...]
    hff = (_rms(x, ng_ref[2:3, :]) * (1.0 + sc2) + sh2).astype(BF16)
    acc = b2_ref[...]
    for j in range(D_FF // FF_CHUNK):
        cs = slice(j * FF_CHUNK, (j + 1) * FF_CHUNK)
        h = _dot(hff, w1_ref[:, cs]) + b1_ref[:, cs]
        h = jnp.square(jnp.maximum(h, 0.0)).astype(BF16)
        acc = acc + _dot(h, w2_ref[cs, :])
    o_ref[...] = x + g2 * _rms(acc, ng_ref[3:4, :])


def _mlp(x, mod, ng, w1, b1, w2, b2, tile0, ntiles):
    assert tile0 % MLP_TILES == 0 and ntiles % MLP_TILES == 0 and TPS % MLP_TILES == 0 and NTP % MLP_TILES == 0
    b0 = tile0 // MLP_TILES
    return pl.pallas_call(
        functools.partial(_mlp_kernel, tile0=tile0),
        out_shape=jax.ShapeDtypeStruct((ntiles * TM, D), F32),
        grid=(ntiles // MLP_TILES,),
        in_specs=[
            pl.BlockSpec((TMM, D), lambda i: (b0 + i, 0)),
            _const_spec((NCOND, 6 * D)), _const_spec((4, D)),
            _const_spec((D, D_FF)), _const_spec((1, D_FF)), _const_spec((D_FF, D)), _const_spec((1, D)),
        ],
        out_specs=pl.BlockSpec((TMM, D), lambda i: (i, 0)),
        compiler_params=_params(), name="mlp",
    )(x, mod, ng, w1, b1, w2, b2)


def _rg_gates(xc, wg_ref, ba_ref, bx_ref, lam_ref, a_scr, b_scr):
    xcb = xc.astype(BF16)
    sp = RG_C * _softplus(-lam_ref[...])
    for s in range(D_RNN // SLAB):
        cs = slice(s * SLAB, (s + 1) * SLAB)
        pre = _dot(xcb[:, cs], wg_ref[s])
        r = jax.nn.sigmoid(pre[:, 0:SLAB] + ba_ref[:, cs])
        i = jax.nn.sigmoid(pre[:, SLAB:2 * SLAB] + bx_ref[:, cs])
        z = r * sp[:, cs]
        a = jnp.exp(-z)
        a_scr[:, cs] = a
        u = jnp.tanh(z) * (1.0 + a * a)
        b_scr[:, cs] = jnp.where(u > 0.0, u * lax.rsqrt(u), 0.0) * (i * xc[:, cs])


SEG = TM // SUBLANES


def _seg_perm(transpose):
    rr = lax.broadcasted_iota(jnp.int32, (TM, TM), 0)
    cc = lax.broadcasted_iota(jnp.int32, (TM, TM), 1)
    if transpose:
        rr, cc = cc, rr
    hit = jnp.logical_and(cc // SEG == rr % SUBLANES, cc % SEG == rr // SUBLANES)
    return jnp.where(hit, 1.0, 0.0).astype(BF16)


def _rg_scan(a_scr, b_scr, h_scr, c_scr, h0, reverse):
    def body(i, carry):
        hh, cc = carry
        grp = (SEG - 1 - i) if reverse else i
        r0 = pl.multiple_of(grp * SUBLANES, SUBLANES)
        a = a_scr[pl.ds(r0, SUBLANES), :]
        hh = a * hh + b_scr[pl.ds(r0, SUBLANES), :]
        cc = a * cc
        h_scr[pl.ds(r0, SUBLANES), :] = hh
        c_scr[pl.ds(r0, SUBLANES), :] = cc
        return hh, cc

    init = (jnp.zeros((SUBLANES, D_RNN), F32), jnp.ones((SUBLANES, D_RNN), F32))
    b, a = lax.fori_loop(0, SEG, body, init, unroll=4)
    row = lax.broadcasted_iota(jnp.int32, (SUBLANES, D_RNN), 0)
    for s in (1, 2, 4):
        shift = (SUBLANES - s) if reverse else s
        valid = (row < SUBLANES - s) if reverse else (row >= s)
        a_s = pltpu.roll(a, shift, 0)
        b_s = pltpu.roll(b, shift, 0)
        b = jnp.where(valid, a * b_s + b, b)
        a = jnp.where(valid, a * a_s, a)
    after = a * h0 + b
    edge = (row == SUBLANES - 1) if reverse else (row == 0)
    enter = jnp.where(edge, h0, pltpu.roll(after, (SUBLANES - 1) if reverse else 1, 0))
    h = h_scr[...] + c_scr[...] * jnp.concatenate([enter] * SEG, axis=0)
    out = after[0:1, :] if reverse else after[SUBLANES - 1:SUBLANES, :]
    return h, out


def _od1_kernel(x_ref, xn_ref, mod_ref, ng_ref, win_ref, cw_ref, cb_ref, wg_ref, ba_ref, bx_ref, lam_ref,
                s0_ref, mid_ref, sfin_ref, xb_scr, a_scr, b_scr, h_scr, c_scr, hc_scr, tail_scr):
    t = pl.program_id(0)
    _, _, cidx, first, last, _ = _tile_info(t)

    @pl.when(first)
    def _():
        tail_scr[...] = jnp.zeros((2 * SUBLANES, D_RNN), F32)
        hc_scr[...] = s0_ref[0, 0]

    sh1, sc1, _, _, _, _ = _mod_rows(mod_ref, cidx)
    xe = jnp.concatenate([x_ref[...], xn_ref[...]], axis=0)
    hm = (_rms(xe, ng_ref[0:1, :]) * (1.0 + sc1) + sh1).astype(BF16)
    hmp = _dot(_seg_perm(False), hm[0:TM, :]).astype(BF16)
    proj = _dot(jnp.concatenate([hmp, hm[TM:TM + HALO, :]], axis=0), win_ref[...])

    xb = proj[0:TM, 0:D_RNN]
    row = lax.broadcasted_iota(jnp.int32, (SUBLANES, D_RNN), 0)
    g30 = xb[TM - 2 * SUBLANES:TM - SUBLANES, :]
    g31 = xb[TM - SUBLANES:TM, :]
    nxt = jnp.where(last, 0.0, proj[TM:TM + 1, 0:D_RNN])
    xb_scr[0:SUBLANES, :] = jnp.where(row == 0, pltpu.roll(tail_scr[0:SUBLANES, :], 1, 0), pltpu.roll(g30, 1, 0))
    xb_scr[SUBLANES:2 * SUBLANES, :] = jnp.where(
        row == 0, pltpu.roll(tail_scr[SUBLANES:2 * SUBLANES, :], 1, 0), pltpu.roll(g31, 1, 0))
    xb_scr[2 * SUBLANES:2 * SUBLANES + TM, :] = xb
    xb_scr[2 * SUBLANES + TM:3 * SUBLANES + TM, :] = jnp.where(
        row == SUBLANES - 1, nxt, pltpu.roll(xb[0:SUBLANES, :], SUBLANES - 1, 0))
    tail_scr[0:SUBLANES, :] = g30
    tail_scr[SUBLANES:2 * SUBLANES, :] = g31
    xc = cb_ref[...]
    for j in range(4):
        xc = xc + cw_ref[j:j + 1, :] * xb_scr[j * SUBLANES:j * SUBLANES + TM, :]

    _rg_gates(xc, wg_ref, ba_ref, bx_ref, lam_ref, a_scr, b_scr)
    h, hc_scr[...] = _rg_scan(a_scr, b_scr, h_scr, c_scr, hc_scr[...], reverse=False)

    mid_ref[:, 0:D_RNN] = h
    mid_ref[:, D_RNN:2 * D_RNN] = jax.nn.gelu(proj[0:TM, D_RNN:2 * D_RNN])
    mid_ref[:, 2 * D_RNN:3 * D_RNN] = xc

    @pl.when(last)
    def _():
        sfin_ref[0, 0] = hc_scr[...]


def _odd_forward(x, mod, ng, w):
    nb16 = TM // HALO
    return pl.pallas_call(
        _od1_kernel,
        out_shape=(jax.ShapeDtypeStruct((NT * TM, OD_MID), F32),
                   jax.ShapeDtypeStruct((NSEQ, 1, 1, D_RNN), F32)),
        grid=(NT,),
        in_specs=[
            _tile_spec(D),
            pl.BlockSpec((HALO, D), lambda i: (jnp.minimum(i + 1, NT - 1) * nb16, 0)),
            _const_spec((NCOND, 6 * D)), _const_spec((4, D)),
            _const_spec((D, 2 * D_RNN)), _const_spec((4, D_RNN)), _const_spec((1, D_RNN)),
            _const_spec((D_RNN // SLAB, SLAB, 2 * SLAB)),
            _const_spec((1, D_RNN)), _const_spec((1, D_RNN)), _const_spec((1, D_RNN)),
            pl.BlockSpec((1, 1, 1, D_RNN), lambda i: (_seq_of(i), 0, 0, 0)),
        ],
        out_specs=(_tile_spec(OD_MID),
                   pl.BlockSpec((1, 1, 1, D_RNN), lambda i: (_seq_of(i), 0, 0, 0))),
        scratch_shapes=[
            pltpu.VMEM((3 * SUBLANES + TM, D_RNN), F32),
            pltpu.VMEM((TM, D_RNN), F32), pltpu.VMEM((TM, D_RNN), F32), pltpu.VMEM((TM, D_RNN), F32),
            pltpu.VMEM((TM, D_RNN), F32),
            pltpu.VMEM((1, D_RNN), F32), pltpu.VMEM((2 * SUBLANES, D_RNN), F32),
        ],
        compiler_params=_params(),
        name="odd_forward",
    )(x, x, mod, ng, w["win"], w["cw"], w["cb"], w["wg"][0], w["ba"][0:1], w["bx"][0:1], w["lam"][0:1],
      w["s0"])


def _od2_kernel(x_ref, mod_ref, ng_ref, mid_ref, wg_ref, ba_ref, bx_ref, lam_ref, wout_ref, s0_ref,
                x1_ref, sfin_ref, a_scr, b_scr, h_scr, c_scr, hc_scr):
    t = NT - 1 - pl.program_id(0)
    _, _, cidx, first, last, _ = _tile_info(t)
    _, _, g1, _, _, _ = _mod_rows(mod_ref, cidx)

    @pl.when(last)
    def _():
        hc_scr[...] = s0_ref[0, 0]

    _rg_gates(mid_ref[:, 2 * D_RNN:3 * D_RNN], wg_ref, ba_ref, bx_ref, lam_ref, a_scr, b_scr)
    h_b, hc_scr[...] = _rg_scan(a_scr, b_scr, h_scr, c_scr, hc_scr[...], reverse=True)

    zp = ((mid_ref[:, 0:D_RNN] + h_b) * mid_ref[:, D_RNN:2 * D_RNN]).astype(BF16)
    z = _dot(_seg_perm(True), zp).astype(BF16)
    y = _dot(z, wout_ref[...])
    x1_ref[...] = x_ref[...] + g1 * _rms(y, ng_ref[1:2, :])

    @pl.when(first)
    def _():
        sfin_ref[0, 0] = hc_scr[...]


def _odd_reverse(x, mod, ng, mid, w):
    return pl.pallas_call(
        _od2_kernel,
        out_shape=(jax.ShapeDtypeStruct((NT * TM, D), F32),
                   jax.ShapeDtypeStruct((NSEQ, 1, 1, D_RNN), F32)),
        grid=(NT,),
        in_specs=[
            _tile_spec(D, rev=True),
            _const_spec((NCOND, 6 * D)), _const_spec((4, D)),
            _tile_spec(OD_MID, rev=True),
            _const_spec((D_RNN // SLAB, SLAB, 2 * SLAB)),
            _const_spec((1, D_RNN)), _const_spec((1, D_RNN)), _const_spec((1, D_RNN)),
            _const_spec((D_RNN, D)),
            pl.BlockSpec((1, 1, 1, D_RNN), lambda i: (_seq_of(NT - 1 - i), 1, 0, 0)),
        ],
        out_specs=(_tile_spec(D, rev=True),
                   pl.BlockSpec((1, 1, 1, D_RNN), lambda i: (_seq_of(NT - 1 - i), 0, 0, 0))),
        scratch_shapes=[
            pltpu.VMEM((TM, D_RNN), F32), pltpu.VMEM((TM, D_RNN), F32), pltpu.VMEM((TM, D_RNN), F32),
            pltpu.VMEM((TM, D_RNN), F32),
            pltpu.VMEM((1, D_RNN), F32),
        ],
        compiler_params=_params(),
        name="odd_reverse",
    )(x, mod, ng, mid, w["wg"][1], w["ba"][1:2], w["bx"][1:2], w["lam"][1:2], w["wout"], w["s0"])


def _pos_tables():
    n = D // 4
    omega = 1.0 / (10000.0 ** (jnp.arange(n, dtype=F32) / n))
    idx = jnp.arange(GRID_W, dtype=F32)[:, None] * omega
    tab = jnp.concatenate([jnp.sin(idx), jnp.cos(idx)], axis=-1)
    return tab, tab


def _block_diag_slabs(w):
    per = SLAB // RG_BS
    w = w.reshape(D_RNN // SLAB, per, RG_BS, RG_BS)
    eye = jnp.eye(per, dtype=w.dtype)
    return jnp.einsum("spij,pq->spiqj", w, eye).reshape(D_RNN // SLAB, SLAB, SLAB)


def kernel(x_prompt, x_sample, c, state_gla, state_rglru, c_ctx, mod_w, mod_b, norm_g, mlp_w1, mlp_b1, mlp_w2,
           mlp_b2, ev_w_in, ev_w_out, sgu_ln_g, sgu_ln_b, sgu_ws, sgu_bs, gla_gate_w2, gla_gate_b, gla_norm_g,
           rg_w_in, rg_conv_w, rg_conv_b, rg_wa, rg_ba, rg_wx, rg_bx, rg_L, rg_w_out):
    assert x_prompt.shape == (BATCH, SEQ, D) and x_sample.shape == (DEC_BATCH, DEC_SEQ, D)
    assert SEQ == TM and DEC_SEQ % TM == 0 and DEPTH == 2
    xp = x_prompt.reshape(NTP * TM, D)
    xs = x_sample.reshape(NTS * TM, D)
    cond8 = jnp.concatenate([c_ctx[None, :], c, jnp.zeros((NCOND - 1 - DEC_BATCH, D), F32)], axis=0)
    mods = _modulation(cond8, mod_w, mod_b)
    rtab, ctab = _pos_tables()

    gmat = jnp.zeros((LANES, 2 * QK), F32)
    gmat = gmat.at[0:GLA_RANK, 0:QK].set(gla_gate_w2[0, 0])
    gmat = gmat.at[GLA_RANK:2 * GLA_RANK, QK:2 * QK].set(gla_gate_w2[0, 1])
    s0_gla = jnp.concatenate([jnp.zeros((BATCH,) + state_gla.shape[2:], F32), state_gla[:, 0]], axis=0)
    ev = {
        "win": ev_w_in[0, :, 0:EV_MAIN].astype(BF16),
        "wlr": jnp.pad(ev_w_in[0, :, EV_MAIN:], ((0, 0), (0, LANES - 2 * GLA_RANK))).astype(BF16),
        "gmat": gmat.astype(BF16),
        "gb": gla_gate_b[0].reshape(1, 2 * QK),
        "lng": sgu_ln_g[0].reshape(1, SGU_WIDTH),
        "lnb": sgu_ln_b[0].reshape(1, SGU_WIDTH),
        "ws": sgu_ws[0].astype(BF16),
        "bst": sgu_bs[0].T,
        "gn": gla_norm_g[0].reshape(1, VW),
        "wout": ev_w_out[0].astype(BF16),
        "p0": s0_gla,
    }
    mid, pf = _even_forward(xp, xs, rtab, ctab, mods[0], norm_g[0], ev)
    x1, pb = _even_reverse(xp, xs, rtab, ctab, mods[0], norm_g[0], mid, ev)
    x2 = _mlp(x1, mods[0], norm_g[0], mlp_w1[0].astype(BF16), mlp_b1[0].reshape(1, D_FF),
              mlp_w2[0].astype(BF16), mlp_b2[0].reshape(1, D), 0, NT)
    new_gla = jnp.stack([pf[:BATCH, 0], pb[:BATCH, 0]], axis=1)[:, None]

    s0_rg = jnp.concatenate([jnp.zeros((BATCH, 2, D_RNN), F32), state_rglru[:, 0]], axis=0)
    od = {
        "win": rg_w_in[0].astype(BF16),
        "cw": rg_conv_w[0],
        "cb": rg_conv_b[0].reshape(1, D_RNN),
        "wg": jnp.stack([
            jnp.concatenate([_block_diag_slabs(rg_wa[0, d]), _block_diag_slabs(rg_wx[0, d])], axis=-1)
            for d in range(2)], axis=0).astype(BF16),
        "ba": rg_ba[0], "bx": rg_bx[0], "lam": rg_L[0],
        "wout": rg_w_out[0].astype(BF16),
        "s0": s0_rg.reshape(NSEQ, 2, 1, D_RNN),
    }
    mid1, sf = _odd_forward(x2, mods[1], norm_g[1], od)
    x3, sb = _odd_reverse(x2, mods[1], norm_g[1], mid1, od)
    mlp1 = (mods[1], norm_g[1], mlp_w1[1].astype(BF16), mlp_b1[1].reshape(1, D_FF),
            mlp_w2[1].astype(BF16), mlp_b2[1].reshape(1, D))
    y_prompt = _mlp(x3, *mlp1, 0, NTP)
    y_sample = _mlp(x3, *mlp1, NTP, NTS)
    new_rg = jnp.stack([sf[:BATCH, 0, 0], sb[:BATCH, 0, 0]], axis=1)[:, None]
    return (y_prompt.reshape(BATCH, SEQ, D), y_sample.reshape(DEC_BATCH, DEC_SEQ, D), new_gla, new_rg)
```

```python
import functools

import jax
import jax.numpy as jnp
from jax import lax
from jax.experimental import pallas as pl
from jax.experimental.pallas import tpu as pltpu

D = 1024
BATCH = 16
SEQ = 256
DEPTH = 2
DEC_BATCH = 4
DEC_SEQ = 4096
GRID_W = 64
D_FF = 4 * D
EPS = 1e-6
SGU_CHUNK = 128
SGU_GROUPS = 4
SGU_WIDTH = D // 2
GLA_HEADS = 4
GLA_DV = 128
GLA_DK = 64
GLA_RANK = 16
GLA_NORMALIZER = 16.0
GLA_CHUNK = 64
QK = GLA_HEADS * GLA_DK
VW = GLA_HEADS * GLA_DV
EV_MAIN = 2 * SGU_WIDTH + 2 * QK + 2 * VW
D_RNN = D
RG_BLOCKS = 16
RG_BS = D_RNN // RG_BLOCKS
RG_C = 8.0
LANES = 128
SUBLANES = 8
SLAB = 256

TM = 256
NTP = BATCH * SEQ // TM
TPS = DEC_SEQ // TM
NTS = DEC_BATCH * TPS
NT = NTP + NTS
NSEQ = BATCH + DEC_BATCH
NCOND = 8
HALO = 16
EV_MID = 2 * VW + VW + 2 * QK + VW + QK
OD_MID = 3 * D_RNN
VMEM_LIMIT = 56 * 1024 * 1024

F32 = jnp.float32
BF16 = jnp.bfloat16


def _tile_info(t):
    is_p = t < NTP
    ts = jnp.maximum(t - NTP, 0)
    sq = ts // TPS
    within = ts % TPS
    cidx = jnp.where(is_p, 0, 1 + sq)
    first = jnp.logical_or(is_p, within == 0)
    last = jnp.logical_or(is_p, within == TPS - 1)
    seq = jnp.where(is_p, t, BATCH + sq)
    return is_p, within, cidx, first, last, seq


def _seq_of(t):
    return jnp.where(t < NTP, t, BATCH + jnp.maximum(t - NTP, 0) // TPS)


def _rms(x, g):
    return x * lax.rsqrt(jnp.mean(x * x, axis=-1, keepdims=True) + EPS) * g


def _dot(a, b):
    return jnp.dot(a, b, preferred_element_type=F32)


def _dot_nt(a, b):
    return lax.dot_general(a, b, (((1,), (1,)), ((), ())), preferred_element_type=F32)


def _dot_tn(a, b):
    return lax.dot_general(a, b, (((0,), (0,)), ((), ())), preferred_element_type=F32)


def _split3(x):
    hi = x.astype(BF16)
    r1 = x - hi.astype(F32)
    mid = r1.astype(BF16)
    lo = (r1 - mid.astype(F32)).astype(BF16)
    return hi, mid, lo


def _dot_exact_lhs(m, parts):
    return _dot(m, parts[0]) + _dot(m, parts[1]) + _dot(m, parts[2])


def _log_sigmoid(x):
    return jnp.minimum(x, 0.0) - jnp.log(1.0 + jnp.exp(-jnp.abs(x)))


def _softplus(x):
    return jnp.maximum(x, 0.0) + jnp.log(1.0 + jnp.exp(-jnp.abs(x)))


def _mod_rows(mod_ref, cidx):
    m = mod_ref[pl.ds(cidx, 1), :]
    return [m[:, j * D:(j + 1) * D] for j in range(6)]


def _load_x0(is_p, within, xp_ref, xs_ref, rtab_ref, ctab_ref):
    rows_per_tile = TM // GRID_W
    r0 = within * rows_per_tile
    posr = jnp.concatenate(
        [jnp.broadcast_to(rtab_ref[pl.ds(r0 + j, 1), :], (GRID_W, D // 2)) for j in range(rows_per_tile)],
        axis=0)
    posc = jnp.concatenate([ctab_ref[...]] * rows_per_tile, axis=0)
    pos = jnp.concatenate([posr, posc], axis=1)
    return jnp.where(is_p, xp_ref[...], xs_ref[...] + pos)


def _mod_kernel(cond_ref, w_ref, b_ref, o_ref):
    c = cond_ref[...]
    sc = (c * jax.nn.sigmoid(c)).astype(BF16)
    o_ref[0] = _dot(sc, w_ref[0].astype(BF16)) + b_ref[0]


def _modulation(cond8, mod_w, mod_b):
    nb = 6 * D // D
    return pl.pallas_call(
        _mod_kernel,
        out_shape=jax.ShapeDtypeStruct((DEPTH, NCOND, 6 * D), F32),
        grid=(DEPTH, nb),
        in_specs=[
            pl.BlockSpec((NCOND, D), lambda l, j: (0, 0)),
            pl.BlockSpec((1, D, D), lambda l, j: (l, 0, j)),
            pl.BlockSpec((1, 1, D), lambda l, j: (l, 0, j)),
        ],
        out_specs=pl.BlockSpec((1, NCOND, D), lambda l, j: (l, 0, j)),
        compiler_params=pltpu.CompilerParams(
            dimension_semantics=("arbitrary", "arbitrary"), vmem_limit_bytes=VMEM_LIMIT),
        name="modulation",
    )(cond8, mod_w, mod_b.reshape(DEPTH, 1, 6 * D))


def _chunk_rows(c):
    return slice(c * GLA_CHUNK, (c + 1) * GLA_CHUNK)


def _pair_lanes(p):
    return slice(p * LANES, (p + 1) * LANES)


def _gla_order(reverse):
    n = TM // GLA_CHUNK
    return list(reversed(range(n))) if reverse else list(range(n))


def _gla_prep(qs, k, la, reverse):
    n = TM // GLA_CHUNK
    ri = lax.broadcasted_iota(jnp.int32, (TM, TM), 0)
    ci = lax.broadcasted_iota(jnp.int32, (TM, TM), 1)
    same = (ri // GLA_CHUNK) == (ci // GLA_CHUNK)
    order = (ci >= ri) if reverse else (ci <= ri)
    cum_m = jnp.where(jnp.logical_and(same, order), 1.0, 0.0).astype(BF16)
    b = _dot_exact_lhs(cum_m, _split3(la))
    ends = [c * GLA_CHUNK if reverse else (c + 1) * GLA_CHUNK - 1 for c in range(n)]
    btot = jnp.concatenate([jnp.broadcast_to(b[e:e + 1, :], (GLA_CHUNK, QK)) for e in ends], axis=0)
    qe = (qs * jnp.exp(b)).astype(BF16)
    ke = k * jnp.exp(-b)
    kd = (k * jnp.exp(btot - b)).astype(BF16)
    dec = jnp.exp(btot)
    lane = lax.broadcasted_iota(jnp.int32, (TM, QK), 1) % LANES
    ke_h = (jnp.where(lane < GLA_DK, ke, 0.0).astype(BF16), jnp.where(lane >= GLA_DK, ke, 0.0).astype(BF16))
    return qe, ke_h, kd, dec


def _gla_products(prep, v_bf, reverse):
    qe, ke_h, kd, _ = prep
    n = TM // GLA_CHUNK
    cr = lax.broadcasted_iota(jnp.int32, (GLA_CHUNK, 2 * GLA_CHUNK), 0)
    cc = lax.broadcasted_iota(jnp.int32, (GLA_CHUNK, 2 * GLA_CHUNK), 1) % GLA_CHUNK
    cmask = (cc >= cr) if reverse else (cc <= cr)
    br = lax.broadcasted_iota(jnp.int32, (2 * GLA_DK, 2 * GLA_DV), 0)
    bc = lax.broadcasted_iota(jnp.int32, (2 * GLA_DK, 2 * GLA_DV), 1)
    diag = (br < GLA_DK) == (bc < GLA_DV)
    att = [[None] * n for _ in range(GLA_HEADS // 2)]
    ds = [[None] * n for _ in range(GLA_HEADS // 2)]
    for p in range(GLA_HEADS // 2):
        vp = v_bf[:, p * 2 * GLA_DV:(p + 1) * 2 * GLA_DV]
        for c in _gla_order(reverse):
            rs, ls = _chunk_rows(c), _pair_lanes(p)
            kk = jnp.concatenate([ke_h[0][rs, ls], ke_h[1][rs, ls]], axis=0)
            att[p][c] = jnp.where(cmask, _dot_nt(qe[rs, ls], kk), 0.0).astype(BF16)
            ds[p][c] = jnp.where(diag, _dot_tn(kd[rs, ls], vp[rs, :]), 0.0)
    return att, ds


def _gla_outputs(prep, prods, v_bf, p_scr, reverse):
    qe, _, _, dec = prep
    att, ds = prods
    n = TM // GLA_CHUNK
    half = TM // 2
    lane_v = lax.broadcasted_iota(jnp.int32, (TM, 2 * GLA_DV), 1)
    rows = [[None] * (GLA_HEADS // 2) for _ in range(n)]
    for p in range(GLA_HEADS // 2):
        ls = _pair_lanes(p)
        vp = v_bf[:, p * 2 * GLA_DV:(p + 1) * 2 * GLA_DV]
        vl = jnp.where(lane_v < GLA_DV, vp, jnp.zeros_like(vp))
        vr = jnp.where(lane_v >= GLA_DV, vp, jnp.zeros_like(vp))
        dec_t = (dec[0:half, ls].T, dec[half:TM, ls].T)
        s = p_scr[p]
        for c in _gla_order(reverse):
            rs = _chunk_rows(c)
            rhs = jnp.concatenate([vl[rs, :], vr[rs, :], s.astype(BF16)], axis=0)
            rows[c][p] = _dot(jnp.concatenate([att[p][c], qe[rs, ls]], axis=1), rhs)
            col = (c % 2) * GLA_CHUNK
            s = dec_t[(c * GLA_CHUNK) // half][:, col:col + 1] * s + ds[p][c]
        p_scr[p] = s
    return jnp.concatenate([jnp.concatenate(r, axis=1) for r in rows], axis=0)


def _gla_state_load(s_ref, p_scr):
    p_scr[...] = jnp.zeros(p_scr.shape, F32)
    for h in range(GLA_HEADS):
        r0, c0 = (h % 2) * GLA_DK, (h % 2) * GLA_DV
        p_scr[h // 2, r0:r0 + GLA_DK, c0:c0 + GLA_DV] = s_ref[0, 0, h]


def _gla_state_store(p_scr, s_ref):
    for h in range(GLA_HEADS):
        r0, c0 = (h % 2) * GLA_DK, (h % 2) * GLA_DV
        s_ref[0, 0, h] = p_scr[h // 2, r0:r0 + GLA_DK, c0:c0 + GLA_DV]


def _ev1_kernel(xp_ref, xs_ref, rtab_ref, ctab_ref, mod_ref, ng_ref, win_ref, wlr_ref, gmat_ref, gb_ref,
                lng_ref, lnb_ref, ws_ref, bst_ref, p0_ref, w1f_ref, w2f_ref,
                mid_ref, pfin_ref, w1b_ref, w2b_ref, p_scr):
    t = pl.program_id(0)
    is_p, within, cidx, first, last, _ = _tile_info(t)
    _cast_slabs(w1f_ref, w2f_ref, w1b_ref, w2b_ref)

    @pl.when(first)
    def _():
        _gla_state_load(p0_ref, p_scr)

    x = _load_x0(is_p, within, xp_ref, xs_ref, rtab_ref, ctab_ref)
    sh1, sc1, _, _, _, _ = _mod_rows(mod_ref, cidx)
    hm = (_rms(x, ng_ref[0:1, :]) * (1.0 + sc1) + sh1).astype(BF16)
    o0 = 2 * SGU_WIDTH
    lr = _dot(hm, wlr_ref[...]).astype(BF16)
    pg = _dot(hm, win_ref[:, o0:EV_MAIN])
    la_all = _log_sigmoid(_dot(lr, gmat_ref[...]) + gb_ref[...]) * (1.0 / GLA_NORMALIZER)
    ps = _dot(hm, win_ref[:, 0:o0])
    qs = pg[:, 0:QK] * (GLA_DK ** -0.5)
    k = pg[:, QK:2 * QK]
    v = pg[:, 2 * QK:2 * QK + VW]
    g = pg[:, 2 * QK + VW:2 * QK + 2 * VW]
    v_bf = v.astype(BF16)
    prep = _gla_prep(qs, k, la_all[:, 0:QK], reverse=False)
    prods = _gla_products(prep, v_bf, reverse=False)

    u = jax.nn.gelu(ps[:, 0:SGU_WIDTH])
    vg = jax.nn.gelu(ps[:, SGU_WIDTH:2 * SGU_WIDTH])
    mu = jnp.mean(vg, axis=-1, keepdims=True)
    vc = vg - mu
    vn = (vc * lax.rsqrt(jnp.mean(vc * vc, axis=-1, keepdims=True) + EPS) * lng_ref[...] + lnb_ref[...]).astype(BF16)
    gd = SGU_WIDTH // SGU_GROUPS
    nch = TM // SGU_CHUNK
    sv_cols = []
    for grp in range(SGU_GROUPS):
        vcat = jnp.concatenate(
            [vn[c * SGU_CHUNK:(c + 1) * SGU_CHUNK, grp * gd:(grp + 1) * gd] for c in range(nch)], axis=1)
        sg = _dot(ws_ref[grp], vcat) + bst_ref[:, grp:grp + 1]
        sv_cols.append(jnp.concatenate([sg[:, c * gd:(c + 1) * gd] for c in range(nch)], axis=0))
    out_a = u * jnp.concatenate(sv_cols, axis=1)

    o_f = _gla_outputs(prep, prods, v_bf, p_scr, reverse=False)
    mid_ref[:, 0:VW] = out_a
    mid_ref[:, VW:2 * VW] = o_f
    mid_ref[:, 2 * VW:3 * VW] = g * jax.nn.sigmoid(g)
    c0 = 3 * VW
    mid_ref[:, c0:c0 + QK] = qs
    mid_ref[:, c0 + QK:c0 + 2 * QK] = k
    mid_ref[:, c0 + 2 * QK:c0 + 2 * QK + VW] = v
    mid_ref[:, c0 + 2 * QK + VW:c0 + 3 * QK + VW] = la_all[:, QK:2 * QK]

    @pl.when(last)
    def _():
        _gla_state_store(p_scr, pfin_ref)


def _const_spec(shape):
    nd = len(shape)
    return pl.BlockSpec(shape, lambda i, _nd=nd: (0,) * _nd, pipeline_mode=pl.Buffered(1))


def _xp_spec(rev):
    if rev:
        return pl.BlockSpec((TM, D), lambda i: (jnp.minimum(NT - 1 - i, NTP - 1), 0))
    return pl.BlockSpec((TM, D), lambda i: (jnp.minimum(i, NTP - 1), 0))


def _xs_spec(rev):
    if rev:
        return pl.BlockSpec((TM, D), lambda i: (jnp.maximum(NT - 1 - i - NTP, 0), 0))
    return pl.BlockSpec((TM, D), lambda i: (jnp.maximum(i - NTP, 0), 0))


def _tile_spec(width, rev=False):
    if rev:
        return pl.BlockSpec((TM, width), lambda i: (NT - 1 - i, 0))
    return pl.BlockSpec((TM, width), lambda i: (i, 0))


def _params():
    return pltpu.CompilerParams(dimension_semantics=("arbitrary",), vmem_limit_bytes=VMEM_LIMIT)


CAST_STEPS = 64


def _cast_specs(layer):
    r1, r2 = D // CAST_STEPS, D_FF // CAST_STEPS
    in_specs = [pl.BlockSpec((1, r1, D_FF), lambda i: (layer, jnp.minimum(i, CAST_STEPS - 1), 0)),
                pl.BlockSpec((1, r2, D), lambda i: (layer, jnp.minimum(i, CAST_STEPS - 1), 0))]
    out_specs = [pl.BlockSpec((r1, D_FF), lambda i: (jnp.minimum(i, CAST_STEPS - 1), 0)),
                 pl.BlockSpec((r2, D), lambda i: (jnp.minimum(i, CAST_STEPS - 1), 0))]
    shapes = [jax.ShapeDtypeStruct((D, D_FF), BF16), jax.ShapeDtypeStruct((D_FF, D), BF16)]
    return in_specs, out_specs, shapes


def _cast_slabs(w1f_ref, w2f_ref, w1b_ref, w2b_ref):
    w1b_ref[...] = w1f_ref[0].astype(BF16)
    w2b_ref[...] = w2f_ref[0].astype(BF16)


def _even_forward(xp, xs, rtab, ctab, mod, ng, w, w1f, w2f, layer):
    state_blk = (1, 1, GLA_HEADS, GLA_DK, GLA_DV)
    cast_in, cast_out, cast_shapes = _cast_specs(layer)
    return pl.pallas_call(
        _ev1_kernel,
        out_shape=(jax.ShapeDtypeStruct((NT * TM, EV_MID), F32),
                   jax.ShapeDtypeStruct((NSEQ, 1) + state_blk[2:], F32), *cast_shapes),
        grid=(NT,),
        in_specs=[
            _xp_spec(False), _xs_spec(False),
            _const_spec((GRID_W, D // 2)), _const_spec((GRID_W, D // 2)),
            _const_spec((NCOND, 6 * D)), _const_spec((4, D)),
            _const_spec((D, EV_MAIN)), _const_spec((D, LANES)), _const_spec((LANES, 2 * QK)),
            _const_spec((1, 2 * QK)),
            _const_spec((1, SGU_WIDTH)), _const_spec((1, SGU_WIDTH)),
            _const_spec((SGU_GROUPS, SGU_CHUNK, SGU_CHUNK)), _const_spec((SGU_CHUNK, SGU_GROUPS)),
            pl.BlockSpec(state_blk, lambda i: (_seq_of(i), 0, 0, 0, 0)),
            *cast_in,
        ],
        out_specs=(_tile_spec(EV_MID),
                   pl.BlockSpec(state_blk, lambda i: (_seq_of(i), 0, 0, 0, 0)), *cast_out),
        scratch_shapes=[pltpu.VMEM((GLA_HEADS // 2, 2 * GLA_DK, 2 * GLA_DV), F32)],
        compiler_params=_params(),
        name="even_forward",
    )(xp, xs, rtab, ctab, mod, ng, w["win"], w["wlr"], w["gmat"], w["gb"], w["lng"], w["lnb"],
      w["ws"], w["bst"], w["p0"], w1f, w2f)


def _ev2_kernel(xp_ref, xs_ref, rtab_ref, ctab_ref, mod_ref, ng_ref, mid_ref, gn_ref, wout_ref, p0_ref,
                x1_ref, pfin_ref, p_scr):
    t = NT - 1 - pl.program_id(0)
    is_p, within, cidx, first, last, _ = _tile_info(t)

    @pl.when(last)
    def _():
        _gla_state_load(p0_ref, p_scr)

    x = _load_x0(is_p, within, xp_ref, xs_ref, rtab_ref, ctab_ref)
    _, _, g1, _, _, _ = _mod_rows(mod_ref, cidx)
    out_a = mid_ref[:, 0:VW]
    o_f = mid_ref[:, VW:2 * VW]
    sg = mid_ref[:, 2 * VW:3 * VW]
    c0 = 3 * VW
    qs = mid_ref[:, c0:c0 + QK]
    k = mid_ref[:, c0 + QK:c0 + 2 * QK]
    v = mid_ref[:, c0 + 2 * QK:c0 + 2 * QK + VW]
    la_b = mid_ref[:, c0 + 2 * QK + VW:c0 + 3 * QK + VW]

    v_bf = v.astype(BF16)
    prep = _gla_prep(qs, k, la_b, reverse=True)
    o = o_f + _gla_outputs(prep, _gla_products(prep, v_bf, reverse=True), v_bf, p_scr, reverse=True)
    heads = []
    for h in range(GLA_HEADS):
        oh = o[:, h * GLA_DV:(h + 1) * GLA_DV]
        heads.append(oh * lax.rsqrt(jnp.mean(oh * oh, axis=-1, keepdims=True) + EPS))
    on = jnp.concatenate(heads, axis=1) * gn_ref[...] * sg
    cat = jnp.concatenate([out_a, on], axis=1).astype(BF16)
    y = _dot(cat, wout_ref[...])
    x1_ref[...] = x + g1 * _rms(y, ng_ref[1:2, :])

    @pl.when(first)
    def _():
        _gla_state_store(p_scr, pfin_ref)


def _even_reverse(xp, xs, rtab, ctab, mod, ng, mid, w):
    state_blk = (1, 1, GLA_HEADS, GLA_DK, GLA_DV)
    return pl.pallas_call(
        _ev2_kernel,
        out_shape=(jax.ShapeDtypeStruct((NT * TM, D), F32),
                   jax.ShapeDtypeStruct((NSEQ, 1) + state_blk[2:], F32)),
        grid=(NT,),
        in_specs=[
            _xp_spec(True), _xs_spec(True),
            _const_spec((GRID_W, D // 2)), _const_spec((GRID_W, D // 2)),
            _const_spec((NCOND, 6 * D)), _const_spec((4, D)),
            _tile_spec(EV_MID, rev=True),
            _const_spec((1, VW)), _const_spec((2 * VW, D)),
            pl.BlockSpec(state_blk, lambda i: (_seq_of(NT - 1 - i), 1, 0, 0, 0)),
        ],
        out_specs=(_tile_spec(D, rev=True),
                   pl.BlockSpec(state_blk, lambda i: (_seq_of(NT - 1 - i), 0, 0, 0, 0))),
        scratch_shapes=[pltpu.VMEM((GLA_HEADS // 2, 2 * GLA_DK, 2 * GLA_DV), F32)],
        compiler_params=_params(),
        name="even_reverse",
    )(xp, xs, rtab, ctab, mod, ng, mid, w["gn"], w["wout"], w["p0"])


FF_CHUNK = 1024
MLP_TILES = 2
TMM = MLP_TILES * TM


def _mlp_kernel(x_ref, mod_ref, ng_ref, w1_ref, b1_ref, w2_ref, b2_ref, o_ref, *, tile0):
    _, _, cidx, _, _, _ = _tile_info(tile0 + pl.program_id(0) * MLP_TILES)
    _, _, _, sh2, sc2, g2 = _mod_rows(mod_ref, cidx)
    x = x_ref[...]
    hff = (_rms(x, ng_ref[2:3, :]) * (1.0 + sc2) + sh2).astype(BF16)
    acc = b2_ref[...]
    for j in range(D_FF // FF_CHUNK):
        cs = slice(j * FF_CHUNK, (j + 1) * FF_CHUNK)
        h = _dot(hff, w1_ref[:, cs]) + b1_ref[:, cs]
        h = jnp.square(jnp.maximum(h, 0.0)).astype(BF16)
        acc = acc + _dot(h, w2_ref[cs, :])
    o_ref[...] = x + g2 * _rms(acc, ng_ref[3:4, :])


def _mlp(x, mod, ng, w1, b1, w2, b2, tile0, ntiles):
    assert tile0 % MLP_TILES == 0 and ntiles % MLP_TILES == 0 and TPS % MLP_TILES == 0 and NTP % MLP_TILES == 0
    b0 = tile0 // MLP_TILES
    return pl.pallas_call(
        functools.partial(_mlp_kernel, tile0=tile0),
        out_shape=jax.ShapeDtypeStruct((ntiles * TM, D), F32),
        grid=(ntiles // MLP_TILES,),
        in_specs=[
            pl.BlockSpec((TMM, D), lambda i: (b0 + i, 0)),
            _const_spec((NCOND, 6 * D)), _const_spec((4, D)),
            _const_spec((D, D_FF)), _const_spec((1, D_FF)), _const_spec((D_FF, D)), _const_spec((1, D)),
        ],
        out_specs=pl.BlockSpec((TMM, D), lambda i: (i, 0)),
        compiler_params=_params(), name="mlp",
    )(x, mod, ng, w1, b1, w2, b2)


def _rg_gates(xc, wg_ref, ba_ref, bx_ref, lam_ref, a_scr, b_scr):
    xcb = xc.astype(BF16)
    sp = RG_C * _softplus(-lam_ref[...])
    for s in range(D_RNN // SLAB):
        cs = slice(s * SLAB, (s + 1) * SLAB)
        pre = _dot(xcb[:, cs], wg_ref[s])
        r = jax.nn.sigmoid(pre[:, 0:SLAB] + ba_ref[:, cs])
        i = jax.nn.sigmoid(pre[:, SLAB:2 * SLAB] + bx_ref[:, cs])
        z = r * sp[:, cs]
        a = jnp.exp(-z)
        a_scr[:, cs] = a
        u = jnp.tanh(z) * (1.0 + a * a)
        b_scr[:, cs] = jnp.where(u > 0.0, u * lax.rsqrt(u), 0.0) * (i * xc[:, cs])


SEG = TM // SUBLANES


def _seg_perm(transpose):
    rr = lax.broadcasted_iota(jnp.int32, (TM, TM), 0)
    cc = lax.broadcasted_iota(jnp.int32, (TM, TM), 1)
    if transpose:
        rr, cc = cc, rr
    hit = jnp.logical_and(cc // SEG == rr % SUBLANES, cc % SEG == rr // SUBLANES)
    return jnp.where(hit, 1.0, 0.0).astype(BF16)


def _rg_scan(a_scr, b_scr, h_scr, c_scr, h0, reverse):
    def body(i, carry):
        hh, cc = carry
        grp = (SEG - 1 - i) if reverse else i
        r0 = pl.multiple_of(grp * SUBLANES, SUBLANES)
        a = a_scr[pl.ds(r0, SUBLANES), :]
        hh = a * hh + b_scr[pl.ds(r0, SUBLANES), :]
        cc = a * cc
        h_scr[pl.ds(r0, SUBLANES), :] = hh
        c_scr[pl.ds(r0, SUBLANES), :] = cc
        return hh, cc

    init = (jnp.zeros((SUBLANES, D_RNN), F32), jnp.ones((SUBLANES, D_RNN), F32))
    b, a = lax.fori_loop(0, SEG, body, init, unroll=4)
    row = lax.broadcasted_iota(jnp.int32, (SUBLANES, D_RNN), 0)
    for s in (1, 2, 4):
        shift = (SUBLANES - s) if reverse else s
        valid = (row < SUBLANES - s) if reverse else (row >= s)
        a_s = pltpu.roll(a, shift, 0)
        b_s = pltpu.roll(b, shift, 0)
        b = jnp.where(valid, a * b_s + b, b)
        a = jnp.where(valid, a * a_s, a)
    after = a * h0 + b
    edge = (row == SUBLANES - 1) if reverse else (row == 0)
    enter = jnp.where(edge, h0, pltpu.roll(after, (SUBLANES - 1) if reverse else 1, 0))
    h = h_scr[...] + c_scr[...] * jnp.concatenate([enter] * SEG, axis=0)
    out = after[0:1, :] if reverse else after[SUBLANES - 1:SUBLANES, :]
    return h, out


def _od1_kernel(x_ref, xn_ref, mod_ref, ng_ref, win_ref, cw_ref, cb_ref, wg_ref, ba_ref, bx_ref, lam_ref,
                s0_ref, w1f_ref, w2f_ref, mid_ref, sfin_ref, w1b_ref, w2b_ref,
                xb_scr, a_scr, b_scr, h_scr, c_scr, hc_scr, tail_scr):
    t = pl.program_id(0)
    _, _, cidx, first, last, _ = _tile_info(t)
    _cast_slabs(w1f_ref, w2f_ref, w1b_ref, w2b_ref)

    @pl.when(first)
    def _():
        tail_scr[...] = jnp.zeros((2 * SUBLANES, D_RNN), F32)
        hc_scr[...] = s0_ref[0, 0]

    sh1, sc1, _, _, _, _ = _mod_rows(mod_ref, cidx)
    xe = jnp.concatenate([x_ref[...], xn_ref[...]], axis=0)
    hm = (_rms(xe, ng_ref[0:1, :]) * (1.0 + sc1) + sh1).astype(BF16)
    hmp = _dot(_seg_perm(False), hm[0:TM, :]).astype(BF16)
    proj = _dot(jnp.concatenate([hmp, hm[TM:TM + HALO, :]], axis=0), win_ref[...])

    xb = proj[0:TM, 0:D_RNN]
    row = lax.broadcasted_iota(jnp.int32, (SUBLANES, D_RNN), 0)
    g30 = xb[TM - 2 * SUBLANES:TM - SUBLANES, :]
    g31 = xb[TM - SUBLANES:TM, :]
    nxt = jnp.where(last, 0.0, proj[TM:TM + 1, 0:D_RNN])
    xb_scr[0:SUBLANES, :] = jnp.where(row == 0, pltpu.roll(tail_scr[0:SUBLANES, :], 1, 0), pltpu.roll(g30, 1, 0))
    xb_scr[SUBLANES:2 * SUBLANES, :] = jnp.where(
        row == 0, pltpu.roll(tail_scr[SUBLANES:2 * SUBLANES, :], 1, 0), pltpu.roll(g31, 1, 0))
    xb_scr[2 * SUBLANES:2 * SUBLANES + TM, :] = xb
    xb_scr[2 * SUBLANES + TM:3 * SUBLANES + TM, :] = jnp.where(
        row == SUBLANES - 1, nxt, pltpu.roll(xb[0:SUBLANES, :], SUBLANES - 1, 0))
    tail_scr[0:SUBLANES, :] = g30
    tail_scr[SUBLANES:2 * SUBLANES, :] = g31
    xc = cb_ref[...]
    for j in range(4):
        xc = xc + cw_ref[j:j + 1, :] * xb_scr[j * SUBLANES:j * SUBLANES + TM, :]

    _rg_gates(xc, wg_ref, ba_ref, bx_ref, lam_ref, a_scr, b_scr)
    h, hc_scr[...] = _rg_scan(a_scr, b_scr, h_scr, c_scr, hc_scr[...], reverse=False)

    mid_ref[:, 0:D_RNN] = h
    mid_ref[:, D_RNN:2 * D_RNN] = jax.nn.gelu(proj[0:TM, D_RNN:2 * D_RNN])
    mid_ref[:, 2 * D_RNN:3 * D_RNN] = xc

    @pl.when(last)
    def _():
        sfin_ref[0, 0] = hc_scr[...]


def _odd_forward(x, mod, ng, w, w1f, w2f, layer):
    nb16 = TM // HALO
    cast_in, cast_out, cast_shapes = _cast_specs(layer)
    return pl.pallas_call(
        _od1_kernel,
        out_shape=(jax.ShapeDtypeStruct((NT * TM, OD_MID), F32),
                   jax.ShapeDtypeStruct((NSEQ, 1, 1, D_RNN), F32), *cast_shapes),
        grid=(NT,),
        in_specs=[
            _tile_spec(D),
            pl.BlockSpec((HALO, D), lambda i: (jnp.minimum(i + 1, NT - 1) * nb16, 0)),
            _const_spec((NCOND, 6 * D)), _const_spec((4, D)),
            _const_spec((D, 2 * D_RNN)), _const_spec((4, D_RNN)), _const_spec((1, D_RNN)),
            _const_spec((D_RNN // SLAB, SLAB, 2 * SLAB)),
            _const_spec((1, D_RNN)), _const_spec((1, D_RNN)), _const_spec((1, D_RNN)),
            pl.BlockSpec((1, 1, 1, D_RNN), lambda i: (_seq_of(i), 0, 0, 0)),
            *cast_in,
        ],
        out_specs=(_tile_spec(OD_MID),
                   pl.BlockSpec((1, 1, 1, D_RNN), lambda i: (_seq_of(i), 0, 0, 0)), *cast_out),
        scratch_shapes=[
            pltpu.VMEM((3 * SUBLANES + TM, D_RNN), F32),
            pltpu.VMEM((TM, D_RNN), F32), pltpu.VMEM((TM, D_RNN), F32), pltpu.VMEM((TM, D_RNN), F32),
            pltpu.VMEM((TM, D_RNN), F32),
            pltpu.VMEM((1, D_RNN), F32), pltpu.VMEM((2 * SUBLANES, D_RNN), F32),
        ],
        compiler_params=_params(),
        name="odd_forward",
    )(x, x, mod, ng, w["win"], w["cw"], w["cb"], w["wg"][0], w["ba"][0:1], w["bx"][0:1], w["lam"][0:1],
      w["s0"], w1f, w2f)


def _od2_kernel(x_ref, mod_ref, ng_ref, mid_ref, wg_ref, ba_ref, bx_ref, lam_ref, wout_ref, s0_ref,
                x1_ref, sfin_ref, a_scr, b_scr, h_scr, c_scr, hc_scr):
    t = NT - 1 - pl.program_id(0)
    _, _, cidx, first, last, _ = _tile_info(t)
    _, _, g1, _, _, _ = _mod_rows(mod_ref, cidx)

    @pl.when(last)
    def _():
        hc_scr[...] = s0_ref[0, 0]

    _rg_gates(mid_ref[:, 2 * D_RNN:3 * D_RNN], wg_ref, ba_ref, bx_ref, lam_ref, a_scr, b_scr)
    h_b, hc_scr[...] = _rg_scan(a_scr, b_scr, h_scr, c_scr, hc_scr[...], reverse=True)

    zp = ((mid_ref[:, 0:D_RNN] + h_b) * mid_ref[:, D_RNN:2 * D_RNN]).astype(BF16)
    z = _dot(_seg_perm(True), zp).astype(BF16)
    y = _dot(z, wout_ref[...])
    x1_ref[...] = x_ref[...] + g1 * _rms(y, ng_ref[1:2, :])

    @pl.when(first)
    def _():
        sfin_ref[0, 0] = hc_scr[...]


def _odd_reverse(x, mod, ng, mid, w):
    return pl.pallas_call(
        _od2_kernel,
        out_shape=(jax.ShapeDtypeStruct((NT * TM, D), F32),
                   jax.ShapeDtypeStruct((NSEQ, 1, 1, D_RNN), F32)),
        grid=(NT,),
        in_specs=[
            _tile_spec(D, rev=True),
            _const_spec((NCOND, 6 * D)), _const_spec((4, D)),
            _tile_spec(OD_MID, rev=True),
            _const_spec((D_RNN // SLAB, SLAB, 2 * SLAB)),
            _const_spec((1, D_RNN)), _const_spec((1, D_RNN)), _const_spec((1, D_RNN)),
            _const_spec((D_RNN, D)),
            pl.BlockSpec((1, 1, 1, D_RNN), lambda i: (_seq_of(NT - 1 - i), 1, 0, 0)),
        ],
        out_specs=(_tile_spec(D, rev=True),
                   pl.BlockSpec((1, 1, 1, D_RNN), lambda i: (_seq_of(NT - 1 - i), 0, 0, 0))),
        scratch_shapes=[
            pltpu.VMEM((TM, D_RNN), F32), pltpu.VMEM((TM, D_RNN), F32), pltpu.VMEM((TM, D_RNN), F32),
            pltpu.VMEM((TM, D_RNN), F32),
            pltpu.VMEM((1, D_RNN), F32),
        ],
        compiler_params=_params(),
        name="odd_reverse",
    )(x, mod, ng, mid, w["wg"][1], w["ba"][1:2], w["bx"][1:2], w["lam"][1:2], w["wout"], w["s0"])


def _pos_tables():
    n = D // 4
    omega = 1.0 / (10000.0 ** (jnp.arange(n, dtype=F32) / n))
    idx = jnp.arange(GRID_W, dtype=F32)[:, None] * omega
    tab = jnp.concatenate([jnp.sin(idx), jnp.cos(idx)], axis=-1)
    return tab, tab


def _block_diag_slabs(w):
    per = SLAB // RG_BS
    w = w.reshape(D_RNN // SLAB, per, RG_BS, RG_BS)
    eye = jnp.eye(per, dtype=w.dtype)
    return jnp.einsum("spij,pq->spiqj", w, eye).reshape(D_RNN // SLAB, SLAB, SLAB)


def kernel(x_prompt, x_sample, c, state_gla, state_rglru, c_ctx, mod_w, mod_b, norm_g, mlp_w1, mlp_b1, mlp_w2,
           mlp_b2, ev_w_in, ev_w_out, sgu_ln_g, sgu_ln_b, sgu_ws, sgu_bs, gla_gate_w2, gla_gate_b, gla_norm_g,
           rg_w_in, rg_conv_w, rg_conv_b, rg_wa, rg_ba, rg_wx, rg_bx, rg_L, rg_w_out):
    assert x_prompt.shape == (BATCH, SEQ, D) and x_sample.shape == (DEC_BATCH, DEC_SEQ, D)
    assert SEQ == TM and DEC_SEQ % TM == 0 and DEPTH == 2
    xp = x_prompt.reshape(NTP * TM, D)
    xs = x_sample.reshape(NTS * TM, D)
    cond8 = jnp.concatenate([c_ctx[None, :], c, jnp.zeros((NCOND - 1 - DEC_BATCH, D), F32)], axis=0)
    mods = _modulation(cond8, mod_w, mod_b)
    rtab, ctab = _pos_tables()

    gmat = jnp.zeros((LANES, 2 * QK), F32)
    gmat = gmat.at[0:GLA_RANK, 0:QK].set(gla_gate_w2[0, 0])
    gmat = gmat.at[GLA_RANK:2 * GLA_RANK, QK:2 * QK].set(gla_gate_w2[0, 1])
    s0_gla = jnp.concatenate([jnp.zeros((BATCH,) + state_gla.shape[2:], F32), state_gla[:, 0]], axis=0)
    ev = {
        "win": ev_w_in[0, :, 0:EV_MAIN].astype(BF16),
        "wlr": jnp.pad(ev_w_in[0, :, EV_MAIN:], ((0, 0), (0, LANES - 2 * GLA_RANK))).astype(BF16),
        "gmat": gmat.astype(BF16),
        "gb": gla_gate_b[0].reshape(1, 2 * QK),
        "lng": sgu_ln_g[0].reshape(1, SGU_WIDTH),
        "lnb": sgu_ln_b[0].reshape(1, SGU_WIDTH),
        "ws": sgu_ws[0].astype(BF16),
        "bst": sgu_bs[0].T,
        "gn": gla_norm_g[0].reshape(1, VW),
        "wout": ev_w_out[0].astype(BF16),
        "p0": s0_gla,
    }
    mid, pf, w1b, w2b = _even_forward(xp, xs, rtab, ctab, mods[0], norm_g[0], ev, mlp_w1, mlp_w2, 0)
    x1, pb = _even_reverse(xp, xs, rtab, ctab, mods[0], norm_g[0], mid, ev)
    x2 = _mlp(x1, mods[0], norm_g[0], w1b, mlp_b1[0].reshape(1, D_FF), w2b, mlp_b2[0].reshape(1, D), 0, NT)
    new_gla = jnp.stack([pf[:BATCH, 0], pb[:BATCH, 0]], axis=1)[:, None]

    s0_rg = jnp.concatenate([jnp.zeros((BATCH, 2, D_RNN), F32), state_rglru[:, 0]], axis=0)
    od = {
        "win": rg_w_in[0].astype(BF16),
        "cw": rg_conv_w[0],
        "cb": rg_conv_b[0].reshape(1, D_RNN),
        "wg": jnp.stack([
            jnp.concatenate([_block_diag_slabs(rg_wa[0, d]), _block_diag_slabs(rg_wx[0, d])], axis=-1)
            for d in range(2)], axis=0).astype(BF16),
        "ba": rg_ba[0], "bx": rg_bx[0], "lam": rg_L[0],
        "wout": rg_w_out[0].astype(BF16),
        "s0": s0_rg.reshape(NSEQ, 2, 1, D_RNN),
    }
    mid1, sf, w1b, w2b = _odd_forward(x2, mods[1], norm_g[1], od, mlp_w1, mlp_w2, 1)
    x3, sb = _odd_reverse(x2, mods[1], norm_g[1], mid1, od)
    mlp1 = (mods[1], norm_g[1], w1b, mlp_b1[1].reshape(1, D_FF), w2b, mlp_b2[1].reshape(1, D))
    y_prompt = _mlp(x3, *mlp1, 0, NTP)
    y_sample = _mlp(x3, *mlp1, NTP, NTS)
    new_rg = jnp.stack([sf[:BATCH, 0, 0], sb[:BATCH, 0, 0]], axis=1)[:, None]
    return (y_prompt.reshape(BATCH, SEQ, D), y_sample.reshape(DEC_BATCH, DEC_SEQ, D), new_gla, new_rg)
```

```python
import functools

import jax
import jax.numpy as jnp
from jax import lax
from jax.experimental import pallas as pl
from jax.experimental.pallas import tpu as pltpu

D = 1024
BATCH = 16
SEQ = 256
DEPTH = 2
DEC_BATCH = 4
DEC_SEQ = 4096
GRID_W = 64
D_FF = 4 * D
EPS = 1e-6
SGU_CHUNK = 128
SGU_GROUPS = 4
SGU_WIDTH = D // 2
GLA_HEADS = 4
GLA_DV = 128
GLA_DK = 64
GLA_RANK = 16
GLA_NORMALIZER = 16.0
GLA_CHUNK = 64
QK = GLA_HEADS * GLA_DK
VW = GLA_HEADS * GLA_DV
EV_MAIN = 2 * SGU_WIDTH + 2 * QK + 2 * VW
D_RNN = D
RG_BLOCKS = 16
RG_BS = D_RNN // RG_BLOCKS
RG_C = 8.0
LANES = 128
SUBLANES = 8
SLAB = 256

TM = 256
NTP = BATCH * SEQ // TM
TPS = DEC_SEQ // TM
NTS = DEC_BATCH * TPS
NT = NTP + NTS
NSEQ = BATCH + DEC_BATCH
NCOND = 8
HALO = 16
EV_MID = 2 * VW + VW + 2 * QK + VW + QK
OD_MID = 3 * D_RNN
VMEM_LIMIT = 56 * 1024 * 1024

F32 = jnp.float32
BF16 = jnp.bfloat16


def _tile_info(t):
    is_p = t < NTP
    ts = jnp.maximum(t - NTP, 0)
    sq = ts // TPS
    within = ts % TPS
    cidx = jnp.where(is_p, 0, 1 + sq)
    first = jnp.logical_or(is_p, within == 0)
    last = jnp.logical_or(is_p, within == TPS - 1)
    seq = jnp.where(is_p, t, BATCH + sq)
    return is_p, within, cidx, first, last, seq


def _seq_of(t):
    return jnp.where(t < NTP, t, BATCH + jnp.maximum(t - NTP, 0) // TPS)


def _rms(x, g):
    return x * lax.rsqrt(jnp.mean(x * x, axis=-1, keepdims=True) + EPS) * g


def _dot(a, b):
    return jnp.dot(a, b, preferred_element_type=F32)


def _dot_nt(a, b):
    return lax.dot_general(a, b, (((1,), (1,)), ((), ())), preferred_element_type=F32)


def _dot_tn(a, b):
    return lax.dot_general(a, b, (((0,), (0,)), ((), ())), preferred_element_type=F32)


def _split3(x):
    hi = x.astype(BF16)
    r1 = x - hi.astype(F32)
    mid = r1.astype(BF16)
    lo = (r1 - mid.astype(F32)).astype(BF16)
    return hi, mid, lo


def _dot_exact_lhs(m, parts):
    return _dot(m, parts[0]) + _dot(m, parts[1]) + _dot(m, parts[2])


def _log_sigmoid(x):
    return jnp.minimum(x, 0.0) - jnp.log(1.0 + jnp.exp(-jnp.abs(x)))


def _softplus(x):
    return jnp.maximum(x, 0.0) + jnp.log(1.0 + jnp.exp(-jnp.abs(x)))


def _mod_rows(mod_ref, cidx):
    m = mod_ref[pl.ds(cidx, 1), :]
    return [m[:, j * D:(j + 1) * D] for j in range(6)]


def _load_x0(is_p, within, xp_ref, xs_ref, rtab_ref, ctab_ref):
    rows_per_tile = TM // GRID_W
    r0 = within * rows_per_tile
    posr = jnp.concatenate(
        [jnp.broadcast_to(rtab_ref[pl.ds(r0 + j, 1), :], (GRID_W, D // 2)) for j in range(rows_per_tile)],
        axis=0)
    posc = jnp.concatenate([ctab_ref[...]] * rows_per_tile, axis=0)
    pos = jnp.concatenate([posr, posc], axis=1)
    return jnp.where(is_p, xp_ref[...], xs_ref[...] + pos)


def _mod_kernel(cond_ref, w_ref, b_ref, o_ref):
    c = cond_ref[...]
    sc = (c * jax.nn.sigmoid(c)).astype(BF16)
    o_ref[0] = _dot(sc, w_ref[0].astype(BF16)) + b_ref[0]


def _modulation(cond8, mod_w, mod_b):
    nb = 6 * D // D
    return pl.pallas_call(
        _mod_kernel,
        out_shape=jax.ShapeDtypeStruct((DEPTH, NCOND, 6 * D), F32),
        grid=(DEPTH, nb),
        in_specs=[
            pl.BlockSpec((NCOND, D), lambda l, j: (0, 0)),
            pl.BlockSpec((1, D, D), lambda l, j: (l, 0, j)),
            pl.BlockSpec((1, 1, D), lambda l, j: (l, 0, j)),
        ],
        out_specs=pl.BlockSpec((1, NCOND, D), lambda l, j: (l, 0, j)),
        compiler_params=pltpu.CompilerParams(
            dimension_semantics=("arbitrary", "arbitrary"), vmem_limit_bytes=VMEM_LIMIT),
        name="modulation",
    )(cond8, mod_w, mod_b.reshape(DEPTH, 1, 6 * D))


def _chunk_rows(c):
    return slice(c * GLA_CHUNK, (c + 1) * GLA_CHUNK)


def _pair_lanes(p):
    return slice(p * LANES, (p + 1) * LANES)


def _gla_order(reverse):
    n = TM // GLA_CHUNK
    return list(reversed(range(n))) if reverse else list(range(n))


def _gla_prep(qs, k, la, reverse):
    n = TM // GLA_CHUNK
    ri = lax.broadcasted_iota(jnp.int32, (TM, TM), 0)
    ci = lax.broadcasted_iota(jnp.int32, (TM, TM), 1)
    same = (ri // GLA_CHUNK) == (ci // GLA_CHUNK)
    order = (ci >= ri) if reverse else (ci <= ri)
    cum_m = jnp.where(jnp.logical_and(same, order), 1.0, 0.0).astype(BF16)
    b = _dot_exact_lhs(cum_m, _split3(la))
    ends = [c * GLA_CHUNK if reverse else (c + 1) * GLA_CHUNK - 1 for c in range(n)]
    btot = jnp.concatenate([jnp.broadcast_to(b[e:e + 1, :], (GLA_CHUNK, QK)) for e in ends], axis=0)
    qe = (qs * jnp.exp(b)).astype(BF16)
    ke = k * jnp.exp(-b)
    kd = (k * jnp.exp(btot - b)).astype(BF16)
    dec = jnp.exp(btot)
    lane = lax.broadcasted_iota(jnp.int32, (TM, QK), 1) % LANES
    ke_h = (jnp.where(lane < GLA_DK, ke, 0.0).astype(BF16), jnp.where(lane >= GLA_DK, ke, 0.0).astype(BF16))
    return qe, ke_h, kd, dec


def _gla_products(prep, v_bf, reverse):
    qe, ke_h, kd, _ = prep
    n = TM // GLA_CHUNK
    cr = lax.broadcasted_iota(jnp.int32, (GLA_CHUNK, 2 * GLA_CHUNK), 0)
    cc = lax.broadcasted_iota(jnp.int32, (GLA_CHUNK, 2 * GLA_CHUNK), 1) % GLA_CHUNK
    cmask = (cc >= cr) if reverse else (cc <= cr)
    br = lax.broadcasted_iota(jnp.int32, (2 * GLA_DK, 2 * GLA_DV), 0)
    bc = lax.broadcasted_iota(jnp.int32, (2 * GLA_DK, 2 * GLA_DV), 1)
    diag = (br < GLA_DK) == (bc < GLA_DV)
    att = [[None] * n for _ in range(GLA_HEADS // 2)]
    ds = [[None] * n for _ in range(GLA_HEADS // 2)]
    for p in range(GLA_HEADS // 2):
        vp = v_bf[:, p * 2 * GLA_DV:(p + 1) * 2 * GLA_DV]
        for c in _gla_order(reverse):
            rs, ls = _chunk_rows(c), _pair_lanes(p)
            kk = jnp.concatenate([ke_h[0][rs, ls], ke_h[1][rs, ls]], axis=0)
            att[p][c] = jnp.where(cmask, _dot_nt(qe[rs, ls], kk), 0.0).astype(BF16)
            ds[p][c] = jnp.where(diag, _dot_tn(kd[rs, ls], vp[rs, :]), 0.0)
    return att, ds


def _gla_outputs(prep, prods, v_bf, p_scr, reverse):
    qe, _, _, dec = prep
    att, ds = prods
    n = TM // GLA_CHUNK
    half = TM // 2
    lane_v = lax.broadcasted_iota(jnp.int32, (TM, 2 * GLA_DV), 1)
    rows = [[None] * (GLA_HEADS // 2) for _ in range(n)]
    for p in range(GLA_HEADS // 2):
        ls = _pair_lanes(p)
        vp = v_bf[:, p * 2 * GLA_DV:(p + 1) * 2 * GLA_DV]
        vl = jnp.where(lane_v < GLA_DV, vp, jnp.zeros_like(vp))
        vr = jnp.where(lane_v >= GLA_DV, vp, jnp.zeros_like(vp))
        dec_t = (dec[0:half, ls].T, dec[half:TM, ls].T)
        s = p_scr[p]
        for c in _gla_order(reverse):
            rs = _chunk_rows(c)
            rhs = jnp.concatenate([vl[rs, :], vr[rs, :], s.astype(BF16)], axis=0)
            rows[c][p] = _dot(jnp.concatenate([att[p][c], qe[rs, ls]], axis=1), rhs)
            col = (c % 2) * GLA_CHUNK
            s = dec_t[(c * GLA_CHUNK) // half][:, col:col + 1] * s + ds[p][c]
        p_scr[p] = s
    return jnp.concatenate([jnp.concatenate(r, axis=1) for r in rows], axis=0)


def _gla_state_load(s_ref, p_scr):
    p_scr[...] = jnp.zeros(p_scr.shape, F32)
    for h in range(GLA_HEADS):
        r0, c0 = (h % 2) * GLA_DK, (h % 2) * GLA_DV
        p_scr[h // 2, r0:r0 + GLA_DK, c0:c0 + GLA_DV] = s_ref[0, 0, h]


def _gla_state_store(p_scr, s_ref):
    for h in range(GLA_HEADS):
        r0, c0 = (h % 2) * GLA_DK, (h % 2) * GLA_DV
        s_ref[0, 0, h] = p_scr[h // 2, r0:r0 + GLA_DK, c0:c0 + GLA_DV]


def _ev1_kernel(xp_ref, xs_ref, rtab_ref, ctab_ref, mod_ref, ng_ref, win_ref, wlr_ref, gmat_ref, gb_ref,
                lng_ref, lnb_ref, ws_ref, bst_ref, p0_ref, w1f_ref, w2f_ref,
                mid_ref, pfin_ref, w1b_ref, w2b_ref, p_scr):
    t = pl.program_id(0)
    is_p, within, cidx, first, last, _ = _tile_info(t)
    _cast_slabs(w1f_ref, w2f_ref, w1b_ref, w2b_ref)

    @pl.when(first)
    def _():
        _gla_state_load(p0_ref, p_scr)

    x = _load_x0(is_p, within, xp_ref, xs_ref, rtab_ref, ctab_ref)
    sh1, sc1, _, _, _, _ = _mod_rows(mod_ref, cidx)
    hm = (_rms(x, ng_ref[0:1, :]) * (1.0 + sc1) + sh1).astype(BF16)
    o0 = 2 * SGU_WIDTH
    lr = _dot(hm, wlr_ref[...]).astype(BF16)
    pg = _dot(hm, win_ref[:, o0:EV_MAIN])
    la_all = _log_sigmoid(_dot(lr, gmat_ref[...]) + gb_ref[...]) * (1.0 / GLA_NORMALIZER)
    ps = _dot(hm, win_ref[:, 0:o0])
    qs = pg[:, 0:QK] * (GLA_DK ** -0.5)
    k = pg[:, QK:2 * QK]
    v = pg[:, 2 * QK:2 * QK + VW]
    g = pg[:, 2 * QK + VW:2 * QK + 2 * VW]
    v_bf = v.astype(BF16)
    prep = _gla_prep(qs, k, la_all[:, 0:QK], reverse=False)
    prods = _gla_products(prep, v_bf, reverse=False)

    u = jax.nn.gelu(ps[:, 0:SGU_WIDTH])
    vg = jax.nn.gelu(ps[:, SGU_WIDTH:2 * SGU_WIDTH])
    mu = jnp.mean(vg, axis=-1, keepdims=True)
    vc = vg - mu
    vn = (vc * lax.rsqrt(jnp.mean(vc * vc, axis=-1, keepdims=True) + EPS) * lng_ref[...] + lnb_ref[...]).astype(BF16)
    gd = SGU_WIDTH // SGU_GROUPS
    nch = TM // SGU_CHUNK
    sv_cols = []
    for grp in range(SGU_GROUPS):
        vcat = jnp.concatenate(
            [vn[c * SGU_CHUNK:(c + 1) * SGU_CHUNK, grp * gd:(grp + 1) * gd] for c in range(nch)], axis=1)
        sg = _dot(ws_ref[grp], vcat) + bst_ref[:, grp:grp + 1]
        sv_cols.append(jnp.concatenate([sg[:, c * gd:(c + 1) * gd] for c in range(nch)], axis=0))
    out_a = u * jnp.concatenate(sv_cols, axis=1)

    o_f = _gla_outputs(prep, prods, v_bf, p_scr, reverse=False)
    mid_ref[:, 0:VW] = out_a
    mid_ref[:, VW:2 * VW] = o_f
    mid_ref[:, 2 * VW:3 * VW] = g * jax.nn.sigmoid(g)
    c0 = 3 * VW
    mid_ref[:, c0:c0 + QK] = qs
    mid_ref[:, c0 + QK:c0 + 2 * QK] = k
    mid_ref[:, c0 + 2 * QK:c0 + 2 * QK + VW] = v
    mid_ref[:, c0 + 2 * QK + VW:c0 + 3 * QK + VW] = la_all[:, QK:2 * QK]

    @pl.when(last)
    def _():
        _gla_state_store(p_scr, pfin_ref)


def _const_spec(shape):
    nd = len(shape)
    return pl.BlockSpec(shape, lambda i, _nd=nd: (0,) * _nd, pipeline_mode=pl.Buffered(1))


def _xp_spec(rev):
    if rev:
        return pl.BlockSpec((TM, D), lambda i: (jnp.minimum(NT - 1 - i, NTP - 1), 0))
    return pl.BlockSpec((TM, D), lambda i: (jnp.minimum(i, NTP - 1), 0))


def _xs_spec(rev):
    if rev:
        return pl.BlockSpec((TM, D), lambda i: (jnp.maximum(NT - 1 - i - NTP, 0), 0))
    return pl.BlockSpec((TM, D), lambda i: (jnp.maximum(i - NTP, 0), 0))


def _tile_spec(width, rev=False):
    if rev:
        return pl.BlockSpec((TM, width), lambda i: (NT - 1 - i, 0))
    return pl.BlockSpec((TM, width), lambda i: (i, 0))


def _params():
    return pltpu.CompilerParams(dimension_semantics=("arbitrary",), vmem_limit_bytes=VMEM_LIMIT)


CAST_STEPS = 64


def _cast_specs(layer):
    r1, r2 = D // CAST_STEPS, D_FF // CAST_STEPS
    in_specs = [pl.BlockSpec((1, r1, D_FF), lambda i: (layer, jnp.minimum(i, CAST_STEPS - 1), 0)),
                pl.BlockSpec((1, r2, D), lambda i: (layer, jnp.minimum(i, CAST_STEPS - 1), 0))]
    out_specs = [pl.BlockSpec((r1, D_FF), lambda i: (jnp.minimum(i, CAST_STEPS - 1), 0)),
                 pl.BlockSpec((r2, D), lambda i: (jnp.minimum(i, CAST_STEPS - 1), 0))]
    shapes = [jax.ShapeDtypeStruct((D, D_FF), BF16), jax.ShapeDtypeStruct((D_FF, D), BF16)]
    return in_specs, out_specs, shapes


def _cast_slabs(w1f_ref, w2f_ref, w1b_ref, w2b_ref):
    w1b_ref[...] = w1f_ref[0].astype(BF16)
    w2b_ref[...] = w2f_ref[0].astype(BF16)


def _even_forward(xp, xs, rtab, ctab, mod, ng, w, w1f, w2f, layer):
    state_blk = (1, 1, GLA_HEADS, GLA_DK, GLA_DV)
    cast_in, cast_out, cast_shapes = _cast_specs(layer)
    return pl.pallas_call(
        _ev1_kernel,
        out_shape=(jax.ShapeDtypeStruct((NT * TM, EV_MID), F32),
                   jax.ShapeDtypeStruct((NSEQ, 1) + state_blk[2:], F32), *cast_shapes),
        grid=(NT,),
        in_specs=[
            _xp_spec(False), _xs_spec(False),
            _const_spec((GRID_W, D // 2)), _const_spec((GRID_W, D // 2)),
            _const_spec((NCOND, 6 * D)), _const_spec((4, D)),
            _const_spec((D, EV_MAIN)), _const_spec((D, LANES)), _const_spec((LANES, 2 * QK)),
            _const_spec((1, 2 * QK)),
            _const_spec((1, SGU_WIDTH)), _const_spec((1, SGU_WIDTH)),
            _const_spec((SGU_GROUPS, SGU_CHUNK, SGU_CHUNK)), _const_spec((SGU_CHUNK, SGU_GROUPS)),
            pl.BlockSpec(state_blk, lambda i: (_seq_of(i), 0, 0, 0, 0)),
            *cast_in,
        ],
        out_specs=(_tile_spec(EV_MID),
                   pl.BlockSpec(state_blk, lambda i: (_seq_of(i), 0, 0, 0, 0)), *cast_out),
        scratch_shapes=[pltpu.VMEM((GLA_HEADS // 2, 2 * GLA_DK, 2 * GLA_DV), F32)],
        compiler_params=_params(),
        name="even_forward",
    )(xp, xs, rtab, ctab, mod, ng, w["win"], w["wlr"], w["gmat"], w["gb"], w["lng"], w["lnb"],
      w["ws"], w["bst"], w["p0"], w1f, w2f)


def _ev2_kernel(xp_ref, xs_ref, rtab_ref, ctab_ref, mod_ref, ng_ref, mid_ref, gn_ref, wout_ref, p0_ref,
                x1_ref, pfin_ref, p_scr):
    t = NT - 1 - pl.program_id(0)
    is_p, within, cidx, first, last, _ = _tile_info(t)

    @pl.when(last)
    def _():
        _gla_state_load(p0_ref, p_scr)

    x = _load_x0(is_p, within, xp_ref, xs_ref, rtab_ref, ctab_ref)
    _, _, g1, _, _, _ = _mod_rows(mod_ref, cidx)
    out_a = mid_ref[:, 0:VW]
    o_f = mid_ref[:, VW:2 * VW]
    sg = mid_ref[:, 2 * VW:3 * VW]
    c0 = 3 * VW
    qs = mid_ref[:, c0:c0 + QK]
    k = mid_ref[:, c0 + QK:c0 + 2 * QK]
    v = mid_ref[:, c0 + 2 * QK:c0 + 2 * QK + VW]
    la_b = mid_ref[:, c0 + 2 * QK + VW:c0 + 3 * QK + VW]

    v_bf = v.astype(BF16)
    prep = _gla_prep(qs, k, la_b, reverse=True)
    o = o_f + _gla_outputs(prep, _gla_products(prep, v_bf, reverse=True), v_bf, p_scr, reverse=True)
    heads = []
    for h in range(GLA_HEADS):
        oh = o[:, h * GLA_DV:(h + 1) * GLA_DV]
        heads.append(oh * lax.rsqrt(jnp.mean(oh * oh, axis=-1, keepdims=True) + EPS))
    on = jnp.concatenate(heads, axis=1) * gn_ref[...] * sg
    cat = jnp.concatenate([out_a, on], axis=1).astype(BF16)
    y = _dot(cat, wout_ref[...])
    x1_ref[...] = x + g1 * _rms(y, ng_ref[1:2, :])

    @pl.when(first)
    def _():
        _gla_state_store(p_scr, pfin_ref)


def _even_reverse(xp, xs, rtab, ctab, mod, ng, mid, w):
    state_blk = (1, 1, GLA_HEADS, GLA_DK, GLA_DV)
    return pl.pallas_call(
        _ev2_kernel,
        out_shape=(jax.ShapeDtypeStruct((NT * TM, D), F32),
                   jax.ShapeDtypeStruct((NSEQ, 1) + state_blk[2:], F32)),
        grid=(NT,),
        in_specs=[
            _xp_spec(True), _xs_spec(True),
            _const_spec((GRID_W, D // 2)), _const_spec((GRID_W, D // 2)),
            _const_spec((NCOND, 6 * D)), _const_spec((4, D)),
            _tile_spec(EV_MID, rev=True),
            _const_spec((1, VW)), _const_spec((2 * VW, D)),
            pl.BlockSpec(state_blk, lambda i: (_seq_of(NT - 1 - i), 1, 0, 0, 0)),
        ],
        out_specs=(_tile_spec(D, rev=True),
                   pl.BlockSpec(state_blk, lambda i: (_seq_of(NT - 1 - i), 0, 0, 0, 0))),
        scratch_shapes=[pltpu.VMEM((GLA_HEADS // 2, 2 * GLA_DK, 2 * GLA_DV), F32)],
        compiler_params=_params(),
        name="even_reverse",
    )(xp, xs, rtab, ctab, mod, ng, mid, w["gn"], w["wout"], w["p0"])


FF_CHUNK = 1024
MLP_TILES = 2
TMM = MLP_TILES * TM


def _mlp_kernel(x_ref, mod_ref, ng_ref, w1_ref, b1_ref, w2_ref, b2_ref, o_ref, *, tile0):
    _, _, cidx, _, _, _ = _tile_info(tile0 + pl.program_id(0) * MLP_TILES)
    _, _, _, sh2, sc2, g2 = _mod_rows(mod_ref, cidx)
    x = x_ref[...]
    hff = (_rms(x, ng_ref[2:3, :]) * (1.0 + sc2) + sh2).astype(BF16)
    acc = b2_ref[...]
    for j in range(D_FF // FF_CHUNK):
        cs = slice(j * FF_CHUNK, (j + 1) * FF_CHUNK)
        h = _dot(hff, w1_ref[:, cs]) + b1_ref[:, cs]
        h = jnp.square(jnp.maximum(h, 0.0)).astype(BF16)
        acc = acc + _dot(h, w2_ref[cs, :])
    o_ref[...] = x + g2 * _rms(acc, ng_ref[3:4, :])


def _mlp(x, mod, ng, w1, b1, w2, b2, tile0, ntiles):
    assert tile0 % MLP_TILES == 0 and ntiles % MLP_TILES == 0 and TPS % MLP_TILES == 0 and NTP % MLP_TILES == 0
    b0 = tile0 // MLP_TILES
    return pl.pallas_call(
        functools.partial(_mlp_kernel, tile0=tile0),
        out_shape=jax.ShapeDtypeStruct((ntiles * TM, D), F32),
        grid=(ntiles // MLP_TILES,),
        in_specs=[
            pl.BlockSpec((TMM, D), lambda i: (b0 + i, 0)),
            _const_spec((NCOND, 6 * D)), _const_spec((4, D)),
            _const_spec((D, D_FF)), _const_spec((1, D_FF)), _const_spec((D_FF, D)), _const_spec((1, D)),
        ],
        out_specs=pl.BlockSpec((TMM, D), lambda i: (i, 0)),
        compiler_params=_params(), name="mlp",
    )(x, mod, ng, w1, b1, w2, b2)


def _rg_gates(xc, wg_ref, ba_ref, bx_ref, lam_ref, a_scr, b_scr, slabs=range(D_RNN // SLAB)):
    xcb = xc.astype(BF16)
    sp = RG_C * _softplus(-lam_ref[...])
    for s in slabs:
        cs = slice(s * SLAB, (s + 1) * SLAB)
        pre = _dot(xcb[:, cs], wg_ref[s])
        r = jax.nn.sigmoid(pre[:, 0:SLAB] + ba_ref[:, cs])
        i = jax.nn.sigmoid(pre[:, SLAB:2 * SLAB] + bx_ref[:, cs])
        z = r * sp[:, cs]
        a = jnp.exp(-z)
        a_scr[:, cs] = a
        u = jnp.tanh(z) * (1.0 + a * a)
        b_scr[:, cs] = jnp.where(u > 0.0, u * lax.rsqrt(u), 0.0) * (i * xc[:, cs])


SEG = TM // SUBLANES


def _seg_perm(transpose):
    rr = lax.broadcasted_iota(jnp.int32, (TM, TM), 0)
    cc = lax.broadcasted_iota(jnp.int32, (TM, TM), 1)
    if transpose:
        rr, cc = cc, rr
    hit = jnp.logical_and(cc // SEG == rr % SUBLANES, cc % SEG == rr // SUBLANES)
    return jnp.where(hit, 1.0, 0.0).astype(BF16)


def _rg_scan(a_scr, b_scr, h_scr, c_scr, h0, reverse):
    def body(i, carry):
        hh, cc = carry
        grp = (SEG - 1 - i) if reverse else i
        r0 = pl.multiple_of(grp * SUBLANES, SUBLANES)
        a = a_scr[pl.ds(r0, SUBLANES), :]
        hh = a * hh + b_scr[pl.ds(r0, SUBLANES), :]
        cc = a * cc
        h_scr[pl.ds(r0, SUBLANES), :] = hh
        c_scr[pl.ds(r0, SUBLANES), :] = cc
        return hh, cc

    init = (jnp.zeros((SUBLANES, D_RNN), F32), jnp.ones((SUBLANES, D_RNN), F32))
    b, a = lax.fori_loop(0, SEG, body, init, unroll=4)
    row = lax.broadcasted_iota(jnp.int32, (SUBLANES, D_RNN), 0)
    for s in (1, 2, 4):
        shift = (SUBLANES - s) if reverse else s
        valid = (row < SUBLANES - s) if reverse else (row >= s)
        a_s = pltpu.roll(a, shift, 0)
        b_s = pltpu.roll(b, shift, 0)
        b = jnp.where(valid, a * b_s + b, b)
        a = jnp.where(valid, a * a_s, a)
    after = a * h0 + b
    edge = (row == SUBLANES - 1) if reverse else (row == 0)
    enter = jnp.where(edge, h0, pltpu.roll(after, (SUBLANES - 1) if reverse else 1, 0))
    h = h_scr[...] + c_scr[...] * jnp.concatenate([enter] * SEG, axis=0)
    out = after[0:1, :] if reverse else after[SUBLANES - 1:SUBLANES, :]
    return h, out


def _od1_kernel(x_ref, xn_ref, mod_ref, ng_ref, win_ref, cw_ref, cb_ref, wg_ref, ba_ref, bx_ref, lam_ref,
                s0_ref, w1f_ref, w2f_ref, mid_ref, sfin_ref, w1b_ref, w2b_ref,
                proj_scr, xb_scr, a_scr, b_scr, h_scr, c_scr, hc_scr, tail_scr):
    i = pl.program_id(0)
    _, _, cidx, _, _, _ = _tile_info(jnp.minimum(i, NT - 1))
    _, _, _, first, last, _ = _tile_info(jnp.maximum(i - 1, 0))
    _cast_slabs(w1f_ref, w2f_ref, w1b_ref, w2b_ref)

    @pl.when(i == 0)
    def _():
        proj_scr[...] = jnp.zeros(proj_scr.shape, F32)

    @pl.when(first)
    def _():
        tail_scr[...] = jnp.zeros((2 * SUBLANES, D_RNN), F32)
        hc_scr[...] = s0_ref[0, 0]

    nslab = D_RNN // SLAB
    pw = 2 * D_RNN // nslab

    def project(lhs, s):
        proj_scr[:, s * pw:(s + 1) * pw] = _dot(lhs, win_ref[:, s * pw:(s + 1) * pw])

    sh1, sc1, _, _, _, _ = _mod_rows(mod_ref, cidx)
    xe = jnp.concatenate([x_ref[...], xn_ref[...]], axis=0)
    hm = (_rms(xe, ng_ref[0:1, :]) * (1.0 + sc1) + sh1).astype(BF16)
    hmp = _dot(_seg_perm(False), hm[0:TM, :]).astype(BF16)
    lhs = jnp.concatenate([hmp, hm[TM:TM + HALO, :]], axis=0)

    row = lax.broadcasted_iota(jnp.int32, (SUBLANES, D_RNN), 0)
    g30 = proj_scr[TM - 2 * SUBLANES:TM - SUBLANES, 0:D_RNN]
    g31 = proj_scr[TM - SUBLANES:TM, 0:D_RNN]
    g0 = proj_scr[0:SUBLANES, 0:D_RNN]
    nxt = jnp.where(last, 0.0, proj_scr[TM:TM + 1, 0:D_RNN])
    xb_scr[0:SUBLANES, :] = jnp.where(row == 0, pltpu.roll(tail_scr[0:SUBLANES, :], 1, 0), pltpu.roll(g30, 1, 0))
    xb_scr[SUBLANES:2 * SUBLANES, :] = jnp.where(
        row == 0, pltpu.roll(tail_scr[SUBLANES:2 * SUBLANES, :], 1, 0), pltpu.roll(g31, 1, 0))
    xb_scr[2 * SUBLANES:2 * SUBLANES + TM, :] = proj_scr[0:TM, 0:D_RNN]
    xb_scr[2 * SUBLANES + TM:3 * SUBLANES + TM, :] = jnp.where(
        row == SUBLANES - 1, nxt, pltpu.roll(g0, SUBLANES - 1, 0))
    tail_scr[0:SUBLANES, :] = g30
    tail_scr[SUBLANES:2 * SUBLANES, :] = g31
    xc = cb_ref[...]
    for j in range(4):
        xc = xc + cw_ref[j:j + 1, :] * xb_scr[j * SUBLANES:j * SUBLANES + TM, :]
    mid_ref[:, 2 * D_RNN:3 * D_RNN] = xc

    project(lhs, 0)
    mid_ref[:, D_RNN:2 * D_RNN] = jax.nn.gelu(proj_scr[0:TM, D_RNN:2 * D_RNN])
    for s in range(nslab):
        if s + 1 < nslab:
            project(lhs, s + 1)
        _rg_gates(mid_ref[:, 2 * D_RNN:3 * D_RNN], wg_ref, ba_ref, bx_ref, lam_ref, a_scr, b_scr, range(s, s + 1))

    h, hc_scr[...] = _rg_scan(a_scr, b_scr, h_scr, c_scr, hc_scr[...], reverse=False)
    mid_ref[:, 0:D_RNN] = h

    @pl.when(last)
    def _():
        sfin_ref[0, 0] = hc_scr[...]


def _odd_forward(x, mod, ng, w, w1f, w2f, layer):
    nb16 = TM // HALO
    cast_in, cast_out, cast_shapes = _cast_specs(layer)

    def proj_tile(i):
        return jnp.minimum(i, NT - 1)

    def scan_tile(i):
        return jnp.maximum(i - 1, 0)

    return pl.pallas_call(
        _od1_kernel,
        out_shape=(jax.ShapeDtypeStruct((NT * TM, OD_MID), F32),
                   jax.ShapeDtypeStruct((NSEQ, 1, 1, D_RNN), F32), *cast_shapes),
        grid=(NT + 1,),
        in_specs=[
            pl.BlockSpec((TM, D), lambda i: (proj_tile(i), 0)),
            pl.BlockSpec((HALO, D), lambda i: (jnp.minimum(proj_tile(i) + 1, NT - 1) * nb16, 0)),
            _const_spec((NCOND, 6 * D)), _const_spec((4, D)),
            _const_spec((D, 2 * D_RNN)), _const_spec((4, D_RNN)), _const_spec((1, D_RNN)),
            _const_spec((D_RNN // SLAB, SLAB, 2 * SLAB)),
            _const_spec((1, D_RNN)), _const_spec((1, D_RNN)), _const_spec((1, D_RNN)),
            pl.BlockSpec((1, 1, 1, D_RNN), lambda i: (_seq_of(scan_tile(i)), 0, 0, 0)),
            *cast_in,
        ],
        out_specs=(pl.BlockSpec((TM, OD_MID), lambda i: (scan_tile(i), 0)),
                   pl.BlockSpec((1, 1, 1, D_RNN), lambda i: (_seq_of(scan_tile(i)), 0, 0, 0)), *cast_out),
        scratch_shapes=[
            pltpu.VMEM((TM + HALO, 2 * D_RNN), F32),
            pltpu.VMEM((3 * SUBLANES + TM, D_RNN), F32),
            pltpu.VMEM((TM, D_RNN), F32), pltpu.VMEM((TM, D_RNN), F32), pltpu.VMEM((TM, D_RNN), F32),
            pltpu.VMEM((TM, D_RNN), F32),
            pltpu.VMEM((1, D_RNN), F32), pltpu.VMEM((2 * SUBLANES, D_RNN), F32),
        ],
        compiler_params=_params(),
        name="odd_forward",
    )(x, x, mod, ng, w["win"], w["cw"], w["cb"], w["wg"][0], w["ba"][0:1], w["bx"][0:1], w["lam"][0:1],
      w["s0"], w1f, w2f)


def _od2_kernel(x_ref, mod_ref, ng_ref, mid_ref, wg_ref, ba_ref, bx_ref, lam_ref, wout_ref, s0_ref,
                x1_ref, sfin_ref, a_scr, b_scr, h_scr, c_scr, hc_scr, zp_scr):
    i = pl.program_id(0)
    live = i < NT
    _, _, _, first, last, _ = _tile_info(NT - 1 - jnp.minimum(i, NT - 1))
    _, _, cidx_prev, _, _, _ = _tile_info(NT - 1 - jnp.maximum(i - 1, 0))
    _, _, g1, _, _, _ = _mod_rows(mod_ref, cidx_prev)

    @pl.when(i == 0)
    def _():
        zp_scr[...] = jnp.zeros(zp_scr.shape, BF16)

    @pl.when(jnp.logical_and(last, live))
    def _():
        hc_scr[...] = s0_ref[0, 0]

    xc = mid_ref[:, 2 * D_RNN:3 * D_RNN]
    nslab = D_RNN // SLAB
    z = _dot(_seg_perm(True), zp_scr[...]).astype(BF16)
    ys = []
    for s in range(nslab):
        ys.append(_dot(z, wout_ref[:, s * SLAB:(s + 1) * SLAB]))
        _rg_gates(xc, wg_ref, ba_ref, bx_ref, lam_ref, a_scr, b_scr, range(s, s + 1))
    x1_ref[...] = x_ref[...] + g1 * _rms(jnp.concatenate(ys, axis=1), ng_ref[1:2, :])
    h_b, hc_scr[...] = _rg_scan(a_scr, b_scr, h_scr, c_scr, hc_scr[...], reverse=True)
    zp_scr[...] = ((mid_ref[:, 0:D_RNN] + h_b) * mid_ref[:, D_RNN:2 * D_RNN]).astype(BF16)

    @pl.when(jnp.logical_and(first, live))
    def _():
        sfin_ref[0, 0] = hc_scr[...]


def _odd_reverse(x, mod, ng, mid, w):
    def scan_tile(i):
        return NT - 1 - jnp.minimum(i, NT - 1)

    def out_tile(i):
        return NT - 1 - jnp.maximum(i - 1, 0)

    return pl.pallas_call(
        _od2_kernel,
        out_shape=(jax.ShapeDtypeStruct((NT * TM, D), F32),
                   jax.ShapeDtypeStruct((NSEQ, 1, 1, D_RNN), F32)),
        grid=(NT + 1,),
        in_specs=[
            pl.BlockSpec((TM, D), lambda i: (out_tile(i), 0)),
            _const_spec((NCOND, 6 * D)), _const_spec((4, D)),
            pl.BlockSpec((TM, OD_MID), lambda i: (scan_tile(i), 0)),
            _const_spec((D_RNN // SLAB, SLAB, 2 * SLAB)),
            _const_spec((1, D_RNN)), _const_spec((1, D_RNN)), _const_spec((1, D_RNN)),
            _const_spec((D_RNN, D)),
            pl.BlockSpec((1, 1, 1, D_RNN), lambda i: (_seq_of(scan_tile(i)), 1, 0, 0)),
        ],
        out_specs=(pl.BlockSpec((TM, D), lambda i: (out_tile(i), 0)),
                   pl.BlockSpec((1, 1, 1, D_RNN), lambda i: (_seq_of(scan_tile(i)), 0, 0, 0))),
        scratch_shapes=[
            pltpu.VMEM((TM, D_RNN), F32), pltpu.VMEM((TM, D_RNN), F32), pltpu.VMEM((TM, D_RNN), F32),
            pltpu.VMEM((TM, D_RNN), F32),
            pltpu.VMEM((1, D_RNN), F32), pltpu.VMEM((TM, D_RNN), BF16),
        ],
        compiler_params=_params(),
        name="odd_reverse",
    )(x, mod, ng, mid, w["wg"][1], w["ba"][1:2], w["bx"][1:2], w["lam"][1:2], w["wout"], w["s0"])


def _pos_tables():
    n = D // 4
    omega = 1.0 / (10000.0 ** (jnp.arange(n, dtype=F32) / n))
    idx = jnp.arange(GRID_W, dtype=F32)[:, None] * omega
    tab = jnp.concatenate([jnp.sin(idx), jnp.cos(idx)], axis=-1)
    return tab, tab


def _block_diag_slabs(w):
    per = SLAB // RG_BS
    w = w.reshape(D_RNN // SLAB, per, RG_BS, RG_BS)
    eye = jnp.eye(per, dtype=w.dtype)
    return jnp.einsum("spij,pq->spiqj", w, eye).reshape(D_RNN // SLAB, SLAB, SLAB)


def kernel(x_prompt, x_sample, c, state_gla, state_rglru, c_ctx, mod_w, mod_b, norm_g, mlp_w1, mlp_b1, mlp_w2,
           mlp_b2, ev_w_in, ev_w_out, sgu_ln_g, sgu_ln_b, sgu_ws, sgu_bs, gla_gate_w2, gla_gate_b, gla_norm_g,
           rg_w_in, rg_conv_w, rg_conv_b, rg_wa, rg_ba, rg_wx, rg_bx, rg_L, rg_w_out):
    assert x_prompt.shape == (BATCH, SEQ, D) and x_sample.shape == (DEC_BATCH, DEC_SEQ, D)
    assert SEQ == TM and DEC_SEQ % TM == 0 and DEPTH == 2
    xp = x_prompt.reshape(NTP * TM, D)
    xs = x_sample.reshape(NTS * TM, D)
    cond8 = jnp.concatenate([c_ctx[None, :], c, jnp.zeros((NCOND - 1 - DEC_BATCH, D), F32)], axis=0)
    mods = _modulation(cond8, mod_w, mod_b)
    rtab, ctab = _pos_tables()

    gmat = jnp.zeros((LANES, 2 * QK), F32)
    gmat = gmat.at[0:GLA_RANK, 0:QK].set(gla_gate_w2[0, 0])
    gmat = gmat.at[GLA_RANK:2 * GLA_RANK, QK:2 * QK].set(gla_gate_w2[0, 1])
    s0_gla = jnp.concatenate([jnp.zeros((BATCH,) + state_gla.shape[2:], F32), state_gla[:, 0]], axis=0)
    ev = {
        "win": ev_w_in[0, :, 0:EV_MAIN].astype(BF16),
        "wlr": jnp.pad(ev_w_in[0, :, EV_MAIN:], ((0, 0), (0, LANES - 2 * GLA_RANK))).astype(BF16),
        "gmat": gmat.astype(BF16),
        "gb": gla_gate_b[0].reshape(1, 2 * QK),
        "lng": sgu_ln_g[0].reshape(1, SGU_WIDTH),
        "lnb": sgu_ln_b[0].reshape(1, SGU_WIDTH),
        "ws": sgu_ws[0].astype(BF16),
        "bst": sgu_bs[0].T,
        "gn": gla_norm_g[0].reshape(1, VW),
        "wout": ev_w_out[0].astype(BF16),
        "p0": s0_gla,
    }
    mid, pf, w1b, w2b = _even_forward(xp, xs, rtab, ctab, mods[0], norm_g[0], ev, mlp_w1, mlp_w2, 0)
    x1, pb = _even_reverse(xp, xs, rtab, ctab, mods[0], norm_g[0], mid, ev)
    x2 = _mlp(x1, mods[0], norm_g[0], w1b, mlp_b1[0].reshape(1, D_FF), w2b, mlp_b2[0].reshape(1, D), 0, NT)
    new_gla = jnp.stack([pf[:BATCH, 0], pb[:BATCH, 0]], axis=1)[:, None]

    s0_rg = jnp.concatenate([jnp.zeros((BATCH, 2, D_RNN), F32), state_rglru[:, 0]], axis=0)
    od = {
        "win": rg_w_in[0].astype(BF16),
        "cw": rg_conv_w[0],
        "cb": rg_conv_b[0].reshape(1, D_RNN),
        "wg": jnp.stack([
            jnp.concatenate([_block_diag_slabs(rg_wa[0, d]), _block_diag_slabs(rg_wx[0, d])], axis=-1)
            for d in range(2)], axis=0).astype(BF16),
        "ba": rg_ba[0], "bx": rg_bx[0], "lam": rg_L[0],
        "wout": rg_w_out[0].astype(BF16),
        "s0": s0_rg.reshape(NSEQ, 2, 1, D_RNN),
    }
    mid1, sf, w1b, w2b = _odd_forward(x2, mods[1], norm_g[1], od, mlp_w1, mlp_w2, 1)
    x3, sb = _odd_reverse(x2, mods[1], norm_g[1], mid1, od)
    mlp1 = (mods[1], norm_g[1], w1b, mlp_b1[1].reshape(1, D_FF), w2b, mlp_b2[1].reshape(1, D))
    y_prompt = _mlp(x3, *mlp1, 0, NTP)
    y_sample = _mlp(x3, *mlp1, NTP, NTS)
    new_rg = jnp.stack([sf[:BATCH, 0, 0], sb[:BATCH, 0, 0]], axis=1)[:, None]
    return (y_prompt.reshape(BATCH, SEQ, D), y_sample.reshape(DEC_BATCH, DEC_SEQ, D), new_gla, new_rg)
```

```python
import functools

import jax
import jax.numpy as jnp
from jax import lax
from jax.experimental import pallas as pl
from jax.experimental.pallas import tpu as pltpu

D = 1024
BATCH = 16
SEQ = 256
DEPTH = 2
DEC_BATCH = 4
DEC_SEQ = 4096
GRID_W = 64
D_FF = 4 * D
EPS = 1e-6
SGU_CHUNK = 128
SGU_GROUPS = 4
SGU_WIDTH = D // 2
GLA_HEADS = 4
GLA_DV = 128
GLA_DK = 64
GLA_RANK = 16
GLA_NORMALIZER = 16.0
GLA_CHUNK = 64
QK = GLA_HEADS * GLA_DK
VW = GLA_HEADS * GLA_DV
EV_MAIN = 2 * SGU_WIDTH + 2 * QK + 2 * VW
D_RNN = D
RG_BLOCKS = 16
RG_BS = D_RNN // RG_BLOCKS
RG_C = 8.0
LANES = 128
SUBLANES = 8
SLAB = 256

TM = 256
NTP = BATCH * SEQ // TM
TPS = DEC_SEQ // TM
NTS = DEC_BATCH * TPS
NT = NTP + NTS
NSEQ = BATCH + DEC_BATCH
NCOND = 8
HALO = 16
EV_MID = 2 * VW + VW + 2 * QK + VW + QK
OD_MID = 3 * D_RNN
VMEM_LIMIT = 56 * 1024 * 1024

F32 = jnp.float32
BF16 = jnp.bfloat16


def _tile_info(t):
    is_p = t < NTP
    ts = jnp.maximum(t - NTP, 0)
    sq = ts // TPS
    within = ts % TPS
    cidx = jnp.where(is_p, 0, 1 + sq)
    first = jnp.logical_or(is_p, within == 0)
    last = jnp.logical_or(is_p, within == TPS - 1)
    seq = jnp.where(is_p, t, BATCH + sq)
    return is_p, within, cidx, first, last, seq


def _seq_of(t):
    return jnp.where(t < NTP, t, BATCH + jnp.maximum(t - NTP, 0) // TPS)


def _rms(x, g):
    return x * lax.rsqrt(jnp.mean(x * x, axis=-1, keepdims=True) + EPS) * g


def _dot(a, b):
    return jnp.dot(a, b, preferred_element_type=F32)


def _dot_nt(a, b):
    return lax.dot_general(a, b, (((1,), (1,)), ((), ())), preferred_element_type=F32)


def _dot_tn(a, b):
    return lax.dot_general(a, b, (((0,), (0,)), ((), ())), preferred_element_type=F32)


def _split3(x):
    hi = x.astype(BF16)
    r1 = x - hi.astype(F32)
    mid = r1.astype(BF16)
    lo = (r1 - mid.astype(F32)).astype(BF16)
    return hi, mid, lo


def _dot_exact_lhs(m, parts):
    return _dot(m, parts[0]) + _dot(m, parts[1]) + _dot(m, parts[2])


def _log_sigmoid(x):
    return jnp.minimum(x, 0.0) - jnp.log(1.0 + jnp.exp(-jnp.abs(x)))


def _softplus(x):
    return jnp.maximum(x, 0.0) + jnp.log(1.0 + jnp.exp(-jnp.abs(x)))


_GELU_C = 0.7978845608028654 * 1.4426950408889634


def _gelu(x):
    return x / (1.0 + jnp.exp2(x * ((-2.0 * _GELU_C) + (-2.0 * 0.044715 * _GELU_C) * (x * x))))


def _mod_rows(mod_ref, cidx):
    m = mod_ref[pl.ds(cidx, 1), :]
    return [m[:, j * D:(j + 1) * D] for j in range(6)]


def _load_x0(is_p, within, xp_ref, xs_ref, rtab_ref, ctab_ref):
    rows_per_tile = TM // GRID_W
    r0 = within * rows_per_tile
    posr = jnp.concatenate(
        [jnp.broadcast_to(rtab_ref[pl.ds(r0 + j, 1), :], (GRID_W, D // 2)) for j in range(rows_per_tile)],
        axis=0)
    posc = jnp.concatenate([ctab_ref[...]] * rows_per_tile, axis=0)
    pos = jnp.concatenate([posr, posc], axis=1)
    return jnp.where(is_p, xp_ref[...], xs_ref[...] + pos)


def _mod_kernel(cond_ref, w_ref, b_ref, o_ref):
    c = cond_ref[...]
    sc = (c * jax.nn.sigmoid(c)).astype(BF16)
    o_ref[0] = _dot(sc, w_ref[0].astype(BF16)) + b_ref[0]


def _modulation(cond8, mod_w, mod_b):
    nb = 6 * D // D
    return pl.pallas_call(
        _mod_kernel,
        out_shape=jax.ShapeDtypeStruct((DEPTH, NCOND, 6 * D), F32),
        grid=(DEPTH, nb),
        in_specs=[
            pl.BlockSpec((NCOND, D), lambda l, j: (0, 0)),
            pl.BlockSpec((1, D, D), lambda l, j: (l, 0, j)),
            pl.BlockSpec((1, 1, D), lambda l, j: (l, 0, j)),
        ],
        out_specs=pl.BlockSpec((1, NCOND, D), lambda l, j: (l, 0, j)),
        compiler_params=pltpu.CompilerParams(
            dimension_semantics=("arbitrary", "arbitrary"), vmem_limit_bytes=VMEM_LIMIT),
        name="modulation",
    )(cond8, mod_w, mod_b.reshape(DEPTH, 1, 6 * D))


def _chunk_rows(c):
    return slice(c * GLA_CHUNK, (c + 1) * GLA_CHUNK)


def _pair_lanes(p):
    return slice(p * LANES, (p + 1) * LANES)


def _gla_order(reverse):
    n = TM // GLA_CHUNK
    return list(reversed(range(n))) if reverse else list(range(n))


def _gla_prep(qs, k, la, reverse):
    n = TM // GLA_CHUNK
    ri = lax.broadcasted_iota(jnp.int32, (TM, TM), 0)
    ci = lax.broadcasted_iota(jnp.int32, (TM, TM), 1)
    same = (ri // GLA_CHUNK) == (ci // GLA_CHUNK)
    order = (ci >= ri) if reverse else (ci <= ri)
    cum_m = jnp.where(jnp.logical_and(same, order), 1.0, 0.0).astype(BF16)
    b = _dot_exact_lhs(cum_m, _split3(la))
    ends = [c * GLA_CHUNK if reverse else (c + 1) * GLA_CHUNK - 1 for c in range(n)]
    btot = jnp.concatenate([jnp.broadcast_to(b[e:e + 1, :], (GLA_CHUNK, QK)) for e in ends], axis=0)
    qe = (qs * jnp.exp(b)).astype(BF16)
    ke = k * jnp.exp(-b)
    kd = (k * jnp.exp(btot - b)).astype(BF16)
    dec = jnp.exp(btot)
    lane = lax.broadcasted_iota(jnp.int32, (TM, QK), 1) % LANES
    ke_h = (jnp.where(lane < GLA_DK, ke, 0.0).astype(BF16), jnp.where(lane >= GLA_DK, ke, 0.0).astype(BF16))
    return qe, ke_h, kd, dec


def _gla_products(prep, v_bf, reverse):
    qe, ke_h, kd, _ = prep
    n = TM // GLA_CHUNK
    cr = lax.broadcasted_iota(jnp.int32, (GLA_CHUNK, 2 * GLA_CHUNK), 0)
    cc = lax.broadcasted_iota(jnp.int32, (GLA_CHUNK, 2 * GLA_CHUNK), 1) % GLA_CHUNK
    cmask = (cc >= cr) if reverse else (cc <= cr)
    br = lax.broadcasted_iota(jnp.int32, (2 * GLA_DK, 2 * GLA_DV), 0)
    bc = lax.broadcasted_iota(jnp.int32, (2 * GLA_DK, 2 * GLA_DV), 1)
    diag = (br < GLA_DK) == (bc < GLA_DV)
    att = [[None] * n for _ in range(GLA_HEADS // 2)]
    ds = [[None] * n for _ in range(GLA_HEADS // 2)]
    for p in range(GLA_HEADS // 2):
        vp = v_bf[:, p * 2 * GLA_DV:(p + 1) * 2 * GLA_DV]
        for c in _gla_order(reverse):
            rs, ls = _chunk_rows(c), _pair_lanes(p)
            kk = jnp.concatenate([ke_h[0][rs, ls], ke_h[1][rs, ls]], axis=0)
            att[p][c] = jnp.where(cmask, _dot_nt(qe[rs, ls], kk), 0.0).astype(BF16)
            ds[p][c] = jnp.where(diag, _dot_tn(kd[rs, ls], vp[rs, :]), 0.0)
    return att, ds


def _gla_outputs(prep, prods, v_bf, p_scr, reverse):
    qe, _, _, dec = prep
    att, ds = prods
    n = TM // GLA_CHUNK
    half = TM // 2
    lane_v = lax.broadcasted_iota(jnp.int32, (TM, 2 * GLA_DV), 1)
    rows = [[None] * (GLA_HEADS // 2) for _ in range(n)]
    for p in range(GLA_HEADS // 2):
        ls = _pair_lanes(p)
        vp = v_bf[:, p * 2 * GLA_DV:(p + 1) * 2 * GLA_DV]
        vl = jnp.where(lane_v < GLA_DV, vp, jnp.zeros_like(vp))
        vr = jnp.where(lane_v >= GLA_DV, vp, jnp.zeros_like(vp))
        dec_t = (dec[0:half, ls].T, dec[half:TM, ls].T)
        s = p_scr[p]
        for c in _gla_order(reverse):
            rs = _chunk_rows(c)
            rhs = jnp.concatenate([vl[rs, :], vr[rs, :], s.astype(BF16)], axis=0)
            rows[c][p] = _dot(jnp.concatenate([att[p][c], qe[rs, ls]], axis=1), rhs)
            col = (c % 2) * GLA_CHUNK
            s = dec_t[(c * GLA_CHUNK) // half][:, col:col + 1] * s + ds[p][c]
        p_scr[p] = s
    return jnp.concatenate([jnp.concatenate(r, axis=1) for r in rows], axis=0)


def _gla_state_load(s_ref, p_scr):
    p_scr[...] = jnp.zeros(p_scr.shape, F32)
    for h in range(GLA_HEADS):
        r0, c0 = (h % 2) * GLA_DK, (h % 2) * GLA_DV
        p_scr[h // 2, r0:r0 + GLA_DK, c0:c0 + GLA_DV] = s_ref[0, 0, h]


def _gla_state_store(p_scr, s_ref):
    for h in range(GLA_HEADS):
        r0, c0 = (h % 2) * GLA_DK, (h % 2) * GLA_DV
        s_ref[0, 0, h] = p_scr[h // 2, r0:r0 + GLA_DK, c0:c0 + GLA_DV]


def _ev1_kernel(xp_ref, xs_ref, rtab_ref, ctab_ref, mod_ref, ng_ref, win_ref, wlr_ref, gmat_ref, gb_ref,
                lng_ref, lnb_ref, ws_ref, bst_ref, p0_ref, w1f_ref, w2f_ref,
                mid_ref, pfin_ref, w1b_ref, w2b_ref, p_scr, pg_scr, ps_scr, la_scr):
    i = pl.program_id(0)
    is_p, within, cidx, _, _, _ = _tile_info(jnp.minimum(i, NT - 1))
    _, _, _, first, last, _ = _tile_info(jnp.maximum(i - 1, 0))
    _cast_slabs(w1f_ref, w2f_ref, w1b_ref, w2b_ref)

    @pl.when(i == 0)
    def _():
        pg_scr[...] = jnp.zeros(pg_scr.shape, F32)
        ps_scr[...] = jnp.zeros(ps_scr.shape, F32)
        la_scr[...] = jnp.zeros(la_scr.shape, F32)

    @pl.when(first)
    def _():
        _gla_state_load(p0_ref, p_scr)

    x = _load_x0(is_p, within, xp_ref, xs_ref, rtab_ref, ctab_ref)
    sh1, sc1, _, _, _, _ = _mod_rows(mod_ref, cidx)
    hm = (_rms(x, ng_ref[0:1, :]) * (1.0 + sc1) + sh1).astype(BF16)
    o0 = 2 * SGU_WIDTH
    lr = _dot(hm, wlr_ref[...]).astype(BF16)

    qs = pg_scr[:, 0:QK] * (GLA_DK ** -0.5)
    k = pg_scr[:, QK:2 * QK]
    v = pg_scr[:, 2 * QK:2 * QK + VW]
    g = pg_scr[:, 2 * QK + VW:2 * QK + 2 * VW]
    v_bf = v.astype(BF16)
    c0 = 3 * VW
    mid_ref[:, 2 * VW:3 * VW] = g * jax.nn.sigmoid(g)
    mid_ref[:, c0:c0 + QK] = qs
    mid_ref[:, c0 + QK:c0 + 2 * QK] = k
    mid_ref[:, c0 + 2 * QK:c0 + 2 * QK + VW] = v
    mid_ref[:, c0 + 2 * QK + VW:c0 + 3 * QK + VW] = la_scr[:, QK:2 * QK]
    prep = _gla_prep(qs, k, la_scr[:, 0:QK], reverse=False)

    pg_scr[...] = _dot(hm, win_ref[:, o0:EV_MAIN])
    u = _gelu(ps_scr[:, 0:SGU_WIDTH])
    vg = _gelu(ps_scr[:, SGU_WIDTH:2 * SGU_WIDTH])
    mu = jnp.mean(vg, axis=-1, keepdims=True)
    vc = vg - mu
    vn = (vc * lax.rsqrt(jnp.mean(vc * vc, axis=-1, keepdims=True) + EPS) * lng_ref[...] + lnb_ref[...]).astype(BF16)
    la_scr[...] = _log_sigmoid(_dot(lr, gmat_ref[...]) + gb_ref[...]) * (1.0 / GLA_NORMALIZER)
    prods = _gla_products(prep, v_bf, reverse=False)
    ps_scr[...] = _dot(hm, win_ref[:, 0:o0])

    gd = SGU_WIDTH // SGU_GROUPS
    nch = TM // SGU_CHUNK
    sv_cols = []
    for grp in range(SGU_GROUPS):
        vcat = jnp.concatenate(
            [vn[c * SGU_CHUNK:(c + 1) * SGU_CHUNK, grp * gd:(grp + 1) * gd] for c in range(nch)], axis=1)
        sg = _dot(ws_ref[grp], vcat) + bst_ref[:, grp:grp + 1]
        sv_cols.append(jnp.concatenate([sg[:, c * gd:(c + 1) * gd] for c in range(nch)], axis=0))
    mid_ref[:, 0:VW] = u * jnp.concatenate(sv_cols, axis=1)
    mid_ref[:, VW:2 * VW] = _gla_outputs(prep, prods, v_bf, p_scr, reverse=False)

    @pl.when(last)
    def _():
        _gla_state_store(p_scr, pfin_ref)


def _const_spec(shape):
    nd = len(shape)
    return pl.BlockSpec(shape, lambda i, _nd=nd: (0,) * _nd, pipeline_mode=pl.Buffered(1))


def _xp_spec(rev):
    if rev:
        return pl.BlockSpec((TM, D), lambda i: (jnp.minimum(NT - 1 - i, NTP - 1), 0))
    return pl.BlockSpec((TM, D), lambda i: (jnp.minimum(i, NTP - 1), 0))


def _xs_spec(rev):
    if rev:
        return pl.BlockSpec((TM, D), lambda i: (jnp.maximum(NT - 1 - i - NTP, 0), 0))
    return pl.BlockSpec((TM, D), lambda i: (jnp.maximum(i - NTP, 0), 0))


def _tile_spec(width, rev=False):
    if rev:
        return pl.BlockSpec((TM, width), lambda i: (NT - 1 - i, 0))
    return pl.BlockSpec((TM, width), lambda i: (i, 0))


def _params():
    return pltpu.CompilerParams(dimension_semantics=("arbitrary",), vmem_limit_bytes=VMEM_LIMIT)


CAST_STEPS = 64


def _cast_specs(layer):
    r1, r2 = D // CAST_STEPS, D_FF // CAST_STEPS
    in_specs = [pl.BlockSpec((1, r1, D_FF), lambda i: (layer, jnp.minimum(i, CAST_STEPS - 1), 0)),
                pl.BlockSpec((1, r2, D), lambda i: (layer, jnp.minimum(i, CAST_STEPS - 1), 0))]
    out_specs = [pl.BlockSpec((r1, D_FF), lambda i: (jnp.minimum(i, CAST_STEPS - 1), 0)),
                 pl.BlockSpec((r2, D), lambda i: (jnp.minimum(i, CAST_STEPS - 1), 0))]
    shapes = [jax.ShapeDtypeStruct((D, D_FF), BF16), jax.ShapeDtypeStruct((D_FF, D), BF16)]
    return in_specs, out_specs, shapes


def _cast_slabs(w1f_ref, w2f_ref, w1b_ref, w2b_ref):
    w1b_ref[...] = w1f_ref[0].astype(BF16)
    w2b_ref[...] = w2f_ref[0].astype(BF16)


def _even_forward(xp, xs, rtab, ctab, mod, ng, w, w1f, w2f, layer):
    state_blk = (1, 1, GLA_HEADS, GLA_DK, GLA_DV)
    cast_in, cast_out, cast_shapes = _cast_specs(layer)

    def mix_tile(i):
        return jnp.maximum(i - 1, 0)

    return pl.pallas_call(
        _ev1_kernel,
        out_shape=(jax.ShapeDtypeStruct((NT * TM, EV_MID), F32),
                   jax.ShapeDtypeStruct((NSEQ, 1) + state_blk[2:], F32), *cast_shapes),
        grid=(NT + 1,),
        in_specs=[
            pl.BlockSpec((TM, D), lambda i: (jnp.minimum(i, NTP - 1), 0)),
            pl.BlockSpec((TM, D), lambda i: (jnp.maximum(jnp.minimum(i, NT - 1) - NTP, 0), 0)),
            _const_spec((GRID_W, D // 2)), _const_spec((GRID_W, D // 2)),
            _const_spec((NCOND, 6 * D)), _const_spec((4, D)),
            _const_spec((D, EV_MAIN)), _const_spec((D, LANES)), _const_spec((LANES, 2 * QK)),
            _const_spec((1, 2 * QK)),
            _const_spec((1, SGU_WIDTH)), _const_spec((1, SGU_WIDTH)),
            _const_spec((SGU_GROUPS, SGU_CHUNK, SGU_CHUNK)), _const_spec((SGU_CHUNK, SGU_GROUPS)),
            pl.BlockSpec(state_blk, lambda i: (_seq_of(mix_tile(i)), 0, 0, 0, 0)),
            *cast_in,
        ],
        out_specs=(pl.BlockSpec((TM, EV_MID), lambda i: (mix_tile(i), 0)),
                   pl.BlockSpec(state_blk, lambda i: (_seq_of(mix_tile(i)), 0, 0, 0, 0)), *cast_out),
        scratch_shapes=[pltpu.VMEM((GLA_HEADS // 2, 2 * GLA_DK, 2 * GLA_DV), F32),
                        pltpu.VMEM((TM, 2 * QK + 2 * VW), F32), pltpu.VMEM((TM, 2 * SGU_WIDTH), F32),
                        pltpu.VMEM((TM, 2 * QK), F32)],
        compiler_params=_params(),
        name="even_forward",
    )(xp, xs, rtab, ctab, mod, ng, w["win"], w["wlr"], w["gmat"], w["gb"], w["lng"], w["lnb"],
      w["ws"], w["bst"], w["p0"], w1f, w2f)


def _ev2_kernel(xp_ref, xs_ref, rtab_ref, ctab_ref, mod_ref, ng_ref, mid_ref, gn_ref, wout_ref, p0_ref,
                x1_ref, pfin_ref, p_scr, cat_scr):
    i = pl.program_id(0)
    live = i < NT
    _, _, _, first, last, _ = _tile_info(NT - 1 - jnp.minimum(i, NT - 1))
    is_p, within, cidx, _, _, _ = _tile_info(NT - 1 - jnp.maximum(i - 1, 0))

    @pl.when(i == 0)
    def _():
        cat_scr[...] = jnp.zeros(cat_scr.shape, BF16)

    @pl.when(jnp.logical_and(last, live))
    def _():
        _gla_state_load(p0_ref, p_scr)

    nblk = D // SLAB
    ys = [None] * nblk

    def project(j):
        ys[j] = _dot(cat_scr[...], wout_ref[:, j * SLAB:(j + 1) * SLAB])

    c0 = 3 * VW
    qs = mid_ref[:, c0:c0 + QK]
    k = mid_ref[:, c0 + QK:c0 + 2 * QK]
    v_bf = mid_ref[:, c0 + 2 * QK:c0 + 2 * QK + VW].astype(BF16)
    la_b = mid_ref[:, c0 + 2 * QK + VW:c0 + 3 * QK + VW]
    project(0)
    prep = _gla_prep(qs, k, la_b, reverse=True)
    project(1)
    prods = _gla_products(prep, v_bf, reverse=True)
    project(2)
    o = mid_ref[:, VW:2 * VW] + _gla_outputs(prep, prods, v_bf, p_scr, reverse=True)
    project(3)
    heads = []
    for h in range(GLA_HEADS):
        oh = o[:, h * GLA_DV:(h + 1) * GLA_DV]
        heads.append(oh * lax.rsqrt(jnp.mean(oh * oh, axis=-1, keepdims=True) + EPS))
    on = jnp.concatenate(heads, axis=1) * gn_ref[...] * mid_ref[:, 2 * VW:3 * VW]
    cat_scr[...] = jnp.concatenate([mid_ref[:, 0:VW], on], axis=1).astype(BF16)

    x = _load_x0(is_p, within, xp_ref, xs_ref, rtab_ref, ctab_ref)
    _, _, g1, _, _, _ = _mod_rows(mod_ref, cidx)
    x1_ref[...] = x + g1 * _rms(jnp.concatenate(ys, axis=1), ng_ref[1:2, :])

    @pl.when(jnp.logical_and(first, live))
    def _():
        _gla_state_store(p_scr, pfin_ref)


def _even_reverse(xp, xs, rtab, ctab, mod, ng, mid, w):
    state_blk = (1, 1, GLA_HEADS, GLA_DK, GLA_DV)
    assert D // SLAB == 4

    def mix_tile(i):
        return NT - 1 - jnp.minimum(i, NT - 1)

    def out_tile(i):
        return NT - 1 - jnp.maximum(i - 1, 0)

    return pl.pallas_call(
        _ev2_kernel,
        out_shape=(jax.ShapeDtypeStruct((NT * TM, D), F32),
                   jax.ShapeDtypeStruct((NSEQ, 1) + state_blk[2:], F32)),
        grid=(NT + 1,),
        in_specs=[
            pl.BlockSpec((TM, D), lambda i: (jnp.minimum(out_tile(i), NTP - 1), 0)),
            pl.BlockSpec((TM, D), lambda i: (jnp.maximum(out_tile(i) - NTP, 0), 0)),
            _const_spec((GRID_W, D // 2)), _const_spec((GRID_W, D // 2)),
            _const_spec((NCOND, 6 * D)), _const_spec((4, D)),
            pl.BlockSpec((TM, EV_MID), lambda i: (mix_tile(i), 0)),
            _const_spec((1, VW)), _const_spec((2 * VW, D)),
            pl.BlockSpec(state_blk, lambda i: (_seq_of(mix_tile(i)), 1, 0, 0, 0)),
        ],
        out_specs=(pl.BlockSpec((TM, D), lambda i: (out_tile(i), 0)),
                   pl.BlockSpec(state_blk, lambda i: (_seq_of(mix_tile(i)), 0, 0, 0, 0))),
        scratch_shapes=[pltpu.VMEM((GLA_HEADS // 2, 2 * GLA_DK, 2 * GLA_DV), F32),
                        pltpu.VMEM((TM, D), BF16)],
        compiler_params=_params(),
        name="even_reverse",
    )(xp, xs, rtab, ctab, mod, ng, mid, w["gn"], w["wout"], w["p0"])


FF_CHUNK = 1024
MLP_TILES = 2
TMM = MLP_TILES * TM


def _mlp_kernel(x_ref, mod_ref, ng_ref, w1_ref, b1_ref, w2_ref, b2_ref, o_ref, *, tile0):
    _, _, cidx, _, _, _ = _tile_info(tile0 + pl.program_id(0) * MLP_TILES)
    _, _, _, sh2, sc2, g2 = _mod_rows(mod_ref, cidx)
    x = x_ref[...]
    hff = (_rms(x, ng_ref[2:3, :]) * (1.0 + sc2) + sh2).astype(BF16)
    acc = b2_ref[...]
    for j in range(D_FF // FF_CHUNK):
        cs = slice(j * FF_CHUNK, (j + 1) * FF_CHUNK)
        h = _dot(hff, w1_ref[:, cs]) + b1_ref[:, cs]
        h = jnp.square(jnp.maximum(h, 0.0)).astype(BF16)
        acc = acc + _dot(h, w2_ref[cs, :])
    o_ref[...] = x + g2 * _rms(acc, ng_ref[3:4, :])


def _mlp(x, mod, ng, w1, b1, w2, b2, tile0, ntiles):
    assert tile0 % MLP_TILES == 0 and ntiles % MLP_TILES == 0 and TPS % MLP_TILES == 0 and NTP % MLP_TILES == 0
    b0 = tile0 // MLP_TILES
    return pl.pallas_call(
        functools.partial(_mlp_kernel, tile0=tile0),
        out_shape=jax.ShapeDtypeStruct((ntiles * TM, D), F32),
        grid=(ntiles // MLP_TILES,),
        in_specs=[
            pl.BlockSpec((TMM, D), lambda i: (b0 + i, 0)),
            _const_spec((NCOND, 6 * D)), _const_spec((4, D)),
            _const_spec((D, D_FF)), _const_spec((1, D_FF)), _const_spec((D_FF, D)), _const_spec((1, D)),
        ],
        out_specs=pl.BlockSpec((TMM, D), lambda i: (i, 0)),
        compiler_params=_params(), name="mlp",
    )(x, mod, ng, w1, b1, w2, b2)


def _rg_gates(xc, wg_ref, ba_ref, bx_ref, lam_ref, a_scr, b_scr, slabs=range(D_RNN // SLAB)):
    xcb = xc.astype(BF16)
    sp = RG_C * _softplus(-lam_ref[...])
    for s in slabs:
        cs = slice(s * SLAB, (s + 1) * SLAB)
        pre = _dot(xcb[:, cs], wg_ref[s])
        r = jax.nn.sigmoid(pre[:, 0:SLAB] + ba_ref[:, cs])
        i = jax.nn.sigmoid(pre[:, SLAB:2 * SLAB] + bx_ref[:, cs])
        z = r * sp[:, cs]
        a = jnp.exp(-z)
        a_scr[:, cs] = a
        u = jnp.tanh(z) * (1.0 + a * a)
        b_scr[:, cs] = jnp.where(u > 0.0, u * lax.rsqrt(u), 0.0) * (i * xc[:, cs])


SEG = TM // SUBLANES


def _seg_perm(transpose):
    rr = lax.broadcasted_iota(jnp.int32, (TM, TM), 0)
    cc = lax.broadcasted_iota(jnp.int32, (TM, TM), 1)
    if transpose:
        rr, cc = cc, rr
    hit = jnp.logical_and(cc // SEG == rr % SUBLANES, cc % SEG == rr // SUBLANES)
    return jnp.where(hit, 1.0, 0.0).astype(BF16)


def _rg_scan(a_scr, b_scr, h_scr, c_scr, h0, reverse):
    def body(i, carry):
        hh, cc = carry
        grp = (SEG - 1 - i) if reverse else i
        r0 = pl.multiple_of(grp * SUBLANES, SUBLANES)
        a = a_scr[pl.ds(r0, SUBLANES), :]
        hh = a * hh + b_scr[pl.ds(r0, SUBLANES), :]
        cc = a * cc
        h_scr[pl.ds(r0, SUBLANES), :] = hh
        c_scr[pl.ds(r0, SUBLANES), :] = cc
        return hh, cc

    init = (jnp.zeros((SUBLANES, D_RNN), F32), jnp.ones((SUBLANES, D_RNN), F32))
    b, a = lax.fori_loop(0, SEG, body, init, unroll=4)
    row = lax.broadcasted_iota(jnp.int32, (SUBLANES, D_RNN), 0)
    for s in (1, 2, 4):
        shift = (SUBLANES - s) if reverse else s
        valid = (row < SUBLANES - s) if reverse else (row >= s)
        a_s = pltpu.roll(a, shift, 0)
        b_s = pltpu.roll(b, shift, 0)
        b = jnp.where(valid, a * b_s + b, b)
        a = jnp.where(valid, a * a_s, a)
    after = a * h0 + b
    edge = (row == SUBLANES - 1) if reverse else (row == 0)
    enter = jnp.where(edge, h0, pltpu.roll(after, (SUBLANES - 1) if reverse else 1, 0))
    h = h_scr[...] + c_scr[...] * jnp.concatenate([enter] * SEG, axis=0)
    out = after[0:1, :] if reverse else after[SUBLANES - 1:SUBLANES, :]
    return h, out


def _od1_kernel(x_ref, xn_ref, mod_ref, ng_ref, win_ref, cw_ref, cb_ref, wg_ref, ba_ref, bx_ref, lam_ref,
                s0_ref, w1f_ref, w2f_ref, mid_ref, sfin_ref, w1b_ref, w2b_ref,
                proj_scr, xb_scr, a_scr, b_scr, h_scr, c_scr, hc_scr, tail_scr):
    i = pl.program_id(0)
    _, _, cidx, _, _, _ = _tile_info(jnp.minimum(i, NT - 1))
    _, _, _, first, last, _ = _tile_info(jnp.maximum(i - 1, 0))
    _cast_slabs(w1f_ref, w2f_ref, w1b_ref, w2b_ref)

    @pl.when(i == 0)
    def _():
        proj_scr[...] = jnp.zeros(proj_scr.shape, F32)

    @pl.when(first)
    def _():
        tail_scr[...] = jnp.zeros((2 * SUBLANES, D_RNN), F32)
        hc_scr[...] = s0_ref[0, 0]

    nslab = D_RNN // SLAB
    pw = 2 * D_RNN // nslab

    def project(lhs, s):
        proj_scr[:, s * pw:(s + 1) * pw] = _dot(lhs, win_ref[:, s * pw:(s + 1) * pw])

    sh1, sc1, _, _, _, _ = _mod_rows(mod_ref, cidx)
    xe = jnp.concatenate([x_ref[...], xn_ref[...]], axis=0)
    hm = (_rms(xe, ng_ref[0:1, :]) * (1.0 + sc1) + sh1).astype(BF16)
    hmp = _dot(_seg_perm(False), hm[0:TM, :]).astype(BF16)
    lhs = jnp.concatenate([hmp, hm[TM:TM + HALO, :]], axis=0)

    row = lax.broadcasted_iota(jnp.int32, (SUBLANES, D_RNN), 0)
    g30 = proj_scr[TM - 2 * SUBLANES:TM - SUBLANES, 0:D_RNN]
    g31 = proj_scr[TM - SUBLANES:TM, 0:D_RNN]
    g0 = proj_scr[0:SUBLANES, 0:D_RNN]
    nxt = jnp.where(last, 0.0, proj_scr[TM:TM + 1, 0:D_RNN])
    xb_scr[0:SUBLANES, :] = jnp.where(row == 0, pltpu.roll(tail_scr[0:SUBLANES, :], 1, 0), pltpu.roll(g30, 1, 0))
    xb_scr[SUBLANES:2 * SUBLANES, :] = jnp.where(
        row == 0, pltpu.roll(tail_scr[SUBLANES:2 * SUBLANES, :], 1, 0), pltpu.roll(g31, 1, 0))
    xb_scr[2 * SUBLANES:2 * SUBLANES + TM, :] = proj_scr[0:TM, 0:D_RNN]
    xb_scr[2 * SUBLANES + TM:3 * SUBLANES + TM, :] = jnp.where(
        row == SUBLANES - 1, nxt, pltpu.roll(g0, SUBLANES - 1, 0))
    tail_scr[0:SUBLANES, :] = g30
    tail_scr[SUBLANES:2 * SUBLANES, :] = g31
    xc = cb_ref[...]
    for j in range(4):
        xc = xc + cw_ref[j:j + 1, :] * xb_scr[j * SUBLANES:j * SUBLANES + TM, :]
    mid_ref[:, 2 * D_RNN:3 * D_RNN] = xc

    project(lhs, 0)
    mid_ref[:, D_RNN:2 * D_RNN] = _gelu(proj_scr[0:TM, D_RNN:2 * D_RNN])
    for s in range(nslab):
        if s + 1 < nslab:
            project(lhs, s + 1)
        _rg_gates(mid_ref[:, 2 * D_RNN:3 * D_RNN], wg_ref, ba_ref, bx_ref, lam_ref, a_scr, b_scr, range(s, s + 1))

    h, hc_scr[...] = _rg_scan(a_scr, b_scr, h_scr, c_scr, hc_scr[...], reverse=False)
    mid_ref[:, 0:D_RNN] = h

    @pl.when(last)
    def _():
        sfin_ref[0, 0] = hc_scr[...]


def _odd_forward(x, mod, ng, w, w1f, w2f, layer):
    nb16 = TM // HALO
    cast_in, cast_out, cast_shapes = _cast_specs(layer)

    def proj_tile(i):
        return jnp.minimum(i, NT - 1)

    def scan_tile(i):
        return jnp.maximum(i - 1, 0)

    return pl.pallas_call(
        _od1_kernel,
        out_shape=(jax.ShapeDtypeStruct((NT * TM, OD_MID), F32),
                   jax.ShapeDtypeStruct((NSEQ, 1, 1, D_RNN), F32), *cast_shapes),
        grid=(NT + 1,),
        in_specs=[
            pl.BlockSpec((TM, D), lambda i: (proj_tile(i), 0)),
            pl.BlockSpec((HALO, D), lambda i: (jnp.minimum(proj_tile(i) + 1, NT - 1) * nb16, 0)),
            _const_spec((NCOND, 6 * D)), _const_spec((4, D)),
            _const_spec((D, 2 * D_RNN)), _const_spec((4, D_RNN)), _const_spec((1, D_RNN)),
            _const_spec((D_RNN // SLAB, SLAB, 2 * SLAB)),
            _const_spec((1, D_RNN)), _const_spec((1, D_RNN)), _const_spec((1, D_RNN)),
            pl.BlockSpec((1, 1, 1, D_RNN), lambda i: (_seq_of(scan_tile(i)), 0, 0, 0)),
            *cast_in,
        ],
        out_specs=(pl.BlockSpec((TM, OD_MID), lambda i: (scan_tile(i), 0)),
                   pl.BlockSpec((1, 1, 1, D_RNN), lambda i: (_seq_of(scan_tile(i)), 0, 0, 0)), *cast_out),
        scratch_shapes=[
            pltpu.VMEM((TM + HALO, 2 * D_RNN), F32),
            pltpu.VMEM((3 * SUBLANES + TM, D_RNN), F32),
            pltpu.VMEM((TM, D_RNN), F32), pltpu.VMEM((TM, D_RNN), F32), pltpu.VMEM((TM, D_RNN), F32),
            pltpu.VMEM((TM, D_RNN), F32),
            pltpu.VMEM((1, D_RNN), F32), pltpu.VMEM((2 * SUBLANES, D_RNN), F32),
        ],
        compiler_params=_params(),
        name="odd_forward",
    )(x, x, mod, ng, w["win"], w["cw"], w["cb"], w["wg"][0], w["ba"][0:1], w["bx"][0:1], w["lam"][0:1],
      w["s0"], w1f, w2f)


def _od2_kernel(x_ref, mod_ref, ng_ref, mid_ref, wg_ref, ba_ref, bx_ref, lam_ref, wout_ref, s0_ref,
                x1_ref, sfin_ref, a_scr, b_scr, h_scr, c_scr, hc_scr, zp_scr):
    i = pl.program_id(0)
    live = i < NT
    _, _, _, first, last, _ = _tile_info(NT - 1 - jnp.minimum(i, NT - 1))
    _, _, cidx_prev, _, _, _ = _tile_info(NT - 1 - jnp.maximum(i - 1, 0))
    _, _, g1, _, _, _ = _mod_rows(mod_ref, cidx_prev)

    @pl.when(i == 0)
    def _():
        zp_scr[...] = jnp.zeros(zp_scr.shape, BF16)

    @pl.when(jnp.logical_and(last, live))
    def _():
        hc_scr[...] = s0_ref[0, 0]

    xc = mid_ref[:, 2 * D_RNN:3 * D_RNN]
    nslab = D_RNN // SLAB
    z = _dot(_seg_perm(True), zp_scr[...]).astype(BF16)
    ys = []
    for s in range(nslab):
        ys.append(_dot(z, wout_ref[:, s * SLAB:(s + 1) * SLAB]))
        _rg_gates(xc, wg_ref, ba_ref, bx_ref, lam_ref, a_scr, b_scr, range(s, s + 1))
    x1_ref[...] = x_ref[...] + g1 * _rms(jnp.concatenate(ys, axis=1), ng_ref[1:2, :])
    h_b, hc_scr[...] = _rg_scan(a_scr, b_scr, h_scr, c_scr, hc_scr[...], reverse=True)
    zp_scr[...] = ((mid_ref[:, 0:D_RNN] + h_b) * mid_ref[:, D_RNN:2 * D_RNN]).astype(BF16)

    @pl.when(jnp.logical_and(first, live))
    def _():
        sfin_ref[0, 0] = hc_scr[...]


def _odd_reverse(x, mod, ng, mid, w):
    def scan_tile(i):
        return NT - 1 - jnp.minimum(i, NT - 1)

    def out_tile(i):
        return NT - 1 - jnp.maximum(i - 1, 0)

    return pl.pallas_call(
        _od2_kernel,
        out_shape=(jax.ShapeDtypeStruct((NT * TM, D), F32),
                   jax.ShapeDtypeStruct((NSEQ, 1, 1, D_RNN), F32)),
        grid=(NT + 1,),
        in_specs=[
            pl.BlockSpec((TM, D), lambda i: (out_tile(i), 0)),
            _const_spec((NCOND, 6 * D)), _const_spec((4, D)),
            pl.BlockSpec((TM, OD_MID), lambda i: (scan_tile(i), 0)),
            _const_spec((D_RNN // SLAB, SLAB, 2 * SLAB)),
            _const_spec((1, D_RNN)), _const_spec((1, D_RNN)), _const_spec((1, D_RNN)),
            _const_spec((D_RNN, D)),
            pl.BlockSpec((1, 1, 1, D_RNN), lambda i: (_seq_of(scan_tile(i)), 1, 0, 0)),
        ],
        out_specs=(pl.BlockSpec((TM, D), lambda i: (out_tile(i), 0)),
                   pl.BlockSpec((1, 1, 1, D_RNN), lambda i: (_seq_of(scan_tile(i)), 0, 0, 0))),
        scratch_shapes=[
            pltpu.VMEM((TM, D_RNN), F32), pltpu.VMEM((TM, D_RNN), F32), pltpu.VMEM((TM, D_RNN), F32),
            pltpu.VMEM((TM, D_RNN), F32),
            pltpu.VMEM((1, D_RNN), F32), pltpu.VMEM((TM, D_RNN), BF16),
        ],
        compiler_params=_params(),
        name="odd_reverse",
    )(x, mod, ng, mid, w["wg"][1], w["ba"][1:2], w["bx"][1:2], w["lam"][1:2], w["wout"], w["s0"])


def _pos_tables():
    n = D // 4
    omega = 1.0 / (10000.0 ** (jnp.arange(n, dtype=F32) / n))
    idx = jnp.arange(GRID_W, dtype=F32)[:, None] * omega
    tab = jnp.concatenate([jnp.sin(idx), jnp.cos(idx)], axis=-1)
    return tab, tab


def _block_diag_slabs(w):
    per = SLAB // RG_BS
    w = w.reshape(D_RNN // SLAB, per, RG_BS, RG_BS)
    eye = jnp.eye(per, dtype=w.dtype)
    return jnp.einsum("spij,pq->spiqj", w, eye).reshape(D_RNN // SLAB, SLAB, SLAB)


def kernel(x_prompt, x_sample, c, state_gla, state_rglru, c_ctx, mod_w, mod_b, norm_g, mlp_w1, mlp_b1, mlp_w2,
           mlp_b2, ev_w_in, ev_w_out, sgu_ln_g, sgu_ln_b, sgu_ws, sgu_bs, gla_gate_w2, gla_gate_b, gla_norm_g,
           rg_w_in, rg_conv_w, rg_conv_b, rg_wa, rg_ba, rg_wx, rg_bx, rg_L, rg_w_out):
    assert x_prompt.shape == (BATCH, SEQ, D) and x_sample.shape == (DEC_BATCH, DEC_SEQ, D)
    assert SEQ == TM and DEC_SEQ % TM == 0 and DEPTH == 2
    xp = x_prompt.reshape(NTP * TM, D)
    xs = x_sample.reshape(NTS * TM, D)
    cond8 = jnp.concatenate([c_ctx[None, :], c, jnp.zeros((NCOND - 1 - DEC_BATCH, D), F32)], axis=0)
    mods = _modulation(cond8, mod_w, mod_b)
    rtab, ctab = _pos_tables()

    gmat = jnp.zeros((LANES, 2 * QK), F32)
    gmat = gmat.at[0:GLA_RANK, 0:QK].set(gla_gate_w2[0, 0])
    gmat = gmat.at[GLA_RANK:2 * GLA_RANK, QK:2 * QK].set(gla_gate_w2[0, 1])
    s0_gla = jnp.concatenate([jnp.zeros((BATCH,) + state_gla.shape[2:], F32), state_gla[:, 0]], axis=0)
    ev = {
        "win": ev_w_in[0, :, 0:EV_MAIN].astype(BF16),
        "wlr": jnp.pad(ev_w_in[0, :, EV_MAIN:], ((0, 0), (0, LANES - 2 * GLA_RANK))).astype(BF16),
        "gmat": gmat.astype(BF16),
        "gb": gla_gate_b[0].reshape(1, 2 * QK),
        "lng": sgu_ln_g[0].reshape(1, SGU_WIDTH),
        "lnb": sgu_ln_b[0].reshape(1, SGU_WIDTH),
        "ws": sgu_ws[0].astype(BF16),
        "bst": sgu_bs[0].T,
        "gn": gla_norm_g[0].reshape(1, VW),
        "wout": ev_w_out[0].astype(BF16),
        "p0": s0_gla,
    }
    mid, pf, w1b, w2b = _even_forward(xp, xs, rtab, ctab, mods[0], norm_g[0], ev, mlp_w1, mlp_w2, 0)
    x1, pb = _even_reverse(xp, xs, rtab, ctab, mods[0], norm_g[0], mid, ev)
    x2 = _mlp(x1, mods[0], norm_g[0], w1b, mlp_b1[0].reshape(1, D_FF), w2b, mlp_b2[0].reshape(1, D), 0, NT)
    new_gla = jnp.stack([pf[:BATCH, 0], pb[:BATCH, 0]], axis=1)[:, None]

    s0_rg = jnp.concatenate([jnp.zeros((BATCH, 2, D_RNN), F32), state_rglru[:, 0]], axis=0)
    od = {
        "win": rg_w_in[0].astype(BF16),
        "cw": rg_conv_w[0],
        "cb": rg_conv_b[0].reshape(1, D_RNN),
        "wg": jnp.stack([
            jnp.concatenate([_block_diag_slabs(rg_wa[0, d]), _block_diag_slabs(rg_wx[0, d])], axis=-1)
            for d in range(2)], axis=0).astype(BF16),
        "ba": rg_ba[0], "bx": rg_bx[0], "lam": rg_L[0],
        "wout": rg_w_out[0].astype(BF16),
        "s0": s0_rg.reshape(NSEQ, 2, 1, D_RNN),
    }
    mid1, sf, w1b, w2b = _odd_forward(x2, mods[1], norm_g[1], od, mlp_w1, mlp_w2, 1)
    x3, sb = _odd_reverse(x2, mods[1], norm_g[1], mid1, od)
    mlp1 = (mods[1], norm_g[1], w1b, mlp_b1[1].reshape(1, D_FF), w2b, mlp_b2[1].reshape(1, D))
    y_prompt = _mlp(x3, *mlp1, 0, NTP)
    y_sample = _mlp(x3, *mlp1, NTP, NTS)
    new_rg = jnp.stack([sf[:BATCH, 0, 0], sb[:BATCH, 0, 0]], axis=1)[:, None]
    return (y_prompt.reshape(BATCH, SEQ, D), y_sample.reshape(DEC_BATCH, DEC_SEQ, D), new_gla, new_rg)
```

```python
import functools

import jax
import jax.numpy as jnp
import numpy as np
from jax import lax
from jax.experimental import pallas as pl
from jax.experimental.pallas import tpu as pltpu

D = 1024
BATCH = 16
SEQ = 256
DEPTH = 2
DEC_BATCH = 4
DEC_SEQ = 4096
GRID_W = 64
D_FF = 4 * D
EPS = 1e-6
SGU_CHUNK = 128
SGU_GROUPS = 4
SGU_WIDTH = D // 2
GLA_HEADS = 4
GLA_DV = 128
GLA_DK = 64
GLA_RANK = 16
GLA_NORMALIZER = 16.0
GLA_CHUNK = 64
QK = GLA_HEADS * GLA_DK
VW = GLA_HEADS * GLA_DV
EV_MAIN = 2 * SGU_WIDTH + 2 * QK + 2 * VW
D_RNN = D
RG_BLOCKS = 16
RG_BS = D_RNN // RG_BLOCKS
RG_C = 8.0
LANES = 128
SUBLANES = 8
SLAB = 256

TM = 256
NTP = BATCH * SEQ // TM
TPS = DEC_SEQ // TM
NTS = DEC_BATCH * TPS
NT = NTP + NTS
NSEQ = BATCH + DEC_BATCH
NCOND = 8
HALO = 16
EV_MID = 2 * VW + VW + 2 * QK + VW + QK
OD_MID = 3 * D_RNN
VMEM_LIMIT = 56 * 1024 * 1024

F32 = jnp.float32
BF16 = jnp.bfloat16


def _tile_info(t):
    is_p = t < NTP
    ts = jnp.maximum(t - NTP, 0)
    sq = ts // TPS
    within = ts % TPS
    cidx = jnp.where(is_p, 0, 1 + sq)
    first = jnp.logical_or(is_p, within == 0)
    last = jnp.logical_or(is_p, within == TPS - 1)
    seq = jnp.where(is_p, t, BATCH + sq)
    return is_p, within, cidx, first, last, seq


def _seq_of(t):
    return jnp.where(t < NTP, t, BATCH + jnp.maximum(t - NTP, 0) // TPS)


def _rms(x, g):
    return x * lax.rsqrt(jnp.mean(x * x, axis=-1, keepdims=True) + EPS) * g


def _dot(a, b):
    return jnp.dot(a, b, preferred_element_type=F32)


def _dot_nt(a, b):
    return lax.dot_general(a, b, (((1,), (1,)), ((), ())), preferred_element_type=F32)


def _dot_tn(a, b):
    return lax.dot_general(a, b, (((0,), (0,)), ((), ())), preferred_element_type=F32)


def _split3(x):
    hi = x.astype(BF16)
    r1 = x - hi.astype(F32)
    mid = r1.astype(BF16)
    lo = (r1 - mid.astype(F32)).astype(BF16)
    return hi, mid, lo


def _dot_exact_lhs(m, parts):
    return _dot(m, parts[0]) + _dot(m, parts[1]) + _dot(m, parts[2])


def _log_sigmoid(x):
    return jnp.minimum(x, 0.0) - jnp.log(1.0 + jnp.exp(-jnp.abs(x)))


def _softplus(x):
    return jnp.maximum(x, 0.0) + jnp.log(1.0 + jnp.exp(-jnp.abs(x)))


LOG2E = 1.4426950408889634
_GELU_C = 0.7978845608028654 * LOG2E


def _sigmoid(x):
    return 1.0 / (1.0 + jnp.exp2(x * (-LOG2E)))


def _gelu(x):
    return x / (1.0 + jnp.exp2(x * ((-2.0 * _GELU_C) + (-2.0 * 0.044715 * _GELU_C) * (x * x))))


def _mod_rows(mod_ref, cidx):
    m = mod_ref[pl.ds(cidx, 1), :]
    return [m[:, j * D:(j + 1) * D] for j in range(6)]


def _load_x0(is_p, within, xp_ref, xs_ref, rtab_ref, ctab_ref):
    rows_per_tile = TM // GRID_W
    r0 = within * rows_per_tile
    posr = jnp.concatenate(
        [jnp.broadcast_to(rtab_ref[pl.ds(r0 + j, 1), :], (GRID_W, D // 2)) for j in range(rows_per_tile)],
        axis=0)
    posc = jnp.concatenate([ctab_ref[...]] * rows_per_tile, axis=0)
    pos = jnp.concatenate([posr, posc], axis=1)
    return jnp.where(is_p, xp_ref[...], xs_ref[...] + pos)


def _mod_kernel(cond_ref, w_ref, b_ref, o_ref):
    c = cond_ref[...]
    sc = (c * _sigmoid(c)).astype(BF16)
    o_ref[0] = _dot(sc, w_ref[0].astype(BF16)) + b_ref[0]


def _modulation(cond8, mod_w, mod_b):
    nb = 6 * D // D
    return pl.pallas_call(
        _mod_kernel,
        out_shape=jax.ShapeDtypeStruct((DEPTH, NCOND, 6 * D), F32),
        grid=(DEPTH, nb),
        in_specs=[
            pl.BlockSpec((NCOND, D), lambda l, j: (0, 0)),
            pl.BlockSpec((1, D, D), lambda l, j: (l, 0, j)),
            pl.BlockSpec((1, 1, D), lambda l, j: (l, 0, j)),
        ],
        out_specs=pl.BlockSpec((1, NCOND, D), lambda l, j: (l, 0, j)),
        compiler_params=pltpu.CompilerParams(
            dimension_semantics=("arbitrary", "arbitrary"), vmem_limit_bytes=VMEM_LIMIT),
        name="modulation",
    )(cond8, mod_w, mod_b.reshape(DEPTH, 1, 6 * D))


def _chunk_rows(c):
    return slice(c * GLA_CHUNK, (c + 1) * GLA_CHUNK)


def _pair_lanes(p):
    return slice(p * LANES, (p + 1) * LANES)


def _gla_order(reverse):
    n = TM // GLA_CHUNK
    return list(reversed(range(n))) if reverse else list(range(n))


def _cum_mask(reverse):
    ri, ci = np.indices((TM, TM))
    order = (ci >= ri) if reverse else (ci <= ri)
    return jnp.asarray(np.logical_and(ri // GLA_CHUNK == ci // GLA_CHUNK, order), BF16)


def _gla_prep(qs, k, la, cum_m, reverse):
    n = TM // GLA_CHUNK
    b = _dot_exact_lhs(cum_m, _split3(la))
    ends = [c * GLA_CHUNK if reverse else (c + 1) * GLA_CHUNK - 1 for c in range(n)]
    btot = jnp.concatenate([jnp.broadcast_to(b[e:e + 1, :], (GLA_CHUNK, QK)) for e in ends], axis=0)
    qe = (qs * jnp.exp(b)).astype(BF16)
    ke = k * jnp.exp(-b)
    kd = (k * jnp.exp(btot - b)).astype(BF16)
    dec = jnp.exp(btot)
    lane = lax.broadcasted_iota(jnp.int32, (TM, QK), 1) % LANES
    ke_h = (jnp.where(lane < GLA_DK, ke, 0.0).astype(BF16), jnp.where(lane >= GLA_DK, ke, 0.0).astype(BF16))
    return qe, ke_h, kd, dec


def _gla_products(prep, v_bf, reverse):
    qe, ke_h, kd, _ = prep
    n = TM // GLA_CHUNK
    cr = lax.broadcasted_iota(jnp.int32, (GLA_CHUNK, 2 * GLA_CHUNK), 0)
    cc = lax.broadcasted_iota(jnp.int32, (GLA_CHUNK, 2 * GLA_CHUNK), 1) % GLA_CHUNK
    cmask = (cc >= cr) if reverse else (cc <= cr)
    br = lax.broadcasted_iota(jnp.int32, (2 * GLA_DK, 2 * GLA_DV), 0)
    bc = lax.broadcasted_iota(jnp.int32, (2 * GLA_DK, 2 * GLA_DV), 1)
    diag = (br < GLA_DK) == (bc < GLA_DV)
    att = [[None] * n for _ in range(GLA_HEADS // 2)]
    ds = [[None] * n for _ in range(GLA_HEADS // 2)]
    for p in range(GLA_HEADS // 2):
        vp = v_bf[:, p * 2 * GLA_DV:(p + 1) * 2 * GLA_DV]
        for c in _gla_order(reverse):
            rs, ls = _chunk_rows(c), _pair_lanes(p)
            kk = jnp.concatenate([ke_h[0][rs, ls], ke_h[1][rs, ls]], axis=0)
            att[p][c] = jnp.where(cmask, _dot_nt(qe[rs, ls], kk), 0.0).astype(BF16)
            ds[p][c] = jnp.where(diag, _dot_tn(kd[rs, ls], vp[rs, :]), 0.0)
    return att, ds


def _gla_outputs(prep, prods, v_bf, p_scr, reverse):
    qe, _, _, dec = prep
    att, ds = prods
    n = TM // GLA_CHUNK
    half = TM // 2
    lane_v = lax.broadcasted_iota(jnp.int32, (TM, 2 * GLA_DV), 1)
    rows = [[None] * (GLA_HEADS // 2) for _ in range(n)]
    for p in range(GLA_HEADS // 2):
        ls = _pair_lanes(p)
        vp = v_bf[:, p * 2 * GLA_DV:(p + 1) * 2 * GLA_DV]
        vl = jnp.where(lane_v < GLA_DV, vp, jnp.zeros_like(vp))
        vr = jnp.where(lane_v >= GLA_DV, vp, jnp.zeros_like(vp))
        dec_t = (dec[0:half, ls].T, dec[half:TM, ls].T)
        s = p_scr[p]
        for c in _gla_order(reverse):
            rs = _chunk_rows(c)
            rhs = jnp.concatenate([vl[rs, :], vr[rs, :], s.astype(BF16)], axis=0)
            rows[c][p] = _dot(jnp.concatenate([att[p][c], qe[rs, ls]], axis=1), rhs)
            col = (c % 2) * GLA_CHUNK
            s = dec_t[(c * GLA_CHUNK) // half][:, col:col + 1] * s + ds[p][c]
        p_scr[p] = s
    return jnp.concatenate([jnp.concatenate(r, axis=1) for r in rows], axis=0)


def _gla_state_load(s_ref, p_scr):
    p_scr[...] = jnp.zeros(p_scr.shape, F32)
    for h in range(GLA_HEADS):
        r0, c0 = (h % 2) * GLA_DK, (h % 2) * GLA_DV
        p_scr[h // 2, r0:r0 + GLA_DK, c0:c0 + GLA_DV] = s_ref[0, 0, h]


def _gla_state_store(p_scr, s_ref):
    for h in range(GLA_HEADS):
        r0, c0 = (h % 2) * GLA_DK, (h % 2) * GLA_DV
        s_ref[0, 0, h] = p_scr[h // 2, r0:r0 + GLA_DK, c0:c0 + GLA_DV]


def _ev1_kernel(xp_ref, xs_ref, rtab_ref, ctab_ref, mod_ref, ng_ref, win_ref, wlr_ref, gmat_ref, gb_ref,
                lng_ref, lnb_ref, ws_ref, bst_ref, cum_ref, p0_ref, w1f_ref, w2f_ref,
                mid_ref, pfin_ref, w1b_ref, w2b_ref, p_scr, pg_scr, ps_scr, la_scr):
    i = pl.program_id(0)
    is_p, within, cidx, _, _, _ = _tile_info(jnp.minimum(i, NT - 1))
    _, _, _, first, last, _ = _tile_info(jnp.maximum(i - 1, 0))
    _cast_slabs(w1f_ref, w2f_ref, w1b_ref, w2b_ref)

    @pl.when(i == 0)
    def _():
        pg_scr[...] = jnp.zeros(pg_scr.shape, F32)
        ps_scr[...] = jnp.zeros(ps_scr.shape, F32)
        la_scr[...] = jnp.zeros(la_scr.shape, F32)

    @pl.when(first)
    def _():
        _gla_state_load(p0_ref, p_scr)

    x = _load_x0(is_p, within, xp_ref, xs_ref, rtab_ref, ctab_ref)
    sh1, sc1, _, _, _, _ = _mod_rows(mod_ref, cidx)
    hm = (_rms(x, ng_ref[0:1, :]) * (1.0 + sc1) + sh1).astype(BF16)
    o0 = 2 * SGU_WIDTH
    lr = _dot(hm, wlr_ref[...]).astype(BF16)

    qs = pg_scr[:, 0:QK] * (GLA_DK ** -0.5)
    k = pg_scr[:, QK:2 * QK]
    v = pg_scr[:, 2 * QK:2 * QK + VW]
    g = pg_scr[:, 2 * QK + VW:2 * QK + 2 * VW]
    v_bf = v.astype(BF16)
    c0 = 3 * VW
    mid_ref[:, 2 * VW:3 * VW] = g * _sigmoid(g)
    mid_ref[:, c0:c0 + QK] = qs
    mid_ref[:, c0 + QK:c0 + 2 * QK] = k
    mid_ref[:, c0 + 2 * QK:c0 + 2 * QK + VW] = v
    mid_ref[:, c0 + 2 * QK + VW:c0 + 3 * QK + VW] = la_scr[:, QK:2 * QK]
    prep = _gla_prep(qs, k, la_scr[:, 0:QK], cum_ref[...], reverse=False)

    pg_scr[...] = _dot(hm, win_ref[:, o0:EV_MAIN])
    u = _gelu(ps_scr[:, 0:SGU_WIDTH])
    vg = _gelu(ps_scr[:, SGU_WIDTH:2 * SGU_WIDTH])
    mu = jnp.mean(vg, axis=-1, keepdims=True)
    vc = vg - mu
    vn = (vc * lax.rsqrt(jnp.mean(vc * vc, axis=-1, keepdims=True) + EPS) * lng_ref[...] + lnb_ref[...]).astype(BF16)
    la_scr[...] = _log_sigmoid(_dot(lr, gmat_ref[...]) + gb_ref[...]) * (1.0 / GLA_NORMALIZER)
    prods = _gla_products(prep, v_bf, reverse=False)
    ps_scr[...] = _dot(hm, win_ref[:, 0:o0])

    gd = SGU_WIDTH // SGU_GROUPS
    nch = TM // SGU_CHUNK
    sv_cols = []
    for grp in range(SGU_GROUPS):
        vcat = jnp.concatenate(
            [vn[c * SGU_CHUNK:(c + 1) * SGU_CHUNK, grp * gd:(grp + 1) * gd] for c in range(nch)], axis=1)
        sg = _dot(ws_ref[grp], vcat) + bst_ref[:, grp:grp + 1]
        sv_cols.append(jnp.concatenate([sg[:, c * gd:(c + 1) * gd] for c in range(nch)], axis=0))
    mid_ref[:, 0:VW] = u * jnp.concatenate(sv_cols, axis=1)
    mid_ref[:, VW:2 * VW] = _gla_outputs(prep, prods, v_bf, p_scr, reverse=False)

    @pl.when(last)
    def _():
        _gla_state_store(p_scr, pfin_ref)


def _const_spec(shape):
    nd = len(shape)
    return pl.BlockSpec(shape, lambda i, _nd=nd: (0,) * _nd, pipeline_mode=pl.Buffered(1))


def _params():
    return pltpu.CompilerParams(dimension_semantics=("arbitrary",), vmem_limit_bytes=VMEM_LIMIT)


CAST_STEPS = 64


def _cast_specs(layer):
    r1, r2 = D // CAST_STEPS, D_FF // CAST_STEPS
    in_specs = [pl.BlockSpec((1, r1, D_FF), lambda i: (layer, jnp.minimum(i, CAST_STEPS - 1), 0)),
                pl.BlockSpec((1, r2, D), lambda i: (layer, jnp.minimum(i, CAST_STEPS - 1), 0))]
    out_specs = [pl.BlockSpec((r1, D_FF), lambda i: (jnp.minimum(i, CAST_STEPS - 1), 0)),
                 pl.BlockSpec((r2, D), lambda i: (jnp.minimum(i, CAST_STEPS - 1), 0))]
    shapes = [jax.ShapeDtypeStruct((D, D_FF), BF16), jax.ShapeDtypeStruct((D_FF, D), BF16)]
    return in_specs, out_specs, shapes


def _cast_slabs(w1f_ref, w2f_ref, w1b_ref, w2b_ref):
    w1b_ref[...] = w1f_ref[0].astype(BF16)
    w2b_ref[...] = w2f_ref[0].astype(BF16)


def _even_forward(xp, xs, rtab, ctab, mod, ng, w, w1f, w2f, layer):
    state_blk = (1, 1, GLA_HEADS, GLA_DK, GLA_DV)
    cast_in, cast_out, cast_shapes = _cast_specs(layer)

    def mix_tile(i):
        return jnp.maximum(i - 1, 0)

    return pl.pallas_call(
        _ev1_kernel,
        out_shape=(jax.ShapeDtypeStruct((NT * TM, EV_MID), F32),
                   jax.ShapeDtypeStruct((NSEQ, 1) + state_blk[2:], F32), *cast_shapes),
        grid=(NT + 1,),
        in_specs=[
            pl.BlockSpec((TM, D), lambda i: (jnp.minimum(i, NTP - 1), 0)),
            pl.BlockSpec((TM, D), lambda i: (jnp.maximum(jnp.minimum(i, NT - 1) - NTP, 0), 0)),
            _const_spec((GRID_W, D // 2)), _const_spec((GRID_W, D // 2)),
            _const_spec((NCOND, 6 * D)), _const_spec((4, D)),
            _const_spec((D, EV_MAIN)), _const_spec((D, LANES)), _const_spec((LANES, 2 * QK)),
            _const_spec((1, 2 * QK)),
            _const_spec((1, SGU_WIDTH)), _const_spec((1, SGU_WIDTH)),
            _const_spec((SGU_GROUPS, SGU_CHUNK, SGU_CHUNK)), _const_spec((SGU_CHUNK, SGU_GROUPS)),
            _const_spec((TM, TM)),
            pl.BlockSpec(state_blk, lambda i: (_seq_of(mix_tile(i)), 0, 0, 0, 0)),
            *cast_in,
        ],
        out_specs=(pl.BlockSpec((TM, EV_MID), lambda i: (mix_tile(i), 0)),
                   pl.BlockSpec(state_blk, lambda i: (_seq_of(mix_tile(i)), 0, 0, 0, 0)), *cast_out),
        scratch_shapes=[pltpu.VMEM((GLA_HEADS // 2, 2 * GLA_DK, 2 * GLA_DV), F32),
                        pltpu.VMEM((TM, 2 * QK + 2 * VW), F32), pltpu.VMEM((TM, 2 * SGU_WIDTH), F32),
                        pltpu.VMEM((TM, 2 * QK), F32)],
        compiler_params=_params(),
        name="even_forward",
    )(xp, xs, rtab, ctab, mod, ng, w["win"], w["wlr"], w["gmat"], w["gb"], w["lng"], w["lnb"],
      w["ws"], w["bst"], _cum_mask(False), w["p0"], w1f, w2f)


def _ev2_kernel(xp_ref, xs_ref, rtab_ref, ctab_ref, mod_ref, ng_ref, mid_ref, gn_ref, wout_ref, cum_ref, p0_ref,
                x1_ref, pfin_ref, p_scr, cat_scr):
    i = pl.program_id(0)
    live = i < NT
    _, _, _, first, last, _ = _tile_info(NT - 1 - jnp.minimum(i, NT - 1))
    is_p, within, cidx, _, _, _ = _tile_info(NT - 1 - jnp.maximum(i - 1, 0))

    @pl.when(i == 0)
    def _():
        cat_scr[...] = jnp.zeros(cat_scr.shape, BF16)

    @pl.when(jnp.logical_and(last, live))
    def _():
        _gla_state_load(p0_ref, p_scr)

    nblk = D // SLAB
    ys = [None] * nblk

    def project(j):
        ys[j] = _dot(cat_scr[...], wout_ref[:, j * SLAB:(j + 1) * SLAB])

    c0 = 3 * VW
    qs = mid_ref[:, c0:c0 + QK]
    k = mid_ref[:, c0 + QK:c0 + 2 * QK]
    v_bf = mid_ref[:, c0 + 2 * QK:c0 + 2 * QK + VW].astype(BF16)
    la_b = mid_ref[:, c0 + 2 * QK + VW:c0 + 3 * QK + VW]
    project(0)
    prep = _gla_prep(qs, k, la_b, cum_ref[...], reverse=True)
    project(1)
    prods = _gla_products(prep, v_bf, reverse=True)
    project(2)
    o = mid_ref[:, VW:2 * VW] + _gla_outputs(prep, prods, v_bf, p_scr, reverse=True)
    project(3)
    heads = []
    for h in range(GLA_HEADS):
        oh = o[:, h * GLA_DV:(h + 1) * GLA_DV]
        heads.append(oh * lax.rsqrt(jnp.mean(oh * oh, axis=-1, keepdims=True) + EPS))
    on = jnp.concatenate(heads, axis=1) * gn_ref[...] * mid_ref[:, 2 * VW:3 * VW]
    cat_scr[...] = jnp.concatenate([mid_ref[:, 0:VW], on], axis=1).astype(BF16)

    x = _load_x0(is_p, within, xp_ref, xs_ref, rtab_ref, ctab_ref)
    _, _, g1, _, _, _ = _mod_rows(mod_ref, cidx)
    x1_ref[...] = x + g1 * _rms(jnp.concatenate(ys, axis=1), ng_ref[1:2, :])

    @pl.when(jnp.logical_and(first, live))
    def _():
        _gla_state_store(p_scr, pfin_ref)


def _even_reverse(xp, xs, rtab, ctab, mod, ng, mid, w):
    state_blk = (1, 1, GLA_HEADS, GLA_DK, GLA_DV)
    assert D // SLAB == 4

    def mix_tile(i):
        return NT - 1 - jnp.minimum(i, NT - 1)

    def out_tile(i):
        return NT - 1 - jnp.maximum(i - 1, 0)

    return pl.pallas_call(
        _ev2_kernel,
        out_shape=(jax.ShapeDtypeStruct((NT * TM, D), F32),
                   jax.ShapeDtypeStruct((NSEQ, 1) + state_blk[2:], F32)),
        grid=(NT + 1,),
        in_specs=[
            pl.BlockSpec((TM, D), lambda i: (jnp.minimum(out_tile(i), NTP - 1), 0)),
            pl.BlockSpec((TM, D), lambda i: (jnp.maximum(out_tile(i) - NTP, 0), 0)),
            _const_spec((GRID_W, D // 2)), _const_spec((GRID_W, D // 2)),
            _const_spec((NCOND, 6 * D)), _const_spec((4, D)),
            pl.BlockSpec((TM, EV_MID), lambda i: (mix_tile(i), 0)),
            _const_spec((1, VW)), _const_spec((2 * VW, D)), _const_spec((TM, TM)),
            pl.BlockSpec(state_blk, lambda i: (_seq_of(mix_tile(i)), 1, 0, 0, 0)),
        ],
        out_specs=(pl.BlockSpec((TM, D), lambda i: (out_tile(i), 0)),
                   pl.BlockSpec(state_blk, lambda i: (_seq_of(mix_tile(i)), 0, 0, 0, 0))),
        scratch_shapes=[pltpu.VMEM((GLA_HEADS // 2, 2 * GLA_DK, 2 * GLA_DV), F32),
                        pltpu.VMEM((TM, D), BF16)],
        compiler_params=_params(),
        name="even_reverse",
    )(xp, xs, rtab, ctab, mod, ng, mid, w["gn"], w["wout"], _cum_mask(True), w["p0"])


FF_CHUNK = 1024
MLP_TILES = 2
TMM = MLP_TILES * TM


def _mlp_kernel(x_ref, mod_ref, ng_ref, w1_ref, b1_ref, w2_ref, b2_ref, o_ref, *, tile0):
    _, _, cidx, _, _, _ = _tile_info(tile0 + pl.program_id(0) * MLP_TILES)
    _, _, _, sh2, sc2, g2 = _mod_rows(mod_ref, cidx)
    x = x_ref[...]
    hff = (_rms(x, ng_ref[2:3, :]) * (1.0 + sc2) + sh2).astype(BF16)
    acc = b2_ref[...]
    for j in range(D_FF // FF_CHUNK):
        cs = slice(j * FF_CHUNK, (j + 1) * FF_CHUNK)
        h = _dot(hff, w1_ref[:, cs]) + b1_ref[:, cs]
        h = jnp.square(jnp.maximum(h, 0.0)).astype(BF16)
        acc = acc + _dot(h, w2_ref[cs, :])
    o_ref[...] = x + g2 * _rms(acc, ng_ref[3:4, :])


def _mlp(x, mod, ng, w1, b1, w2, b2, tile0, ntiles):
    assert tile0 % MLP_TILES == 0 and ntiles % MLP_TILES == 0 and TPS % MLP_TILES == 0 and NTP % MLP_TILES == 0
    b0 = tile0 // MLP_TILES
    return pl.pallas_call(
        functools.partial(_mlp_kernel, tile0=tile0),
        out_shape=jax.ShapeDtypeStruct((ntiles * TM, D), F32),
        grid=(ntiles // MLP_TILES,),
        in_specs=[
            pl.BlockSpec((TMM, D), lambda i: (b0 + i, 0)),
            _const_spec((NCOND, 6 * D)), _const_spec((4, D)),
            _const_spec((D, D_FF)), _const_spec((1, D_FF)), _const_spec((D_FF, D)), _const_spec((1, D)),
        ],
        out_specs=pl.BlockSpec((TMM, D), lambda i: (i, 0)),
        compiler_params=_params(), name="mlp",
    )(x, mod, ng, w1, b1, w2, b2)


def _rg_gates(xc, wg_ref, ba_ref, bx_ref, lam_ref, a_scr, b_scr, slabs=range(D_RNN // SLAB)):
    xcb = xc.astype(BF16)
    sp = RG_C * _softplus(-lam_ref[...])
    for s in slabs:
        cs = slice(s * SLAB, (s + 1) * SLAB)
        pre = _dot(xcb[:, cs], wg_ref[s])
        r = _sigmoid(pre[:, 0:SLAB] + ba_ref[:, cs])
        i = _sigmoid(pre[:, SLAB:2 * SLAB] + bx_ref[:, cs])
        z = r * sp[:, cs]
        a = jnp.exp(-z)
        a_scr[:, cs] = a
        u = jnp.tanh(z) * (1.0 + a * a)
        b_scr[:, cs] = jnp.where(u > 0.0, u * lax.rsqrt(u), 0.0) * (i * xc[:, cs])


SEG = TM // SUBLANES


def _seg_perm(transpose):
    rr, cc = np.indices((TM, TM))
    if transpose:
        rr, cc = cc, rr
    return jnp.asarray(np.logical_and(cc // SEG == rr % SUBLANES, cc % SEG == rr // SUBLANES), BF16)


def _rg_scan(a_scr, b_scr, h_scr, c_scr, h0, reverse):
    def body(i, carry):
        hh, cc = carry
        grp = (SEG - 1 - i) if reverse else i
        r0 = pl.multiple_of(grp * SUBLANES, SUBLANES)
        a = a_scr[pl.ds(r0, SUBLANES), :]
        hh = a * hh + b_scr[pl.ds(r0, SUBLANES), :]
        cc = a * cc
        h_scr[pl.ds(r0, SUBLANES), :] = hh
        c_scr[pl.ds(r0, SUBLANES), :] = cc
        return hh, cc

    init = (jnp.zeros((SUBLANES, D_RNN), F32), jnp.ones((SUBLANES, D_RNN), F32))
    b, a = lax.fori_loop(0, SEG, body, init, unroll=True)
    row = lax.broadcasted_iota(jnp.int32, (SUBLANES, D_RNN), 0)
    for s in (1, 2, 4):
        shift = (SUBLANES - s) if reverse else s
        valid = (row < SUBLANES - s) if reverse else (row >= s)
        a_s = pltpu.roll(a, shift, 0)
        b_s = pltpu.roll(b, shift, 0)
        b = jnp.where(valid, a * b_s + b, b)
        a = jnp.where(valid, a * a_s, a)
    after = a * h0 + b
    edge = (row == SUBLANES - 1) if reverse else (row == 0)
    enter = jnp.where(edge, h0, pltpu.roll(after, (SUBLANES - 1) if reverse else 1, 0))
    h = h_scr[...] + c_scr[...] * jnp.concatenate([enter] * SEG, axis=0)
    out = after[0:1, :] if reverse else after[SUBLANES - 1:SUBLANES, :]
    return h, out


def _od1_kernel(x_ref, xn_ref, mod_ref, ng_ref, win_ref, cw_ref, cb_ref, wg_ref, ba_ref, bx_ref, lam_ref,
                perm_ref, s0_ref, w1f_ref, w2f_ref, mid_ref, sfin_ref, w1b_ref, w2b_ref,
                proj_scr, xb_scr, a_scr, b_scr, h_scr, c_scr, hc_scr, tail_scr):
    i = pl.program_id(0)
    _, _, cidx, _, _, _ = _tile_info(jnp.minimum(i, NT - 1))
    _, _, _, first, last, _ = _tile_info(jnp.maximum(i - 1, 0))
    _cast_slabs(w1f_ref, w2f_ref, w1b_ref, w2b_ref)

    @pl.when(i == 0)
    def _():
        proj_scr[...] = jnp.zeros(proj_scr.shape, F32)

    @pl.when(first)
    def _():
        tail_scr[...] = jnp.zeros((2 * SUBLANES, D_RNN), F32)
        hc_scr[...] = s0_ref[0, 0]

    nslab = D_RNN // SLAB
    pw = 2 * D_RNN // nslab

    def project(lhs, s):
        proj_scr[:, s * pw:(s + 1) * pw] = _dot(lhs, win_ref[:, s * pw:(s + 1) * pw])

    sh1, sc1, _, _, _, _ = _mod_rows(mod_ref, cidx)
    xe = jnp.concatenate([x_ref[...], xn_ref[...]], axis=0)
    hm = (_rms(xe, ng_ref[0:1, :]) * (1.0 + sc1) + sh1).astype(BF16)
    hmp = _dot(perm_ref[...], hm[0:TM, :]).astype(BF16)
    lhs = jnp.concatenate([hmp, hm[TM:TM + HALO, :]], axis=0)

    row = lax.broadcasted_iota(jnp.int32, (SUBLANES, D_RNN), 0)
    g30 = proj_scr[TM - 2 * SUBLANES:TM - SUBLANES, 0:D_RNN]
    g31 = proj_scr[TM - SUBLANES:TM, 0:D_RNN]
    g0 = proj_scr[0:SUBLANES, 0:D_RNN]
    nxt = jnp.where(last, 0.0, proj_scr[TM:TM + 1, 0:D_RNN])
    xb_scr[0:SUBLANES, :] = jnp.where(row == 0, pltpu.roll(tail_scr[0:SUBLANES, :], 1, 0), pltpu.roll(g30, 1, 0))
    xb_scr[SUBLANES:2 * SUBLANES, :] = jnp.where(
        row == 0, pltpu.roll(tail_scr[SUBLANES:2 * SUBLANES, :], 1, 0), pltpu.roll(g31, 1, 0))
    xb_scr[2 * SUBLANES:2 * SUBLANES + TM, :] = proj_scr[0:TM, 0:D_RNN]
    xb_scr[2 * SUBLANES + TM:3 * SUBLANES + TM, :] = jnp.where(
        row == SUBLANES - 1, nxt, pltpu.roll(g0, SUBLANES - 1, 0))
    tail_scr[0:SUBLANES, :] = g30
    tail_scr[SUBLANES:2 * SUBLANES, :] = g31

    xc = cb_ref[...]
    for j in range(4):
        xc = xc + cw_ref[j:j + 1, :] * xb_scr[j * SUBLANES:j * SUBLANES + TM, :]
    mid_ref[:, 2 * D_RNN:3 * D_RNN] = xc

    project(lhs, 0)
    mid_ref[:, D_RNN:2 * D_RNN] = _gelu(proj_scr[0:TM, D_RNN:2 * D_RNN])
    for s in range(nslab):
        if s + 1 < nslab:
            project(lhs, s + 1)
        _rg_gates(mid_ref[:, 2 * D_RNN:3 * D_RNN], wg_ref, ba_ref, bx_ref, lam_ref, a_scr, b_scr, range(s, s + 1))

    h, hc_scr[...] = _rg_scan(a_scr, b_scr, h_scr, c_scr, hc_scr[...], reverse=False)
    mid_ref[:, 0:D_RNN] = h

    @pl.when(last)
    def _():
        sfin_ref[0, 0] = hc_scr[...]


def _odd_forward(x, mod, ng, w, w1f, w2f, layer):
    nb16 = TM // HALO
    cast_in, cast_out, cast_shapes = _cast_specs(layer)

    def proj_tile(i):
        return jnp.minimum(i, NT - 1)

    def scan_tile(i):
        return jnp.maximum(i - 1, 0)

    return pl.pallas_call(
        _od1_kernel,
        out_shape=(jax.ShapeDtypeStruct((NT * TM, OD_MID), F32),
                   jax.ShapeDtypeStruct((NSEQ, 1, 1, D_RNN), F32), *cast_shapes),
        grid=(NT + 1,),
        in_specs=[
            pl.BlockSpec((TM, D), lambda i: (proj_tile(i), 0)),
            pl.BlockSpec((HALO, D), lambda i: (jnp.minimum(proj_tile(i) + 1, NT - 1) * nb16, 0)),
            _const_spec((NCOND, 6 * D)), _const_spec((4, D)),
            _const_spec((D, 2 * D_RNN)), _const_spec((4, D_RNN)), _const_spec((1, D_RNN)),
            _const_spec((D_RNN // SLAB, SLAB, 2 * SLAB)),
            _const_spec((1, D_RNN)), _const_spec((1, D_RNN)), _const_spec((1, D_RNN)),
            _const_spec((TM, TM)),
            pl.BlockSpec((1, 1, 1, D_RNN), lambda i: (_seq_of(scan_tile(i)), 0, 0, 0)),
            *cast_in,
        ],
        out_specs=(pl.BlockSpec((TM, OD_MID), lambda i: (scan_tile(i), 0)),
                   pl.BlockSpec((1, 1, 1, D_RNN), lambda i: (_seq_of(scan_tile(i)), 0, 0, 0)), *cast_out),
        scratch_shapes=[
            pltpu.VMEM((TM + HALO, 2 * D_RNN), F32),
            pltpu.VMEM((3 * SUBLANES + TM, D_RNN), F32),
            pltpu.VMEM((TM, D_RNN), F32), pltpu.VMEM((TM, D_RNN), F32), pltpu.VMEM((TM, D_RNN), F32),
            pltpu.VMEM((TM, D_RNN), F32),
            pltpu.VMEM((1, D_RNN), F32), pltpu.VMEM((2 * SUBLANES, D_RNN), F32),
        ],
        compiler_params=_params(),
        name="odd_forward",
    )(x, x, mod, ng, w["win"], w["cw"], w["cb"], w["wg"][0], w["ba"][0:1], w["bx"][0:1], w["lam"][0:1],
      _seg_perm(False), w["s0"], w1f, w2f)


def _od2_kernel(x_ref, mod_ref, ng_ref, mid_ref, wg_ref, ba_ref, bx_ref, lam_ref, wout_ref, perm_ref, s0_ref,
                x1_ref, sfin_ref, a_scr, b_scr, h_scr, c_scr, hc_scr, zp_scr):
    i = pl.program_id(0)
    live = i < NT
    _, _, _, first, last, _ = _tile_info(NT - 1 - jnp.minimum(i, NT - 1))
    _, _, cidx_prev, _, _, _ = _tile_info(NT - 1 - jnp.maximum(i - 1, 0))
    _, _, g1, _, _, _ = _mod_rows(mod_ref, cidx_prev)

    @pl.when(i == 0)
    def _():
        zp_scr[...] = jnp.zeros(zp_scr.shape, BF16)

    @pl.when(jnp.logical_and(last, live))
    def _():
        hc_scr[...] = s0_ref[0, 0]

    xc = mid_ref[:, 2 * D_RNN:3 * D_RNN]
    nslab = D_RNN // SLAB
    z = _dot(perm_ref[...], zp_scr[...]).astype(BF16)
    ys = []
    for s in range(nslab):
        ys.append(_dot(z, wout_ref[:, s * SLAB:(s + 1) * SLAB]))
        _rg_gates(xc, wg_ref, ba_ref, bx_ref, lam_ref, a_scr, b_scr, range(s, s + 1))
    x1_ref[...] = x_ref[...] + g1 * _rms(jnp.concatenate(ys, axis=1), ng_ref[1:2, :])
    h_b, hc_scr[...] = _rg_scan(a_scr, b_scr, h_scr, c_scr, hc_scr[...], reverse=True)
    zp_scr[...] = ((mid_ref[:, 0:D_RNN] + h_b) * mid_ref[:, D_RNN:2 * D_RNN]).astype(BF16)

    @pl.when(jnp.logical_and(first, live))
    def _():
        sfin_ref[0, 0] = hc_scr[...]


def _odd_reverse(x, mod, ng, mid, w):
    def scan_tile(i):
        return NT - 1 - jnp.minimum(i, NT - 1)

    def out_tile(i):
        return NT - 1 - jnp.maximum(i - 1, 0)

    return pl.pallas_call(
        _od2_kernel,
        out_shape=(jax.ShapeDtypeStruct((NT * TM, D), F32),
                   jax.ShapeDtypeStruct((NSEQ, 1, 1, D_RNN), F32)),
        grid=(NT + 1,),
        in_specs=[
            pl.BlockSpec((TM, D), lambda i: (out_tile(i), 0)),
            _const_spec((NCOND, 6 * D)), _const_spec((4, D)),
            pl.BlockSpec((TM, OD_MID), lambda i: (scan_tile(i), 0)),
            _const_spec((D_RNN // SLAB, SLAB, 2 * SLAB)),
            _const_spec((1, D_RNN)), _const_spec((1, D_RNN)), _const_spec((1, D_RNN)),
            _const_spec((D_RNN, D)), _const_spec((TM, TM)),
            pl.BlockSpec((1, 1, 1, D_RNN), lambda i: (_seq_of(scan_tile(i)), 1, 0, 0)),
        ],
        out_specs=(pl.BlockSpec((TM, D), lambda i: (out_tile(i), 0)),
                   pl.BlockSpec((1, 1, 1, D_RNN), lambda i: (_seq_of(scan_tile(i)), 0, 0, 0))),
        scratch_shapes=[
            pltpu.VMEM((TM, D_RNN), F32), pltpu.VMEM((TM, D_RNN), F32), pltpu.VMEM((TM, D_RNN), F32),
            pltpu.VMEM((TM, D_RNN), F32),
            pltpu.VMEM((1, D_RNN), F32), pltpu.VMEM((TM, D_RNN), BF16),
        ],
        compiler_params=_params(),
        name="odd_reverse",
    )(x, mod, ng, mid, w["wg"][1], w["ba"][1:2], w["bx"][1:2], w["lam"][1:2], w["wout"], _seg_perm(True), w["s0"])


def _pos_tables():
    n = D // 4
    omega = 1.0 / (10000.0 ** (jnp.arange(n, dtype=F32) / n))
    idx = jnp.arange(GRID_W, dtype=F32)[:, None] * omega
    tab = jnp.concatenate([jnp.sin(idx), jnp.cos(idx)], axis=-1)
    return tab, tab


def _block_diag_slabs(w):
    per = SLAB // RG_BS
    w = w.reshape(D_RNN // SLAB, per, RG_BS, RG_BS)
    eye = jnp.eye(per, dtype=w.dtype)
    return jnp.einsum("spij,pq->spiqj", w, eye).reshape(D_RNN // SLAB, SLAB, SLAB)


def kernel(x_prompt, x_sample, c, state_gla, state_rglru, c_ctx, mod_w, mod_b, norm_g, mlp_w1, mlp_b1, mlp_w2,
           mlp_b2, ev_w_in, ev_w_out, sgu_ln_g, sgu_ln_b, sgu_ws, sgu_bs, gla_gate_w2, gla_gate_b, gla_norm_g,
           rg_w_in, rg_conv_w, rg_conv_b, rg_wa, rg_ba, rg_wx, rg_bx, rg_L, rg_w_out):
    assert x_prompt.shape == (BATCH, SEQ, D) and x_sample.shape == (DEC_BATCH, DEC_SEQ, D)
    assert SEQ == TM and DEC_SEQ % TM == 0 and DEPTH == 2
    xp = x_prompt.reshape(NTP * TM, D)
    xs = x_sample.reshape(NTS * TM, D)
    cond8 = jnp.concatenate([c_ctx[None, :], c, jnp.zeros((NCOND - 1 - DEC_BATCH, D), F32)], axis=0)
    mods = _modulation(cond8, mod_w, mod_b)
    rtab, ctab = _pos_tables()

    gmat = jnp.zeros((LANES, 2 * QK), F32)
    gmat = gmat.at[0:GLA_RANK, 0:QK].set(gla_gate_w2[0, 0])
    gmat = gmat.at[GLA_RANK:2 * GLA_RANK, QK:2 * QK].set(gla_gate_w2[0, 1])
    s0_gla = jnp.concatenate([jnp.zeros((BATCH,) + state_gla.shape[2:], F32), state_gla[:, 0]], axis=0)
    ev = {
        "win": ev_w_in[0, :, 0:EV_MAIN].astype(BF16),
        "wlr": jnp.pad(ev_w_in[0, :, EV_MAIN:], ((0, 0), (0, LANES - 2 * GLA_RANK))).astype(BF16),
        "gmat": gmat.astype(BF16),
        "gb": gla_gate_b[0].reshape(1, 2 * QK),
        "lng": sgu_ln_g[0].reshape(1, SGU_WIDTH),
        "lnb": sgu_ln_b[0].reshape(1, SGU_WIDTH),
        "ws": sgu_ws[0].astype(BF16),
        "bst": sgu_bs[0].T,
        "gn": gla_norm_g[0].reshape(1, VW),
        "wout": ev_w_out[0].astype(BF16),
        "p0": s0_gla,
    }
    mid, pf, w1b, w2b = _even_forward(xp, xs, rtab, ctab, mods[0], norm_g[0], ev, mlp_w1, mlp_w2, 0)
    x1, pb = _even_reverse(xp, xs, rtab, ctab, mods[0], norm_g[0], mid, ev)
    x2 = _mlp(x1, mods[0], norm_g[0], w1b, mlp_b1[0].reshape(1, D_FF), w2b, mlp_b2[0].reshape(1, D), 0, NT)
    new_gla = jnp.stack([pf[:BATCH, 0], pb[:BATCH, 0]], axis=1)[:, None]

    s0_rg = jnp.concatenate([jnp.zeros((BATCH, 2, D_RNN), F32), state_rglru[:, 0]], axis=0)
    od = {
        "win": rg_w_in[0].astype(BF16),
        "cw": rg_conv_w[0],
        "cb": rg_conv_b[0].reshape(1, D_RNN),
        "wg": jnp.stack([
            jnp.concatenate([_block_diag_slabs(rg_wa[0, d]), _block_diag_slabs(rg_wx[0, d])], axis=-1)
            for d in range(2)], axis=0).astype(BF16),
        "ba": rg_ba[0], "bx": rg_bx[0], "lam": rg_L[0],
        "wout": rg_w_out[0].astype(BF16),
        "s0": s0_rg.reshape(NSEQ, 2, 1, D_RNN),
    }
    mid1, sf, w1b, w2b = _odd_forward(x2, mods[1], norm_g[1], od, mlp_w1, mlp_w2, 1)
    x3, sb = _odd_reverse(x2, mods[1], norm_g[1], mid1, od)
    mlp1 = (mods[1], norm_g[1], w1b, mlp_b1[1].reshape(1, D_FF), w2b, mlp_b2[1].reshape(1, D))
    y_prompt = _mlp(x3, *mlp1, 0, NTP)
    y_sample = _mlp(x3, *mlp1, NTP, NTS)
    new_rg = jnp.stack([sf[:BATCH, 0, 0], sb[:BATCH, 0, 0]], axis=1)[:, None]
    return (y_prompt.reshape(BATCH, SEQ, D), y_sample.reshape(DEC_BATCH, DEC_SEQ, D), new_gla, new_rg)
```

```python
import functools

import jax
import jax.numpy as jnp
import numpy as np
from jax import lax
from jax.experimental import pallas as pl
from jax.experimental.pallas import tpu as pltpu

D = 1024
BATCH = 16
SEQ = 256
DEPTH = 2
DEC_BATCH = 4
DEC_SEQ = 4096
GRID_W = 64
D_FF = 4 * D
EPS = 1e-6
SGU_CHUNK = 128
SGU_GROUPS = 4
SGU_WIDTH = D // 2
GLA_HEADS = 4
GLA_DV = 128
GLA_DK = 64
GLA_RANK = 16
GLA_NORMALIZER = 16.0
GLA_CHUNK = 64
QK = GLA_HEADS * GLA_DK
VW = GLA_HEADS * GLA_DV
EV_MAIN = 2 * SGU_WIDTH + 2 * QK + 2 * VW
D_RNN = D
RG_BLOCKS = 16
RG_BS = D_RNN // RG_BLOCKS
RG_C = 8.0
LANES = 128
SUBLANES = 8
SLAB = 256

TM = 256
NTP = BATCH * SEQ // TM
TPS = DEC_SEQ // TM
NTS = DEC_BATCH * TPS
NT = NTP + NTS
NSEQ = BATCH + DEC_BATCH
NCOND = 8
HALO = 16
EV_MID = 2 * VW + VW + 2 * QK + VW + QK
OD_MID = 3 * D_RNN
VMEM_LIMIT = 56 * 1024 * 1024

F32 = jnp.float32
BF16 = jnp.bfloat16


def _tile_info(t):
    is_p = t < NTP
    ts = jnp.maximum(t - NTP, 0)
    sq = ts // TPS
    within = ts % TPS
    cidx = jnp.where(is_p, 0, 1 + sq)
    first = jnp.logical_or(is_p, within == 0)
    last = jnp.logical_or(is_p, within == TPS - 1)
    seq = jnp.where(is_p, t, BATCH + sq)
    return is_p, within, cidx, first, last, seq


def _seq_of(t):
    return jnp.where(t < NTP, t, BATCH + jnp.maximum(t - NTP, 0) // TPS)


def _rms(x, g):
    return x * lax.rsqrt(jnp.mean(x * x, axis=-1, keepdims=True) + EPS) * g


def _dot(a, b):
    return jnp.dot(a, b, preferred_element_type=F32)


def _dot_nt(a, b):
    return lax.dot_general(a, b, (((1,), (1,)), ((), ())), preferred_element_type=F32)


def _dot_tn(a, b):
    return lax.dot_general(a, b, (((0,), (0,)), ((), ())), preferred_element_type=F32)


def _split3(x):
    hi = x.astype(BF16)
    r1 = x - hi.astype(F32)
    mid = r1.astype(BF16)
    lo = (r1 - mid.astype(F32)).astype(BF16)
    return hi, mid, lo


def _dot_exact_lhs(m, parts):
    return _dot(m, parts[0]) + _dot(m, parts[1]) + _dot(m, parts[2])


def _log_sigmoid(x):
    return jnp.minimum(x, 0.0) - jnp.log(1.0 + jnp.exp(-jnp.abs(x)))


def _softplus(x):
    return jnp.maximum(x, 0.0) + jnp.log(1.0 + jnp.exp(-jnp.abs(x)))


LOG2E = 1.4426950408889634
_GELU_C = 0.7978845608028654 * LOG2E


def _sigmoid(x):
    return 1.0 / (1.0 + jnp.exp2(x * (-LOG2E)))


def _gelu(x):
    return x / (1.0 + jnp.exp2(x * ((-2.0 * _GELU_C) + (-2.0 * 0.044715 * _GELU_C) * (x * x))))


def _mod_rows(mod_ref, cidx):
    m = mod_ref[pl.ds(cidx, 1), :]
    return [m[:, j * D:(j + 1) * D] for j in range(6)]


def _load_x0(is_p, within, xp_ref, xs_ref, rtab_ref, ctab_ref):
    rows_per_tile = TM // GRID_W
    r0 = within * rows_per_tile
    posr = jnp.concatenate(
        [jnp.broadcast_to(rtab_ref[pl.ds(r0 + j, 1), :], (GRID_W, D // 2)) for j in range(rows_per_tile)],
        axis=0)
    posc = jnp.concatenate([ctab_ref[...]] * rows_per_tile, axis=0)
    pos = jnp.concatenate([posr, posc], axis=1)
    return jnp.where(is_p, xp_ref[...], xs_ref[...] + pos)


def _mod_kernel(cond_ref, w_ref, b_ref, o_ref):
    c = cond_ref[...]
    sc = (c * _sigmoid(c)).astype(BF16)
    o_ref[0] = _dot(sc, w_ref[0].astype(BF16)) + b_ref[0]


def _modulation(cond8, mod_w, mod_b):
    nb = 6 * D // D
    return pl.pallas_call(
        _mod_kernel,
        out_shape=jax.ShapeDtypeStruct((DEPTH, NCOND, 6 * D), F32),
        grid=(DEPTH, nb),
        in_specs=[
            pl.BlockSpec((NCOND, D), lambda l, j: (0, 0)),
            pl.BlockSpec((1, D, D), lambda l, j: (l, 0, j)),
            pl.BlockSpec((1, 1, D), lambda l, j: (l, 0, j)),
        ],
        out_specs=pl.BlockSpec((1, NCOND, D), lambda l, j: (l, 0, j)),
        compiler_params=pltpu.CompilerParams(
            dimension_semantics=("arbitrary", "arbitrary"), vmem_limit_bytes=VMEM_LIMIT),
        name="modulation",
    )(cond8, mod_w, mod_b.reshape(DEPTH, 1, 6 * D))


def _chunk_rows(c):
    return slice(c * GLA_CHUNK, (c + 1) * GLA_CHUNK)


def _pair_lanes(p):
    return slice(p * LANES, (p + 1) * LANES)


def _gla_order(reverse):
    n = TM // GLA_CHUNK
    return list(reversed(range(n))) if reverse else list(range(n))


def _cum_mask(reverse):
    ri, ci = np.indices((TM, TM))
    order = (ci >= ri) if reverse else (ci <= ri)
    return jnp.asarray(np.logical_and(ri // GLA_CHUNK == ci // GLA_CHUNK, order), BF16)


def _gla_prep(qs, k, la, cum_m, reverse):
    n = TM // GLA_CHUNK
    b = _dot_exact_lhs(cum_m, _split3(la))
    ends = [c * GLA_CHUNK if reverse else (c + 1) * GLA_CHUNK - 1 for c in range(n)]
    btot = jnp.concatenate([jnp.broadcast_to(b[e:e + 1, :], (GLA_CHUNK, QK)) for e in ends], axis=0)
    qe = (qs * jnp.exp(b)).astype(BF16)
    ke = k * jnp.exp(-b)
    kd = (k * jnp.exp(btot - b)).astype(BF16)
    dec = jnp.exp(btot)
    lane = lax.broadcasted_iota(jnp.int32, (TM, QK), 1) % LANES
    ke_h = (jnp.where(lane < GLA_DK, ke, 0.0).astype(BF16), jnp.where(lane >= GLA_DK, ke, 0.0).astype(BF16))
    return qe, ke_h, kd, dec


def _gla_products(prep, v_bf, reverse):
    qe, ke_h, kd, _ = prep
    n = TM // GLA_CHUNK
    cr = lax.broadcasted_iota(jnp.int32, (GLA_CHUNK, 2 * GLA_CHUNK), 0)
    cc = lax.broadcasted_iota(jnp.int32, (GLA_CHUNK, 2 * GLA_CHUNK), 1) % GLA_CHUNK
    cmask = (cc >= cr) if reverse else (cc <= cr)
    br = lax.broadcasted_iota(jnp.int32, (2 * GLA_DK, 2 * GLA_DV), 0)
    bc = lax.broadcasted_iota(jnp.int32, (2 * GLA_DK, 2 * GLA_DV), 1)
    diag = (br < GLA_DK) == (bc < GLA_DV)
    att = [[None] * n for _ in range(GLA_HEADS // 2)]
    ds = [[None] * n for _ in range(GLA_HEADS // 2)]
    for p in range(GLA_HEADS // 2):
        vp = v_bf[:, p * 2 * GLA_DV:(p + 1) * 2 * GLA_DV]
        for c in _gla_order(reverse):
            rs, ls = _chunk_rows(c), _pair_lanes(p)
            kk = jnp.concatenate([ke_h[0][rs, ls], ke_h[1][rs, ls]], axis=0)
            att[p][c] = jnp.where(cmask, _dot_nt(qe[rs, ls], kk), 0.0).astype(BF16)
            ds[p][c] = jnp.where(diag, _dot_tn(kd[rs, ls], vp[rs, :]), 0.0)
    return att, ds


def _gla_outputs(prep, prods, v_bf, p_scr, reverse):
    qe, _, _, dec = prep
    att, ds = prods
    n = TM // GLA_CHUNK
    half = TM // 2
    lane_v = lax.broadcasted_iota(jnp.int32, (TM, 2 * GLA_DV), 1)
    rows = [[None] * (GLA_HEADS // 2) for _ in range(n)]
    for p in range(GLA_HEADS // 2):
        ls = _pair_lanes(p)
        vp = v_bf[:, p * 2 * GLA_DV:(p + 1) * 2 * GLA_DV]
        vl = jnp.where(lane_v < GLA_DV, vp, jnp.zeros_like(vp))
        vr = jnp.where(lane_v >= GLA_DV, vp, jnp.zeros_like(vp))
        dec_t = (dec[0:half, ls].T, dec[half:TM, ls].T)
        s = p_scr[p]
        for c in _gla_order(reverse):
            rs = _chunk_rows(c)
            rhs = jnp.concatenate([vl[rs, :], vr[rs, :], s.astype(BF16)], axis=0)
            rows[c][p] = _dot(jnp.concatenate([att[p][c], qe[rs, ls]], axis=1), rhs)
            col = (c % 2) * GLA_CHUNK
            s = dec_t[(c * GLA_CHUNK) // half][:, col:col + 1] * s + ds[p][c]
        p_scr[p] = s
    return jnp.concatenate([jnp.concatenate(r, axis=1) for r in rows], axis=0)


def _gla_state_load(s_ref, p_scr):
    p_scr[...] = jnp.zeros(p_scr.shape, F32)
    for h in range(GLA_HEADS):
        r0, c0 = (h % 2) * GLA_DK, (h % 2) * GLA_DV
        p_scr[h // 2, r0:r0 + GLA_DK, c0:c0 + GLA_DV] = s_ref[0, 0, h]


def _gla_state_store(p_scr, s_ref):
    for h in range(GLA_HEADS):
        r0, c0 = (h % 2) * GLA_DK, (h % 2) * GLA_DV
        s_ref[0, 0, h] = p_scr[h // 2, r0:r0 + GLA_DK, c0:c0 + GLA_DV]


def _ev1_kernel(xp_ref, xs_ref, rtab_ref, ctab_ref, mod_ref, ng_ref, win_ref, wlr_ref, gmat_ref, gb_ref,
                lng_ref, lnb_ref, ws_ref, bst_ref, cum_ref, p0_ref, w1f_ref, w2f_ref,
                mid_ref, pfin_ref, w1b_ref, w2b_ref, p_scr, pg_scr, ps_scr, la_scr):
    i = pl.program_id(0)
    is_p, within, cidx, _, _, _ = _tile_info(jnp.minimum(i, NT - 1))
    _, _, _, first, last, _ = _tile_info(jnp.maximum(i - 1, 0))
    _cast_slabs(w1f_ref, w2f_ref, w1b_ref, w2b_ref)

    @pl.when(i == 0)
    def _():
        pg_scr[...] = jnp.zeros(pg_scr.shape, F32)
        ps_scr[...] = jnp.zeros(ps_scr.shape, F32)
        la_scr[...] = jnp.zeros(la_scr.shape, F32)

    @pl.when(first)
    def _():
        _gla_state_load(p0_ref, p_scr)

    x = _load_x0(is_p, within, xp_ref, xs_ref, rtab_ref, ctab_ref)
    sh1, sc1, _, _, _, _ = _mod_rows(mod_ref, cidx)
    hm = (_rms(x, ng_ref[0:1, :]) * (1.0 + sc1) + sh1).astype(BF16)
    o0 = 2 * SGU_WIDTH
    lr = _dot(hm, wlr_ref[...]).astype(BF16)

    qs = pg_scr[:, 0:QK] * (GLA_DK ** -0.5)
    k = pg_scr[:, QK:2 * QK]
    v = pg_scr[:, 2 * QK:2 * QK + VW]
    g = pg_scr[:, 2 * QK + VW:2 * QK + 2 * VW]
    v_bf = v.astype(BF16)
    c0 = 3 * VW
    mid_ref[:, 2 * VW:3 * VW] = g * _sigmoid(g)
    mid_ref[:, c0:c0 + QK] = qs
    mid_ref[:, c0 + QK:c0 + 2 * QK] = k
    mid_ref[:, c0 + 2 * QK:c0 + 2 * QK + VW] = v
    mid_ref[:, c0 + 2 * QK + VW:c0 + 3 * QK + VW] = la_scr[:, QK:2 * QK]
    prep = _gla_prep(qs, k, la_scr[:, 0:QK], cum_ref[...], reverse=False)

    pg_scr[...] = _dot(hm, win_ref[:, o0:EV_MAIN])
    u = _gelu(ps_scr[:, 0:SGU_WIDTH])
    vg = _gelu(ps_scr[:, SGU_WIDTH:2 * SGU_WIDTH])
    mu = jnp.mean(vg, axis=-1, keepdims=True)
    vc = vg - mu
    vn = (vc * lax.rsqrt(jnp.mean(vc * vc, axis=-1, keepdims=True) + EPS) * lng_ref[...] + lnb_ref[...]).astype(BF16)
    la_scr[...] = _log_sigmoid(_dot(lr, gmat_ref[...]) + gb_ref[...]) * (1.0 / GLA_NORMALIZER)
    prods = _gla_products(prep, v_bf, reverse=False)
    ps_scr[...] = _dot(hm, win_ref[:, 0:o0])

    gd = SGU_WIDTH // SGU_GROUPS
    nch = TM // SGU_CHUNK
    sv_cols = []
    for grp in range(SGU_GROUPS):
        vcat = jnp.concatenate(
            [vn[c * SGU_CHUNK:(c + 1) * SGU_CHUNK, grp * gd:(grp + 1) * gd] for c in range(nch)], axis=1)
        sg = _dot(ws_ref[grp], vcat) + bst_ref[:, grp:grp + 1]
        sv_cols.append(jnp.concatenate([sg[:, c * gd:(c + 1) * gd] for c in range(nch)], axis=0))
    mid_ref[:, 0:VW] = u * jnp.concatenate(sv_cols, axis=1)
    mid_ref[:, VW:2 * VW] = _gla_outputs(prep, prods, v_bf, p_scr, reverse=False)

    @pl.when(last)
    def _():
        _gla_state_store(p_scr, pfin_ref)


def _const_spec(shape):
    nd = len(shape)
    return pl.BlockSpec(shape, lambda i, _nd=nd: (0,) * _nd, pipeline_mode=pl.Buffered(1))


def _params():
    return pltpu.CompilerParams(dimension_semantics=("arbitrary",), vmem_limit_bytes=VMEM_LIMIT)


CAST_STEPS = 64


def _cast_specs(layer):
    r1, r2 = D // CAST_STEPS, D_FF // CAST_STEPS
    in_specs = [pl.BlockSpec((1, r1, D_FF), lambda i: (layer, jnp.minimum(i, CAST_STEPS - 1), 0)),
                pl.BlockSpec((1, r2, D), lambda i: (layer, jnp.minimum(i, CAST_STEPS - 1), 0))]
    out_specs = [pl.BlockSpec((r1, D_FF), lambda i: (jnp.minimum(i, CAST_STEPS - 1), 0)),
                 pl.BlockSpec((r2, D), lambda i: (jnp.minimum(i, CAST_STEPS - 1), 0))]
    shapes = [jax.ShapeDtypeStruct((D, D_FF), BF16), jax.ShapeDtypeStruct((D_FF, D), BF16)]
    return in_specs, out_specs, shapes


def _cast_slabs(w1f_ref, w2f_ref, w1b_ref, w2b_ref):
    w1b_ref[...] = w1f_ref[0].astype(BF16)
    w2b_ref[...] = w2f_ref[0].astype(BF16)


def _even_forward(xp, xs, rtab, ctab, mod, ng, w, w1f, w2f, layer):
    state_blk = (1, 1, GLA_HEADS, GLA_DK, GLA_DV)
    cast_in, cast_out, cast_shapes = _cast_specs(layer)

    def mix_tile(i):
        return jnp.maximum(i - 1, 0)

    return pl.pallas_call(
        _ev1_kernel,
        out_shape=(jax.ShapeDtypeStruct((NT * TM, EV_MID), F32),
                   jax.ShapeDtypeStruct((NSEQ, 1) + state_blk[2:], F32), *cast_shapes),
        grid=(NT + 1,),
        in_specs=[
            pl.BlockSpec((TM, D), lambda i: (jnp.minimum(i, NTP - 1), 0)),
            pl.BlockSpec((TM, D), lambda i: (jnp.maximum(jnp.minimum(i, NT - 1) - NTP, 0), 0)),
            _const_spec((GRID_W, D // 2)), _const_spec((GRID_W, D // 2)),
            _const_spec((NCOND, 6 * D)), _const_spec((4, D)),
            _const_spec((D, EV_MAIN)), _const_spec((D, LANES)), _const_spec((LANES, 2 * QK)),
            _const_spec((1, 2 * QK)),
            _const_spec((1, SGU_WIDTH)), _const_spec((1, SGU_WIDTH)),
            _const_spec((SGU_GROUPS, SGU_CHUNK, SGU_CHUNK)), _const_spec((SGU_CHUNK, SGU_GROUPS)),
            _const_spec((TM, TM)),
            pl.BlockSpec(state_blk, lambda i: (_seq_of(mix_tile(i)), 0, 0, 0, 0)),
            *cast_in,
        ],
        out_specs=(pl.BlockSpec((TM, EV_MID), lambda i: (mix_tile(i), 0)),
                   pl.BlockSpec(state_blk, lambda i: (_seq_of(mix_tile(i)), 0, 0, 0, 0)), *cast_out),
        scratch_shapes=[pltpu.VMEM((GLA_HEADS // 2, 2 * GLA_DK, 2 * GLA_DV), F32),
                        pltpu.VMEM((TM, 2 * QK + 2 * VW), F32), pltpu.VMEM((TM, 2 * SGU_WIDTH), F32),
                        pltpu.VMEM((TM, 2 * QK), F32)],
        compiler_params=_params(),
        name="even_forward",
    )(xp, xs, rtab, ctab, mod, ng, w["win"], w["wlr"], w["gmat"], w["gb"], w["lng"], w["lnb"],
      w["ws"], w["bst"], _cum_mask(False), w["p0"], w1f, w2f)


def _ev2_kernel(xp_ref, xs_ref, rtab_ref, ctab_ref, mod_ref, ng_ref, mid_ref, gn_ref, wout_ref, cum_ref, p0_ref,
                x1_ref, pfin_ref, p_scr, cat_scr):
    i = pl.program_id(0)
    live = i < NT
    _, _, _, first, last, _ = _tile_info(NT - 1 - jnp.minimum(i, NT - 1))
    is_p, within, cidx, _, _, _ = _tile_info(NT - 1 - jnp.maximum(i - 1, 0))

    @pl.when(i == 0)
    def _():
        cat_scr[...] = jnp.zeros(cat_scr.shape, BF16)

    @pl.when(jnp.logical_and(last, live))
    def _():
        _gla_state_load(p0_ref, p_scr)

    nblk = D // SLAB
    ys = [None] * nblk

    def project(j):
        ys[j] = _dot(cat_scr[...], wout_ref[:, j * SLAB:(j + 1) * SLAB])

    c0 = 3 * VW
    qs = mid_ref[:, c0:c0 + QK]
    k = mid_ref[:, c0 + QK:c0 + 2 * QK]
    v_bf = mid_ref[:, c0 + 2 * QK:c0 + 2 * QK + VW].astype(BF16)
    la_b = mid_ref[:, c0 + 2 * QK + VW:c0 + 3 * QK + VW]
    project(0)
    prep = _gla_prep(qs, k, la_b, cum_ref[...], reverse=True)
    project(1)
    prods = _gla_products(prep, v_bf, reverse=True)
    project(2)
    o = mid_ref[:, VW:2 * VW] + _gla_outputs(prep, prods, v_bf, p_scr, reverse=True)
    project(3)
    heads = []
    for h in range(GLA_HEADS):
        oh = o[:, h * GLA_DV:(h + 1) * GLA_DV]
        heads.append(oh * lax.rsqrt(jnp.mean(oh * oh, axis=-1, keepdims=True) + EPS))
    on = jnp.concatenate(heads, axis=1) * gn_ref[...] * mid_ref[:, 2 * VW:3 * VW]
    cat_scr[...] = jnp.concatenate([mid_ref[:, 0:VW], on], axis=1).astype(BF16)

    x = _load_x0(is_p, within, xp_ref, xs_ref, rtab_ref, ctab_ref)
    _, _, g1, _, _, _ = _mod_rows(mod_ref, cidx)
    x1_ref[...] = x + g1 * _rms(jnp.concatenate(ys, axis=1), ng_ref[1:2, :])

    @pl.when(jnp.logical_and(first, live))
    def _():
        _gla_state_store(p_scr, pfin_ref)


def _even_reverse(xp, xs, rtab, ctab, mod, ng, mid, w):
    state_blk = (1, 1, GLA_HEADS, GLA_DK, GLA_DV)
    assert D // SLAB == 4

    def mix_tile(i):
        return NT - 1 - jnp.minimum(i, NT - 1)

    def out_tile(i):
        return NT - 1 - jnp.maximum(i - 1, 0)

    return pl.pallas_call(
        _ev2_kernel,
        out_shape=(jax.ShapeDtypeStruct((NT * TM, D), F32),
                   jax.ShapeDtypeStruct((NSEQ, 1) + state_blk[2:], F32)),
        grid=(NT + 1,),
        in_specs=[
            pl.BlockSpec((TM, D), lambda i: (jnp.minimum(out_tile(i), NTP - 1), 0)),
            pl.BlockSpec((TM, D), lambda i: (jnp.maximum(out_tile(i) - NTP, 0), 0)),
            _const_spec((GRID_W, D // 2)), _const_spec((GRID_W, D // 2)),
            _const_spec((NCOND, 6 * D)), _const_spec((4, D)),
            pl.BlockSpec((TM, EV_MID), lambda i: (mix_tile(i), 0)),
            _const_spec((1, VW)), _const_spec((2 * VW, D)), _const_spec((TM, TM)),
            pl.BlockSpec(state_blk, lambda i: (_seq_of(mix_tile(i)), 1, 0, 0, 0)),
        ],
        out_specs=(pl.BlockSpec((TM, D), lambda i: (out_tile(i), 0)),
                   pl.BlockSpec(state_blk, lambda i: (_seq_of(mix_tile(i)), 0, 0, 0, 0))),
        scratch_shapes=[pltpu.VMEM((GLA_HEADS // 2, 2 * GLA_DK, 2 * GLA_DV), F32),
                        pltpu.VMEM((TM, D), BF16)],
        compiler_params=_params(),
        name="even_reverse",
    )(xp, xs, rtab, ctab, mod, ng, mid, w["gn"], w["wout"], _cum_mask(True), w["p0"])


FF_CHUNK = 1024
MLP_TILES = 2
TMM = MLP_TILES * TM


def _mlp_kernel(x_ref, mod_ref, ng_ref, w1_ref, b1_ref, w2_ref, b2_ref, o_ref, *, tile0):
    _, _, cidx, _, _, _ = _tile_info(tile0 + pl.program_id(0) * MLP_TILES)
    _, _, _, sh2, sc2, g2 = _mod_rows(mod_ref, cidx)
    x = x_ref[...]
    hff = (_rms(x, ng_ref[2:3, :]) * (1.0 + sc2) + sh2).astype(BF16)
    acc = b2_ref[...]
    for j in range(D_FF // FF_CHUNK):
        cs = slice(j * FF_CHUNK, (j + 1) * FF_CHUNK)
        h = _dot(hff, w1_ref[:, cs]) + b1_ref[:, cs]
        h = jnp.square(jnp.maximum(h, 0.0)).astype(BF16)
        acc = acc + _dot(h, w2_ref[cs, :])
    o_ref[...] = x + g2 * _rms(acc, ng_ref[3:4, :])


def _mlp(x, mod, ng, w1, b1, w2, b2, tile0, ntiles):
    assert tile0 % MLP_TILES == 0 and ntiles % MLP_TILES == 0 and TPS % MLP_TILES == 0 and NTP % MLP_TILES == 0
    b0 = tile0 // MLP_TILES
    return pl.pallas_call(
        functools.partial(_mlp_kernel, tile0=tile0),
        out_shape=jax.ShapeDtypeStruct((ntiles * TM, D), F32),
        grid=(ntiles // MLP_TILES,),
        in_specs=[
            pl.BlockSpec((TMM, D), lambda i: (b0 + i, 0)),
            _const_spec((NCOND, 6 * D)), _const_spec((4, D)),
            _const_spec((D, D_FF)), _const_spec((1, D_FF)), _const_spec((D_FF, D)), _const_spec((1, D)),
        ],
        out_specs=pl.BlockSpec((TMM, D), lambda i: (i, 0)),
        compiler_params=_params(), name="mlp",
    )(x, mod, ng, w1, b1, w2, b2)


def _rg_gates(xc, wg_ref, ba_ref, bx_ref, lam_ref, a_scr, b_scr, slabs=range(D_RNN // SLAB)):
    xcb = xc.astype(BF16)
    sp = RG_C * _softplus(-lam_ref[...])
    for s in slabs:
        cs = slice(s * SLAB, (s + 1) * SLAB)
        pre = _dot(xcb[:, cs], wg_ref[s])
        r = _sigmoid(pre[:, 0:SLAB] + ba_ref[:, cs])
        i = _sigmoid(pre[:, SLAB:2 * SLAB] + bx_ref[:, cs])
        z = r * sp[:, cs]
        a = jnp.exp(-z)
        a_scr[:, cs] = a
        u = jnp.tanh(z) * (1.0 + a * a)
        b_scr[:, cs] = jnp.where(u > 0.0, u * lax.rsqrt(u), 0.0) * (i * xc[:, cs])


SEG = TM // SUBLANES


def _seg_perm(transpose):
    rr, cc = np.indices((TM, TM))
    if transpose:
        rr, cc = cc, rr
    return jnp.asarray(np.logical_and(cc // SEG == rr % SUBLANES, cc % SEG == rr // SUBLANES), BF16)


def _rg_scan(a_scr, b_scr, h_scr, c_scr, h0, reverse):
    def body(i, carry):
        hh, cc = carry
        grp = (SEG - 1 - i) if reverse else i
        r0 = pl.multiple_of(grp * SUBLANES, SUBLANES)
        a = a_scr[pl.ds(r0, SUBLANES), :]
        hh = a * hh + b_scr[pl.ds(r0, SUBLANES), :]
        cc = a * cc
        h_scr[pl.ds(r0, SUBLANES), :] = hh
        c_scr[pl.ds(r0, SUBLANES), :] = cc
        return hh, cc

    init = (jnp.zeros((SUBLANES, D_RNN), F32), jnp.ones((SUBLANES, D_RNN), F32))
    b, a = lax.fori_loop(0, SEG, body, init, unroll=4 if reverse else True)
    row = lax.broadcasted_iota(jnp.int32, (SUBLANES, D_RNN), 0)
    for s in (1, 2, 4):
        shift = (SUBLANES - s) if reverse else s
        valid = (row < SUBLANES - s) if reverse else (row >= s)
        a_s = pltpu.roll(a, shift, 0)
        b_s = pltpu.roll(b, shift, 0)
        b = jnp.where(valid, a * b_s + b, b)
        a = jnp.where(valid, a * a_s, a)
    after = a * h0 + b
    edge = (row == SUBLANES - 1) if reverse else (row == 0)
    enter = jnp.where(edge, h0, pltpu.roll(after, (SUBLANES - 1) if reverse else 1, 0))
    h = h_scr[...] + c_scr[...] * jnp.concatenate([enter] * SEG, axis=0)
    out = after[0:1, :] if reverse else after[SUBLANES - 1:SUBLANES, :]
    return h, out


def _od1_kernel(x_ref, xn_ref, mod_ref, ng_ref, win_ref, cw_ref, cb_ref, wg_ref, ba_ref, bx_ref, lam_ref,
                perm_ref, s0_ref, w1f_ref, w2f_ref, mid_ref, sfin_ref, w1b_ref, w2b_ref,
                proj_scr, xb_scr, a_scr, b_scr, h_scr, c_scr, hc_scr, tail_scr):
    i = pl.program_id(0)
    _, _, cidx, _, _, _ = _tile_info(jnp.minimum(i, NT - 1))
    _, _, _, first, last, _ = _tile_info(jnp.maximum(i - 1, 0))
    _cast_slabs(w1f_ref, w2f_ref, w1b_ref, w2b_ref)

    @pl.when(i == 0)
    def _():
        proj_scr[...] = jnp.zeros(proj_scr.shape, F32)

    @pl.when(first)
    def _():
        tail_scr[...] = jnp.zeros((2 * SUBLANES, D_RNN), F32)
        hc_scr[...] = s0_ref[0, 0]

    nslab = D_RNN // SLAB
    pw = 2 * D_RNN // nslab

    def project(lhs, s):
        proj_scr[:, s * pw:(s + 1) * pw] = _dot(lhs, win_ref[:, s * pw:(s + 1) * pw])

    sh1, sc1, _, _, _, _ = _mod_rows(mod_ref, cidx)
    xe = jnp.concatenate([x_ref[...], xn_ref[...]], axis=0)
    hm = (_rms(xe, ng_ref[0:1, :]) * (1.0 + sc1) + sh1).astype(BF16)
    hmp = _dot(perm_ref[...], hm[0:TM, :]).astype(BF16)
    lhs = jnp.concatenate([hmp, hm[TM:TM + HALO, :]], axis=0)

    row = lax.broadcasted_iota(jnp.int32, (SUBLANES, D_RNN), 0)
    g30 = proj_scr[TM - 2 * SUBLANES:TM - SUBLANES, 0:D_RNN]
    g31 = proj_scr[TM - SUBLANES:TM, 0:D_RNN]
    g0 = proj_scr[0:SUBLANES, 0:D_RNN]
    nxt = jnp.where(last, 0.0, proj_scr[TM:TM + 1, 0:D_RNN])
    xb_scr[0:SUBLANES, :] = jnp.where(row == 0, pltpu.roll(tail_scr[0:SUBLANES, :], 1, 0), pltpu.roll(g30, 1, 0))
    xb_scr[SUBLANES:2 * SUBLANES, :] = jnp.where(
        row == 0, pltpu.roll(tail_scr[SUBLANES:2 * SUBLANES, :], 1, 0), pltpu.roll(g31, 1, 0))
    xb_scr[2 * SUBLANES:2 * SUBLANES + TM, :] = proj_scr[0:TM, 0:D_RNN]
    xb_scr[2 * SUBLANES + TM:3 * SUBLANES + TM, :] = jnp.where(
        row == SUBLANES - 1, nxt, pltpu.roll(g0, SUBLANES - 1, 0))
    tail_scr[0:SUBLANES, :] = g30
    tail_scr[SUBLANES:2 * SUBLANES, :] = g31

    xc = cb_ref[...]
    for j in range(4):
        xc = xc + cw_ref[j:j + 1, :] * xb_scr[j * SUBLANES:j * SUBLANES + TM, :]
    mid_ref[:, 2 * D_RNN:3 * D_RNN] = xc

    project(lhs, 0)
    mid_ref[:, D_RNN:2 * D_RNN] = _gelu(proj_scr[0:TM, D_RNN:2 * D_RNN])
    for s in range(nslab):
        if s + 1 < nslab:
            project(lhs, s + 1)
        _rg_gates(mid_ref[:, 2 * D_RNN:3 * D_RNN], wg_ref, ba_ref, bx_ref, lam_ref, a_scr, b_scr, range(s, s + 1))

    h, hc_scr[...] = _rg_scan(a_scr, b_scr, h_scr, c_scr, hc_scr[...], reverse=False)
    mid_ref[:, 0:D_RNN] = h

    @pl.when(last)
    def _():
        sfin_ref[0, 0] = hc_scr[...]


def _odd_forward(x, mod, ng, w, w1f, w2f, layer):
    nb16 = TM // HALO
    cast_in, cast_out, cast_shapes = _cast_specs(layer)

    def proj_tile(i):
        return jnp.minimum(i, NT - 1)

    def scan_tile(i):
        return jnp.maximum(i - 1, 0)

    return pl.pallas_call(
        _od1_kernel,
        out_shape=(jax.ShapeDtypeStruct((NT * TM, OD_MID), F32),
                   jax.ShapeDtypeStruct((NSEQ, 1, 1, D_RNN), F32), *cast_shapes),
        grid=(NT + 1,),
        in_specs=[
            pl.BlockSpec((TM, D), lambda i: (proj_tile(i), 0)),
            pl.BlockSpec((HALO, D), lambda i: (jnp.minimum(proj_tile(i) + 1, NT - 1) * nb16, 0)),
            _const_spec((NCOND, 6 * D)), _const_spec((4, D)),
            _const_spec((D, 2 * D_RNN)), _const_spec((4, D_RNN)), _const_spec((1, D_RNN)),
            _const_spec((D_RNN // SLAB, SLAB, 2 * SLAB)),
            _const_spec((1, D_RNN)), _const_spec((1, D_RNN)), _const_spec((1, D_RNN)),
            _const_spec((TM, TM)),
            pl.BlockSpec((1, 1, 1, D_RNN), lambda i: (_seq_of(scan_tile(i)), 0, 0, 0)),
            *cast_in,
        ],
        out_specs=(pl.BlockSpec((TM, OD_MID), lambda i: (scan_tile(i), 0)),
                   pl.BlockSpec((1, 1, 1, D_RNN), lambda i: (_seq_of(scan_tile(i)), 0, 0, 0)), *cast_out),
        scratch_shapes=[
            pltpu.VMEM((TM + HALO, 2 * D_RNN), F32),
            pltpu.VMEM((3 * SUBLANES + TM, D_RNN), F32),
            pltpu.VMEM((TM, D_RNN), F32), pltpu.VMEM((TM, D_RNN), F32), pltpu.VMEM((TM, D_RNN), F32),
            pltpu.VMEM((TM, D_RNN), F32),
            pltpu.VMEM((1, D_RNN), F32), pltpu.VMEM((2 * SUBLANES, D_RNN), F32),
        ],
        compiler_params=_params(),
        name="odd_forward",
    )(x, x, mod, ng, w["win"], w["cw"], w["cb"], w["wg"][0], w["ba"][0:1], w["bx"][0:1], w["lam"][0:1],
      _seg_perm(False), w["s0"], w1f, w2f)


def _od2_kernel(x_ref, mod_ref, ng_ref, mid_ref, wg_ref, ba_ref, bx_ref, lam_ref, wout_ref, perm_ref, s0_ref,
                x1_ref, sfin_ref, a_scr, b_scr, h_scr, c_scr, hc_scr, zp_scr):
    i = pl.program_id(0)
    live = i < NT
    _, _, _, first, last, _ = _tile_info(NT - 1 - jnp.minimum(i, NT - 1))
    _, _, cidx_prev, _, _, _ = _tile_info(NT - 1 - jnp.maximum(i - 1, 0))
    _, _, g1, _, _, _ = _mod_rows(mod_ref, cidx_prev)

    @pl.when(i == 0)
    def _():
        zp_scr[...] = jnp.zeros(zp_scr.shape, BF16)

    @pl.when(jnp.logical_and(last, live))
    def _():
        hc_scr[...] = s0_ref[0, 0]

    xc = mid_ref[:, 2 * D_RNN:3 * D_RNN]
    nslab = D_RNN // SLAB
    z = _dot(perm_ref[...], zp_scr[...]).astype(BF16)
    ys = []
    for s in range(nslab):
        ys.append(_dot(z, wout_ref[:, s * SLAB:(s + 1) * SLAB]))
        _rg_gates(xc, wg_ref, ba_ref, bx_ref, lam_ref, a_scr, b_scr, range(s, s + 1))
    x1_ref[...] = x_ref[...] + g1 * _rms(jnp.concatenate(ys, axis=1), ng_ref[1:2, :])
    h_b, hc_scr[...] = _rg_scan(a_scr, b_scr, h_scr, c_scr, hc_scr[...], reverse=True)
    zp_scr[...] = ((mid_ref[:, 0:D_RNN] + h_b) * mid_ref[:, D_RNN:2 * D_RNN]).astype(BF16)

    @pl.when(jnp.logical_and(first, live))
    def _():
        sfin_ref[0, 0] = hc_scr[...]


def _odd_reverse(x, mod, ng, mid, w):
    def scan_tile(i):
        return NT - 1 - jnp.minimum(i, NT - 1)

    def out_tile(i):
        return NT - 1 - jnp.maximum(i - 1, 0)

    return pl.pallas_call(
        _od2_kernel,
        out_shape=(jax.ShapeDtypeStruct((NT * TM, D), F32),
                   jax.ShapeDtypeStruct((NSEQ, 1, 1, D_RNN), F32)),
        grid=(NT + 1,),
        in_specs=[
            pl.BlockSpec((TM, D), lambda i: (out_tile(i), 0)),
            _const_spec((NCOND, 6 * D)), _const_spec((4, D)),
            pl.BlockSpec((TM, OD_MID), lambda i: (scan_tile(i), 0)),
            _const_spec((D_RNN // SLAB, SLAB, 2 * SLAB)),
            _const_spec((1, D_RNN)), _const_spec((1, D_RNN)), _const_spec((1, D_RNN)),
            _const_spec((D_RNN, D)), _const_spec((TM, TM)),
            pl.BlockSpec((1, 1, 1, D_RNN), lambda i: (_seq_of(scan_tile(i)), 1, 0, 0)),
        ],
        out_specs=(pl.BlockSpec((TM, D), lambda i: (out_tile(i), 0)),
                   pl.BlockSpec((1, 1, 1, D_RNN), lambda i: (_seq_of(scan_tile(i)), 0, 0, 0))),
        scratch_shapes=[
            pltpu.VMEM((TM, D_RNN), F32), pltpu.VMEM((TM, D_RNN), F32), pltpu.VMEM((TM, D_RNN), F32),
            pltpu.VMEM((TM, D_RNN), F32),
            pltpu.VMEM((1, D_RNN), F32), pltpu.VMEM((TM, D_RNN), BF16),
        ],
        compiler_params=_params(),
        name="odd_reverse",
    )(x, mod, ng, mid, w["wg"][1], w["ba"][1:2], w["bx"][1:2], w["lam"][1:2], w["wout"], _seg_perm(True), w["s0"])


def _pos_tables():
    n = D // 4
    omega = 1.0 / (10000.0 ** (jnp.arange(n, dtype=F32) / n))
    idx = jnp.arange(GRID_W, dtype=F32)[:, None] * omega
    tab = jnp.concatenate([jnp.sin(idx), jnp.cos(idx)], axis=-1)
    return tab, tab


def _block_diag_slabs(w):
    per = SLAB // RG_BS
    w = w.reshape(D_RNN // SLAB, per, RG_BS, RG_BS)
    eye = jnp.eye(per, dtype=w.dtype)
    return jnp.einsum("spij,pq->spiqj", w, eye).reshape(D_RNN // SLAB, SLAB, SLAB)


def kernel(x_prompt, x_sample, c, state_gla, state_rglru, c_ctx, mod_w, mod_b, norm_g, mlp_w1, mlp_b1, mlp_w2,
           mlp_b2, ev_w_in, ev_w_out, sgu_ln_g, sgu_ln_b, sgu_ws, sgu_bs, gla_gate_w2, gla_gate_b, gla_norm_g,
           rg_w_in, rg_conv_w, rg_conv_b, rg_wa, rg_ba, rg_wx, rg_bx, rg_L, rg_w_out):
    assert x_prompt.shape == (BATCH, SEQ, D) and x_sample.shape == (DEC_BATCH, DEC_SEQ, D)
    assert SEQ == TM and DEC_SEQ % TM == 0 and DEPTH == 2
    xp = x_prompt.reshape(NTP * TM, D)
    xs = x_sample.reshape(NTS * TM, D)
    cond8 = jnp.concatenate([c_ctx[None, :], c, jnp.zeros((NCOND - 1 - DEC_BATCH, D), F32)], axis=0)
    mods = _modulation(cond8, mod_w, mod_b)
    rtab, ctab = _pos_tables()

    gmat = jnp.zeros((LANES, 2 * QK), F32)
    gmat = gmat.at[0:GLA_RANK, 0:QK].set(gla_gate_w2[0, 0])
    gmat = gmat.at[GLA_RANK:2 * GLA_RANK, QK:2 * QK].set(gla_gate_w2[0, 1])
    s0_gla = jnp.concatenate([jnp.zeros((BATCH,) + state_gla.shape[2:], F32), state_gla[:, 0]], axis=0)
    ev = {
        "win": ev_w_in[0, :, 0:EV_MAIN].astype(BF16),
        "wlr": jnp.pad(ev_w_in[0, :, EV_MAIN:], ((0, 0), (0, LANES - 2 * GLA_RANK))).astype(BF16),
        "gmat": gmat.astype(BF16),
        "gb": gla_gate_b[0].reshape(1, 2 * QK),
        "lng": sgu_ln_g[0].reshape(1, SGU_WIDTH),
        "lnb": sgu_ln_b[0].reshape(1, SGU_WIDTH),
        "ws": sgu_ws[0].astype(BF16),
        "bst": sgu_bs[0].T,
        "gn": gla_norm_g[0].reshape(1, VW),
        "wout": ev_w_out[0].astype(BF16),
        "p0": s0_gla,
    }
    mid, pf, w1b, w2b = _even_forward(xp, xs, rtab, ctab, mods[0], norm_g[0], ev, mlp_w1, mlp_w2, 0)
    x1, pb = _even_reverse(xp, xs, rtab, ctab, mods[0], norm_g[0], mid, ev)
    x2 = _mlp(x1, mods[0], norm_g[0], w1b, mlp_b1[0].reshape(1, D_FF), w2b, mlp_b2[0].reshape(1, D), 0, NT)
    new_gla = jnp.stack([pf[:BATCH, 0], pb[:BATCH, 0]], axis=1)[:, None]

    s0_rg = jnp.concatenate([jnp.zeros((BATCH, 2, D_RNN), F32), state_rglru[:, 0]], axis=0)
    od = {
        "win": rg_w_in[0].astype(BF16),
        "cw": rg_conv_w[0],
        "cb": rg_conv_b[0].reshape(1, D_RNN),
        "wg": jnp.stack([
            jnp.concatenate([_block_diag_slabs(rg_wa[0, d]), _block_diag_slabs(rg_wx[0, d])], axis=-1)
            for d in range(2)], axis=0).astype(BF16),
        "ba": rg_ba[0], "bx": rg_bx[0], "lam": rg_L[0],
        "wout": rg_w_out[0].astype(BF16),
        "s0": s0_rg.reshape(NSEQ, 2, 1, D_RNN),
    }
    mid1, sf, w1b, w2b = _odd_forward(x2, mods[1], norm_g[1], od, mlp_w1, mlp_w2, 1)
    x3, sb = _odd_reverse(x2, mods[1], norm_g[1], mid1, od)
    mlp1 = (mods[1], norm_g[1], w1b, mlp_b1[1].reshape(1, D_FF), w2b, mlp_b2[1].reshape(1, D))
    y_prompt = _mlp(x3, *mlp1, 0, NTP)
    y_sample = _mlp(x3, *mlp1, NTP, NTS)
    new_rg = jnp.stack([sf[:BATCH, 0, 0], sb[:BATCH, 0, 0]], axis=1)[:, None]
    return (y_prompt.reshape(BATCH, SEQ, D), y_sample.reshape(DEC_BATCH, DEC_SEQ, D), new_gla, new_rg)
```

```python
import functools

import jax
import jax.numpy as jnp
import numpy as np
from jax import lax
from jax.experimental import pallas as pl
from jax.experimental.pallas import tpu as pltpu

D = 1024
BATCH = 16
SEQ = 256
DEPTH = 2
DEC_BATCH = 4
DEC_SEQ = 4096
GRID_W = 64
D_FF = 4 * D
EPS = 1e-6
SGU_CHUNK = 128
SGU_GROUPS = 4
SGU_WIDTH = D // 2
GLA_HEADS = 4
GLA_DV = 128
GLA_DK = 64
GLA_RANK = 16
GLA_NORMALIZER = 16.0
GLA_CHUNK = 64
QK = GLA_HEADS * GLA_DK
VW = GLA_HEADS * GLA_DV
EV_MAIN = 2 * SGU_WIDTH + 2 * QK + 2 * VW
EV_IN = EV_MAIN + 2 * GLA_RANK
D_RNN = D
RG_BLOCKS = 16
RG_BS = D_RNN // RG_BLOCKS
RG_C = 8.0
LANES = 128
SUBLANES = 8
SLAB = 256

TM = 256
NTP = BATCH * SEQ // TM
TPS = DEC_SEQ // TM
NTS = DEC_BATCH * TPS
NT = NTP + NTS
NSEQ = BATCH + DEC_BATCH
NCOND = 8
HALO = 16
EV_MID = 2 * VW + VW + 2 * QK + VW + QK
OD_MID = 3 * D_RNN
VMEM_LIMIT = 56 * 1024 * 1024

F32 = jnp.float32
BF16 = jnp.bfloat16


def _tile_info(t):
    is_p = t < NTP
    ts = jnp.maximum(t - NTP, 0)
    sq = ts // TPS
    within = ts % TPS
    cidx = jnp.where(is_p, 0, 1 + sq)
    first = jnp.logical_or(is_p, within == 0)
    last = jnp.logical_or(is_p, within == TPS - 1)
    seq = jnp.where(is_p, t, BATCH + sq)
    return is_p, within, cidx, first, last, seq


def _seq_of(t):
    return jnp.where(t < NTP, t, BATCH + jnp.maximum(t - NTP, 0) // TPS)


def _rms(x, g):
    return x * lax.rsqrt(jnp.mean(x * x, axis=-1, keepdims=True) + EPS) * g


def _dot(a, b):
    return jnp.dot(a, b, preferred_element_type=F32)


def _dot_nt(a, b):
    return lax.dot_general(a, b, (((1,), (1,)), ((), ())), preferred_element_type=F32)


def _dot_tn(a, b):
    return lax.dot_general(a, b, (((0,), (0,)), ((), ())), preferred_element_type=F32)


def _split3(x):
    hi = x.astype(BF16)
    r1 = x - hi.astype(F32)
    mid = r1.astype(BF16)
    lo = (r1 - mid.astype(F32)).astype(BF16)
    return hi, mid, lo


def _dot_exact_lhs(m, parts):
    return _dot(m, parts[0]) + _dot(m, parts[1]) + _dot(m, parts[2])


def _log_sigmoid(x):
    return jnp.minimum(x, 0.0) - jnp.log(1.0 + jnp.exp(-jnp.abs(x)))


def _softplus(x):
    return jnp.maximum(x, 0.0) + jnp.log(1.0 + jnp.exp(-jnp.abs(x)))


LOG2E = 1.4426950408889634
_GELU_C = 0.7978845608028654 * LOG2E


def _sigmoid(x):
    return 1.0 / (1.0 + jnp.exp2(x * (-LOG2E)))


def _gelu(x):
    return x / (1.0 + jnp.exp2(x * ((-2.0 * _GELU_C) + (-2.0 * 0.044715 * _GELU_C) * (x * x))))


def _mod_rows(mod_ref, cidx):
    m = mod_ref[pl.ds(cidx, 1), :]
    return [m[:, j * D:(j + 1) * D] for j in range(6)]


def _load_x0(is_p, within, xp_ref, xs_ref, rtab_ref, ctab_ref):
    rows_per_tile = TM // GRID_W
    r0 = within * rows_per_tile
    posr = jnp.concatenate(
        [jnp.broadcast_to(rtab_ref[pl.ds(r0 + j, 1), :], (GRID_W, D // 2)) for j in range(rows_per_tile)],
        axis=0)
    posc = jnp.concatenate([ctab_ref[...]] * rows_per_tile, axis=0)
    pos = jnp.concatenate([posr, posc], axis=1)
    return jnp.where(is_p, xp_ref[...], xs_ref[...] + pos)


def _mod_kernel(cond_ref, w_ref, b_ref, o_ref):
    c = cond_ref[...]
    sc = (c * _sigmoid(c)).astype(BF16)
    o_ref[0] = _dot(sc, w_ref[0].astype(BF16)) + b_ref[0]


def _modulation(cond8, mod_w, mod_b):
    nb = 6 * D // D
    return pl.pallas_call(
        _mod_kernel,
        out_shape=jax.ShapeDtypeStruct((DEPTH, NCOND, 6 * D), F32),
        grid=(DEPTH, nb),
        in_specs=[
            pl.BlockSpec((NCOND, D), lambda l, j: (0, 0)),
            pl.BlockSpec((1, D, D), lambda l, j: (l, 0, j)),
            pl.BlockSpec((1, 1, D), lambda l, j: (l, 0, j)),
        ],
        out_specs=pl.BlockSpec((1, NCOND, D), lambda l, j: (l, 0, j)),
        compiler_params=pltpu.CompilerParams(
            dimension_semantics=("arbitrary", "arbitrary"), vmem_limit_bytes=VMEM_LIMIT),
        name="modulation",
    )(cond8, mod_w, mod_b.reshape(DEPTH, 1, 6 * D))


def _chunk_rows(c):
    return slice(c * GLA_CHUNK, (c + 1) * GLA_CHUNK)


def _pair_lanes(p):
    return slice(p * LANES, (p + 1) * LANES)


def _gla_order(reverse):
    n = TM // GLA_CHUNK
    return list(reversed(range(n))) if reverse else list(range(n))


def _cum_mask(reverse):
    ri, ci = np.indices((TM, TM))
    order = (ci >= ri) if reverse else (ci <= ri)
    return jnp.asarray(np.logical_and(ri // GLA_CHUNK == ci // GLA_CHUNK, order), BF16)


def _gla_prep(qs, k, la, cum_m, reverse):
    n = TM // GLA_CHUNK
    b = _dot_exact_lhs(cum_m, _split3(la))
    ends = [c * GLA_CHUNK if reverse else (c + 1) * GLA_CHUNK - 1 for c in range(n)]
    btot = jnp.concatenate([jnp.broadcast_to(b[e:e + 1, :], (GLA_CHUNK, QK)) for e in ends], axis=0)
    qe = (qs * jnp.exp(b)).astype(BF16)
    ke = k * jnp.exp(-b)
    kd = (k * jnp.exp(btot - b)).astype(BF16)
    dec = jnp.exp(btot)
    lane = lax.broadcasted_iota(jnp.int32, (TM, QK), 1) % LANES
    ke_h = (jnp.where(lane < GLA_DK, ke, 0.0).astype(BF16), jnp.where(lane >= GLA_DK, ke, 0.0).astype(BF16))
    return qe, ke_h, kd, dec


def _gla_products(prep, v_bf, reverse):
    qe, ke_h, kd, _ = prep
    n = TM // GLA_CHUNK
    cr = lax.broadcasted_iota(jnp.int32, (GLA_CHUNK, 2 * GLA_CHUNK), 0)
    cc = lax.broadcasted_iota(jnp.int32, (GLA_CHUNK, 2 * GLA_CHUNK), 1) % GLA_CHUNK
    cmask = (cc >= cr) if reverse else (cc <= cr)
    br = lax.broadcasted_iota(jnp.int32, (2 * GLA_DK, 2 * GLA_DV), 0)
    bc = lax.broadcasted_iota(jnp.int32, (2 * GLA_DK, 2 * GLA_DV), 1)
    diag = (br < GLA_DK) == (bc < GLA_DV)
    att = [[None] * n for _ in range(GLA_HEADS // 2)]
    ds = [[None] * n for _ in range(GLA_HEADS // 2)]
    for p in range(GLA_HEADS // 2):
        vp = v_bf[:, p * 2 * GLA_DV:(p + 1) * 2 * GLA_DV]
        for c in _gla_order(reverse):
            rs, ls = _chunk_rows(c), _pair_lanes(p)
            kk = jnp.concatenate([ke_h[0][rs, ls], ke_h[1][rs, ls]], axis=0)
            att[p][c] = jnp.where(cmask, _dot_nt(qe[rs, ls], kk), 0.0).astype(BF16)
            ds[p][c] = jnp.where(diag, _dot_tn(kd[rs, ls], vp[rs, :]), 0.0)
    return att, ds


def _gla_outputs(prep, prods, v_bf, p_scr, reverse):
    qe, _, _, dec = prep
    att, ds = prods
    n = TM // GLA_CHUNK
    half = TM // 2
    lane_v = lax.broadcasted_iota(jnp.int32, (TM, 2 * GLA_DV), 1)
    rows = [[None] * (GLA_HEADS // 2) for _ in range(n)]
    for p in range(GLA_HEADS // 2):
        ls = _pair_lanes(p)
        vp = v_bf[:, p * 2 * GLA_DV:(p + 1) * 2 * GLA_DV]
        vl = jnp.where(lane_v < GLA_DV, vp, jnp.zeros_like(vp))
        vr = jnp.where(lane_v >= GLA_DV, vp, jnp.zeros_like(vp))
        dec_t = (dec[0:half, ls].T, dec[half:TM, ls].T)
        s = p_scr[p]
        for c in _gla_order(reverse):
            rs = _chunk_rows(c)
            rhs = jnp.concatenate([vl[rs, :], vr[rs, :], s.astype(BF16)], axis=0)
            rows[c][p] = _dot(jnp.concatenate([att[p][c], qe[rs, ls]], axis=1), rhs)
            col = (c % 2) * GLA_CHUNK
            s = dec_t[(c * GLA_CHUNK) // half][:, col:col + 1] * s + ds[p][c]
        p_scr[p] = s
    return jnp.concatenate([jnp.concatenate(r, axis=1) for r in rows], axis=0)


def _gla_state_load(s_ref, p_scr):
    p_scr[...] = jnp.zeros(p_scr.shape, F32)
    for h in range(GLA_HEADS):
        r0, c0 = (h % 2) * GLA_DK, (h % 2) * GLA_DV
        p_scr[h // 2, r0:r0 + GLA_DK, c0:c0 + GLA_DV] = s_ref[0, 0, h]


def _gla_state_store(p_scr, s_ref):
    for h in range(GLA_HEADS):
        r0, c0 = (h % 2) * GLA_DK, (h % 2) * GLA_DV
        s_ref[0, 0, h] = p_scr[h // 2, r0:r0 + GLA_DK, c0:c0 + GLA_DV]


def _ev1_kernel(xp_ref, xs_ref, rtab_ref, ctab_ref, mod_ref, ng_ref, win_ref, gmat_ref, gb_ref,
                lng_ref, lnb_ref, ws_ref, bst_ref, cum_ref, p0_ref, w1f_ref, w2f_ref,
                mid_ref, pfin_ref, w1b_ref, w2b_ref, p_scr, pg_scr, ps_scr, la_scr):
    i = pl.program_id(0)
    is_p, within, cidx, _, _, _ = _tile_info(jnp.minimum(i, NT - 1))
    _, _, _, first, last, _ = _tile_info(jnp.maximum(i - 1, 0))
    _cast_slabs(w1f_ref, w2f_ref, w1b_ref, w2b_ref)

    @pl.when(i == 0)
    def _():
        pg_scr[...] = jnp.zeros(pg_scr.shape, F32)
        ps_scr[...] = jnp.zeros(ps_scr.shape, F32)
        la_scr[...] = jnp.zeros(la_scr.shape, F32)

    @pl.when(first)
    def _():
        _gla_state_load(p0_ref, p_scr)

    x = _load_x0(is_p, within, xp_ref, xs_ref, rtab_ref, ctab_ref)
    sh1, sc1, _, _, _, _ = _mod_rows(mod_ref, cidx)
    hm = (_rms(x, ng_ref[0:1, :]) * (1.0 + sc1) + sh1).astype(BF16)
    o0 = 2 * SGU_WIDTH
    lr = _dot(hm, win_ref[:, EV_MAIN:EV_IN]).astype(BF16)

    qs = pg_scr[:, 0:QK] * (GLA_DK ** -0.5)
    k = pg_scr[:, QK:2 * QK]
    v = pg_scr[:, 2 * QK:2 * QK + VW]
    g = pg_scr[:, 2 * QK + VW:2 * QK + 2 * VW]
    v_bf = v.astype(BF16)
    c0 = 3 * VW
    mid_ref[:, 2 * VW:3 * VW] = g * _sigmoid(g)
    mid_ref[:, c0:c0 + QK] = qs
    mid_ref[:, c0 + QK:c0 + 2 * QK] = k
    mid_ref[:, c0 + 2 * QK:c0 + 2 * QK + VW] = v
    mid_ref[:, c0 + 2 * QK + VW:c0 + 3 * QK + VW] = la_scr[:, QK:2 * QK]
    prep = _gla_prep(qs, k, la_scr[:, 0:QK], cum_ref[...], reverse=False)

    pg_scr[...] = _dot(hm, win_ref[:, o0:EV_MAIN])
    u = _gelu(ps_scr[:, 0:SGU_WIDTH])
    vg = _gelu(ps_scr[:, SGU_WIDTH:2 * SGU_WIDTH])
    mu = jnp.mean(vg, axis=-1, keepdims=True)
    vc = vg - mu
    vn = (vc * lax.rsqrt(jnp.mean(vc * vc, axis=-1, keepdims=True) + EPS) * lng_ref[...] + lnb_ref[...]).astype(BF16)
    la_scr[...] = _log_sigmoid(_dot(lr, gmat_ref[...]) + gb_ref[...]) * (1.0 / GLA_NORMALIZER)
    prods = _gla_products(prep, v_bf, reverse=False)
    ps_scr[...] = _dot(hm, win_ref[:, 0:o0])

    gd = SGU_WIDTH // SGU_GROUPS
    nch = TM // SGU_CHUNK
    sv_cols = []
    for grp in range(SGU_GROUPS):
        vcat = jnp.concatenate(
            [vn[c * SGU_CHUNK:(c + 1) * SGU_CHUNK, grp * gd:(grp + 1) * gd] for c in range(nch)], axis=1)
        sg = _dot(ws_ref[grp], vcat) + bst_ref[:, grp:grp + 1]
        sv_cols.append(jnp.concatenate([sg[:, c * gd:(c + 1) * gd] for c in range(nch)], axis=0))
    mid_ref[:, 0:VW] = u * jnp.concatenate(sv_cols, axis=1)
    mid_ref[:, VW:2 * VW] = _gla_outputs(prep, prods, v_bf, p_scr, reverse=False)

    @pl.when(last)
    def _():
        _gla_state_store(p_scr, pfin_ref)


def _const_spec(shape):
    nd = len(shape)
    return pl.BlockSpec(shape, lambda i, _nd=nd: (0,) * _nd, pipeline_mode=pl.Buffered(1))


def _params():
    return pltpu.CompilerParams(dimension_semantics=("arbitrary",), vmem_limit_bytes=VMEM_LIMIT)


CAST_STEPS = 64


def _cast_specs(layer):
    r1, r2 = D // CAST_STEPS, D_FF // CAST_STEPS
    in_specs = [pl.BlockSpec((1, r1, D_FF), lambda i: (layer, jnp.minimum(i, CAST_STEPS - 1), 0)),
                pl.BlockSpec((1, r2, D), lambda i: (layer, jnp.minimum(i, CAST_STEPS - 1), 0))]
    out_specs = [pl.BlockSpec((r1, D_FF), lambda i: (jnp.minimum(i, CAST_STEPS - 1), 0)),
                 pl.BlockSpec((r2, D), lambda i: (jnp.minimum(i, CAST_STEPS - 1), 0))]
    shapes = [jax.ShapeDtypeStruct((D, D_FF), BF16), jax.ShapeDtypeStruct((D_FF, D), BF16)]
    return in_specs, out_specs, shapes


def _cast_slabs(w1f_ref, w2f_ref, w1b_ref, w2b_ref):
    w1b_ref[...] = w1f_ref[0].astype(BF16)
    w2b_ref[...] = w2f_ref[0].astype(BF16)


def _even_forward(xp, xs, rtab, ctab, mod, ng, w, w1f, w2f, layer):
    state_blk = (1, 1, GLA_HEADS, GLA_DK, GLA_DV)
    cast_in, cast_out, cast_shapes = _cast_specs(layer)

    def mix_tile(i):
        return jnp.maximum(i - 1, 0)

    return pl.pallas_call(
        _ev1_kernel,
        out_shape=(jax.ShapeDtypeStruct((NT * TM, EV_MID), F32),
                   jax.ShapeDtypeStruct((NSEQ, 1) + state_blk[2:], F32), *cast_shapes),
        grid=(NT + 1,),
        in_specs=[
            pl.BlockSpec((TM, D), lambda i: (jnp.minimum(i, NTP - 1), 0)),
            pl.BlockSpec((TM, D), lambda i: (jnp.maximum(jnp.minimum(i, NT - 1) - NTP, 0), 0)),
            _const_spec((GRID_W, D // 2)), _const_spec((GRID_W, D // 2)),
            _const_spec((NCOND, 6 * D)), _const_spec((4, D)),
            _const_spec((D, EV_IN)), _const_spec((2 * GLA_RANK, 2 * QK)),
            _const_spec((1, 2 * QK)),
            _const_spec((1, SGU_WIDTH)), _const_spec((1, SGU_WIDTH)),
            _const_spec((SGU_GROUPS, SGU_CHUNK, SGU_CHUNK)), _const_spec((SGU_CHUNK, SGU_GROUPS)),
            _const_spec((TM, TM)),
            pl.BlockSpec(state_blk, lambda i: (_seq_of(mix_tile(i)), 0, 0, 0, 0)),
            *cast_in,
        ],
        out_specs=(pl.BlockSpec((TM, EV_MID), lambda i: (mix_tile(i), 0)),
                   pl.BlockSpec(state_blk, lambda i: (_seq_of(mix_tile(i)), 0, 0, 0, 0)), *cast_out),
        scratch_shapes=[pltpu.VMEM((GLA_HEADS // 2, 2 * GLA_DK, 2 * GLA_DV), F32),
                        pltpu.VMEM((TM, 2 * QK + 2 * VW), F32), pltpu.VMEM((TM, 2 * SGU_WIDTH), F32),
                        pltpu.VMEM((TM, 2 * QK), F32)],
        compiler_params=_params(),
        name="even_forward",
    )(xp, xs, rtab, ctab, mod, ng, w["win"], w["gmat"], w["gb"], w["lng"], w["lnb"],
      w["ws"], w["bst"], _cum_mask(False), w["p0"], w1f, w2f)


def _ev2_kernel(xp_ref, xs_ref, rtab_ref, ctab_ref, mod_ref, ng_ref, mid_ref, gn_ref, wout_ref, cum_ref, p0_ref,
                x1_ref, pfin_ref, p_scr, cat_scr):
    i = pl.program_id(0)
    live = i < NT
    _, _, _, first, last, _ = _tile_info(NT - 1 - jnp.minimum(i, NT - 1))
    is_p, within, cidx, _, _, _ = _tile_info(NT - 1 - jnp.maximum(i - 1, 0))

    @pl.when(i == 0)
    def _():
        cat_scr[...] = jnp.zeros(cat_scr.shape, BF16)

    @pl.when(jnp.logical_and(last, live))
    def _():
        _gla_state_load(p0_ref, p_scr)

    nblk = D // SLAB
    ys = [None] * nblk

    def project(j):
        ys[j] = _dot(cat_scr[...], wout_ref[:, j * SLAB:(j + 1) * SLAB])

    c0 = 3 * VW
    qs = mid_ref[:, c0:c0 + QK]
    k = mid_ref[:, c0 + QK:c0 + 2 * QK]
    v_bf = mid_ref[:, c0 + 2 * QK:c0 + 2 * QK + VW].astype(BF16)
    la_b = mid_ref[:, c0 + 2 * QK + VW:c0 + 3 * QK + VW]
    project(0)
    prep = _gla_prep(qs, k, la_b, cum_ref[...], reverse=True)
    project(1)
    prods = _gla_products(prep, v_bf, reverse=True)
    project(2)
    o = mid_ref[:, VW:2 * VW] + _gla_outputs(prep, prods, v_bf, p_scr, reverse=True)
    project(3)
    heads = []
    for h in range(GLA_HEADS):
        oh = o[:, h * GLA_DV:(h + 1) * GLA_DV]
        heads.append(oh * lax.rsqrt(jnp.mean(oh * oh, axis=-1, keepdims=True) + EPS))
    on = jnp.concatenate(heads, axis=1) * gn_ref[...] * mid_ref[:, 2 * VW:3 * VW]
    cat_scr[...] = jnp.concatenate([mid_ref[:, 0:VW], on], axis=1).astype(BF16)

    x = _load_x0(is_p, within, xp_ref, xs_ref, rtab_ref, ctab_ref)
    _, _, g1, _, _, _ = _mod_rows(mod_ref, cidx)
    x1_ref[...] = x + g1 * _rms(jnp.concatenate(ys, axis=1), ng_ref[1:2, :])

    @pl.when(jnp.logical_and(first, live))
    def _():
        _gla_state_store(p_scr, pfin_ref)


def _even_reverse(xp, xs, rtab, ctab, mod, ng, mid, w):
    state_blk = (1, 1, GLA_HEADS, GLA_DK, GLA_DV)
    assert D // SLAB == 4

    def mix_tile(i):
        return NT - 1 - jnp.minimum(i, NT - 1)

    def out_tile(i):
        return NT - 1 - jnp.maximum(i - 1, 0)

    return pl.pallas_call(
        _ev2_kernel,
        out_shape=(jax.ShapeDtypeStruct((NT * TM, D), F32),
                   jax.ShapeDtypeStruct((NSEQ, 1) + state_blk[2:], F32)),
        grid=(NT + 1,),
        in_specs=[
            pl.BlockSpec((TM, D), lambda i: (jnp.minimum(out_tile(i), NTP - 1), 0)),
            pl.BlockSpec((TM, D), lambda i: (jnp.maximum(out_tile(i) - NTP, 0), 0)),
            _const_spec((GRID_W, D // 2)), _const_spec((GRID_W, D // 2)),
            _const_spec((NCOND, 6 * D)), _const_spec((4, D)),
            pl.BlockSpec((TM, EV_MID), lambda i: (mix_tile(i), 0)),
            _const_spec((1, VW)), _const_spec((2 * VW, D)), _const_spec((TM, TM)),
            pl.BlockSpec(state_blk, lambda i: (_seq_of(mix_tile(i)), 1, 0, 0, 0)),
        ],
        out_specs=(pl.BlockSpec((TM, D), lambda i: (out_tile(i), 0)),
                   pl.BlockSpec(state_blk, lambda i: (_seq_of(mix_tile(i)), 0, 0, 0, 0))),
        scratch_shapes=[pltpu.VMEM((GLA_HEADS // 2, 2 * GLA_DK, 2 * GLA_DV), F32),
                        pltpu.VMEM((TM, D), BF16)],
        compiler_params=_params(),
        name="even_reverse",
    )(xp, xs, rtab, ctab, mod, ng, mid, w["gn"], w["wout"], _cum_mask(True), w["p0"])


FF_CHUNK = 1024
MLP_TILES = 2
TMM = MLP_TILES * TM


def _mlp_kernel(x_ref, mod_ref, ng_ref, w1_ref, b1_ref, w2_ref, b2_ref, o_ref, *, tile0):
    _, _, cidx, _, _, _ = _tile_info(tile0 + pl.program_id(0) * MLP_TILES)
    _, _, _, sh2, sc2, g2 = _mod_rows(mod_ref, cidx)
    x = x_ref[...]
    hff = (_rms(x, ng_ref[2:3, :]) * (1.0 + sc2) + sh2).astype(BF16)
    acc = b2_ref[...]
    for j in range(D_FF // FF_CHUNK):
        cs = slice(j * FF_CHUNK, (j + 1) * FF_CHUNK)
        h = _dot(hff, w1_ref[:, cs]) + b1_ref[:, cs]
        h = jnp.square(jnp.maximum(h, 0.0)).astype(BF16)
        acc = acc + _dot(h, w2_ref[cs, :])
    o_ref[...] = x + g2 * _rms(acc, ng_ref[3:4, :])


def _mlp(x, mod, ng, w1, b1, w2, b2, tile0, ntiles):
    assert tile0 % MLP_TILES == 0 and ntiles % MLP_TILES == 0 and TPS % MLP_TILES == 0 and NTP % MLP_TILES == 0
    b0 = tile0 // MLP_TILES
    return pl.pallas_call(
        functools.partial(_mlp_kernel, tile0=tile0),
        out_shape=jax.ShapeDtypeStruct((ntiles * TM, D), F32),
        grid=(ntiles // MLP_TILES,),
        in_specs=[
            pl.BlockSpec((TMM, D), lambda i: (b0 + i, 0)),
            _const_spec((NCOND, 6 * D)), _const_spec((4, D)),
            _const_spec((D, D_FF)), _const_spec((1, D_FF)), _const_spec((D_FF, D)), _const_spec((1, D)),
        ],
        out_specs=pl.BlockSpec((TMM, D), lambda i: (i, 0)),
        compiler_params=_params(), name="mlp",
    )(x, mod, ng, w1, b1, w2, b2)


def _rg_gates(xc, wg_ref, ba_ref, bx_ref, lam_ref, a_scr, b_scr, slabs=range(D_RNN // SLAB)):
    xcb = xc.astype(BF16)
    sp = RG_C * _softplus(-lam_ref[...])
    for s in slabs:
        cs = slice(s * SLAB, (s + 1) * SLAB)
        pre = _dot(xcb[:, cs], wg_ref[s])
        r = _sigmoid(pre[:, 0:SLAB] + ba_ref[:, cs])
        i = _sigmoid(pre[:, SLAB:2 * SLAB] + bx_ref[:, cs])
        z = r * sp[:, cs]
        a = jnp.exp(-z)
        a_scr[:, cs] = a
        u = jnp.tanh(z) * (1.0 + a * a)
        b_scr[:, cs] = jnp.where(u > 0.0, u * lax.rsqrt(u), 0.0) * (i * xc[:, cs])


SEG = TM // SUBLANES


def _seg_perm(transpose):
    rr, cc = np.indices((TM, TM))
    if transpose:
        rr, cc = cc, rr
    return jnp.asarray(np.logical_and(cc // SEG == rr % SUBLANES, cc % SEG == rr // SUBLANES), BF16)


def _rg_scan(a_scr, b_scr, h_scr, c_scr, h0, reverse):
    def body(i, carry):
        hh, cc = carry
        grp = (SEG - 1 - i) if reverse else i
        r0 = pl.multiple_of(grp * SUBLANES, SUBLANES)
        a = a_scr[pl.ds(r0, SUBLANES), :]
        hh = a * hh + b_scr[pl.ds(r0, SUBLANES), :]
        cc = a * cc
        h_scr[pl.ds(r0, SUBLANES), :] = hh
        c_scr[pl.ds(r0, SUBLANES), :] = cc
        return hh, cc

    init = (jnp.zeros((SUBLANES, D_RNN), F32), jnp.ones((SUBLANES, D_RNN), F32))
    b, a = lax.fori_loop(0, SEG, body, init, unroll=4 if reverse else True)
    row = lax.broadcasted_iota(jnp.int32, (SUBLANES, D_RNN), 0)
    for s in (1, 2, 4):
        shift = (SUBLANES - s) if reverse else s
        valid = (row < SUBLANES - s) if reverse else (row >= s)
        a_s = pltpu.roll(a, shift, 0)
        b_s = pltpu.roll(b, shift, 0)
        b = jnp.where(valid, a * b_s + b, b)
        a = jnp.where(valid, a * a_s, a)
    after = a * h0 + b
    edge = (row == SUBLANES - 1) if reverse else (row == 0)
    enter = jnp.where(edge, h0, pltpu.roll(after, (SUBLANES - 1) if reverse else 1, 0))
    h = h_scr[...] + c_scr[...] * jnp.concatenate([enter] * SEG, axis=0)
    out = after[0:1, :] if reverse else after[SUBLANES - 1:SUBLANES, :]
    return h, out


def _od1_kernel(x_ref, xn_ref, mod_ref, ng_ref, win_ref, cw_ref, cb_ref, wg_ref, ba_ref, bx_ref, lam_ref,
                perm_ref, s0_ref, w1f_ref, w2f_ref, mid_ref, sfin_ref, w1b_ref, w2b_ref,
                proj_scr, xb_scr, a_scr, b_scr, h_scr, c_scr, hc_scr, tail_scr):
    i = pl.program_id(0)
    _, _, cidx, _, _, _ = _tile_info(jnp.minimum(i, NT - 1))
    _, _, _, first, last, _ = _tile_info(jnp.maximum(i - 1, 0))
    _cast_slabs(w1f_ref, w2f_ref, w1b_ref, w2b_ref)

    @pl.when(i == 0)
    def _():
        proj_scr[...] = jnp.zeros(proj_scr.shape, F32)

    @pl.when(first)
    def _():
        tail_scr[...] = jnp.zeros((2 * SUBLANES, D_RNN), F32)
        hc_scr[...] = s0_ref[0, 0]

    nslab = D_RNN // SLAB
    pw = 2 * D_RNN // nslab

    def project(lhs, s):
        proj_scr[:, s * pw:(s + 1) * pw] = _dot(lhs, win_ref[:, s * pw:(s + 1) * pw])

    sh1, sc1, _, _, _, _ = _mod_rows(mod_ref, cidx)
    xe = jnp.concatenate([x_ref[...], xn_ref[...]], axis=0)
    hm = (_rms(xe, ng_ref[0:1, :]) * (1.0 + sc1) + sh1).astype(BF16)
    hmp = _dot(perm_ref[...], hm[0:TM, :]).astype(BF16)
    lhs = jnp.concatenate([hmp, hm[TM:TM + HALO, :]], axis=0)

    row = lax.broadcasted_iota(jnp.int32, (SUBLANES, D_RNN), 0)
    g30 = proj_scr[TM - 2 * SUBLANES:TM - SUBLANES, 0:D_RNN]
    g31 = proj_scr[TM - SUBLANES:TM, 0:D_RNN]
    g0 = proj_scr[0:SUBLANES, 0:D_RNN]
    nxt = jnp.where(last, 0.0, proj_scr[TM:TM + 1, 0:D_RNN])
    xb_scr[0:SUBLANES, :] = jnp.where(row == 0, pltpu.roll(tail_scr[0:SUBLANES, :], 1, 0), pltpu.roll(g30, 1, 0))
    xb_scr[SUBLANES:2 * SUBLANES, :] = jnp.where(
        row == 0, pltpu.roll(tail_scr[SUBLANES:2 * SUBLANES, :], 1, 0), pltpu.roll(g31, 1, 0))
    xb_scr[2 * SUBLANES:2 * SUBLANES + TM, :] = proj_scr[0:TM, 0:D_RNN]
    xb_scr[2 * SUBLANES + TM:3 * SUBLANES + TM, :] = jnp.where(
        row == SUBLANES - 1, nxt, pltpu.roll(g0, SUBLANES - 1, 0))
    tail_scr[0:SUBLANES, :] = g30
    tail_scr[SUBLANES:2 * SUBLANES, :] = g31

    xc = cb_ref[...]
    for j in range(4):
        xc = xc + cw_ref[j:j + 1, :] * xb_scr[j * SUBLANES:j * SUBLANES + TM, :]
    mid_ref[:, 2 * D_RNN:3 * D_RNN] = xc

    project(lhs, 0)
    mid_ref[:, D_RNN:2 * D_RNN] = _gelu(proj_scr[0:TM, D_RNN:2 * D_RNN])
    for s in range(nslab):
        if s + 1 < nslab:
            project(lhs, s + 1)
        _rg_gates(mid_ref[:, 2 * D_RNN:3 * D_RNN], wg_ref, ba_ref, bx_ref, lam_ref, a_scr, b_scr, range(s, s + 1))

    h, hc_scr[...] = _rg_scan(a_scr, b_scr, h_scr, c_scr, hc_scr[...], reverse=False)
    mid_ref[:, 0:D_RNN] = h

    @pl.when(last)
    def _():
        sfin_ref[0, 0] = hc_scr[...]


def _odd_forward(x, mod, ng, w, w1f, w2f, layer):
    nb16 = TM // HALO
    cast_in, cast_out, cast_shapes = _cast_specs(layer)

    def proj_tile(i):
        return jnp.minimum(i, NT - 1)

    def scan_tile(i):
        return jnp.maximum(i - 1, 0)

    return pl.pallas_call(
        _od1_kernel,
        out_shape=(jax.ShapeDtypeStruct((NT * TM, OD_MID), F32),
                   jax.ShapeDtypeStruct((NSEQ, 1, 1, D_RNN), F32), *cast_shapes),
        grid=(NT + 1,),
        in_specs=[
            pl.BlockSpec((TM, D), lambda i: (proj_tile(i), 0)),
            pl.BlockSpec((HALO, D), lambda i: (jnp.minimum(proj_tile(i) + 1, NT - 1) * nb16, 0)),
            _const_spec((NCOND, 6 * D)), _const_spec((4, D)),
            _const_spec((D, 2 * D_RNN)), _const_spec((4, D_RNN)), _const_spec((1, D_RNN)),
            _const_spec((D_RNN // SLAB, SLAB, 2 * SLAB)),
            _const_spec((1, D_RNN)), _const_spec((1, D_RNN)), _const_spec((1, D_RNN)),
            _const_spec((TM, TM)),
            pl.BlockSpec((1, 1, 1, D_RNN), lambda i: (_seq_of(scan_tile(i)), 0, 0, 0)),
            *cast_in,
        ],
        out_specs=(pl.BlockSpec((TM, OD_MID), lambda i: (scan_tile(i), 0)),
                   pl.BlockSpec((1, 1, 1, D_RNN), lambda i: (_seq_of(scan_tile(i)), 0, 0, 0)), *cast_out),
        scratch_shapes=[
            pltpu.VMEM((TM + HALO, 2 * D_RNN), F32),
            pltpu.VMEM((3 * SUBLANES + TM, D_RNN), F32),
            pltpu.VMEM((TM, D_RNN), F32), pltpu.VMEM((TM, D_RNN), F32), pltpu.VMEM((TM, D_RNN), F32),
            pltpu.VMEM((TM, D_RNN), F32),
            pltpu.VMEM((1, D_RNN), F32), pltpu.VMEM((2 * SUBLANES, D_RNN), F32),
        ],
        compiler_params=_params(),
        name="odd_forward",
    )(x, x, mod, ng, w["win"], w["cw"], w["cb"], w["wg"][0], w["ba"][0:1], w["bx"][0:1], w["lam"][0:1],
      _seg_perm(False), w["s0"], w1f, w2f)


def _od2_kernel(x_ref, mod_ref, ng_ref, mid_ref, wg_ref, ba_ref, bx_ref, lam_ref, wout_ref, perm_ref, s0_ref,
                x1_ref, sfin_ref, a_scr, b_scr, h_scr, c_scr, hc_scr, zp_scr):
    i = pl.program_id(0)
    live = i < NT
    _, _, _, first, last, _ = _tile_info(NT - 1 - jnp.minimum(i, NT - 1))
    _, _, cidx_prev, _, _, _ = _tile_info(NT - 1 - jnp.maximum(i - 1, 0))
    _, _, g1, _, _, _ = _mod_rows(mod_ref, cidx_prev)

    @pl.when(i == 0)
    def _():
        zp_scr[...] = jnp.zeros(zp_scr.shape, BF16)

    @pl.when(jnp.logical_and(last, live))
    def _():
        hc_scr[...] = s0_ref[0, 0]

    xc = mid_ref[:, 2 * D_RNN:3 * D_RNN]
    nslab = D_RNN // SLAB
    z = _dot(perm_ref[...], zp_scr[...]).astype(BF16)
    ys = []
    for s in range(nslab):
        ys.append(_dot(z, wout_ref[:, s * SLAB:(s + 1) * SLAB]))
        _rg_gates(xc, wg_ref, ba_ref, bx_ref, lam_ref, a_scr, b_scr, range(s, s + 1))
    x1_ref[...] = x_ref[...] + g1 * _rms(jnp.concatenate(ys, axis=1), ng_ref[1:2, :])
    h_b, hc_scr[...] = _rg_scan(a_scr, b_scr, h_scr, c_scr, hc_scr[...], reverse=True)
    zp_scr[...] = ((mid_ref[:, 0:D_RNN] + h_b) * mid_ref[:, D_RNN:2 * D_RNN]).astype(BF16)

    @pl.when(jnp.logical_and(first, live))
    def _():
        sfin_ref[0, 0] = hc_scr[...]


def _odd_reverse(x, mod, ng, mid, w):
    def scan_tile(i):
        return NT - 1 - jnp.minimum(i, NT - 1)

    def out_tile(i):
        return NT - 1 - jnp.maximum(i - 1, 0)

    return pl.pallas_call(
        _od2_kernel,
        out_shape=(jax.ShapeDtypeStruct((NT * TM, D), F32),
                   jax.ShapeDtypeStruct((NSEQ, 1, 1, D_RNN), F32)),
        grid=(NT + 1,),
        in_specs=[
            pl.BlockSpec((TM, D), lambda i: (out_tile(i), 0)),
            _const_spec((NCOND, 6 * D)), _const_spec((4, D)),
            pl.BlockSpec((TM, OD_MID), lambda i: (scan_tile(i), 0)),
            _const_spec((D_RNN // SLAB, SLAB, 2 * SLAB)),
            _const_spec((1, D_RNN)), _const_spec((1, D_RNN)), _const_spec((1, D_RNN)),
            _const_spec((D_RNN, D)), _const_spec((TM, TM)),
            pl.BlockSpec((1, 1, 1, D_RNN), lambda i: (_seq_of(scan_tile(i)), 1, 0, 0)),
        ],
        out_specs=(pl.BlockSpec((TM, D), lambda i: (out_tile(i), 0)),
                   pl.BlockSpec((1, 1, 1, D_RNN), lambda i: (_seq_of(scan_tile(i)), 0, 0, 0))),
        scratch_shapes=[
            pltpu.VMEM((TM, D_RNN), F32), pltpu.VMEM((TM, D_RNN), F32), pltpu.VMEM((TM, D_RNN), F32),
            pltpu.VMEM((TM, D_RNN), F32),
            pltpu.VMEM((1, D_RNN), F32), pltpu.VMEM((TM, D_RNN), BF16),
        ],
        compiler_params=_params(),
        name="odd_reverse",
    )(x, mod, ng, mid, w["wg"][1], w["ba"][1:2], w["bx"][1:2], w["lam"][1:2], w["wout"], _seg_perm(True), w["s0"])


def _pos_tables():
    n = D // 4
    omega = 1.0 / (10000.0 ** (jnp.arange(n, dtype=F32) / n))
    idx = jnp.arange(GRID_W, dtype=F32)[:, None] * omega
    tab = jnp.concatenate([jnp.sin(idx), jnp.cos(idx)], axis=-1)
    return tab, tab


def _block_diag_slabs(w):
    per = SLAB // RG_BS
    rows = jnp.tile(w.reshape(D_RNN // SLAB, SLAB, RG_BS), (1, 1, per))
    ri, ci = np.indices((SLAB, SLAB))
    return jnp.where(jnp.asarray(ri // RG_BS == ci // RG_BS), rows, 0.0)


def kernel(x_prompt, x_sample, c, state_gla, state_rglru, c_ctx, mod_w, mod_b, norm_g, mlp_w1, mlp_b1, mlp_w2,
           mlp_b2, ev_w_in, ev_w_out, sgu_ln_g, sgu_ln_b, sgu_ws, sgu_bs, gla_gate_w2, gla_gate_b, gla_norm_g,
           rg_w_in, rg_conv_w, rg_conv_b, rg_wa, rg_ba, rg_wx, rg_bx, rg_L, rg_w_out):
    assert x_prompt.shape == (BATCH, SEQ, D) and x_sample.shape == (DEC_BATCH, DEC_SEQ, D)
    assert SEQ == TM and DEC_SEQ % TM == 0 and DEPTH == 2
    xp = x_prompt.reshape(NTP * TM, D)
    xs = x_sample.reshape(NTS * TM, D)
    cond8 = jnp.concatenate([c_ctx[None, :], c, jnp.zeros((NCOND - 1 - DEC_BATCH, D), F32)], axis=0)
    mods = _modulation(cond8, mod_w, mod_b)
    rtab, ctab = _pos_tables()

    gz = jnp.zeros((GLA_RANK, QK), F32)
    gmat = jnp.concatenate([jnp.concatenate([gla_gate_w2[0, 0], gz], axis=1),
                            jnp.concatenate([gz, gla_gate_w2[0, 1]], axis=1)], axis=0)
    s0_gla = jnp.concatenate([jnp.zeros((BATCH,) + state_gla.shape[2:], F32), state_gla[:, 0]], axis=0)
    ev = {
        "win": ev_w_in[0].astype(BF16),
        "gmat": gmat.astype(BF16),
        "gb": gla_gate_b[0].reshape(1, 2 * QK),
        "lng": sgu_ln_g[0].reshape(1, SGU_WIDTH),
        "lnb": sgu_ln_b[0].reshape(1, SGU_WIDTH),
        "ws": sgu_ws[0].astype(BF16),
        "bst": sgu_bs[0].T,
        "gn": gla_norm_g[0].reshape(1, VW),
        "wout": ev_w_out[0].astype(BF16),
        "p0": s0_gla,
    }
    mid, pf, w1b, w2b = _even_forward(xp, xs, rtab, ctab, mods[0], norm_g[0], ev, mlp_w1, mlp_w2, 0)
    x1, pb = _even_reverse(xp, xs, rtab, ctab, mods[0], norm_g[0], mid, ev)
    x2 = _mlp(x1, mods[0], norm_g[0], w1b, mlp_b1[0].reshape(1, D_FF), w2b, mlp_b2[0].reshape(1, D), 0, NT)
    new_gla = jnp.stack([pf[:BATCH, 0], pb[:BATCH, 0]], axis=1)[:, None]

    s0_rg = jnp.concatenate([jnp.zeros((BATCH, 2, D_RNN), F32), state_rglru[:, 0]], axis=0)
    od = {
        "win": rg_w_in[0].astype(BF16),
        "cw": rg_conv_w[0],
        "cb": rg_conv_b[0].reshape(1, D_RNN),
        "wg": jnp.stack([
            jnp.concatenate([_block_diag_slabs(rg_wa[0, d]), _block_diag_slabs(rg_wx[0, d])], axis=-1)
            for d in range(2)], axis=0).astype(BF16),
        "ba": rg_ba[0], "bx": rg_bx[0], "lam": rg_L[0],
        "wout": rg_w_out[0].astype(BF16),
        "s0": s0_rg.reshape(NSEQ, 2, 1, D_RNN),
    }
    mid1, sf, w1b, w2b = _odd_forward(x2, mods[1], norm_g[1], od, mlp_w1, mlp_w2, 1)
    x3, sb = _odd_reverse(x2, mods[1], norm_g[1], mid1, od)
    mlp1 = (mods[1], norm_g[1], w1b, mlp_b1[1].reshape(1, D_FF), w2b, mlp_b2[1].reshape(1, D))
    y_prompt = _mlp(x3, *mlp1, 0, NTP)
    y_sample = _mlp(x3, *mlp1, NTP, NTS)
    new_rg = jnp.stack([sf[:BATCH, 0, 0], sb[:BATCH, 0, 0]], axis=1)[:, None]
    return (y_prompt.reshape(BATCH, SEQ, D), y_sample.reshape(DEC_BATCH, DEC_SEQ, D), new_gla, new_rg)
```

```python
import functools

import jax
import jax.numpy as jnp
import numpy as np
from jax import lax
from jax.experimental import pallas as pl
from jax.experimental.pallas import tpu as pltpu

D = 1024
BATCH = 16
SEQ = 256
DEPTH = 2
DEC_BATCH = 4
DEC_SEQ = 4096
GRID_W = 64
D_FF = 4 * D
EPS = 1e-6
SGU_CHUNK = 128
SGU_GROUPS = 4
SGU_WIDTH = D // 2
GLA_HEADS = 4
GLA_DV = 128
GLA_DK = 64
GLA_RANK = 16
GLA_NORMALIZER = 16.0
GLA_CHUNK = 64
QK = GLA_HEADS * GLA_DK
VW = GLA_HEADS * GLA_DV
EV_MAIN = 2 * SGU_WIDTH + 2 * QK + 2 * VW
EV_IN = EV_MAIN + 2 * GLA_RANK
D_RNN = D
RG_BLOCKS = 16
RG_BS = D_RNN // RG_BLOCKS
RG_C = 8.0
LANES = 128
SUBLANES = 8
SLAB = 256

TM = 256
NTP = BATCH * SEQ // TM
TPS = DEC_SEQ // TM
NTS = DEC_BATCH * TPS
NT = NTP + NTS
NSEQ = BATCH + DEC_BATCH
NCOND = 8
HALO = 16
EV_MID = 2 * VW + VW + 2 * QK + VW + QK
OD_MID = 3 * D_RNN
VMEM_LIMIT = 56 * 1024 * 1024

F32 = jnp.float32
BF16 = jnp.bfloat16


def _tile_info(t):
    is_p = t < NTP
    ts = jnp.maximum(t - NTP, 0)
    sq = ts // TPS
    within = ts % TPS
    cidx = jnp.where(is_p, 0, 1 + sq)
    first = jnp.logical_or(is_p, within == 0)
    last = jnp.logical_or(is_p, within == TPS - 1)
    seq = jnp.where(is_p, t, BATCH + sq)
    return is_p, within, cidx, first, last, seq


def _seq_of(t):
    return jnp.where(t < NTP, t, BATCH + jnp.maximum(t - NTP, 0) // TPS)


def _rms(x, g):
    return x * lax.rsqrt(jnp.mean(x * x, axis=-1, keepdims=True) + EPS) * g


def _dot(a, b):
    return jnp.dot(a, b, preferred_element_type=F32)


def _dot_nt(a, b):
    return lax.dot_general(a, b, (((1,), (1,)), ((), ())), preferred_element_type=F32)


def _dot_tn(a, b):
    return lax.dot_general(a, b, (((0,), (0,)), ((), ())), preferred_element_type=F32)


def _split3(x):
    hi = x.astype(BF16)
    r1 = x - hi.astype(F32)
    mid = r1.astype(BF16)
    lo = (r1 - mid.astype(F32)).astype(BF16)
    return hi, mid, lo


def _dot_exact_lhs(m, parts):
    return _dot(m, parts[0]) + _dot(m, parts[1]) + _dot(m, parts[2])


def _log_sigmoid(x):
    return jnp.minimum(x, 0.0) - jnp.log(1.0 + jnp.exp(-jnp.abs(x)))


def _softplus(x):
    return jnp.maximum(x, 0.0) + jnp.log(1.0 + jnp.exp(-jnp.abs(x)))


LOG2E = 1.4426950408889634
_GELU_C = 0.7978845608028654 * LOG2E


def _sigmoid(x):
    return 1.0 / (1.0 + jnp.exp2(x * (-LOG2E)))


def _gelu(x):
    return x / (1.0 + jnp.exp2(x * ((-2.0 * _GELU_C) + (-2.0 * 0.044715 * _GELU_C) * (x * x))))


def _mod_rows(mod_ref, cidx):
    m = mod_ref[pl.ds(cidx, 1), :]
    return [m[:, j * D:(j + 1) * D] for j in range(6)]


def _load_x0(is_p, within, xp_ref, xs_ref, rtab_ref, ctab_ref):
    rows_per_tile = TM // GRID_W
    r0 = within * rows_per_tile
    posr = jnp.concatenate(
        [jnp.broadcast_to(rtab_ref[pl.ds(r0 + j, 1), :], (GRID_W, D // 2)) for j in range(rows_per_tile)],
        axis=0)
    posc = jnp.concatenate([ctab_ref[...]] * rows_per_tile, axis=0)
    pos = jnp.concatenate([posr, posc], axis=1)
    return jnp.where(is_p, xp_ref[...], xs_ref[...] + pos)


def _mod_kernel(cond_ref, w_ref, b_ref, o_ref):
    c = cond_ref[...]
    sc = (c * _sigmoid(c)).astype(BF16)
    o_ref[0] = _dot(sc, w_ref[0].astype(BF16)) + b_ref[pl.ds(pl.program_id(0), 1), :]


def _modulation(cond8, mod_w, mod_b):
    nb = 6 * D // D
    return pl.pallas_call(
        _mod_kernel,
        out_shape=jax.ShapeDtypeStruct((DEPTH, NCOND, 6 * D), F32),
        grid=(DEPTH, nb),
        in_specs=[
            pl.BlockSpec((NCOND, D), lambda l, j: (0, 0)),
            pl.BlockSpec((1, D, D), lambda l, j: (l, 0, j)),
            pl.BlockSpec((DEPTH, D), lambda l, j: (0, j)),
        ],
        out_specs=pl.BlockSpec((1, NCOND, D), lambda l, j: (l, 0, j)),
        compiler_params=pltpu.CompilerParams(
            dimension_semantics=("arbitrary", "arbitrary"), vmem_limit_bytes=VMEM_LIMIT),
        name="modulation",
    )(cond8, mod_w, mod_b)


def _chunk_rows(c):
    return slice(c * GLA_CHUNK, (c + 1) * GLA_CHUNK)


def _pair_lanes(p):
    return slice(p * LANES, (p + 1) * LANES)


def _gla_order(reverse):
    n = TM // GLA_CHUNK
    return list(reversed(range(n))) if reverse else list(range(n))


def _cum_mask(reverse):
    ri, ci = np.indices((TM, TM))
    order = (ci >= ri) if reverse else (ci <= ri)
    return jnp.asarray(np.logical_and(ri // GLA_CHUNK == ci // GLA_CHUNK, order), BF16)


def _gla_prep(qs, k, la, cum_m, reverse):
    n = TM // GLA_CHUNK
    b = _dot_exact_lhs(cum_m, _split3(la))
    ends = [c * GLA_CHUNK if reverse else (c + 1) * GLA_CHUNK - 1 for c in range(n)]
    btot = jnp.concatenate([jnp.broadcast_to(b[e:e + 1, :], (GLA_CHUNK, QK)) for e in ends], axis=0)
    qe = (qs * jnp.exp(b)).astype(BF16)
    ke = k * jnp.exp(-b)
    kd = (k * jnp.exp(btot - b)).astype(BF16)
    dec = jnp.exp(btot)
    lane = lax.broadcasted_iota(jnp.int32, (TM, QK), 1) % LANES
    ke_h = (jnp.where(lane < GLA_DK, ke, 0.0).astype(BF16), jnp.where(lane >= GLA_DK, ke, 0.0).astype(BF16))
    return qe, ke_h, kd, dec


def _gla_products(prep, v_bf, reverse):
    qe, ke_h, kd, _ = prep
    n = TM // GLA_CHUNK
    cr = lax.broadcasted_iota(jnp.int32, (GLA_CHUNK, 2 * GLA_CHUNK), 0)
    cc = lax.broadcasted_iota(jnp.int32, (GLA_CHUNK, 2 * GLA_CHUNK), 1) % GLA_CHUNK
    cmask = (cc >= cr) if reverse else (cc <= cr)
    br = lax.broadcasted_iota(jnp.int32, (2 * GLA_DK, 2 * GLA_DV), 0)
    bc = lax.broadcasted_iota(jnp.int32, (2 * GLA_DK, 2 * GLA_DV), 1)
    diag = (br < GLA_DK) == (bc < GLA_DV)
    att = [[None] * n for _ in range(GLA_HEADS // 2)]
    ds = [[None] * n for _ in range(GLA_HEADS // 2)]
    for p in range(GLA_HEADS // 2):
        vp = v_bf[:, p * 2 * GLA_DV:(p + 1) * 2 * GLA_DV]
        for c in _gla_order(reverse):
            rs, ls = _chunk_rows(c), _pair_lanes(p)
            kk = jnp.concatenate([ke_h[0][rs, ls], ke_h[1][rs, ls]], axis=0)
            att[p][c] = jnp.where(cmask, _dot_nt(qe[rs, ls], kk), 0.0).astype(BF16)
            ds[p][c] = jnp.where(diag, _dot_tn(kd[rs, ls], vp[rs, :]), 0.0)
    return att, ds


def _gla_outputs(prep, prods, v_bf, p_scr, reverse):
    qe, _, _, dec = prep
    att, ds = prods
    n = TM // GLA_CHUNK
    half = TM // 2
    lane_v = lax.broadcasted_iota(jnp.int32, (TM, 2 * GLA_DV), 1)
    rows = [[None] * (GLA_HEADS // 2) for _ in range(n)]
    for p in range(GLA_HEADS // 2):
        ls = _pair_lanes(p)
        vp = v_bf[:, p * 2 * GLA_DV:(p + 1) * 2 * GLA_DV]
        vl = jnp.where(lane_v < GLA_DV, vp, jnp.zeros_like(vp))
        vr = jnp.where(lane_v >= GLA_DV, vp, jnp.zeros_like(vp))
        dec_t = (dec[0:half, ls].T, dec[half:TM, ls].T)
        s = p_scr[p]
        for c in _gla_order(reverse):
            rs = _chunk_rows(c)
            rhs = jnp.concatenate([vl[rs, :], vr[rs, :], s.astype(BF16)], axis=0)
            rows[c][p] = _dot(jnp.concatenate([att[p][c], qe[rs, ls]], axis=1), rhs)
            col = (c % 2) * GLA_CHUNK
            s = dec_t[(c * GLA_CHUNK) // half][:, col:col + 1] * s + ds[p][c]
        p_scr[p] = s
    return jnp.concatenate([jnp.concatenate(r, axis=1) for r in rows], axis=0)


def _gla_state_load(s_ref, p_scr):
    p_scr[...] = jnp.zeros(p_scr.shape, F32)
    for h in range(GLA_HEADS):
        r0, c0 = (h % 2) * GLA_DK, (h % 2) * GLA_DV
        p_scr[h // 2, r0:r0 + GLA_DK, c0:c0 + GLA_DV] = s_ref[0, 0, h]


def _gla_state_store(p_scr, s_ref):
    for h in range(GLA_HEADS):
        r0, c0 = (h % 2) * GLA_DK, (h % 2) * GLA_DV
        s_ref[0, 0, h] = p_scr[h // 2, r0:r0 + GLA_DK, c0:c0 + GLA_DV]


def _ev1_kernel(xp_ref, xs_ref, rtab_ref, ctab_ref, mod_ref, ng_ref, win_ref, gmat_ref, gb_ref,
                lng_ref, lnb_ref, ws_ref, bst_ref, cum_ref, p0_ref, w1f_ref, w2f_ref,
                mid_ref, pfin_ref, w1b_ref, w2b_ref, p_scr, pg_scr, ps_scr, la_scr, *, layer):
    i = pl.program_id(0)
    mod_ref, ng_ref = mod_ref.at[layer], ng_ref.at[layer]
    is_p, within, cidx, _, _, _ = _tile_info(jnp.minimum(i, NT - 1))
    _, _, _, first, last, _ = _tile_info(jnp.maximum(i - 1, 0))
    _cast_slabs(w1f_ref, w2f_ref, w1b_ref, w2b_ref)

    @pl.when(i == 0)
    def _():
        pg_scr[...] = jnp.zeros(pg_scr.shape, F32)
        ps_scr[...] = jnp.zeros(ps_scr.shape, F32)
        la_scr[...] = jnp.zeros(la_scr.shape, F32)

    @pl.when(first)
    def _():
        _gla_state_load(p0_ref, p_scr)

    x = _load_x0(is_p, within, xp_ref, xs_ref, rtab_ref, ctab_ref)
    sh1, sc1, _, _, _, _ = _mod_rows(mod_ref, cidx)
    hm = (_rms(x, ng_ref[0:1, :]) * (1.0 + sc1) + sh1).astype(BF16)
    o0 = 2 * SGU_WIDTH
    lr = _dot(hm, win_ref[:, EV_MAIN:EV_IN]).astype(BF16)

    qs = pg_scr[:, 0:QK] * (GLA_DK ** -0.5)
    k = pg_scr[:, QK:2 * QK]
    v = pg_scr[:, 2 * QK:2 * QK + VW]
    g = pg_scr[:, 2 * QK + VW:2 * QK + 2 * VW]
    v_bf = v.astype(BF16)
    c0 = 3 * VW
    mid_ref[:, 2 * VW:3 * VW] = g * _sigmoid(g)
    mid_ref[:, c0:c0 + QK] = qs
    mid_ref[:, c0 + QK:c0 + 2 * QK] = k
    mid_ref[:, c0 + 2 * QK:c0 + 2 * QK + VW] = v
    mid_ref[:, c0 + 2 * QK + VW:c0 + 3 * QK + VW] = la_scr[:, QK:2 * QK]
    prep = _gla_prep(qs, k, la_scr[:, 0:QK], cum_ref[...], reverse=False)

    pg_scr[...] = _dot(hm, win_ref[:, o0:EV_MAIN])
    u = _gelu(ps_scr[:, 0:SGU_WIDTH])
    vg = _gelu(ps_scr[:, SGU_WIDTH:2 * SGU_WIDTH])
    mu = jnp.mean(vg, axis=-1, keepdims=True)
    vc = vg - mu
    vn = (vc * lax.rsqrt(jnp.mean(vc * vc, axis=-1, keepdims=True) + EPS) * lng_ref[...] + lnb_ref[...]).astype(BF16)
    gb = jnp.concatenate([gb_ref[0:1, :], gb_ref[1:2, :]], axis=1)
    la_scr[...] = _log_sigmoid(_dot(lr, gmat_ref[...]) + gb) * (1.0 / GLA_NORMALIZER)
    prods = _gla_products(prep, v_bf, reverse=False)
    ps_scr[...] = _dot(hm, win_ref[:, 0:o0])

    gd = SGU_WIDTH // SGU_GROUPS
    nch = TM // SGU_CHUNK
    sv_cols = []
    for grp in range(SGU_GROUPS):
        vcat = jnp.concatenate(
            [vn[c * SGU_CHUNK:(c + 1) * SGU_CHUNK, grp * gd:(grp + 1) * gd] for c in range(nch)], axis=1)
        sg = _dot(ws_ref[grp], vcat) + bst_ref[:, grp:grp + 1]
        sv_cols.append(jnp.concatenate([sg[:, c * gd:(c + 1) * gd] for c in range(nch)], axis=0))
    mid_ref[:, 0:VW] = u * jnp.concatenate(sv_cols, axis=1)
    mid_ref[:, VW:2 * VW] = _gla_outputs(prep, prods, v_bf, p_scr, reverse=False)

    @pl.when(last)
    def _():
        _gla_state_store(p_scr, pfin_ref)


def _const_spec(shape):
    nd = len(shape)
    return pl.BlockSpec(shape, lambda i, _nd=nd: (0,) * _nd, pipeline_mode=pl.Buffered(1))


def _params():
    return pltpu.CompilerParams(dimension_semantics=("arbitrary",), vmem_limit_bytes=VMEM_LIMIT)


CAST_STEPS = 64


def _cast_specs(layer):
    r1, r2 = D // CAST_STEPS, D_FF // CAST_STEPS
    in_specs = [pl.BlockSpec((1, r1, D_FF), lambda i: (layer, jnp.minimum(i, CAST_STEPS - 1), 0)),
                pl.BlockSpec((1, r2, D), lambda i: (layer, jnp.minimum(i, CAST_STEPS - 1), 0))]
    out_specs = [pl.BlockSpec((r1, D_FF), lambda i: (jnp.minimum(i, CAST_STEPS - 1), 0)),
                 pl.BlockSpec((r2, D), lambda i: (jnp.minimum(i, CAST_STEPS - 1), 0))]
    shapes = [jax.ShapeDtypeStruct((D, D_FF), BF16), jax.ShapeDtypeStruct((D_FF, D), BF16)]
    return in_specs, out_specs, shapes


def _cast_slabs(w1f_ref, w2f_ref, w1b_ref, w2b_ref):
    w1b_ref[...] = w1f_ref[0].astype(BF16)
    w2b_ref[...] = w2f_ref[0].astype(BF16)


def _even_forward(xp, xs, rtab, ctab, mod, ng, w, w1f, w2f, layer):
    state_blk = (1, 1, GLA_HEADS, GLA_DK, GLA_DV)
    cast_in, cast_out, cast_shapes = _cast_specs(layer)

    def mix_tile(i):
        return jnp.maximum(i - 1, 0)

    return pl.pallas_call(
        functools.partial(_ev1_kernel, layer=layer),
        out_shape=(jax.ShapeDtypeStruct((NT * TM, EV_MID), F32),
                   jax.ShapeDtypeStruct((NSEQ, 1) + state_blk[2:], F32), *cast_shapes),
        grid=(NT + 1,),
        in_specs=[
            pl.BlockSpec((TM, D), lambda i: (jnp.minimum(i, NTP - 1), 0)),
            pl.BlockSpec((TM, D), lambda i: (jnp.maximum(jnp.minimum(i, NT - 1) - NTP, 0), 0)),
            _const_spec((GRID_W, D // 2)), _const_spec((GRID_W, D // 2)),
            _const_spec((DEPTH, NCOND, 6 * D)), _const_spec((DEPTH, 4, D)),
            _const_spec((D, EV_IN)), _const_spec((2 * GLA_RANK, 2 * QK)),
            _const_spec((2, QK)),
            _const_spec((1, SGU_WIDTH)), _const_spec((1, SGU_WIDTH)),
            _const_spec((SGU_GROUPS, SGU_CHUNK, SGU_CHUNK)), _const_spec((SGU_CHUNK, SGU_GROUPS)),
            _const_spec((TM, TM)),
            pl.BlockSpec(state_blk, lambda i: (_seq_of(mix_tile(i)), 0, 0, 0, 0)),
            *cast_in,
        ],
        out_specs=(pl.BlockSpec((TM, EV_MID), lambda i: (mix_tile(i), 0)),
                   pl.BlockSpec(state_blk, lambda i: (_seq_of(mix_tile(i)), 0, 0, 0, 0)), *cast_out),
        scratch_shapes=[pltpu.VMEM((GLA_HEADS // 2, 2 * GLA_DK, 2 * GLA_DV), F32),
                        pltpu.VMEM((TM, 2 * QK + 2 * VW), F32), pltpu.VMEM((TM, 2 * SGU_WIDTH), F32),
                        pltpu.VMEM((TM, 2 * QK), F32)],
        compiler_params=_params(),
        name="even_forward",
    )(xp, xs, rtab, ctab, mod, ng, w["win"], w["gmat"], w["gb"], w["lng"], w["lnb"],
      w["ws"], w["bst"], _cum_mask(False), w["p0"], w1f, w2f)


def _ev2_kernel(xp_ref, xs_ref, rtab_ref, ctab_ref, mod_ref, ng_ref, mid_ref, gn_ref, wout_ref, cum_ref, p0_ref,
                x1_ref, pfin_ref, p_scr, cat_scr, *, layer):
    i = pl.program_id(0)
    mod_ref, ng_ref = mod_ref.at[layer], ng_ref.at[layer]
    live = i < NT
    _, _, _, first, last, _ = _tile_info(NT - 1 - jnp.minimum(i, NT - 1))
    is_p, within, cidx, _, _, _ = _tile_info(NT - 1 - jnp.maximum(i - 1, 0))

    @pl.when(i == 0)
    def _():
        cat_scr[...] = jnp.zeros(cat_scr.shape, BF16)

    @pl.when(jnp.logical_and(last, live))
    def _():
        _gla_state_load(p0_ref, p_scr)

    nblk = D // SLAB
    ys = [None] * nblk

    def project(j):
        ys[j] = _dot(cat_scr[...], wout_ref[:, j * SLAB:(j + 1) * SLAB])

    c0 = 3 * VW
    qs = mid_ref[:, c0:c0 + QK]
    k = mid_ref[:, c0 + QK:c0 + 2 * QK]
    v_bf = mid_ref[:, c0 + 2 * QK:c0 + 2 * QK + VW].astype(BF16)
    la_b = mid_ref[:, c0 + 2 * QK + VW:c0 + 3 * QK + VW]
    project(0)
    prep = _gla_prep(qs, k, la_b, cum_ref[...], reverse=True)
    project(1)
    prods = _gla_products(prep, v_bf, reverse=True)
    project(2)
    o = mid_ref[:, VW:2 * VW] + _gla_outputs(prep, prods, v_bf, p_scr, reverse=True)
    project(3)
    heads = []
    for h in range(GLA_HEADS):
        oh = o[:, h * GLA_DV:(h + 1) * GLA_DV]
        heads.append(oh * lax.rsqrt(jnp.mean(oh * oh, axis=-1, keepdims=True) + EPS))
    on = jnp.concatenate(heads, axis=1) * gn_ref[...] * mid_ref[:, 2 * VW:3 * VW]
    cat_scr[...] = jnp.concatenate([mid_ref[:, 0:VW], on], axis=1).astype(BF16)

    x = _load_x0(is_p, within, xp_ref, xs_ref, rtab_ref, ctab_ref)
    _, _, g1, _, _, _ = _mod_rows(mod_ref, cidx)
    x1_ref[...] = x + g1 * _rms(jnp.concatenate(ys, axis=1), ng_ref[1:2, :])

    @pl.when(jnp.logical_and(first, live))
    def _():
        _gla_state_store(p_scr, pfin_ref)


def _even_reverse(xp, xs, rtab, ctab, mod, ng, mid, w, layer):
    state_blk = (1, 1, GLA_HEADS, GLA_DK, GLA_DV)
    assert D // SLAB == 4

    def mix_tile(i):
        return NT - 1 - jnp.minimum(i, NT - 1)

    def out_tile(i):
        return NT - 1 - jnp.maximum(i - 1, 0)

    return pl.pallas_call(
        functools.partial(_ev2_kernel, layer=layer),
        out_shape=(jax.ShapeDtypeStruct((NT * TM, D), F32),
                   jax.ShapeDtypeStruct((NSEQ, 1) + state_blk[2:], F32)),
        grid=(NT + 1,),
        in_specs=[
            pl.BlockSpec((TM, D), lambda i: (jnp.minimum(out_tile(i), NTP - 1), 0)),
            pl.BlockSpec((TM, D), lambda i: (jnp.maximum(out_tile(i) - NTP, 0), 0)),
            _const_spec((GRID_W, D // 2)), _const_spec((GRID_W, D // 2)),
            _const_spec((DEPTH, NCOND, 6 * D)), _const_spec((DEPTH, 4, D)),
            pl.BlockSpec((TM, EV_MID), lambda i: (mix_tile(i), 0)),
            _const_spec((1, VW)), _const_spec((2 * VW, D)), _const_spec((TM, TM)),
            pl.BlockSpec(state_blk, lambda i: (_seq_of(mix_tile(i)), 1, 0, 0, 0)),
        ],
        out_specs=(pl.BlockSpec((TM, D), lambda i: (out_tile(i), 0)),
                   pl.BlockSpec(state_blk, lambda i: (_seq_of(mix_tile(i)), 0, 0, 0, 0))),
        scratch_shapes=[pltpu.VMEM((GLA_HEADS // 2, 2 * GLA_DK, 2 * GLA_DV), F32),
                        pltpu.VMEM((TM, D), BF16)],
        compiler_params=_params(),
        name="even_reverse",
    )(xp, xs, rtab, ctab, mod, ng, mid, w["gn"], w["wout"], _cum_mask(True), w["p0"])


FF_CHUNK = 1024
MLP_TILES = 2
TMM = MLP_TILES * TM


def _mlp_kernel(x_ref, mod_ref, ng_ref, w1_ref, b1_ref, w2_ref, b2_ref, o_ref, *, tile0, layer):
    mod_ref, ng_ref = mod_ref.at[layer], ng_ref.at[layer]
    b1_ref, b2_ref = b1_ref.at[layer:layer + 1], b2_ref.at[layer:layer + 1]
    _, _, cidx, _, _, _ = _tile_info(tile0 + pl.program_id(0) * MLP_TILES)
    _, _, _, sh2, sc2, g2 = _mod_rows(mod_ref, cidx)
    x = x_ref[...]
    hff = (_rms(x, ng_ref[2:3, :]) * (1.0 + sc2) + sh2).astype(BF16)
    acc = b2_ref[...]
    for j in range(D_FF // FF_CHUNK):
        cs = slice(j * FF_CHUNK, (j + 1) * FF_CHUNK)
        h = _dot(hff, w1_ref[:, cs]) + b1_ref[:, cs]
        h = jnp.square(jnp.maximum(h, 0.0)).astype(BF16)
        acc = acc + _dot(h, w2_ref[cs, :])
    o_ref[...] = x + g2 * _rms(acc, ng_ref[3:4, :])


def _mlp(x, mod, ng, w1, b1, w2, b2, layer, tile0, ntiles):
    assert tile0 % MLP_TILES == 0 and ntiles % MLP_TILES == 0 and TPS % MLP_TILES == 0 and NTP % MLP_TILES == 0
    b0 = tile0 // MLP_TILES
    return pl.pallas_call(
        functools.partial(_mlp_kernel, tile0=tile0, layer=layer),
        out_shape=jax.ShapeDtypeStruct((ntiles * TM, D), F32),
        grid=(ntiles // MLP_TILES,),
        in_specs=[
            pl.BlockSpec((TMM, D), lambda i: (b0 + i, 0)),
            _const_spec((DEPTH, NCOND, 6 * D)), _const_spec((DEPTH, 4, D)),
            _const_spec((D, D_FF)), _const_spec((DEPTH, D_FF)), _const_spec((D_FF, D)), _const_spec((DEPTH, D)),
        ],
        out_specs=pl.BlockSpec((TMM, D), lambda i: (i, 0)),
        compiler_params=_params(), name="mlp",
    )(x, mod, ng, w1, b1, w2, b2)


def _rg_gates(xc, wg_ref, ba_ref, bx_ref, lam_ref, a_scr, b_scr, slabs=range(D_RNN // SLAB)):
    xcb = xc.astype(BF16)
    sp = RG_C * _softplus(-lam_ref[...])
    for s in slabs:
        cs = slice(s * SLAB, (s + 1) * SLAB)
        pre = _dot(xcb[:, cs], wg_ref[s])
        r = _sigmoid(pre[:, 0:SLAB] + ba_ref[:, cs])
        i = _sigmoid(pre[:, SLAB:2 * SLAB] + bx_ref[:, cs])
        z = r * sp[:, cs]
        a = jnp.exp(-z)
        a_scr[:, cs] = a
        u = jnp.tanh(z) * (1.0 + a * a)
        b_scr[:, cs] = jnp.where(u > 0.0, u * lax.rsqrt(u), 0.0) * (i * xc[:, cs])


SEG = TM // SUBLANES


def _seg_perm(transpose):
    rr, cc = np.indices((TM, TM))
    if transpose:
        rr, cc = cc, rr
    return jnp.asarray(np.logical_and(cc // SEG == rr % SUBLANES, cc % SEG == rr // SUBLANES), BF16)


def _rg_scan(a_scr, b_scr, h_scr, c_scr, h0, reverse):
    def body(i, carry):
        hh, cc = carry
        grp = (SEG - 1 - i) if reverse else i
        r0 = pl.multiple_of(grp * SUBLANES, SUBLANES)
        a = a_scr[pl.ds(r0, SUBLANES), :]
        hh = a * hh + b_scr[pl.ds(r0, SUBLANES), :]
        cc = a * cc
        h_scr[pl.ds(r0, SUBLANES), :] = hh
        c_scr[pl.ds(r0, SUBLANES), :] = cc
        return hh, cc

    init = (jnp.zeros((SUBLANES, D_RNN), F32), jnp.ones((SUBLANES, D_RNN), F32))
    b, a = lax.fori_loop(0, SEG, body, init, unroll=4 if reverse else True)
    row = lax.broadcasted_iota(jnp.int32, (SUBLANES, D_RNN), 0)
    for s in (1, 2, 4):
        shift = (SUBLANES - s) if reverse else s
        valid = (row < SUBLANES - s) if reverse else (row >= s)
        a_s = pltpu.roll(a, shift, 0)
        b_s = pltpu.roll(b, shift, 0)
        b = jnp.where(valid, a * b_s + b, b)
        a = jnp.where(valid, a * a_s, a)
    after = a * h0 + b
    edge = (row == SUBLANES - 1) if reverse else (row == 0)
    enter = jnp.where(edge, h0, pltpu.roll(after, (SUBLANES - 1) if reverse else 1, 0))
    h = h_scr[...] + c_scr[...] * jnp.concatenate([enter] * SEG, axis=0)
    out = after[0:1, :] if reverse else after[SUBLANES - 1:SUBLANES, :]
    return h, out


def _od1_kernel(x_ref, xn_ref, mod_ref, ng_ref, win_ref, cw_ref, cb_ref, wg_ref, ba_ref, bx_ref, lam_ref,
                perm_ref, s0_ref, w1f_ref, w2f_ref, mid_ref, sfin_ref, w1b_ref, w2b_ref,
                proj_scr, xb_scr, a_scr, b_scr, h_scr, c_scr, hc_scr, tail_scr, *, layer):
    i = pl.program_id(0)
    mod_ref, ng_ref = mod_ref.at[layer], ng_ref.at[layer]
    ba_ref, bx_ref, lam_ref = ba_ref.at[0:1], bx_ref.at[0:1], lam_ref.at[0:1]
    _, _, cidx, _, _, _ = _tile_info(jnp.minimum(i, NT - 1))
    _, _, _, first, last, _ = _tile_info(jnp.maximum(i - 1, 0))
    _cast_slabs(w1f_ref, w2f_ref, w1b_ref, w2b_ref)

    @pl.when(i == 0)
    def _():
        proj_scr[...] = jnp.zeros(proj_scr.shape, F32)

    @pl.when(first)
    def _():
        tail_scr[...] = jnp.zeros((2 * SUBLANES, D_RNN), F32)
        hc_scr[...] = s0_ref[0, 0]

    nslab = D_RNN // SLAB
    pw = 2 * D_RNN // nslab

    def project(lhs, s):
        proj_scr[:, s * pw:(s + 1) * pw] = _dot(lhs, win_ref[:, s * pw:(s + 1) * pw])

    sh1, sc1, _, _, _, _ = _mod_rows(mod_ref, cidx)
    xe = jnp.concatenate([x_ref[...], xn_ref[...]], axis=0)
    hm = (_rms(xe, ng_ref[0:1, :]) * (1.0 + sc1) + sh1).astype(BF16)
    hmp = _dot(perm_ref[...], hm[0:TM, :]).astype(BF16)
    lhs = jnp.concatenate([hmp, hm[TM:TM + HALO, :]], axis=0)

    row = lax.broadcasted_iota(jnp.int32, (SUBLANES, D_RNN), 0)
    g30 = proj_scr[TM - 2 * SUBLANES:TM - SUBLANES, 0:D_RNN]
    g31 = proj_scr[TM - SUBLANES:TM, 0:D_RNN]
    g0 = proj_scr[0:SUBLANES, 0:D_RNN]
    nxt = jnp.where(last, 0.0, proj_scr[TM:TM + 1, 0:D_RNN])
    xb_scr[0:SUBLANES, :] = jnp.where(row == 0, pltpu.roll(tail_scr[0:SUBLANES, :], 1, 0), pltpu.roll(g30, 1, 0))
    xb_scr[SUBLANES:2 * SUBLANES, :] = jnp.where(
        row == 0, pltpu.roll(tail_scr[SUBLANES:2 * SUBLANES, :], 1, 0), pltpu.roll(g31, 1, 0))
    xb_scr[2 * SUBLANES:2 * SUBLANES + TM, :] = proj_scr[0:TM, 0:D_RNN]
    xb_scr[2 * SUBLANES + TM:3 * SUBLANES + TM, :] = jnp.where(
        row == SUBLANES - 1, nxt, pltpu.roll(g0, SUBLANES - 1, 0))
    tail_scr[0:SUBLANES, :] = g30
    tail_scr[SUBLANES:2 * SUBLANES, :] = g31

    xc = cb_ref[...]
    for j in range(4):
        xc = xc + cw_ref[j:j + 1, :] * xb_scr[j * SUBLANES:j * SUBLANES + TM, :]
    mid_ref[:, 2 * D_RNN:3 * D_RNN] = xc

    project(lhs, 0)
    mid_ref[:, D_RNN:2 * D_RNN] = _gelu(proj_scr[0:TM, D_RNN:2 * D_RNN])
    for s in range(nslab):
        if s + 1 < nslab:
            project(lhs, s + 1)
        _rg_gates(mid_ref[:, 2 * D_RNN:3 * D_RNN], wg_ref, ba_ref, bx_ref, lam_ref, a_scr, b_scr, range(s, s + 1))

    h, hc_scr[...] = _rg_scan(a_scr, b_scr, h_scr, c_scr, hc_scr[...], reverse=False)
    mid_ref[:, 0:D_RNN] = h

    @pl.when(last)
    def _():
        sfin_ref[0, 0] = hc_scr[...]


def _odd_forward(x, mod, ng, w, w1f, w2f, layer):
    nb16 = TM // HALO
    cast_in, cast_out, cast_shapes = _cast_specs(layer)

    def proj_tile(i):
        return jnp.minimum(i, NT - 1)

    def scan_tile(i):
        return jnp.maximum(i - 1, 0)

    return pl.pallas_call(
        functools.partial(_od1_kernel, layer=layer),
        out_shape=(jax.ShapeDtypeStruct((NT * TM, OD_MID), F32),
                   jax.ShapeDtypeStruct((NSEQ, 1, 1, D_RNN), F32), *cast_shapes),
        grid=(NT + 1,),
        in_specs=[
            pl.BlockSpec((TM, D), lambda i: (proj_tile(i), 0)),
            pl.BlockSpec((HALO, D), lambda i: (jnp.minimum(proj_tile(i) + 1, NT - 1) * nb16, 0)),
            _const_spec((DEPTH, NCOND, 6 * D)), _const_spec((DEPTH, 4, D)),
            _const_spec((D, 2 * D_RNN)), _const_spec((4, D_RNN)), _const_spec((1, D_RNN)),
            _const_spec((D_RNN // SLAB, SLAB, 2 * SLAB)),
            _const_spec((2, D_RNN)), _const_spec((2, D_RNN)), _const_spec((2, D_RNN)),
            _const_spec((TM, TM)),
            pl.BlockSpec((1, 1, 1, D_RNN), lambda i: (_seq_of(scan_tile(i)), 0, 0, 0)),
            *cast_in,
        ],
        out_specs=(pl.BlockSpec((TM, OD_MID), lambda i: (scan_tile(i), 0)),
                   pl.BlockSpec((1, 1, 1, D_RNN), lambda i: (_seq_of(scan_tile(i)), 0, 0, 0)), *cast_out),
        scratch_shapes=[
            pltpu.VMEM((TM + HALO, 2 * D_RNN), F32),
            pltpu.VMEM((3 * SUBLANES + TM, D_RNN), F32),
            pltpu.VMEM((TM, D_RNN), F32), pltpu.VMEM((TM, D_RNN), F32), pltpu.VMEM((TM, D_RNN), F32),
            pltpu.VMEM((TM, D_RNN), F32),
            pltpu.VMEM((1, D_RNN), F32), pltpu.VMEM((2 * SUBLANES, D_RNN), F32),
        ],
        compiler_params=_params(),
        name="odd_forward",
    )(x, x, mod, ng, w["win"], w["cw"], w["cb"], w["wg"][0], w["ba"], w["bx"], w["lam"],
      _seg_perm(False), w["s0"], w1f, w2f)


def _od2_kernel(x_ref, mod_ref, ng_ref, mid_ref, wg_ref, ba_ref, bx_ref, lam_ref, wout_ref, perm_ref, s0_ref,
                x1_ref, sfin_ref, a_scr, b_scr, h_scr, c_scr, hc_scr, zp_scr, *, layer):
    i = pl.program_id(0)
    mod_ref, ng_ref = mod_ref.at[layer], ng_ref.at[layer]
    ba_ref, bx_ref, lam_ref = ba_ref.at[1:2], bx_ref.at[1:2], lam_ref.at[1:2]
    live = i < NT
    _, _, _, first, last, _ = _tile_info(NT - 1 - jnp.minimum(i, NT - 1))
    _, _, cidx_prev, _, _, _ = _tile_info(NT - 1 - jnp.maximum(i - 1, 0))
    _, _, g1, _, _, _ = _mod_rows(mod_ref, cidx_prev)

    @pl.when(i == 0)
    def _():
        zp_scr[...] = jnp.zeros(zp_scr.shape, BF16)

    @pl.when(jnp.logical_and(last, live))
    def _():
        hc_scr[...] = s0_ref[0, 0]

    xc = mid_ref[:, 2 * D_RNN:3 * D_RNN]
    nslab = D_RNN // SLAB
    z = _dot(perm_ref[...], zp_scr[...]).astype(BF16)
    ys = []
    for s in range(nslab):
        ys.append(_dot(z, wout_ref[:, s * SLAB:(s + 1) * SLAB]))
        _rg_gates(xc, wg_ref, ba_ref, bx_ref, lam_ref, a_scr, b_scr, range(s, s + 1))
    x1_ref[...] = x_ref[...] + g1 * _rms(jnp.concatenate(ys, axis=1), ng_ref[1:2, :])
    h_b, hc_scr[...] = _rg_scan(a_scr, b_scr, h_scr, c_scr, hc_scr[...], reverse=True)
    zp_scr[...] = ((mid_ref[:, 0:D_RNN] + h_b) * mid_ref[:, D_RNN:2 * D_RNN]).astype(BF16)

    @pl.when(jnp.logical_and(first, live))
    def _():
        sfin_ref[0, 0] = hc_scr[...]


def _odd_reverse(x, mod, ng, mid, w, layer):
    def scan_tile(i):
        return NT - 1 - jnp.minimum(i, NT - 1)

    def out_tile(i):
        return NT - 1 - jnp.maximum(i - 1, 0)

    return pl.pallas_call(
        functools.partial(_od2_kernel, layer=layer),
        out_shape=(jax.ShapeDtypeStruct((NT * TM, D), F32),
                   jax.ShapeDtypeStruct((NSEQ, 1, 1, D_RNN), F32)),
        grid=(NT + 1,),
        in_specs=[
            pl.BlockSpec((TM, D), lambda i: (out_tile(i), 0)),
            _const_spec((DEPTH, NCOND, 6 * D)), _const_spec((DEPTH, 4, D)),
            pl.BlockSpec((TM, OD_MID), lambda i: (scan_tile(i), 0)),
            _const_spec((D_RNN // SLAB, SLAB, 2 * SLAB)),
            _const_spec((2, D_RNN)), _const_spec((2, D_RNN)), _const_spec((2, D_RNN)),
            _const_spec((D_RNN, D)), _const_spec((TM, TM)),
            pl.BlockSpec((1, 1, 1, D_RNN), lambda i: (_seq_of(scan_tile(i)), 1, 0, 0)),
        ],
        out_specs=(pl.BlockSpec((TM, D), lambda i: (out_tile(i), 0)),
                   pl.BlockSpec((1, 1, 1, D_RNN), lambda i: (_seq_of(scan_tile(i)), 0, 0, 0))),
        scratch_shapes=[
            pltpu.VMEM((TM, D_RNN), F32), pltpu.VMEM((TM, D_RNN), F32), pltpu.VMEM((TM, D_RNN), F32),
            pltpu.VMEM((TM, D_RNN), F32),
            pltpu.VMEM((1, D_RNN), F32), pltpu.VMEM((TM, D_RNN), BF16),
        ],
        compiler_params=_params(),
        name="odd_reverse",
    )(x, mod, ng, mid, w["wg"][1], w["ba"], w["bx"], w["lam"], w["wout"], _seg_perm(True), w["s0"])


def _pos_tables():
    n = D // 4
    omega = 1.0 / (10000.0 ** (jnp.arange(n, dtype=F32) / n))
    idx = jnp.arange(GRID_W, dtype=F32)[:, None] * omega
    tab = jnp.concatenate([jnp.sin(idx), jnp.cos(idx)], axis=-1)
    return tab, tab


def _block_diag_slabs(w):
    per = SLAB // RG_BS
    rows = jnp.tile(w.reshape(D_RNN // SLAB, SLAB, RG_BS), (1, 1, per))
    ri, ci = np.indices((SLAB, SLAB))
    return jnp.where(jnp.asarray(ri // RG_BS == ci // RG_BS), rows, 0.0)


def kernel(x_prompt, x_sample, c, state_gla, state_rglru, c_ctx, mod_w, mod_b, norm_g, mlp_w1, mlp_b1, mlp_w2,
           mlp_b2, ev_w_in, ev_w_out, sgu_ln_g, sgu_ln_b, sgu_ws, sgu_bs, gla_gate_w2, gla_gate_b, gla_norm_g,
           rg_w_in, rg_conv_w, rg_conv_b, rg_wa, rg_ba, rg_wx, rg_bx, rg_L, rg_w_out):
    assert x_prompt.shape == (BATCH, SEQ, D) and x_sample.shape == (DEC_BATCH, DEC_SEQ, D)
    assert SEQ == TM and DEC_SEQ % TM == 0 and DEPTH == 2
    xp = x_prompt.reshape(NTP * TM, D)
    xs = x_sample.reshape(NTS * TM, D)
    cond8 = jnp.concatenate([c_ctx[None, :], c, jnp.zeros((NCOND - 1 - DEC_BATCH, D), F32)], axis=0)
    mods = _modulation(cond8, mod_w, mod_b)
    rtab, ctab = _pos_tables()

    gz = jnp.zeros((GLA_RANK, QK), F32)
    gmat = jnp.concatenate([jnp.concatenate([gla_gate_w2[0, 0], gz], axis=1),
                            jnp.concatenate([gz, gla_gate_w2[0, 1]], axis=1)], axis=0)
    s0_gla = jnp.concatenate([jnp.zeros((BATCH,) + state_gla.shape[2:], F32), state_gla[:, 0]], axis=0)
    ev = {
        "win": ev_w_in[0].astype(BF16),
        "gmat": gmat.astype(BF16),
        "gb": gla_gate_b[0],
        "lng": sgu_ln_g[0].reshape(1, SGU_WIDTH),
        "lnb": sgu_ln_b[0].reshape(1, SGU_WIDTH),
        "ws": sgu_ws[0].astype(BF16),
        "bst": sgu_bs[0].T,
        "gn": gla_norm_g[0].reshape(1, VW),
        "wout": ev_w_out[0].astype(BF16),
        "p0": s0_gla,
    }
    mid, pf, w1b, w2b = _even_forward(xp, xs, rtab, ctab, mods, norm_g, ev, mlp_w1, mlp_w2, 0)
    x1, pb = _even_reverse(xp, xs, rtab, ctab, mods, norm_g, mid, ev, 0)
    x2 = _mlp(x1, mods, norm_g, w1b, mlp_b1, w2b, mlp_b2, 0, 0, NT)
    new_gla = jnp.stack([pf[:BATCH, 0], pb[:BATCH, 0]], axis=1)[:, None]

    s0_rg = jnp.concatenate([jnp.zeros((BATCH, 2, D_RNN), F32), state_rglru[:, 0]], axis=0)
    od = {
        "win": rg_w_in[0].astype(BF16),
        "cw": rg_conv_w[0],
        "cb": rg_conv_b[0].reshape(1, D_RNN),
        "wg": jnp.stack([
            jnp.concatenate([_block_diag_slabs(rg_wa[0, d]), _block_diag_slabs(rg_wx[0, d])], axis=-1)
            for d in range(2)], axis=0).astype(BF16),
        "ba": rg_ba[0], "bx": rg_bx[0], "lam": rg_L[0],
        "wout": rg_w_out[0].astype(BF16),
        "s0": s0_rg.reshape(NSEQ, 2, 1, D_RNN),
    }
    mid1, sf, w1b, w2b = _odd_forward(x2, mods, norm_g, od, mlp_w1, mlp_w2, 1)
    x3, sb = _odd_reverse(x2, mods, norm_g, mid1, od, 1)
    mlp1 = (mods, norm_g, w1b, mlp_b1, w2b, mlp_b2, 1)
    y_prompt = _mlp(x3, *mlp1, 0, NTP)
    y_sample = _mlp(x3, *mlp1, NTP, NTS)
    new_rg = jnp.stack([sf[:BATCH, 0, 0], sb[:BATCH, 0, 0]], axis=1)[:, None]
    return (y_prompt.reshape(BATCH, SEQ, D), y_sample.reshape(DEC_BATCH, DEC_SEQ, D), new_gla, new_rg)
```

```python
import functools

import jax
import jax.numpy as jnp
import numpy as np
from jax import lax
from jax.experimental import pallas as pl
from jax.experimental.pallas import tpu as pltpu

D = 1024
BATCH = 16
SEQ = 256
DEPTH = 2
DEC_BATCH = 4
DEC_SEQ = 4096
GRID_W = 64
D_FF = 4 * D
EPS = 1e-6
SGU_CHUNK = 128
SGU_GROUPS = 4
SGU_WIDTH = D // 2
GLA_HEADS = 4
GLA_DV = 128
GLA_DK = 64
GLA_RANK = 16
GLA_NORMALIZER = 16.0
GLA_CHUNK = 64
QK = GLA_HEADS * GLA_DK
VW = GLA_HEADS * GLA_DV
EV_MAIN = 2 * SGU_WIDTH + 2 * QK + 2 * VW
EV_IN = EV_MAIN + 2 * GLA_RANK
D_RNN = D
RG_BLOCKS = 16
RG_BS = D_RNN // RG_BLOCKS
RG_C = 8.0
LANES = 128
SUBLANES = 8
SLAB = 256

TM = 256
NTP = BATCH * SEQ // TM
TPS = DEC_SEQ // TM
NTS = DEC_BATCH * TPS
NT = NTP + NTS
NSEQ = BATCH + DEC_BATCH
NCOND = 8
HALO = 16
EV_MID = 2 * VW + VW + 2 * QK + VW + QK
OD_MID = 3 * D_RNN
VMEM_LIMIT = 56 * 1024 * 1024

F32 = jnp.float32
BF16 = jnp.bfloat16


def _tile_info(t):
    is_p = t < NTP
    ts = jnp.maximum(t - NTP, 0)
    sq = ts // TPS
    within = ts % TPS
    cidx = jnp.where(is_p, 0, 1 + sq)
    first = jnp.logical_or(is_p, within == 0)
    last = jnp.logical_or(is_p, within == TPS - 1)
    seq = jnp.where(is_p, t, BATCH + sq)
    return is_p, within, cidx, first, last, seq


def _seq_of(t):
    return jnp.where(t < NTP, t, BATCH + jnp.maximum(t - NTP, 0) // TPS)


def _rms(x, g):
    return x * lax.rsqrt(jnp.mean(x * x, axis=-1, keepdims=True) + EPS) * g


def _dot(a, b):
    return jnp.dot(a, b, preferred_element_type=F32)


def _dot_nt(a, b):
    return lax.dot_general(a, b, (((1,), (1,)), ((), ())), preferred_element_type=F32)


def _dot_tn(a, b):
    return lax.dot_general(a, b, (((0,), (0,)), ((), ())), preferred_element_type=F32)


def _split3(x):
    hi = x.astype(BF16)
    r1 = x - hi.astype(F32)
    mid = r1.astype(BF16)
    lo = (r1 - mid.astype(F32)).astype(BF16)
    return hi, mid, lo


def _dot_exact_lhs(m, parts):
    return _dot(m, parts[0]) + _dot(m, parts[1]) + _dot(m, parts[2])


def _log_sigmoid(x):
    return jnp.minimum(x, 0.0) - jnp.log(1.0 + jnp.exp(-jnp.abs(x)))


def _softplus(x):
    return jnp.maximum(x, 0.0) + jnp.log(1.0 + jnp.exp(-jnp.abs(x)))


LOG2E = 1.4426950408889634
_GELU_C = 0.7978845608028654 * LOG2E


def _sigmoid(x):
    return 1.0 / (1.0 + jnp.exp2(x * (-LOG2E)))


def _gelu(x):
    return x / (1.0 + jnp.exp2(x * ((-2.0 * _GELU_C) + (-2.0 * 0.044715 * _GELU_C) * (x * x))))


def _mod_rows(mod_ref, cidx):
    m = mod_ref[pl.ds(cidx, 1), :]
    return [m[:, j * D:(j + 1) * D] for j in range(6)]


def _load_x0(is_p, within, xp_ref, xs_ref, rtab_ref, ctab_ref):
    rows_per_tile = TM // GRID_W
    r0 = within * rows_per_tile
    posr = jnp.concatenate(
        [jnp.broadcast_to(rtab_ref[pl.ds(r0 + j, 1), :], (GRID_W, D // 2)) for j in range(rows_per_tile)],
        axis=0)
    posc = jnp.concatenate([ctab_ref[...]] * rows_per_tile, axis=0)
    pos = jnp.concatenate([posr, posc], axis=1)
    return jnp.where(is_p, xp_ref[...], xs_ref[...] + pos)


def _mod_kernel(cond_ref, w_ref, b_ref, o_ref):
    c = cond_ref[...]
    sc = (c * _sigmoid(c)).astype(BF16)
    o_ref[0] = _dot(sc, w_ref[0].astype(BF16)) + b_ref[pl.ds(pl.program_id(0), 1), :]


def _modulation(cond8, mod_w, mod_b):
    wn = 2 * D
    return pl.pallas_call(
        _mod_kernel,
        out_shape=jax.ShapeDtypeStruct((DEPTH, NCOND, 6 * D), F32),
        grid=(DEPTH, 6 * D // wn),
        in_specs=[
            pl.BlockSpec((NCOND, D), lambda l, j: (0, 0)),
            pl.BlockSpec((1, D, wn), lambda l, j: (l, 0, j)),
            pl.BlockSpec((DEPTH, wn), lambda l, j: (0, j)),
        ],
        out_specs=pl.BlockSpec((1, NCOND, wn), lambda l, j: (l, 0, j)),
        compiler_params=pltpu.CompilerParams(
            dimension_semantics=("arbitrary", "arbitrary"), vmem_limit_bytes=VMEM_LIMIT),
        name="modulation",
    )(cond8, mod_w, mod_b)


def _chunk_rows(c):
    return slice(c * GLA_CHUNK, (c + 1) * GLA_CHUNK)


def _pair_lanes(p):
    return slice(p * LANES, (p + 1) * LANES)


def _gla_order(reverse):
    n = TM // GLA_CHUNK
    return list(reversed(range(n))) if reverse else list(range(n))


def _cum_mask(reverse):
    ri, ci = np.indices((TM, TM))
    order = (ci >= ri) if reverse else (ci <= ri)
    return jnp.asarray(np.logical_and(ri // GLA_CHUNK == ci // GLA_CHUNK, order), BF16)


def _gla_prep(qs, k, la, cum_m, reverse):
    n = TM // GLA_CHUNK
    b = _dot_exact_lhs(cum_m, _split3(la))
    ends = [c * GLA_CHUNK if reverse else (c + 1) * GLA_CHUNK - 1 for c in range(n)]
    btot = jnp.concatenate([jnp.broadcast_to(b[e:e + 1, :], (GLA_CHUNK, QK)) for e in ends], axis=0)
    qe = (qs * jnp.exp(b)).astype(BF16)
    ke = k * jnp.exp(-b)
    kd = (k * jnp.exp(btot - b)).astype(BF16)
    dec = jnp.exp(btot)
    lane = lax.broadcasted_iota(jnp.int32, (TM, QK), 1) % LANES
    ke_h = (jnp.where(lane < GLA_DK, ke, 0.0).astype(BF16), jnp.where(lane >= GLA_DK, ke, 0.0).astype(BF16))
    return qe, ke_h, kd, dec


def _gla_products(prep, v_bf, reverse):
    qe, ke_h, kd, _ = prep
    n = TM // GLA_CHUNK
    cr = lax.broadcasted_iota(jnp.int32, (GLA_CHUNK, 2 * GLA_CHUNK), 0)
    cc = lax.broadcasted_iota(jnp.int32, (GLA_CHUNK, 2 * GLA_CHUNK), 1) % GLA_CHUNK
    cmask = (cc >= cr) if reverse else (cc <= cr)
    br = lax.broadcasted_iota(jnp.int32, (2 * GLA_DK, 2 * GLA_DV), 0)
    bc = lax.broadcasted_iota(jnp.int32, (2 * GLA_DK, 2 * GLA_DV), 1)
    diag = (br < GLA_DK) == (bc < GLA_DV)
    att = [[None] * n for _ in range(GLA_HEADS // 2)]
    ds = [[None] * n for _ in range(GLA_HEADS // 2)]
    for p in range(GLA_HEADS // 2):
        vp = v_bf[:, p * 2 * GLA_DV:(p + 1) * 2 * GLA_DV]
        for c in _gla_order(reverse):
            rs, ls = _chunk_rows(c), _pair_lanes(p)
            kk = jnp.concatenate([ke_h[0][rs, ls], ke_h[1][rs, ls]], axis=0)
            att[p][c] = jnp.where(cmask, _dot_nt(qe[rs, ls], kk), 0.0).astype(BF16)
            ds[p][c] = jnp.where(diag, _dot_tn(kd[rs, ls], vp[rs, :]), 0.0)
    return att, ds


def _gla_outputs(prep, prods, v_bf, p_scr, reverse):
    qe, _, _, dec = prep
    att, ds = prods
    n = TM // GLA_CHUNK
    half = TM // 2
    lane_v = lax.broadcasted_iota(jnp.int32, (TM, 2 * GLA_DV), 1)
    rows = [[None] * (GLA_HEADS // 2) for _ in range(n)]
    for p in range(GLA_HEADS // 2):
        ls = _pair_lanes(p)
        vp = v_bf[:, p * 2 * GLA_DV:(p + 1) * 2 * GLA_DV]
        vl = jnp.where(lane_v < GLA_DV, vp, jnp.zeros_like(vp))
        vr = jnp.where(lane_v >= GLA_DV, vp, jnp.zeros_like(vp))
        dec_t = (dec[0:half, ls].T, dec[half:TM, ls].T)
        s = p_scr[p]
        for c in _gla_order(reverse):
            rs = _chunk_rows(c)
            rhs = jnp.concatenate([vl[rs, :], vr[rs, :], s.astype(BF16)], axis=0)
            rows[c][p] = _dot(jnp.concatenate([att[p][c], qe[rs, ls]], axis=1), rhs)
            col = (c % 2) * GLA_CHUNK
            s = dec_t[(c * GLA_CHUNK) // half][:, col:col + 1] * s + ds[p][c]
        p_scr[p] = s
    return jnp.concatenate([jnp.concatenate(r, axis=1) for r in rows], axis=0)


def _gla_state_load(s_ref, p_scr):
    p_scr[...] = jnp.zeros(p_scr.shape, F32)
    for h in range(GLA_HEADS):
        r0, c0 = (h % 2) * GLA_DK, (h % 2) * GLA_DV
        p_scr[h // 2, r0:r0 + GLA_DK, c0:c0 + GLA_DV] = s_ref[0, 0, h]


def _gla_state_store(p_scr, s_ref):
    for h in range(GLA_HEADS):
        r0, c0 = (h % 2) * GLA_DK, (h % 2) * GLA_DV
        s_ref[0, 0, h] = p_scr[h // 2, r0:r0 + GLA_DK, c0:c0 + GLA_DV]


def _ev1_kernel(xp_ref, xs_ref, rtab_ref, ctab_ref, mod_ref, ng_ref, win_ref, gmat_ref, gb_ref,
                lng_ref, lnb_ref, ws_ref, bst_ref, cum_ref, p0_ref, w1f_ref, w2f_ref,
                mid_ref, pfin_ref, w1b_ref, w2b_ref, p_scr, pg_scr, ps_scr, la_scr, *, layer):
    i = pl.program_id(0)
    mod_ref, ng_ref = mod_ref.at[layer], ng_ref.at[layer]
    is_p, within, cidx, _, _, _ = _tile_info(jnp.minimum(i, NT - 1))
    _, _, _, first, last, _ = _tile_info(jnp.maximum(i - 1, 0))
    _cast_slabs(w1f_ref, w2f_ref, w1b_ref, w2b_ref)

    @pl.when(i == 0)
    def _():
        pg_scr[...] = jnp.zeros(pg_scr.shape, F32)
        ps_scr[...] = jnp.zeros(ps_scr.shape, F32)
        la_scr[...] = jnp.zeros(la_scr.shape, F32)

    @pl.when(first)
    def _():
        _gla_state_load(p0_ref, p_scr)

    x = _load_x0(is_p, within, xp_ref, xs_ref, rtab_ref, ctab_ref)
    sh1, sc1, _, _, _, _ = _mod_rows(mod_ref, cidx)
    hm = (_rms(x, ng_ref[0:1, :]) * (1.0 + sc1) + sh1).astype(BF16)
    o0 = 2 * SGU_WIDTH
    lr = _dot(hm, win_ref[:, EV_MAIN:EV_IN]).astype(BF16)

    qs = pg_scr[:, 0:QK] * (GLA_DK ** -0.5)
    k = pg_scr[:, QK:2 * QK]
    v = pg_scr[:, 2 * QK:2 * QK + VW]
    g = pg_scr[:, 2 * QK + VW:2 * QK + 2 * VW]
    v_bf = v.astype(BF16)
    c0 = 3 * VW
    mid_ref[:, 2 * VW:3 * VW] = g * _sigmoid(g)
    mid_ref[:, c0:c0 + QK] = qs
    mid_ref[:, c0 + QK:c0 + 2 * QK] = k
    mid_ref[:, c0 + 2 * QK:c0 + 2 * QK + VW] = v
    mid_ref[:, c0 + 2 * QK + VW:c0 + 3 * QK + VW] = la_scr[:, QK:2 * QK]
    prep = _gla_prep(qs, k, la_scr[:, 0:QK], cum_ref[...], reverse=False)

    pg_scr[...] = _dot(hm, win_ref[:, o0:EV_MAIN])
    u = _gelu(ps_scr[:, 0:SGU_WIDTH])
    vg = _gelu(ps_scr[:, SGU_WIDTH:2 * SGU_WIDTH])
    mu = jnp.mean(vg, axis=-1, keepdims=True)
    vc = vg - mu
    vn = (vc * lax.rsqrt(jnp.mean(vc * vc, axis=-1, keepdims=True) + EPS) * lng_ref[...] + lnb_ref[...]).astype(BF16)
    gb = jnp.concatenate([gb_ref[0:1, :], gb_ref[1:2, :]], axis=1)
    la_scr[...] = _log_sigmoid(_dot(lr, gmat_ref[...]) + gb) * (1.0 / GLA_NORMALIZER)
    prods = _gla_products(prep, v_bf, reverse=False)
    ps_scr[...] = _dot(hm, win_ref[:, 0:o0])

    gd = SGU_WIDTH // SGU_GROUPS
    nch = TM // SGU_CHUNK
    sv_cols = []
    for grp in range(SGU_GROUPS):
        vcat = jnp.concatenate(
            [vn[c * SGU_CHUNK:(c + 1) * SGU_CHUNK, grp * gd:(grp + 1) * gd] for c in range(nch)], axis=1)
        sg = _dot(ws_ref[grp], vcat) + bst_ref[:, grp:grp + 1]
        sv_cols.append(jnp.concatenate([sg[:, c * gd:(c + 1) * gd] for c in range(nch)], axis=0))
    mid_ref[:, 0:VW] = u * jnp.concatenate(sv_cols, axis=1)
    mid_ref[:, VW:2 * VW] = _gla_outputs(prep, prods, v_bf, p_scr, reverse=False)

    @pl.when(last)
    def _():
        _gla_state_store(p_scr, pfin_ref)


def _const_spec(shape):
    nd = len(shape)
    return pl.BlockSpec(shape, lambda i, _nd=nd: (0,) * _nd, pipeline_mode=pl.Buffered(1))


def _params():
    return pltpu.CompilerParams(dimension_semantics=("arbitrary",), vmem_limit_bytes=VMEM_LIMIT)


CAST_STEPS = 64


def _cast_specs(layer):
    r1, r2 = D // CAST_STEPS, D_FF // CAST_STEPS
    in_specs = [pl.BlockSpec((1, r1, D_FF), lambda i: (layer, jnp.minimum(i, CAST_STEPS - 1), 0)),
                pl.BlockSpec((1, r2, D), lambda i: (layer, jnp.minimum(i, CAST_STEPS - 1), 0))]
    out_specs = [pl.BlockSpec((r1, D_FF), lambda i: (jnp.minimum(i, CAST_STEPS - 1), 0)),
                 pl.BlockSpec((r2, D), lambda i: (jnp.minimum(i, CAST_STEPS - 1), 0))]
    shapes = [jax.ShapeDtypeStruct((D, D_FF), BF16), jax.ShapeDtypeStruct((D_FF, D), BF16)]
    return in_specs, out_specs, shapes


def _cast_slabs(w1f_ref, w2f_ref, w1b_ref, w2b_ref):
    w1b_ref[...] = w1f_ref[0].astype(BF16)
    w2b_ref[...] = w2f_ref[0].astype(BF16)


def _even_forward(xp, xs, rtab, ctab, mod, ng, w, w1f, w2f, layer):
    state_blk = (1, 1, GLA_HEADS, GLA_DK, GLA_DV)
    cast_in, cast_out, cast_shapes = _cast_specs(layer)

    def mix_tile(i):
        return jnp.maximum(i - 1, 0)

    return pl.pallas_call(
        functools.partial(_ev1_kernel, layer=layer),
        out_shape=(jax.ShapeDtypeStruct((NT * TM, EV_MID), F32),
                   jax.ShapeDtypeStruct((NSEQ, 1) + state_blk[2:], F32), *cast_shapes),
        grid=(NT + 1,),
        in_specs=[
            pl.BlockSpec((TM, D), lambda i: (jnp.minimum(i, NTP - 1), 0)),
            pl.BlockSpec((TM, D), lambda i: (jnp.maximum(jnp.minimum(i, NT - 1) - NTP, 0), 0)),
            _const_spec((GRID_W, D // 2)), _const_spec((GRID_W, D // 2)),
            _const_spec((DEPTH, NCOND, 6 * D)), _const_spec((DEPTH, 4, D)),
            _const_spec((D, EV_IN)), _const_spec((2 * GLA_RANK, 2 * QK)),
            _const_spec((2, QK)),
            _const_spec((1, SGU_WIDTH)), _const_spec((1, SGU_WIDTH)),
            _const_spec((SGU_GROUPS, SGU_CHUNK, SGU_CHUNK)), _const_spec((SGU_CHUNK, SGU_GROUPS)),
            _const_spec((TM, TM)),
            pl.BlockSpec(state_blk, lambda i: (_seq_of(mix_tile(i)), 0, 0, 0, 0)),
            *cast_in,
        ],
        out_specs=(pl.BlockSpec((TM, EV_MID), lambda i: (mix_tile(i), 0)),
                   pl.BlockSpec(state_blk, lambda i: (_seq_of(mix_tile(i)), 0, 0, 0, 0)), *cast_out),
        scratch_shapes=[pltpu.VMEM((GLA_HEADS // 2, 2 * GLA_DK, 2 * GLA_DV), F32),
                        pltpu.VMEM((TM, 2 * QK + 2 * VW), F32), pltpu.VMEM((TM, 2 * SGU_WIDTH), F32),
                        pltpu.VMEM((TM, 2 * QK), F32)],
        compiler_params=_params(),
        name="even_forward",
    )(xp, xs, rtab, ctab, mod, ng, w["win"], w["gmat"], w["gb"], w["lng"], w["lnb"],
      w["ws"], w["bst"], _cum_mask(False), w["p0"], w1f, w2f)


def _ev2_kernel(xp_ref, xs_ref, rtab_ref, ctab_ref, mod_ref, ng_ref, mid_ref, gn_ref, wout_ref, cum_ref, p0_ref,
                x1_ref, pfin_ref, p_scr, cat_scr, *, layer):
    i = pl.program_id(0)
    mod_ref, ng_ref = mod_ref.at[layer], ng_ref.at[layer]
    live = i < NT
    _, _, _, first, last, _ = _tile_info(NT - 1 - jnp.minimum(i, NT - 1))
    is_p, within, cidx, _, _, _ = _tile_info(NT - 1 - jnp.maximum(i - 1, 0))

    @pl.when(i == 0)
    def _():
        cat_scr[...] = jnp.zeros(cat_scr.shape, BF16)

    @pl.when(jnp.logical_and(last, live))
    def _():
        _gla_state_load(p0_ref, p_scr)

    nblk = D // SLAB
    ys = [None] * nblk

    def project(j):
        ys[j] = _dot(cat_scr[...], wout_ref[:, j * SLAB:(j + 1) * SLAB])

    c0 = 3 * VW
    qs = mid_ref[:, c0:c0 + QK]
    k = mid_ref[:, c0 + QK:c0 + 2 * QK]
    v_bf = mid_ref[:, c0 + 2 * QK:c0 + 2 * QK + VW].astype(BF16)
    la_b = mid_ref[:, c0 + 2 * QK + VW:c0 + 3 * QK + VW]
    project(0)
    prep = _gla_prep(qs, k, la_b, cum_ref[...], reverse=True)
    project(1)
    prods = _gla_products(prep, v_bf, reverse=True)
    project(2)
    o = mid_ref[:, VW:2 * VW] + _gla_outputs(prep, prods, v_bf, p_scr, reverse=True)
    project(3)
    heads = []
    for h in range(GLA_HEADS):
        oh = o[:, h * GLA_DV:(h + 1) * GLA_DV]
        heads.append(oh * lax.rsqrt(jnp.mean(oh * oh, axis=-1, keepdims=True) + EPS))
    on = jnp.concatenate(heads, axis=1) * gn_ref[...] * mid_ref[:, 2 * VW:3 * VW]
    cat_scr[...] = jnp.concatenate([mid_ref[:, 0:VW], on], axis=1).astype(BF16)

    x = _load_x0(is_p, within, xp_ref, xs_ref, rtab_ref, ctab_ref)
    _, _, g1, _, _, _ = _mod_rows(mod_ref, cidx)
    x1_ref[...] = x + g1 * _rms(jnp.concatenate(ys, axis=1), ng_ref[1:2, :])

    @pl.when(jnp.logical_and(first, live))
    def _():
        _gla_state_store(p_scr, pfin_ref)


def _even_reverse(xp, xs, rtab, ctab, mod, ng, mid, w, layer):
    state_blk = (1, 1, GLA_HEADS, GLA_DK, GLA_DV)
    assert D // SLAB == 4

    def mix_tile(i):
        return NT - 1 - jnp.minimum(i, NT - 1)

    def out_tile(i):
        return NT - 1 - jnp.maximum(i - 1, 0)

    return pl.pallas_call(
        functools.partial(_ev2_kernel, layer=layer),
        out_shape=(jax.ShapeDtypeStruct((NT * TM, D), F32),
                   jax.ShapeDtypeStruct((NSEQ, 1) + state_blk[2:], F32)),
        grid=(NT + 1,),
        in_specs=[
            pl.BlockSpec((TM, D), lambda i: (jnp.minimum(out_tile(i), NTP - 1), 0)),
            pl.BlockSpec((TM, D), lambda i: (jnp.maximum(out_tile(i) - NTP, 0), 0)),
            _const_spec((GRID_W, D // 2)), _const_spec((GRID_W, D // 2)),
            _const_spec((DEPTH, NCOND, 6 * D)), _const_spec((DEPTH, 4, D)),
            pl.BlockSpec((TM, EV_MID), lambda i: (mix_tile(i), 0)),
            _const_spec((1, VW)), _const_spec((2 * VW, D)), _const_spec((TM, TM)),
            pl.BlockSpec(state_blk, lambda i: (_seq_of(mix_tile(i)), 1, 0, 0, 0)),
        ],
        out_specs=(pl.BlockSpec((TM, D), lambda i: (out_tile(i), 0)),
                   pl.BlockSpec(state_blk, lambda i: (_seq_of(mix_tile(i)), 0, 0, 0, 0))),
        scratch_shapes=[pltpu.VMEM((GLA_HEADS // 2, 2 * GLA_DK, 2 * GLA_DV), F32),
                        pltpu.VMEM((TM, D), BF16)],
        compiler_params=_params(),
        name="even_reverse",
    )(xp, xs, rtab, ctab, mod, ng, mid, w["gn"], w["wout"], _cum_mask(True), w["p0"])


FF_CHUNK = 1024
MLP_TILES = 2
TMM = MLP_TILES * TM


def _mlp_kernel(x_ref, mod_ref, ng_ref, w1_ref, b1_ref, w2_ref, b2_ref, o_ref, *, tile0, layer):
    mod_ref, ng_ref = mod_ref.at[layer], ng_ref.at[layer]
    b1_ref, b2_ref = b1_ref.at[layer:layer + 1], b2_ref.at[layer:layer + 1]
    _, _, cidx, _, _, _ = _tile_info(tile0 + pl.program_id(0) * MLP_TILES)
    _, _, _, sh2, sc2, g2 = _mod_rows(mod_ref, cidx)
    x = x_ref[...]
    hff = (_rms(x, ng_ref[2:3, :]) * (1.0 + sc2) + sh2).astype(BF16)
    acc = b2_ref[...]
    for j in range(D_FF // FF_CHUNK):
        cs = slice(j * FF_CHUNK, (j + 1) * FF_CHUNK)
        h = _dot(hff, w1_ref[:, cs]) + b1_ref[:, cs]
        h = jnp.square(jnp.maximum(h, 0.0)).astype(BF16)
        acc = acc + _dot(h, w2_ref[cs, :])
    o_ref[...] = x + g2 * _rms(acc, ng_ref[3:4, :])


def _mlp(x, mod, ng, w1, b1, w2, b2, layer, tile0, ntiles):
    assert tile0 % MLP_TILES == 0 and ntiles % MLP_TILES == 0 and TPS % MLP_TILES == 0 and NTP % MLP_TILES == 0
    b0 = tile0 // MLP_TILES
    return pl.pallas_call(
        functools.partial(_mlp_kernel, tile0=tile0, layer=layer),
        out_shape=jax.ShapeDtypeStruct((ntiles * TM, D), F32),
        grid=(ntiles // MLP_TILES,),
        in_specs=[
            pl.BlockSpec((TMM, D), lambda i: (b0 + i, 0)),
            _const_spec((DEPTH, NCOND, 6 * D)), _const_spec((DEPTH, 4, D)),
            _const_spec((D, D_FF)), _const_spec((DEPTH, D_FF)), _const_spec((D_FF, D)), _const_spec((DEPTH, D)),
        ],
        out_specs=pl.BlockSpec((TMM, D), lambda i: (i, 0)),
        compiler_params=_params(), name="mlp",
    )(x, mod, ng, w1, b1, w2, b2)


def _rg_gates(xc, wg_ref, ba_ref, bx_ref, lam_ref, a_scr, b_scr, slabs):
    xcb = xc.astype(BF16)
    sp = RG_C * _softplus(-lam_ref[...])
    for s in slabs:
        cs = slice(s * SLAB, (s + 1) * SLAB)
        pre = _dot(xcb[:, cs], wg_ref[s])
        r = _sigmoid(pre[:, 0:SLAB] + ba_ref[:, cs])
        i = _sigmoid(pre[:, SLAB:2 * SLAB] + bx_ref[:, cs])
        z = r * sp[:, cs]
        a = jnp.exp(-z)
        a_scr[:, cs] = a
        u = jnp.tanh(z) * (1.0 + a * a)
        b_scr[:, cs] = jnp.where(u > 0.0, u * lax.rsqrt(u), 0.0) * (i * xc[:, cs])


SEG = TM // SUBLANES


def _seg_perm(transpose):
    rr, cc = np.indices((TM, TM))
    if transpose:
        rr, cc = cc, rr
    return jnp.asarray(np.logical_and(cc // SEG == rr % SUBLANES, cc % SEG == rr // SUBLANES), BF16)


def _rg_scan(a_scr, b_scr, h_scr, c_scr, h0, reverse):
    def body(i, carry):
        hh, cc = carry
        grp = (SEG - 1 - i) if reverse else i
        r0 = pl.multiple_of(grp * SUBLANES, SUBLANES)
        a = a_scr[pl.ds(r0, SUBLANES), :]
        hh = a * hh + b_scr[pl.ds(r0, SUBLANES), :]
        cc = a * cc
        h_scr[pl.ds(r0, SUBLANES), :] = hh
        c_scr[pl.ds(r0, SUBLANES), :] = cc
        return hh, cc

    init = (jnp.zeros((SUBLANES, D_RNN), F32), jnp.ones((SUBLANES, D_RNN), F32))
    b, a = lax.fori_loop(0, SEG, body, init, unroll=4 if reverse else True)
    row = lax.broadcasted_iota(jnp.int32, (SUBLANES, D_RNN), 0)
    for s in (1, 2, 4):
        shift = (SUBLANES - s) if reverse else s
        valid = (row < SUBLANES - s) if reverse else (row >= s)
        a_s = pltpu.roll(a, shift, 0)
        b_s = pltpu.roll(b, shift, 0)
        b = jnp.where(valid, a * b_s + b, b)
        a = jnp.where(valid, a * a_s, a)
    after = a * h0 + b
    edge = (row == SUBLANES - 1) if reverse else (row == 0)
    enter = jnp.where(edge, h0, pltpu.roll(after, (SUBLANES - 1) if reverse else 1, 0))
    h = h_scr[...] + c_scr[...] * jnp.concatenate([enter] * SEG, axis=0)
    out = after[0:1, :] if reverse else after[SUBLANES - 1:SUBLANES, :]
    return h, out


def _od1_kernel(x_ref, xn_ref, mod_ref, ng_ref, win_ref, cw_ref, cb_ref, wg_ref, ba_ref, bx_ref, lam_ref,
                perm_ref, s0_ref, w1f_ref, w2f_ref, mid_ref, sfin_ref, w1b_ref, w2b_ref,
                proj_scr, xb_scr, a_scr, b_scr, h_scr, c_scr, hc_scr, tail_scr, *, layer):
    i = pl.program_id(0)
    mod_ref, ng_ref = mod_ref.at[layer], ng_ref.at[layer]
    ba_ref, bx_ref, lam_ref = ba_ref.at[0:1], bx_ref.at[0:1], lam_ref.at[0:1]
    _, _, cidx, _, _, _ = _tile_info(jnp.minimum(i, NT - 1))
    _, _, _, first, last, _ = _tile_info(jnp.maximum(i - 1, 0))
    _cast_slabs(w1f_ref, w2f_ref, w1b_ref, w2b_ref)

    @pl.when(i == 0)
    def _():
        proj_scr[...] = jnp.zeros(proj_scr.shape, F32)

    @pl.when(first)
    def _():
        tail_scr[...] = jnp.zeros((2 * SUBLANES, D_RNN), F32)
        hc_scr[...] = s0_ref[0, 0]

    nslab = D_RNN // SLAB
    pw = 2 * D_RNN // nslab

    def project(lhs, s):
        proj_scr[:, s * pw:(s + 1) * pw] = _dot(lhs, win_ref[:, s * pw:(s + 1) * pw])

    sh1, sc1, _, _, _, _ = _mod_rows(mod_ref, cidx)
    xe = jnp.concatenate([x_ref[...], xn_ref[...]], axis=0)
    hm = (_rms(xe, ng_ref[0:1, :]) * (1.0 + sc1) + sh1).astype(BF16)
    hmp = _dot(perm_ref[...], hm[0:TM, :]).astype(BF16)
    lhs = jnp.concatenate([hmp, hm[TM:TM + HALO, :]], axis=0)

    row = lax.broadcasted_iota(jnp.int32, (SUBLANES, D_RNN), 0)
    g30 = proj_scr[TM - 2 * SUBLANES:TM - SUBLANES, 0:D_RNN]
    g31 = proj_scr[TM - SUBLANES:TM, 0:D_RNN]
    g0 = proj_scr[0:SUBLANES, 0:D_RNN]
    nxt = jnp.where(last, 0.0, proj_scr[TM:TM + 1, 0:D_RNN])
    xb_scr[0:SUBLANES, :] = jnp.where(row == 0, pltpu.roll(tail_scr[0:SUBLANES, :], 1, 0), pltpu.roll(g30, 1, 0))
    xb_scr[SUBLANES:2 * SUBLANES, :] = jnp.where(
        row == 0, pltpu.roll(tail_scr[SUBLANES:2 * SUBLANES, :], 1, 0), pltpu.roll(g31, 1, 0))
    xb_scr[2 * SUBLANES:2 * SUBLANES + TM, :] = proj_scr[0:TM, 0:D_RNN]
    xb_scr[2 * SUBLANES + TM:3 * SUBLANES + TM, :] = jnp.where(
        row == SUBLANES - 1, nxt, pltpu.roll(g0, SUBLANES - 1, 0))
    tail_scr[0:SUBLANES, :] = g30
    tail_scr[SUBLANES:2 * SUBLANES, :] = g31

    xc = cb_ref[...]
    for j in range(4):
        xc = xc + cw_ref[j:j + 1, :] * xb_scr[j * SUBLANES:j * SUBLANES + TM, :]
    mid_ref[:, 2 * D_RNN:3 * D_RNN] = xc

    project(lhs, 0)
    mid_ref[:, D_RNN:2 * D_RNN] = _gelu(proj_scr[0:TM, D_RNN:2 * D_RNN])
    for s in range(nslab):
        if s + 1 < nslab:
            project(lhs, s + 1)
        _rg_gates(mid_ref[:, 2 * D_RNN:3 * D_RNN], wg_ref, ba_ref, bx_ref, lam_ref, a_scr, b_scr, range(s, s + 1))

    h, hc_scr[...] = _rg_scan(a_scr, b_scr, h_scr, c_scr, hc_scr[...], reverse=False)
    mid_ref[:, 0:D_RNN] = h

    @pl.when(last)
    def _():
        sfin_ref[0, 0] = hc_scr[...]


def _odd_forward(x, mod, ng, w, w1f, w2f, layer):
    nb16 = TM // HALO
    cast_in, cast_out, cast_shapes = _cast_specs(layer)

    def proj_tile(i):
        return jnp.minimum(i, NT - 1)

    def scan_tile(i):
        return jnp.maximum(i - 1, 0)

    return pl.pallas_call(
        functools.partial(_od1_kernel, layer=layer),
        out_shape=(jax.ShapeDtypeStruct((NT * TM, OD_MID), F32),
                   jax.ShapeDtypeStruct((NSEQ, 1, 1, D_RNN), F32), *cast_shapes),
        grid=(NT + 1,),
        in_specs=[
            pl.BlockSpec((TM, D), lambda i: (proj_tile(i), 0)),
            pl.BlockSpec((HALO, D), lambda i: (jnp.minimum(proj_tile(i) + 1, NT - 1) * nb16, 0)),
            _const_spec((DEPTH, NCOND, 6 * D)), _const_spec((DEPTH, 4, D)),
            _const_spec((D, 2 * D_RNN)), _const_spec((4, D_RNN)), _const_spec((1, D_RNN)),
            _const_spec((D_RNN // SLAB, SLAB, 2 * SLAB)),
            _const_spec((2, D_RNN)), _const_spec((2, D_RNN)), _const_spec((2, D_RNN)),
            _const_spec((TM, TM)),
            pl.BlockSpec((1, 1, 1, D_RNN), lambda i: (_seq_of(scan_tile(i)), 0, 0, 0)),
            *cast_in,
        ],
        out_specs=(pl.BlockSpec((TM, OD_MID), lambda i: (scan_tile(i), 0)),
                   pl.BlockSpec((1, 1, 1, D_RNN), lambda i: (_seq_of(scan_tile(i)), 0, 0, 0)), *cast_out),
        scratch_shapes=[
            pltpu.VMEM((TM + HALO, 2 * D_RNN), F32),
            pltpu.VMEM((3 * SUBLANES + TM, D_RNN), F32),
            pltpu.VMEM((TM, D_RNN), F32), pltpu.VMEM((TM, D_RNN), F32), pltpu.VMEM((TM, D_RNN), F32),
            pltpu.VMEM((TM, D_RNN), F32),
            pltpu.VMEM((1, D_RNN), F32), pltpu.VMEM((2 * SUBLANES, D_RNN), F32),
        ],
        compiler_params=_params(),
        name="odd_forward",
    )(x, x, mod, ng, w["win"], w["cw"], w["cb"], w["wg"][0], w["ba"], w["bx"], w["lam"],
      _seg_perm(False), w["s0"], w1f, w2f)


def _od2_kernel(x_ref, mod_ref, ng_ref, mid_ref, wg_ref, ba_ref, bx_ref, lam_ref, wout_ref, perm_ref, s0_ref,
                x1_ref, sfin_ref, a_scr, b_scr, h_scr, c_scr, hc_scr, zp_scr, *, layer):
    i = pl.program_id(0)
    mod_ref, ng_ref = mod_ref.at[layer], ng_ref.at[layer]
    ba_ref, bx_ref, lam_ref = ba_ref.at[1:2], bx_ref.at[1:2], lam_ref.at[1:2]
    live = i < NT
    _, _, _, first, last, _ = _tile_info(NT - 1 - jnp.minimum(i, NT - 1))
    _, _, cidx_prev, _, _, _ = _tile_info(NT - 1 - jnp.maximum(i - 1, 0))
    _, _, g1, _, _, _ = _mod_rows(mod_ref, cidx_prev)

    @pl.when(i == 0)
    def _():
        zp_scr[...] = jnp.zeros(zp_scr.shape, BF16)

    @pl.when(jnp.logical_and(last, live))
    def _():
        hc_scr[...] = s0_ref[0, 0]

    xc = mid_ref[:, 2 * D_RNN:3 * D_RNN]
    nslab = D_RNN // SLAB
    z = _dot(perm_ref[...], zp_scr[...]).astype(BF16)
    ys = []
    for s in range(nslab):
        ys.append(_dot(z, wout_ref[:, s * SLAB:(s + 1) * SLAB]))
        _rg_gates(xc, wg_ref, ba_ref, bx_ref, lam_ref, a_scr, b_scr, range(s, s + 1))
    x1_ref[...] = x_ref[...] + g1 * _rms(jnp.concatenate(ys, axis=1), ng_ref[1:2, :])
    h_b, hc_scr[...] = _rg_scan(a_scr, b_scr, h_scr, c_scr, hc_scr[...], reverse=True)
    zp_scr[...] = ((mid_ref[:, 0:D_RNN] + h_b) * mid_ref[:, D_RNN:2 * D_RNN]).astype(BF16)

    @pl.when(jnp.logical_and(first, live))
    def _():
        sfin_ref[0, 0] = hc_scr[...]


def _odd_reverse(x, mod, ng, mid, w, layer):
    def scan_tile(i):
        return NT - 1 - jnp.minimum(i, NT - 1)

    def out_tile(i):
        return NT - 1 - jnp.maximum(i - 1, 0)

    return pl.pallas_call(
        functools.partial(_od2_kernel, layer=layer),
        out_shape=(jax.ShapeDtypeStruct((NT * TM, D), F32),
                   jax.ShapeDtypeStruct((NSEQ, 1, 1, D_RNN), F32)),
        grid=(NT + 1,),
        in_specs=[
            pl.BlockSpec((TM, D), lambda i: (out_tile(i), 0)),
            _const_spec((DEPTH, NCOND, 6 * D)), _const_spec((DEPTH, 4, D)),
            pl.BlockSpec((TM, OD_MID), lambda i: (scan_tile(i), 0)),
            _const_spec((D_RNN // SLAB, SLAB, 2 * SLAB)),
            _const_spec((2, D_RNN)), _const_spec((2, D_RNN)), _const_spec((2, D_RNN)),
            _const_spec((D_RNN, D)), _const_spec((TM, TM)),
            pl.BlockSpec((1, 1, 1, D_RNN), lambda i: (_seq_of(scan_tile(i)), 1, 0, 0)),
        ],
        out_specs=(pl.BlockSpec((TM, D), lambda i: (out_tile(i), 0)),
                   pl.BlockSpec((1, 1, 1, D_RNN), lambda i: (_seq_of(scan_tile(i)), 0, 0, 0))),
        scratch_shapes=[
            pltpu.VMEM((TM, D_RNN), F32), pltpu.VMEM((TM, D_RNN), F32), pltpu.VMEM((TM, D_RNN), F32),
            pltpu.VMEM((TM, D_RNN), F32),
            pltpu.VMEM((1, D_RNN), F32), pltpu.VMEM((TM, D_RNN), BF16),
        ],
        compiler_params=_params(),
        name="odd_reverse",
    )(x, mod, ng, mid, w["wg"][1], w["ba"], w["bx"], w["lam"], w["wout"], _seg_perm(True), w["s0"])


def _pos_tables():
    n = D // 4
    omega = 1.0 / (10000.0 ** (jnp.arange(n, dtype=F32) / n))
    idx = jnp.arange(GRID_W, dtype=F32)[:, None] * omega
    tab = jnp.concatenate([jnp.sin(idx), jnp.cos(idx)], axis=-1)
    return tab, tab


def _block_diag_slabs(w):
    per = SLAB // RG_BS
    rows = jnp.tile(w.reshape(D_RNN // SLAB, SLAB, RG_BS), (1, 1, per))
    ri, ci = np.indices((SLAB, SLAB))
    return jnp.where(jnp.asarray(ri // RG_BS == ci // RG_BS), rows, 0.0)


def kernel(x_prompt, x_sample, c, state_gla, state_rglru, c_ctx, mod_w, mod_b, norm_g, mlp_w1, mlp_b1, mlp_w2,
           mlp_b2, ev_w_in, ev_w_out, sgu_ln_g, sgu_ln_b, sgu_ws, sgu_bs, gla_gate_w2, gla_gate_b, gla_norm_g,
           rg_w_in, rg_conv_w, rg_conv_b, rg_wa, rg_ba, rg_wx, rg_bx, rg_L, rg_w_out):
    assert x_prompt.shape == (BATCH, SEQ, D) and x_sample.shape == (DEC_BATCH, DEC_SEQ, D)
    assert SEQ == TM and DEC_SEQ % TM == 0 and DEPTH == 2
    xp = x_prompt.reshape(NTP * TM, D)
    xs = x_sample.reshape(NTS * TM, D)
    cond8 = jnp.concatenate([c_ctx[None, :], c, jnp.zeros((NCOND - 1 - DEC_BATCH, D), F32)], axis=0)
    mods = _modulation(cond8, mod_w, mod_b)
    rtab, ctab = _pos_tables()

    gz = jnp.zeros((GLA_RANK, QK), F32)
    gmat = jnp.concatenate([jnp.concatenate([gla_gate_w2[0, 0], gz], axis=1),
                            jnp.concatenate([gz, gla_gate_w2[0, 1]], axis=1)], axis=0)
    s0_gla = jnp.concatenate([jnp.zeros((BATCH,) + state_gla.shape[2:], F32), state_gla[:, 0]], axis=0)
    ev = {
        "win": ev_w_in[0].astype(BF16),
        "gmat": gmat.astype(BF16),
        "gb": gla_gate_b[0],
        "lng": sgu_ln_g[0].reshape(1, SGU_WIDTH),
        "lnb": sgu_ln_b[0].reshape(1, SGU_WIDTH),
        "ws": sgu_ws[0].astype(BF16),
        "bst": sgu_bs[0].T,
        "gn": gla_norm_g[0].reshape(1, VW),
        "wout": ev_w_out[0].astype(BF16),
        "p0": s0_gla,
    }
    mid, pf, w1b, w2b = _even_forward(xp, xs, rtab, ctab, mods, norm_g, ev, mlp_w1, mlp_w2, 0)
    x1, pb = _even_reverse(xp, xs, rtab, ctab, mods, norm_g, mid, ev, 0)
    x2 = _mlp(x1, mods, norm_g, w1b, mlp_b1, w2b, mlp_b2, 0, 0, NT)
    new_gla = jnp.stack([pf[:BATCH, 0], pb[:BATCH, 0]], axis=1)[:, None]

    s0_rg = jnp.concatenate([jnp.zeros((BATCH, 2, D_RNN), F32), state_rglru[:, 0]], axis=0)
    od = {
        "win": rg_w_in[0].astype(BF16),
        "cw": rg_conv_w[0],
        "cb": rg_conv_b[0].reshape(1, D_RNN),
        "wg": jnp.stack([
            jnp.concatenate([_block_diag_slabs(rg_wa[0, d]), _block_diag_slabs(rg_wx[0, d])], axis=-1)
            for d in range(2)], axis=0).astype(BF16),
        "ba": rg_ba[0], "bx": rg_bx[0], "lam": rg_L[0],
        "wout": rg_w_out[0].astype(BF16),
        "s0": s0_rg.reshape(NSEQ, 2, 1, D_RNN),
    }
    mid1, sf, w1b, w2b = _odd_forward(x2, mods, norm_g, od, mlp_w1, mlp_w2, 1)
    x3, sb = _odd_reverse(x2, mods, norm_g, mid1, od, 1)
    mlp1 = (mods, norm_g, w1b, mlp_b1, w2b, mlp_b2, 1)
    y_prompt = _mlp(x3, *mlp1, 0, NTP)
    y_sample = _mlp(x3, *mlp1, NTP, NTS)
    new_rg = jnp.stack([sf[:BATCH, 0, 0], sb[:BATCH, 0, 0]], axis=1)[:, None]
    return (y_prompt.reshape(BATCH, SEQ, D), y_sample.reshape(DEC_BATCH, DEC_SEQ, D), new_gla, new_rg)
```

```python
import functools

import jax
import jax.numpy as jnp
import numpy as np
from jax import lax
from jax.experimental import pallas as pl
from jax.experimental.pallas import tpu as pltpu

D = 1024
BATCH = 16
SEQ = 256
DEPTH = 2
DEC_BATCH = 4
DEC_SEQ = 4096
GRID_W = 64
D_FF = 4 * D
EPS = 1e-6
SGU_CHUNK = 128
SGU_GROUPS = 4
SGU_WIDTH = D // 2
GLA_HEADS = 4
GLA_DV = 128
GLA_DK = 64
GLA_RANK = 16
GLA_NORMALIZER = 16.0
GLA_CHUNK = 64
QK = GLA_HEADS * GLA_DK
VW = GLA_HEADS * GLA_DV
EV_MAIN = 2 * SGU_WIDTH + 2 * QK + 2 * VW
EV_IN = EV_MAIN + 2 * GLA_RANK
D_RNN = D
RG_BLOCKS = 16
RG_BS = D_RNN // RG_BLOCKS
RG_C = 8.0
LANES = 128
SUBLANES = 8
SLAB = 256

TM = 256
NTP = BATCH * SEQ // TM
TPS = DEC_SEQ // TM
NTS = DEC_BATCH * TPS
NT = NTP + NTS
NSEQ = BATCH + DEC_BATCH
NCOND = 8
HALO = 16
EV_MID = 3 * VW
EV_MID_B = 3 * QK + VW
OD_MID = 3 * D_RNN
VMEM_LIMIT = 56 * 1024 * 1024

F32 = jnp.float32
BF16 = jnp.bfloat16


def _tile_info(t):
    is_p = t < NTP
    ts = jnp.maximum(t - NTP, 0)
    sq = ts // TPS
    within = ts % TPS
    cidx = jnp.where(is_p, 0, 1 + sq)
    first = jnp.logical_or(is_p, within == 0)
    last = jnp.logical_or(is_p, within == TPS - 1)
    seq = jnp.where(is_p, t, BATCH + sq)
    return is_p, within, cidx, first, last, seq


def _seq_of(t):
    return jnp.where(t < NTP, t, BATCH + jnp.maximum(t - NTP, 0) // TPS)


def _rms(x, g):
    return x * lax.rsqrt(jnp.mean(x * x, axis=-1, keepdims=True) + EPS) * g


def _dot(a, b):
    return jnp.dot(a, b, preferred_element_type=F32)


def _dot_nt(a, b):
    return lax.dot_general(a, b, (((1,), (1,)), ((), ())), preferred_element_type=F32)


def _dot_tn(a, b):
    return lax.dot_general(a, b, (((0,), (0,)), ((), ())), preferred_element_type=F32)


def _split3(x):
    hi = x.astype(BF16)
    r1 = x - hi.astype(F32)
    mid = r1.astype(BF16)
    lo = (r1 - mid.astype(F32)).astype(BF16)
    return hi, mid, lo


def _dot_exact_lhs(m, parts):
    return _dot(m, parts[0]) + _dot(m, parts[1]) + _dot(m, parts[2])


def _log_sigmoid(x):
    return jnp.minimum(x, 0.0) - jnp.log(1.0 + jnp.exp(-jnp.abs(x)))


def _softplus(x):
    return jnp.maximum(x, 0.0) + jnp.log(1.0 + jnp.exp(-jnp.abs(x)))


LOG2E = 1.4426950408889634
_GELU_C = 0.7978845608028654 * LOG2E


def _sigmoid(x):
    return 1.0 / (1.0 + jnp.exp2(x * (-LOG2E)))


def _gelu(x):
    return x / (1.0 + jnp.exp2(x * ((-2.0 * _GELU_C) + (-2.0 * 0.044715 * _GELU_C) * (x * x))))


def _mod_rows(mod_ref, cidx):
    m = mod_ref[pl.ds(cidx, 1), :]
    return [m[:, j * D:(j + 1) * D] for j in range(6)]


def _load_x0(is_p, within, xp_ref, xs_ref, rtab_ref, ctab_ref):
    rows_per_tile = TM // GRID_W
    r0 = within * rows_per_tile
    posr = jnp.concatenate(
        [jnp.broadcast_to(rtab_ref[pl.ds(r0 + j, 1), :], (GRID_W, D // 2)) for j in range(rows_per_tile)],
        axis=0)
    posc = jnp.concatenate([ctab_ref[...]] * rows_per_tile, axis=0)
    pos = jnp.concatenate([posr, posc], axis=1)
    return jnp.where(is_p, xp_ref[...], xs_ref[...] + pos)


def _mod_kernel(cond_ref, w_ref, b_ref, o_ref):
    c = cond_ref[...]
    sc = (c * _sigmoid(c)).astype(BF16)
    o_ref[0] = _dot(sc, w_ref[0].astype(BF16)) + b_ref[pl.ds(pl.program_id(0), 1), :]


def _modulation(cond8, mod_w, mod_b):
    wn = 2 * D
    return pl.pallas_call(
        _mod_kernel,
        out_shape=jax.ShapeDtypeStruct((DEPTH, NCOND, 6 * D), F32),
        grid=(DEPTH, 6 * D // wn),
        in_specs=[
            pl.BlockSpec((NCOND, D), lambda l, j: (0, 0)),
            pl.BlockSpec((1, D, wn), lambda l, j: (l, 0, j)),
            pl.BlockSpec((DEPTH, wn), lambda l, j: (0, j)),
        ],
        out_specs=pl.BlockSpec((1, NCOND, wn), lambda l, j: (l, 0, j)),
        compiler_params=pltpu.CompilerParams(
            dimension_semantics=("arbitrary", "arbitrary"), vmem_limit_bytes=VMEM_LIMIT),
        name="modulation",
    )(cond8, mod_w, mod_b)


def _chunk_rows(c):
    return slice(c * GLA_CHUNK, (c + 1) * GLA_CHUNK)


def _pair_lanes(p):
    return slice(p * LANES, (p + 1) * LANES)


def _gla_order(reverse):
    n = TM // GLA_CHUNK
    return list(reversed(range(n))) if reverse else list(range(n))


def _cum_mask(reverse):
    ri, ci = np.indices((TM, TM))
    order = (ci >= ri) if reverse else (ci <= ri)
    return jnp.asarray(np.logical_and(ri // GLA_CHUNK == ci // GLA_CHUNK, order), BF16)


def _gla_prep(qs, k, la, cum_m, reverse):
    n = TM // GLA_CHUNK
    b = _dot_exact_lhs(cum_m, _split3(la))
    ends = [c * GLA_CHUNK if reverse else (c + 1) * GLA_CHUNK - 1 for c in range(n)]
    btot = jnp.concatenate([jnp.broadcast_to(b[e:e + 1, :], (GLA_CHUNK, QK)) for e in ends], axis=0)
    qe = (qs * jnp.exp(b)).astype(BF16)
    ke = (k * jnp.exp(-b)).astype(BF16)
    kd = (k * jnp.exp(btot - b)).astype(BF16)
    dec = jnp.exp(btot)
    return qe, ke, kd, dec


def _gla_products(prep, v_bf, reverse):
    qe, ke, kd, _ = prep
    n = TM // GLA_CHUNK
    lane = lax.broadcasted_iota(jnp.int32, (TM, QK), 1) % LANES
    zero = jnp.zeros_like(ke)
    ke_h = (jnp.where(lane < GLA_DK, ke, zero), jnp.where(lane >= GLA_DK, ke, zero))
    cr = lax.broadcasted_iota(jnp.int32, (GLA_CHUNK, 2 * GLA_CHUNK), 0)
    cc = lax.broadcasted_iota(jnp.int32, (GLA_CHUNK, 2 * GLA_CHUNK), 1) % GLA_CHUNK
    cmask = (cc >= cr) if reverse else (cc <= cr)
    br = lax.broadcasted_iota(jnp.int32, (2 * GLA_DK, 2 * GLA_DV), 0)
    bc = lax.broadcasted_iota(jnp.int32, (2 * GLA_DK, 2 * GLA_DV), 1)
    diag = (br < GLA_DK) == (bc < GLA_DV)
    att = [[None] * n for _ in range(GLA_HEADS // 2)]
    ds = [[None] * n for _ in range(GLA_HEADS // 2)]
    for p in range(GLA_HEADS // 2):
        vp = v_bf[:, p * 2 * GLA_DV:(p + 1) * 2 * GLA_DV]
        for c in _gla_order(reverse):
            rs, ls = _chunk_rows(c), _pair_lanes(p)
            kk = jnp.concatenate([ke_h[0][rs, ls], ke_h[1][rs, ls]], axis=0)
            att[p][c] = jnp.where(cmask, _dot_nt(qe[rs, ls], kk), 0.0).astype(BF16)
            ds[p][c] = jnp.where(diag, _dot_tn(kd[rs, ls], vp[rs, :]), 0.0)
    return att, ds


def _gla_outputs(prep, prods, v_bf, p_scr, reverse):
    qe, _, _, dec = prep
    att, ds = prods
    n = TM // GLA_CHUNK
    half = TM // 2
    lane_v = lax.broadcasted_iota(jnp.int32, (TM, 2 * GLA_DV), 1)
    rows = [[None] * (GLA_HEADS // 2) for _ in range(n)]
    for p in range(GLA_HEADS // 2):
        ls = _pair_lanes(p)
        vp = v_bf[:, p * 2 * GLA_DV:(p + 1) * 2 * GLA_DV]
        vl = jnp.where(lane_v < GLA_DV, vp, jnp.zeros_like(vp))
        vr = jnp.where(lane_v >= GLA_DV, vp, jnp.zeros_like(vp))
        dec_t = (dec[0:half, ls].T, dec[half:TM, ls].T)
        s = p_scr[p]
        for c in _gla_order(reverse):
            rs = _chunk_rows(c)
            rhs = jnp.concatenate([vl[rs, :], vr[rs, :], s.astype(BF16)], axis=0)
            rows[c][p] = _dot(jnp.concatenate([att[p][c], qe[rs, ls]], axis=1), rhs)
            col = (c % 2) * GLA_CHUNK
            s = dec_t[(c * GLA_CHUNK) // half][:, col:col + 1] * s + ds[p][c]
        p_scr[p] = s
    return jnp.concatenate([jnp.concatenate(r, axis=1) for r in rows], axis=0)


def _gla_state_load(s_ref, p_scr):
    p_scr[...] = jnp.zeros(p_scr.shape, F32)
    for h in range(GLA_HEADS):
        r0, c0 = (h % 2) * GLA_DK, (h % 2) * GLA_DV
        p_scr[h // 2, r0:r0 + GLA_DK, c0:c0 + GLA_DV] = s_ref[0, 0, h]


def _gla_state_store(p_scr, s_ref):
    for h in range(GLA_HEADS):
        r0, c0 = (h % 2) * GLA_DK, (h % 2) * GLA_DV
        s_ref[0, 0, h] = p_scr[h // 2, r0:r0 + GLA_DK, c0:c0 + GLA_DV]


def _ev1_kernel(xp_ref, xs_ref, rtab_ref, ctab_ref, mod_ref, ng_ref, win_ref, gmat_ref, gb_ref,
                lng_ref, lnb_ref, ws_ref, bst_ref, cum_ref, cumr_ref, p0_ref, w1f_ref, w2f_ref,
                mid_ref, midb_ref, decb_ref, pfin_ref, w1b_ref, w2b_ref, p_scr, pg_scr, ps_scr, la_scr, *, layer):
    i = pl.program_id(0)
    mod_ref, ng_ref = mod_ref.at[layer], ng_ref.at[layer]
    is_p, within, cidx, _, _, _ = _tile_info(jnp.minimum(i, NT - 1))
    _, _, _, first, last, _ = _tile_info(jnp.maximum(i - 1, 0))
    _cast_slabs(w1f_ref, w2f_ref, w1b_ref, w2b_ref)

    @pl.when(i == 0)
    def _():
        pg_scr[...] = jnp.zeros(pg_scr.shape, F32)
        ps_scr[...] = jnp.zeros(ps_scr.shape, F32)
        la_scr[...] = jnp.zeros(la_scr.shape, F32)

    @pl.when(first)
    def _():
        _gla_state_load(p0_ref, p_scr)

    x = _load_x0(is_p, within, xp_ref, xs_ref, rtab_ref, ctab_ref)
    sh1, sc1, _, _, _, _ = _mod_rows(mod_ref, cidx)
    hm = (_rms(x, ng_ref[0:1, :]) * (1.0 + sc1) + sh1).astype(BF16)
    o0 = 2 * SGU_WIDTH
    lr = _dot(hm, win_ref[:, EV_MAIN:EV_IN]).astype(BF16)

    qs = pg_scr[:, 0:QK] * (GLA_DK ** -0.5)
    k = pg_scr[:, QK:2 * QK]
    v = pg_scr[:, 2 * QK:2 * QK + VW]
    g = pg_scr[:, 2 * QK + VW:2 * QK + 2 * VW]
    v_bf = v.astype(BF16)
    mid_ref[:, 2 * VW:3 * VW] = g * _sigmoid(g)
    midb_ref[:, 3 * QK:3 * QK + VW] = v_bf
    prep = _gla_prep(qs, k, la_scr[:, 0:QK], cum_ref[...], reverse=False)
    la_b = la_scr[:, QK:2 * QK]

    pg_scr[...] = _dot(hm, win_ref[:, o0:EV_MAIN])
    u = _gelu(ps_scr[:, 0:SGU_WIDTH])
    vg = _gelu(ps_scr[:, SGU_WIDTH:2 * SGU_WIDTH])
    mu = jnp.mean(vg, axis=-1, keepdims=True)
    vc = vg - mu
    vn = (vc * lax.rsqrt(jnp.mean(vc * vc, axis=-1, keepdims=True) + EPS) * lng_ref[...] + lnb_ref[...]).astype(BF16)
    gb = jnp.concatenate([gb_ref[0:1, :], gb_ref[1:2, :]], axis=1)
    la_scr[...] = _log_sigmoid(_dot(lr, gmat_ref[...]) + gb) * (1.0 / GLA_NORMALIZER)
    prods = _gla_products(prep, v_bf, reverse=False)
    ps_scr[...] = _dot(hm, win_ref[:, 0:o0])

    qe_b, ke_b, kd_b, dec_b = _gla_prep(qs, k, la_b, cumr_ref[...], reverse=True)
    midb_ref[:, 0:QK] = qe_b
    midb_ref[:, QK:2 * QK] = ke_b
    midb_ref[:, 2 * QK:3 * QK] = kd_b
    nchunk = TM // GLA_CHUNK
    decb_ref[...] = jnp.concatenate(
        [dec_b[c * GLA_CHUNK:c * GLA_CHUNK + 1, :] for c in range(nchunk)]
        + [jnp.zeros((SUBLANES - nchunk, QK), F32)], axis=0)

    gd = SGU_WIDTH // SGU_GROUPS
    nch = TM // SGU_CHUNK
    sv_cols = []
    for grp in range(SGU_GROUPS):
        vcat = jnp.concatenate(
            [vn[c * SGU_CHUNK:(c + 1) * SGU_CHUNK, grp * gd:(grp + 1) * gd] for c in range(nch)], axis=1)
        sg = _dot(ws_ref[grp], vcat) + bst_ref[:, grp:grp + 1]
        sv_cols.append(jnp.concatenate([sg[:, c * gd:(c + 1) * gd] for c in range(nch)], axis=0))
    mid_ref[:, 0:VW] = u * jnp.concatenate(sv_cols, axis=1)
    mid_ref[:, VW:2 * VW] = _gla_outputs(prep, prods, v_bf, p_scr, reverse=False)

    @pl.when(last)
    def _():
        _gla_state_store(p_scr, pfin_ref)


def _const_spec(shape):
    nd = len(shape)
    return pl.BlockSpec(shape, lambda i, _nd=nd: (0,) * _nd, pipeline_mode=pl.Buffered(1))


def _params():
    return pltpu.CompilerParams(dimension_semantics=("arbitrary",), vmem_limit_bytes=VMEM_LIMIT)


CAST_STEPS = 64


def _cast_specs(layer):
    r1, r2 = D // CAST_STEPS, D_FF // CAST_STEPS
    in_specs = [pl.BlockSpec((1, r1, D_FF), lambda i: (layer, jnp.minimum(i, CAST_STEPS - 1), 0)),
                pl.BlockSpec((1, r2, D), lambda i: (layer, jnp.minimum(i, CAST_STEPS - 1), 0))]
    out_specs = [pl.BlockSpec((r1, D_FF), lambda i: (jnp.minimum(i, CAST_STEPS - 1), 0)),
                 pl.BlockSpec((r2, D), lambda i: (jnp.minimum(i, CAST_STEPS - 1), 0))]
    shapes = [jax.ShapeDtypeStruct((D, D_FF), BF16), jax.ShapeDtypeStruct((D_FF, D), BF16)]
    return in_specs, out_specs, shapes


def _cast_slabs(w1f_ref, w2f_ref, w1b_ref, w2b_ref):
    w1b_ref[...] = w1f_ref[0].astype(BF16)
    w2b_ref[...] = w2f_ref[0].astype(BF16)


def _even_forward(xp, xs, rtab, ctab, mod, ng, w, w1f, w2f, layer):
    state_blk = (1, 1, GLA_HEADS, GLA_DK, GLA_DV)
    cast_in, cast_out, cast_shapes = _cast_specs(layer)

    def mix_tile(i):
        return jnp.maximum(i - 1, 0)

    return pl.pallas_call(
        functools.partial(_ev1_kernel, layer=layer),
        out_shape=(jax.ShapeDtypeStruct((NT * TM, EV_MID), F32),
                   jax.ShapeDtypeStruct((NT * TM, EV_MID_B), BF16),
                   jax.ShapeDtypeStruct((NT * SUBLANES, QK), F32),
                   jax.ShapeDtypeStruct((NSEQ, 1) + state_blk[2:], F32), *cast_shapes),
        grid=(NT + 1,),
        in_specs=[
            pl.BlockSpec((TM, D), lambda i: (jnp.minimum(i, NTP - 1), 0)),
            pl.BlockSpec((TM, D), lambda i: (jnp.maximum(jnp.minimum(i, NT - 1) - NTP, 0), 0)),
            _const_spec((GRID_W, D // 2)), _const_spec((GRID_W, D // 2)),
            _const_spec((DEPTH, NCOND, 6 * D)), _const_spec((DEPTH, 4, D)),
            _const_spec((D, EV_IN)), _const_spec((2 * GLA_RANK, 2 * QK)),
            _const_spec((2, QK)),
            _const_spec((1, SGU_WIDTH)), _const_spec((1, SGU_WIDTH)),
            _const_spec((SGU_GROUPS, SGU_CHUNK, SGU_CHUNK)), _const_spec((SGU_CHUNK, SGU_GROUPS)),
            _const_spec((TM, TM)), _const_spec((TM, TM)),
            pl.BlockSpec(state_blk, lambda i: (_seq_of(mix_tile(i)), 0, 0, 0, 0)),
            *cast_in,
        ],
        out_specs=(pl.BlockSpec((TM, EV_MID), lambda i: (mix_tile(i), 0)),
                   pl.BlockSpec((TM, EV_MID_B), lambda i: (mix_tile(i), 0)),
                   pl.BlockSpec((SUBLANES, QK), lambda i: (mix_tile(i), 0)),
                   pl.BlockSpec(state_blk, lambda i: (_seq_of(mix_tile(i)), 0, 0, 0, 0)), *cast_out),
        scratch_shapes=[pltpu.VMEM((GLA_HEADS // 2, 2 * GLA_DK, 2 * GLA_DV), F32),
                        pltpu.VMEM((TM, 2 * QK + 2 * VW), F32), pltpu.VMEM((TM, 2 * SGU_WIDTH), F32),
                        pltpu.VMEM((TM, 2 * QK), F32)],
        compiler_params=_params(),
        name="even_forward",
    )(xp, xs, rtab, ctab, mod, ng, w["win"], w["gmat"], w["gb"], w["lng"], w["lnb"],
      w["ws"], w["bst"], _cum_mask(False), _cum_mask(True), w["p0"], w1f, w2f)


def _ev2_kernel(xp_ref, xs_ref, rtab_ref, ctab_ref, mod_ref, ng_ref, mid_ref, midb_ref, decb_ref, gn_ref, wout_ref,
                p0_ref,
                x1_ref, pfin_ref, p_scr, cat_scr, *, layer):
    i = pl.program_id(0)
    mod_ref, ng_ref = mod_ref.at[layer], ng_ref.at[layer]
    live = i < NT
    _, _, _, first, last, _ = _tile_info(NT - 1 - jnp.minimum(i, NT - 1))
    is_p, within, cidx, _, _, _ = _tile_info(NT - 1 - jnp.maximum(i - 1, 0))

    @pl.when(i == 0)
    def _():
        cat_scr[...] = jnp.zeros(cat_scr.shape, BF16)

    @pl.when(jnp.logical_and(last, live))
    def _():
        _gla_state_load(p0_ref, p_scr)

    nblk = D // SLAB
    ys = [None] * nblk

    def project(j):
        ys[j] = _dot(cat_scr[...], wout_ref[:, j * SLAB:(j + 1) * SLAB])

    v_bf = midb_ref[:, 3 * QK:3 * QK + VW]
    dec = jnp.concatenate([jnp.broadcast_to(decb_ref[c:c + 1, :], (GLA_CHUNK, QK)) for c in range(TM // GLA_CHUNK)],
                          axis=0)
    prep = (midb_ref[:, 0:QK], midb_ref[:, QK:2 * QK], midb_ref[:, 2 * QK:3 * QK], dec)
    project(0)
    project(1)
    prods = _gla_products(prep, v_bf, reverse=True)
    project(2)
    o = mid_ref[:, VW:2 * VW] + _gla_outputs(prep, prods, v_bf, p_scr, reverse=True)
    project(3)
    heads = []
    for h in range(GLA_HEADS):
        oh = o[:, h * GLA_DV:(h + 1) * GLA_DV]
        heads.append(oh * lax.rsqrt(jnp.mean(oh * oh, axis=-1, keepdims=True) + EPS))
    on = jnp.concatenate(heads, axis=1) * gn_ref[...] * mid_ref[:, 2 * VW:3 * VW]
    cat_scr[...] = jnp.concatenate([mid_ref[:, 0:VW], on], axis=1).astype(BF16)

    x = _load_x0(is_p, within, xp_ref, xs_ref, rtab_ref, ctab_ref)
    _, _, g1, _, _, _ = _mod_rows(mod_ref, cidx)
    x1_ref[...] = x + g1 * _rms(jnp.concatenate(ys, axis=1), ng_ref[1:2, :])

    @pl.when(jnp.logical_and(first, live))
    def _():
        _gla_state_store(p_scr, pfin_ref)


def _even_reverse(xp, xs, rtab, ctab, mod, ng, mid, midb, decb, w, layer):
    state_blk = (1, 1, GLA_HEADS, GLA_DK, GLA_DV)
    assert D // SLAB == 4

    def mix_tile(i):
        return NT - 1 - jnp.minimum(i, NT - 1)

    def out_tile(i):
        return NT - 1 - jnp.maximum(i - 1, 0)

    return pl.pallas_call(
        functools.partial(_ev2_kernel, layer=layer),
        out_shape=(jax.ShapeDtypeStruct((NT * TM, D), F32),
                   jax.ShapeDtypeStruct((NSEQ, 1) + state_blk[2:], F32)),
        grid=(NT + 1,),
        in_specs=[
            pl.BlockSpec((TM, D), lambda i: (jnp.minimum(out_tile(i), NTP - 1), 0)),
            pl.BlockSpec((TM, D), lambda i: (jnp.maximum(out_tile(i) - NTP, 0), 0)),
            _const_spec((GRID_W, D // 2)), _const_spec((GRID_W, D // 2)),
            _const_spec((DEPTH, NCOND, 6 * D)), _const_spec((DEPTH, 4, D)),
            pl.BlockSpec((TM, EV_MID), lambda i: (mix_tile(i), 0)),
            pl.BlockSpec((TM, EV_MID_B), lambda i: (mix_tile(i), 0)),
            pl.BlockSpec((SUBLANES, QK), lambda i: (mix_tile(i), 0)),
            _const_spec((1, VW)), _const_spec((2 * VW, D)),
            pl.BlockSpec(state_blk, lambda i: (_seq_of(mix_tile(i)), 1, 0, 0, 0)),
        ],
        out_specs=(pl.BlockSpec((TM, D), lambda i: (out_tile(i), 0)),
                   pl.BlockSpec(state_blk, lambda i: (_seq_of(mix_tile(i)), 0, 0, 0, 0))),
        scratch_shapes=[pltpu.VMEM((GLA_HEADS // 2, 2 * GLA_DK, 2 * GLA_DV), F32),
                        pltpu.VMEM((TM, D), BF16)],
        compiler_params=_params(),
        name="even_reverse",
    )(xp, xs, rtab, ctab, mod, ng, mid, midb, decb, w["gn"], w["wout"], w["p0"])


FF_CHUNK = 1024
MLP_TILES = 2
TMM = MLP_TILES * TM


def _mlp_kernel(x_ref, mod_ref, ng_ref, w1_ref, b1_ref, w2_ref, b2_ref, o_ref, *, tile0, layer):
    mod_ref, ng_ref = mod_ref.at[layer], ng_ref.at[layer]
    b1_ref, b2_ref = b1_ref.at[layer:layer + 1], b2_ref.at[layer:layer + 1]
    _, _, cidx, _, _, _ = _tile_info(tile0 + pl.program_id(0) * MLP_TILES)
    _, _, _, sh2, sc2, g2 = _mod_rows(mod_ref, cidx)
    x = x_ref[...]
    hff = (_rms(x, ng_ref[2:3, :]) * (1.0 + sc2) + sh2).astype(BF16)
    acc = b2_ref[...]
    for j in range(D_FF // FF_CHUNK):
        cs = slice(j * FF_CHUNK, (j + 1) * FF_CHUNK)
        h = _dot(hff, w1_ref[:, cs]) + b1_ref[:, cs]
        h = jnp.square(jnp.maximum(h, 0.0)).astype(BF16)
        acc = acc + _dot(h, w2_ref[cs, :])
    o_ref[...] = x + g2 * _rms(acc, ng_ref[3:4, :])


def _mlp(x, mod, ng, w1, b1, w2, b2, layer, tile0, ntiles):
    assert tile0 % MLP_TILES == 0 and ntiles % MLP_TILES == 0 and TPS % MLP_TILES == 0 and NTP % MLP_TILES == 0
    b0 = tile0 // MLP_TILES
    return pl.pallas_call(
        functools.partial(_mlp_kernel, tile0=tile0, layer=layer),
        out_shape=jax.ShapeDtypeStruct((ntiles * TM, D), F32),
        grid=(ntiles // MLP_TILES,),
        in_specs=[
            pl.BlockSpec((TMM, D), lambda i: (b0 + i, 0)),
            _const_spec((DEPTH, NCOND, 6 * D)), _const_spec((DEPTH, 4, D)),
            _const_spec((D, D_FF)), _const_spec((DEPTH, D_FF)), _const_spec((D_FF, D)), _const_spec((DEPTH, D)),
        ],
        out_specs=pl.BlockSpec((TMM, D), lambda i: (i, 0)),
        compiler_params=_params(), name="mlp",
    )(x, mod, ng, w1, b1, w2, b2)


def _rg_gates(xc, wg_ref, ba_ref, bx_ref, lam_ref, a_scr, b_scr, slabs):
    xcb = xc.astype(BF16)
    sp = RG_C * _softplus(-lam_ref[...])
    for s in slabs:
        cs = slice(s * SLAB, (s + 1) * SLAB)
        pre = _dot(xcb[:, cs], wg_ref[s])
        r = _sigmoid(pre[:, 0:SLAB] + ba_ref[:, cs])
        i = _sigmoid(pre[:, SLAB:2 * SLAB] + bx_ref[:, cs])
        z = r * sp[:, cs]
        a = jnp.exp(-z)
        a_scr[:, cs] = a
        u = jnp.tanh(z) * (1.0 + a * a)
        b_scr[:, cs] = jnp.where(u > 0.0, u * lax.rsqrt(u), 0.0) * (i * xc[:, cs])


SEG = TM // SUBLANES


def _seg_perm(transpose):
    rr, cc = np.indices((TM, TM))
    if transpose:
        rr, cc = cc, rr
    return jnp.asarray(np.logical_and(cc // SEG == rr % SUBLANES, cc % SEG == rr // SUBLANES), BF16)


def _rg_scan(a_scr, b_scr, h_scr, c_scr, h0, reverse):
    def body(i, carry):
        hh, cc = carry
        grp = (SEG - 1 - i) if reverse else i
        r0 = pl.multiple_of(grp * SUBLANES, SUBLANES)
        a = a_scr[pl.ds(r0, SUBLANES), :]
        hh = a * hh + b_scr[pl.ds(r0, SUBLANES), :]
        cc = a * cc
        h_scr[pl.ds(r0, SUBLANES), :] = hh
        c_scr[pl.ds(r0, SUBLANES), :] = cc
        return hh, cc

    init = (jnp.zeros((SUBLANES, D_RNN), F32), jnp.ones((SUBLANES, D_RNN), F32))
    b, a = lax.fori_loop(0, SEG, body, init, unroll=4 if reverse else True)
    row = lax.broadcasted_iota(jnp.int32, (SUBLANES, D_RNN), 0)
    for s in (1, 2, 4):
        shift = (SUBLANES - s) if reverse else s
        valid = (row < SUBLANES - s) if reverse else (row >= s)
        a_s = pltpu.roll(a, shift, 0)
        b_s = pltpu.roll(b, shift, 0)
        b = jnp.where(valid, a * b_s + b, b)
        a = jnp.where(valid, a * a_s, a)
    after = a * h0 + b
    edge = (row == SUBLANES - 1) if reverse else (row == 0)
    enter = jnp.where(edge, h0, pltpu.roll(after, (SUBLANES - 1) if reverse else 1, 0))
    h = h_scr[...] + c_scr[...] * jnp.concatenate([enter] * SEG, axis=0)
    out = after[0:1, :] if reverse else after[SUBLANES - 1:SUBLANES, :]
    return h, out


def _od1_kernel(x_ref, xn_ref, mod_ref, ng_ref, win_ref, cw_ref, cb_ref, wg_ref, ba_ref, bx_ref, lam_ref,
                perm_ref, s0_ref, w1f_ref, w2f_ref, mid_ref, sfin_ref, w1b_ref, w2b_ref,
                proj_scr, xb_scr, a_scr, b_scr, h_scr, c_scr, hc_scr, tail_scr, *, layer):
    i = pl.program_id(0)
    mod_ref, ng_ref = mod_ref.at[layer], ng_ref.at[layer]
    ba_ref, bx_ref, lam_ref = ba_ref.at[0:1], bx_ref.at[0:1], lam_ref.at[0:1]
    _, _, cidx, _, _, _ = _tile_info(jnp.minimum(i, NT - 1))
    _, _, _, first, last, _ = _tile_info(jnp.maximum(i - 1, 0))
    _cast_slabs(w1f_ref, w2f_ref, w1b_ref, w2b_ref)

    @pl.when(i == 0)
    def _():
        proj_scr[...] = jnp.zeros(proj_scr.shape, F32)

    @pl.when(first)
    def _():
        tail_scr[...] = jnp.zeros((2 * SUBLANES, D_RNN), F32)
        hc_scr[...] = s0_ref[0, 0]

    nslab = D_RNN // SLAB
    pw = 2 * D_RNN // nslab

    def project(lhs, s):
        proj_scr[:, s * pw:(s + 1) * pw] = _dot(lhs, win_ref[:, s * pw:(s + 1) * pw])

    sh1, sc1, _, _, _, _ = _mod_rows(mod_ref, cidx)
    xe = jnp.concatenate([x_ref[...], xn_ref[...]], axis=0)
    hm = (_rms(xe, ng_ref[0:1, :]) * (1.0 + sc1) + sh1).astype(BF16)
    hmp = _dot(perm_ref[...], hm[0:TM, :]).astype(BF16)
    lhs = jnp.concatenate([hmp, hm[TM:TM + HALO, :]], axis=0)

    row = lax.broadcasted_iota(jnp.int32, (SUBLANES, D_RNN), 0)
    g30 = proj_scr[TM - 2 * SUBLANES:TM - SUBLANES, 0:D_RNN]
    g31 = proj_scr[TM - SUBLANES:TM, 0:D_RNN]
    g0 = proj_scr[0:SUBLANES, 0:D_RNN]
    nxt = jnp.where(last, 0.0, proj_scr[TM:TM + 1, 0:D_RNN])
    xb_scr[0:SUBLANES, :] = jnp.where(row == 0, pltpu.roll(tail_scr[0:SUBLANES, :], 1, 0), pltpu.roll(g30, 1, 0))
    xb_scr[SUBLANES:2 * SUBLANES, :] = jnp.where(
        row == 0, pltpu.roll(tail_scr[SUBLANES:2 * SUBLANES, :], 1, 0), pltpu.roll(g31, 1, 0))
    xb_scr[2 * SUBLANES:2 * SUBLANES + TM, :] = proj_scr[0:TM, 0:D_RNN]
    xb_scr[2 * SUBLANES + TM:3 * SUBLANES + TM, :] = jnp.where(
        row == SUBLANES - 1, nxt, pltpu.roll(g0, SUBLANES - 1, 0))
    tail_scr[0:SUBLANES, :] = g30
    tail_scr[SUBLANES:2 * SUBLANES, :] = g31

    xc = cb_ref[...]
    for j in range(4):
        xc = xc + cw_ref[j:j + 1, :] * xb_scr[j * SUBLANES:j * SUBLANES + TM, :]
    mid_ref[:, 2 * D_RNN:3 * D_RNN] = xc

    project(lhs, 0)
    mid_ref[:, D_RNN:2 * D_RNN] = _gelu(proj_scr[0:TM, D_RNN:2 * D_RNN])
    for s in range(nslab):
        if s + 1 < nslab:
            project(lhs, s + 1)
        _rg_gates(mid_ref[:, 2 * D_RNN:3 * D_RNN], wg_ref, ba_ref, bx_ref, lam_ref, a_scr, b_scr, range(s, s + 1))

    h, hc_scr[...] = _rg_scan(a_scr, b_scr, h_scr, c_scr, hc_scr[...], reverse=False)
    mid_ref[:, 0:D_RNN] = h

    @pl.when(last)
    def _():
        sfin_ref[0, 0] = hc_scr[...]


def _odd_forward(x, mod, ng, w, w1f, w2f, layer):
    nb16 = TM // HALO
    cast_in, cast_out, cast_shapes = _cast_specs(layer)

    def proj_tile(i):
        return jnp.minimum(i, NT - 1)

    def scan_tile(i):
        return jnp.maximum(i - 1, 0)

    return pl.pallas_call(
        functools.partial(_od1_kernel, layer=layer),
        out_shape=(jax.ShapeDtypeStruct((NT * TM, OD_MID), F32),
                   jax.ShapeDtypeStruct((NSEQ, 1, 1, D_RNN), F32), *cast_shapes),
        grid=(NT + 1,),
        in_specs=[
            pl.BlockSpec((TM, D), lambda i: (proj_tile(i), 0)),
            pl.BlockSpec((HALO, D), lambda i: (jnp.minimum(proj_tile(i) + 1, NT - 1) * nb16, 0)),
            _const_spec((DEPTH, NCOND, 6 * D)), _const_spec((DEPTH, 4, D)),
            _const_spec((D, 2 * D_RNN)), _const_spec((4, D_RNN)), _const_spec((1, D_RNN)),
            _const_spec((D_RNN // SLAB, SLAB, 2 * SLAB)),
            _const_spec((2, D_RNN)), _const_spec((2, D_RNN)), _const_spec((2, D_RNN)),
            _const_spec((TM, TM)),
            pl.BlockSpec((1, 1, 1, D_RNN), lambda i: (_seq_of(scan_tile(i)), 0, 0, 0)),
            *cast_in,
        ],
        out_specs=(pl.BlockSpec((TM, OD_MID), lambda i: (scan_tile(i), 0)),
                   pl.BlockSpec((1, 1, 1, D_RNN), lambda i: (_seq_of(scan_tile(i)), 0, 0, 0)), *cast_out),
        scratch_shapes=[
            pltpu.VMEM((TM + HALO, 2 * D_RNN), F32),
            pltpu.VMEM((3 * SUBLANES + TM, D_RNN), F32),
            pltpu.VMEM((TM, D_RNN), F32), pltpu.VMEM((TM, D_RNN), F32), pltpu.VMEM((TM, D_RNN), F32),
            pltpu.VMEM((TM, D_RNN), F32),
            pltpu.VMEM((1, D_RNN), F32), pltpu.VMEM((2 * SUBLANES, D_RNN), F32),
        ],
        compiler_params=_params(),
        name="odd_forward",
    )(x, x, mod, ng, w["win"], w["cw"], w["cb"], w["wg"][0], w["ba"], w["bx"], w["lam"],
      _seg_perm(False), w["s0"], w1f, w2f)


def _od2_kernel(x_ref, mod_ref, ng_ref, mid_ref, wg_ref, ba_ref, bx_ref, lam_ref, wout_ref, perm_ref, s0_ref,
                x1_ref, sfin_ref, a_scr, b_scr, h_scr, c_scr, hc_scr, zp_scr, *, layer):
    i = pl.program_id(0)
    mod_ref, ng_ref = mod_ref.at[layer], ng_ref.at[layer]
    ba_ref, bx_ref, lam_ref = ba_ref.at[1:2], bx_ref.at[1:2], lam_ref.at[1:2]
    live = i < NT
    _, _, _, first, last, _ = _tile_info(NT - 1 - jnp.minimum(i, NT - 1))
    _, _, cidx_prev, _, _, _ = _tile_info(NT - 1 - jnp.maximum(i - 1, 0))
    _, _, g1, _, _, _ = _mod_rows(mod_ref, cidx_prev)

    @pl.when(i == 0)
    def _():
        zp_scr[...] = jnp.zeros(zp_scr.shape, BF16)

    @pl.when(jnp.logical_and(last, live))
    def _():
        hc_scr[...] = s0_ref[0, 0]

    xc = mid_ref[:, 2 * D_RNN:3 * D_RNN]
    nslab = D_RNN // SLAB
    z = _dot(perm_ref[...], zp_scr[...]).astype(BF16)
    ys = []
    for s in range(nslab):
        ys.append(_dot(z, wout_ref[:, s * SLAB:(s + 1) * SLAB]))
        _rg_gates(xc, wg_ref, ba_ref, bx_ref, lam_ref, a_scr, b_scr, range(s, s + 1))
    x1_ref[...] = x_ref[...] + g1 * _rms(jnp.concatenate(ys, axis=1), ng_ref[1:2, :])
    h_b, hc_scr[...] = _rg_scan(a_scr, b_scr, h_scr, c_scr, hc_scr[...], reverse=True)
    zp_scr[...] = ((mid_ref[:, 0:D_RNN] + h_b) * mid_ref[:, D_RNN:2 * D_RNN]).astype(BF16)

    @pl.when(jnp.logical_and(first, live))
    def _():
        sfin_ref[0, 0] = hc_scr[...]


def _odd_reverse(x, mod, ng, mid, w, layer):
    def scan_tile(i):
        return NT - 1 - jnp.minimum(i, NT - 1)

    def out_tile(i):
        return NT - 1 - jnp.maximum(i - 1, 0)

    return pl.pallas_call(
        functools.partial(_od2_kernel, layer=layer),
        out_shape=(jax.ShapeDtypeStruct((NT * TM, D), F32),
                   jax.ShapeDtypeStruct((NSEQ, 1, 1, D_RNN), F32)),
        grid=(NT + 1,),
        in_specs=[
            pl.BlockSpec((TM, D), lambda i: (out_tile(i), 0)),
            _const_spec((DEPTH, NCOND, 6 * D)), _const_spec((DEPTH, 4, D)),
            pl.BlockSpec((TM, OD_MID), lambda i: (scan_tile(i), 0)),
            _const_spec((D_RNN // SLAB, SLAB, 2 * SLAB)),
            _const_spec((2, D_RNN)), _const_spec((2, D_RNN)), _const_spec((2, D_RNN)),
            _const_spec((D_RNN, D)), _const_spec((TM, TM)),
            pl.BlockSpec((1, 1, 1, D_RNN), lambda i: (_seq_of(scan_tile(i)), 1, 0, 0)),
        ],
        out_specs=(pl.BlockSpec((TM, D), lambda i: (out_tile(i), 0)),
                   pl.BlockSpec((1, 1, 1, D_RNN), lambda i: (_seq_of(scan_tile(i)), 0, 0, 0))),
        scratch_shapes=[
            pltpu.VMEM((TM, D_RNN), F32), pltpu.VMEM((TM, D_RNN), F32), pltpu.VMEM((TM, D_RNN), F32),
            pltpu.VMEM((TM, D_RNN), F32),
            pltpu.VMEM((1, D_RNN), F32), pltpu.VMEM((TM, D_RNN), BF16),
        ],
        compiler_params=_params(),
        name="odd_reverse",
    )(x, mod, ng, mid, w["wg"][1], w["ba"], w["bx"], w["lam"], w["wout"], _seg_perm(True), w["s0"])


def _pos_tables():
    n = D // 4
    omega = 1.0 / (10000.0 ** (jnp.arange(n, dtype=F32) / n))
    idx = jnp.arange(GRID_W, dtype=F32)[:, None] * omega
    tab = jnp.concatenate([jnp.sin(idx), jnp.cos(idx)], axis=-1)
    return tab, tab


def _block_diag_slabs(w):
    per = SLAB // RG_BS
    rows = jnp.tile(w.reshape(D_RNN // SLAB, SLAB, RG_BS), (1, 1, per))
    ri, ci = np.indices((SLAB, SLAB))
    return jnp.where(jnp.asarray(ri // RG_BS == ci // RG_BS), rows, 0.0)


def kernel(x_prompt, x_sample, c, state_gla, state_rglru, c_ctx, mod_w, mod_b, norm_g, mlp_w1, mlp_b1, mlp_w2,
           mlp_b2, ev_w_in, ev_w_out, sgu_ln_g, sgu_ln_b, sgu_ws, sgu_bs, gla_gate_w2, gla_gate_b, gla_norm_g,
           rg_w_in, rg_conv_w, rg_conv_b, rg_wa, rg_ba, rg_wx, rg_bx, rg_L, rg_w_out):
    assert x_prompt.shape == (BATCH, SEQ, D) and x_sample.shape == (DEC_BATCH, DEC_SEQ, D)
    assert SEQ == TM and DEC_SEQ % TM == 0 and DEPTH == 2
    xp = x_prompt.reshape(NTP * TM, D)
    xs = x_sample.reshape(NTS * TM, D)
    cond8 = jnp.concatenate([c_ctx[None, :], c, jnp.zeros((NCOND - 1 - DEC_BATCH, D), F32)], axis=0)
    mods = _modulation(cond8, mod_w, mod_b)
    rtab, ctab = _pos_tables()

    gz = jnp.zeros((GLA_RANK, QK), F32)
    gmat = jnp.concatenate([jnp.concatenate([gla_gate_w2[0, 0], gz], axis=1),
                            jnp.concatenate([gz, gla_gate_w2[0, 1]], axis=1)], axis=0)
    s0_gla = jnp.concatenate([jnp.zeros((BATCH,) + state_gla.shape[2:], F32), state_gla[:, 0]], axis=0)
    ev = {
        "win": ev_w_in[0].astype(BF16),
        "gmat": gmat.astype(BF16),
        "gb": gla_gate_b[0],
        "lng": sgu_ln_g[0].reshape(1, SGU_WIDTH),
        "lnb": sgu_ln_b[0].reshape(1, SGU_WIDTH),
        "ws": sgu_ws[0].astype(BF16),
        "bst": sgu_bs[0].T,
        "gn": gla_norm_g[0].reshape(1, VW),
        "wout": ev_w_out[0].astype(BF16),
        "p0": s0_gla,
    }
    mid, midb, decb, pf, w1b, w2b = _even_forward(xp, xs, rtab, ctab, mods, norm_g, ev, mlp_w1, mlp_w2, 0)
    x1, pb = _even_reverse(xp, xs, rtab, ctab, mods, norm_g, mid, midb, decb, ev, 0)
    x2 = _mlp(x1, mods, norm_g, w1b, mlp_b1, w2b, mlp_b2, 0, 0, NT)
    new_gla = jnp.stack([pf[:BATCH, 0], pb[:BATCH, 0]], axis=1)[:, None]

    s0_rg = jnp.concatenate([jnp.zeros((BATCH, 2, D_RNN), F32), state_rglru[:, 0]], axis=0)
    od = {
        "win": rg_w_in[0].astype(BF16),
        "cw": rg_conv_w[0],
        "cb": rg_conv_b[0].reshape(1, D_RNN),
        "wg": jnp.stack([
            jnp.concatenate([_block_diag_slabs(rg_wa[0, d]), _block_diag_slabs(rg_wx[0, d])], axis=-1)
            for d in range(2)], axis=0).astype(BF16),
        "ba": rg_ba[0], "bx": rg_bx[0], "lam": rg_L[0],
        "wout": rg_w_out[0].astype(BF16),
        "s0": s0_rg.reshape(NSEQ, 2, 1, D_RNN),
    }
    mid1, sf, w1b, w2b = _odd_forward(x2, mods, norm_g, od, mlp_w1, mlp_w2, 1)
    x3, sb = _odd_reverse(x2, mods, norm_g, mid1, od, 1)
    mlp1 = (mods, norm_g, w1b, mlp_b1, w2b, mlp_b2, 1)
    y_prompt = _mlp(x3, *mlp1, 0, NTP)
    y_sample = _mlp(x3, *mlp1, NTP, NTS)
    new_rg = jnp.stack([sf[:BATCH, 0, 0], sb[:BATCH, 0, 0]], axis=1)[:, None]
    return (y_prompt.reshape(BATCH, SEQ, D), y_sample.reshape(DEC_BATCH, DEC_SEQ, D), new_gla, new_rg)
```

```python
import functools

import jax
import jax.numpy as jnp
import numpy as np
from jax import lax
from jax.experimental import pallas as pl
from jax.experimental.pallas import tpu as pltpu

D = 1024
BATCH = 16
SEQ = 256
DEPTH = 2
DEC_BATCH = 4
DEC_SEQ = 4096
GRID_W = 64
D_FF = 4 * D
EPS = 1e-6
SGU_CHUNK = 128
SGU_GROUPS = 4
SGU_WIDTH = D // 2
GLA_HEADS = 4
GLA_DV = 128
GLA_DK = 64
GLA_RANK = 16
GLA_NORMALIZER = 16.0
GLA_CHUNK = 64
QK = GLA_HEADS * GLA_DK
VW = GLA_HEADS * GLA_DV
EV_MAIN = 2 * SGU_WIDTH + 2 * QK + 2 * VW
EV_IN = EV_MAIN + 2 * GLA_RANK
D_RNN = D
RG_BLOCKS = 16
RG_BS = D_RNN // RG_BLOCKS
RG_C = 8.0
LANES = 128
SUBLANES = 8
SLAB = 256

TM = 256
NTP = BATCH * SEQ // TM
TPS = DEC_SEQ // TM
NTS = DEC_BATCH * TPS
NT = NTP + NTS
NSEQ = BATCH + DEC_BATCH
NCOND = 8
HALO = 16
EV_MID = 3 * VW
EV_MID_B = 3 * QK + VW
OD_MID = 3 * D_RNN
VMEM_LIMIT = 56 * 1024 * 1024

F32 = jnp.float32
BF16 = jnp.bfloat16


def _tile_info(t):
    is_p = t < NTP
    ts = jnp.maximum(t - NTP, 0)
    sq = ts // TPS
    within = ts % TPS
    cidx = jnp.where(is_p, 0, 1 + sq)
    first = jnp.logical_or(is_p, within == 0)
    last = jnp.logical_or(is_p, within == TPS - 1)
    seq = jnp.where(is_p, t, BATCH + sq)
    return is_p, within, cidx, first, last, seq


def _seq_of(t):
    return jnp.where(t < NTP, t, BATCH + jnp.maximum(t - NTP, 0) // TPS)


def _rms(x, g):
    return x * lax.rsqrt(jnp.mean(x * x, axis=-1, keepdims=True) + EPS) * g


def _dot(a, b):
    return jnp.dot(a, b, preferred_element_type=F32)


def _dot_nt(a, b):
    return lax.dot_general(a, b, (((1,), (1,)), ((), ())), preferred_element_type=F32)


def _dot_tn(a, b):
    return lax.dot_general(a, b, (((0,), (0,)), ((), ())), preferred_element_type=F32)


def _split3(x):
    hi = x.astype(BF16)
    r1 = x - hi.astype(F32)
    mid = r1.astype(BF16)
    lo = (r1 - mid.astype(F32)).astype(BF16)
    return hi, mid, lo


def _dot_exact_lhs(m, parts):
    return _dot(m, parts[0]) + _dot(m, parts[1]) + _dot(m, parts[2])


def _log_sigmoid(x):
    return jnp.minimum(x, 0.0) - jnp.log(1.0 + jnp.exp(-jnp.abs(x)))


def _softplus(x):
    return jnp.maximum(x, 0.0) + jnp.log(1.0 + jnp.exp(-jnp.abs(x)))


LOG2E = 1.4426950408889634
_GELU_C = 0.7978845608028654 * LOG2E


def _sigmoid(x):
    return 1.0 / (1.0 + jnp.exp2(x * (-LOG2E)))


def _gelu(x):
    return x / (1.0 + jnp.exp2(x * ((-2.0 * _GELU_C) + (-2.0 * 0.044715 * _GELU_C) * (x * x))))


def _mod_rows(mod_ref, cidx):
    m = mod_ref[pl.ds(cidx, 1), :]
    return [m[:, j * D:(j + 1) * D] for j in range(6)]


def _load_x0(is_p, within, xp_ref, xs_ref, rtab_ref, ctab_ref):
    rows_per_tile = TM // GRID_W
    r0 = within * rows_per_tile
    posr = jnp.concatenate(
        [jnp.broadcast_to(rtab_ref[pl.ds(r0 + j, 1), :], (GRID_W, D // 2)) for j in range(rows_per_tile)],
        axis=0)
    posc = jnp.concatenate([ctab_ref[...]] * rows_per_tile, axis=0)
    pos = jnp.concatenate([posr, posc], axis=1)
    return jnp.where(is_p, xp_ref[...], xs_ref[...] + pos)


def _mod_kernel(cond_ref, w_ref, b_ref, o_ref):
    c = cond_ref[...]
    sc = (c * _sigmoid(c)).astype(BF16)
    o_ref[0] = _dot(sc, w_ref[0].astype(BF16)) + b_ref[pl.ds(pl.program_id(0), 1), :]


def _modulation(cond8, mod_w, mod_b):
    wn = 2 * D
    return pl.pallas_call(
        _mod_kernel,
        out_shape=jax.ShapeDtypeStruct((DEPTH, NCOND, 6 * D), F32),
        grid=(DEPTH, 6 * D // wn),
        in_specs=[
            pl.BlockSpec((NCOND, D), lambda l, j: (0, 0)),
            pl.BlockSpec((1, D, wn), lambda l, j: (l, 0, j)),
            pl.BlockSpec((DEPTH, wn), lambda l, j: (0, j)),
        ],
        out_specs=pl.BlockSpec((1, NCOND, wn), lambda l, j: (l, 0, j)),
        compiler_params=pltpu.CompilerParams(
            dimension_semantics=("arbitrary", "arbitrary"), vmem_limit_bytes=VMEM_LIMIT),
        name="modulation",
    )(cond8, mod_w, mod_b)


def _chunk_rows(c):
    return slice(c * GLA_CHUNK, (c + 1) * GLA_CHUNK)


def _pair_lanes(p):
    return slice(p * LANES, (p + 1) * LANES)


def _gla_order(reverse):
    n = TM // GLA_CHUNK
    return list(reversed(range(n))) if reverse else list(range(n))


def _cum_mask(reverse):
    ri, ci = np.indices((TM, TM))
    order = (ci >= ri) if reverse else (ci <= ri)
    return jnp.asarray(np.logical_and(ri // GLA_CHUNK == ci // GLA_CHUNK, order), BF16)


def _gla_prep(qs, k, la, cum_m, reverse):
    n = TM // GLA_CHUNK
    b = _dot_exact_lhs(cum_m, _split3(la))
    ends = [c * GLA_CHUNK if reverse else (c + 1) * GLA_CHUNK - 1 for c in range(n)]
    btot = jnp.concatenate([jnp.broadcast_to(b[e:e + 1, :], (GLA_CHUNK, QK)) for e in ends], axis=0)
    qe = (qs * jnp.exp(b)).astype(BF16)
    ke = (k * jnp.exp(-b)).astype(BF16)
    kd = (k * jnp.exp(btot - b)).astype(BF16)
    dec = jnp.exp(btot)
    return qe, ke, kd, dec


def _gla_products(prep, v_bf, reverse):
    qe, ke, kd, _ = prep
    n = TM // GLA_CHUNK
    lane = lax.broadcasted_iota(jnp.int32, (TM, QK), 1) % LANES
    zero = jnp.zeros_like(ke)
    ke_h = (jnp.where(lane < GLA_DK, ke, zero), jnp.where(lane >= GLA_DK, ke, zero))
    cr = lax.broadcasted_iota(jnp.int32, (GLA_CHUNK, 2 * GLA_CHUNK), 0)
    cc = lax.broadcasted_iota(jnp.int32, (GLA_CHUNK, 2 * GLA_CHUNK), 1) % GLA_CHUNK
    cmask = (cc >= cr) if reverse else (cc <= cr)
    br = lax.broadcasted_iota(jnp.int32, (2 * GLA_DK, 2 * GLA_DV), 0)
    bc = lax.broadcasted_iota(jnp.int32, (2 * GLA_DK, 2 * GLA_DV), 1)
    diag = (br < GLA_DK) == (bc < GLA_DV)
    att = [[None] * n for _ in range(GLA_HEADS // 2)]
    ds = [[None] * n for _ in range(GLA_HEADS // 2)]
    for p in range(GLA_HEADS // 2):
        vp = v_bf[:, p * 2 * GLA_DV:(p + 1) * 2 * GLA_DV]
        for c in _gla_order(reverse):
            rs, ls = _chunk_rows(c), _pair_lanes(p)
            kk = jnp.concatenate([ke_h[0][rs, ls], ke_h[1][rs, ls]], axis=0)
            att[p][c] = jnp.where(cmask, _dot_nt(qe[rs, ls], kk), 0.0).astype(BF16)
            ds[p][c] = jnp.where(diag, _dot_tn(kd[rs, ls], vp[rs, :]), 0.0)
    return att, ds


def _gla_outputs(prep, prods, v_bf, p_scr, reverse):
    qe, _, _, dec = prep
    att, ds = prods
    n = TM // GLA_CHUNK
    half = TM // 2
    lane_v = lax.broadcasted_iota(jnp.int32, (TM, 2 * GLA_DV), 1)
    rows = [[None] * (GLA_HEADS // 2) for _ in range(n)]
    for p in range(GLA_HEADS // 2):
        ls = _pair_lanes(p)
        vp = v_bf[:, p * 2 * GLA_DV:(p + 1) * 2 * GLA_DV]
        vl = jnp.where(lane_v < GLA_DV, vp, jnp.zeros_like(vp))
        vr = jnp.where(lane_v >= GLA_DV, vp, jnp.zeros_like(vp))
        dec_t = (dec[0:half, ls].T, dec[half:TM, ls].T)
        s = p_scr[p]
        for c in _gla_order(reverse):
            rs = _chunk_rows(c)
            rhs = jnp.concatenate([vl[rs, :], vr[rs, :], s.astype(BF16)], axis=0)
            rows[c][p] = _dot(jnp.concatenate([att[p][c], qe[rs, ls]], axis=1), rhs)
            col = (c % 2) * GLA_CHUNK
            s = dec_t[(c * GLA_CHUNK) // half][:, col:col + 1] * s + ds[p][c]
        p_scr[p] = s
    return jnp.concatenate([jnp.concatenate(r, axis=1) for r in rows], axis=0)


def _gla_state_load(s_ref, p_scr):
    p_scr[...] = jnp.zeros(p_scr.shape, F32)
    for h in range(GLA_HEADS):
        r0, c0 = (h % 2) * GLA_DK, (h % 2) * GLA_DV
        p_scr[h // 2, r0:r0 + GLA_DK, c0:c0 + GLA_DV] = s_ref[0, 0, h]


def _gla_state_store(p_scr, s_ref):
    for h in range(GLA_HEADS):
        r0, c0 = (h % 2) * GLA_DK, (h % 2) * GLA_DV
        s_ref[0, 0, h] = p_scr[h // 2, r0:r0 + GLA_DK, c0:c0 + GLA_DV]


def _ev1_kernel(xp_ref, xs_ref, rtab_ref, ctab_ref, mod_ref, ng_ref, win_ref, gmat_ref, gb_ref,
                lng_ref, lnb_ref, ws_ref, bst_ref, cum_ref, cumr_ref, p0_ref, w1f_ref, w2f_ref,
                mid_ref, midb_ref, decb_ref, pfin_ref, w1b_ref, w2b_ref, p_scr, pg_scr, ps_scr, la_scr, *, layer):
    i = pl.program_id(0)
    mod_ref, ng_ref = mod_ref.at[layer], ng_ref.at[layer]
    is_p, within, cidx, _, _, _ = _tile_info(jnp.minimum(i, NT - 1))
    _, _, _, first, last, _ = _tile_info(jnp.maximum(i - 1, 0))
    _cast_slabs(w1f_ref, w2f_ref, w1b_ref, w2b_ref)

    @pl.when(i == 0)
    def _():
        pg_scr[...] = jnp.zeros(pg_scr.shape, F32)
        ps_scr[...] = jnp.zeros(ps_scr.shape, F32)
        la_scr[...] = jnp.zeros(la_scr.shape, F32)

    @pl.when(first)
    def _():
        _gla_state_load(p0_ref, p_scr)

    x = _load_x0(is_p, within, xp_ref, xs_ref, rtab_ref, ctab_ref)
    sh1, sc1, _, _, _, _ = _mod_rows(mod_ref, cidx)
    hm = (_rms(x, ng_ref[0:1, :]) * (1.0 + sc1) + sh1).astype(BF16)
    o0 = 2 * SGU_WIDTH
    lr = _dot(hm, win_ref[:, EV_MAIN:EV_IN]).astype(BF16)

    qs = pg_scr[:, 0:QK] * (GLA_DK ** -0.5)
    k = pg_scr[:, QK:2 * QK]
    v = pg_scr[:, 2 * QK:2 * QK + VW]
    g = pg_scr[:, 2 * QK + VW:2 * QK + 2 * VW]
    v_bf = v.astype(BF16)
    mid_ref[:, 2 * VW:3 * VW] = g * _sigmoid(g)
    midb_ref[:, 3 * QK:3 * QK + VW] = v_bf
    prep = _gla_prep(qs, k, la_scr[:, 0:QK], cum_ref[...], reverse=False)
    la_b = la_scr[:, QK:2 * QK]

    pg_scr[...] = _dot(hm, win_ref[:, o0:EV_MAIN])
    u = _gelu(ps_scr[:, 0:SGU_WIDTH])
    vg = _gelu(ps_scr[:, SGU_WIDTH:2 * SGU_WIDTH])
    mu = jnp.mean(vg, axis=-1, keepdims=True)
    vc = vg - mu
    vn = (vc * lax.rsqrt(jnp.mean(vc * vc, axis=-1, keepdims=True) + EPS) * lng_ref[...] + lnb_ref[...]).astype(BF16)
    gb = jnp.concatenate([gb_ref[0:1, :], gb_ref[1:2, :]], axis=1)
    la_scr[...] = _log_sigmoid(_dot(lr, gmat_ref[...]) + gb) * (1.0 / GLA_NORMALIZER)
    prods = _gla_products(prep, v_bf, reverse=False)
    ps_scr[...] = _dot(hm, win_ref[:, 0:o0])

    gd = SGU_WIDTH // SGU_GROUPS
    nch = TM // SGU_CHUNK
    sv_cols = []
    for grp in range(SGU_GROUPS):
        vcat = jnp.concatenate(
            [vn[c * SGU_CHUNK:(c + 1) * SGU_CHUNK, grp * gd:(grp + 1) * gd] for c in range(nch)], axis=1)
        sg = _dot(ws_ref[grp], vcat) + bst_ref[:, grp:grp + 1]
        sv_cols.append(jnp.concatenate([sg[:, c * gd:(c + 1) * gd] for c in range(nch)], axis=0))
    mid_ref[:, 0:VW] = u * jnp.concatenate(sv_cols, axis=1)
    mid_ref[:, VW:2 * VW] = _gla_outputs(prep, prods, v_bf, p_scr, reverse=False)

    qe_b, ke_b, kd_b, dec_b = _gla_prep(qs, k, la_b, cumr_ref[...], reverse=True)
    midb_ref[:, 0:QK] = qe_b
    midb_ref[:, QK:2 * QK] = ke_b
    midb_ref[:, 2 * QK:3 * QK] = kd_b
    nchunk = TM // GLA_CHUNK
    decb_ref[...] = jnp.concatenate(
        [dec_b[c * GLA_CHUNK:c * GLA_CHUNK + 1, :] for c in range(nchunk)]
        + [jnp.zeros((SUBLANES - nchunk, QK), F32)], axis=0)

    @pl.when(last)
    def _():
        _gla_state_store(p_scr, pfin_ref)


def _const_spec(shape):
    nd = len(shape)
    return pl.BlockSpec(shape, lambda i, _nd=nd: (0,) * _nd, pipeline_mode=pl.Buffered(1))


def _params():
    return pltpu.CompilerParams(dimension_semantics=("arbitrary",), vmem_limit_bytes=VMEM_LIMIT)


CAST_STEPS = 64


def _cast_specs(layer):
    r1, r2 = D // CAST_STEPS, D_FF // CAST_STEPS
    in_specs = [pl.BlockSpec((1, r1, D_FF), lambda i: (layer, jnp.minimum(i, CAST_STEPS - 1), 0)),
                pl.BlockSpec((1, r2, D), lambda i: (layer, jnp.minimum(i, CAST_STEPS - 1), 0))]
    out_specs = [pl.BlockSpec((r1, D_FF), lambda i: (jnp.minimum(i, CAST_STEPS - 1), 0)),
                 pl.BlockSpec((r2, D), lambda i: (jnp.minimum(i, CAST_STEPS - 1), 0))]
    shapes = [jax.ShapeDtypeStruct((D, D_FF), BF16), jax.ShapeDtypeStruct((D_FF, D), BF16)]
    return in_specs, out_specs, shapes


def _cast_slabs(w1f_ref, w2f_ref, w1b_ref, w2b_ref):
    w1b_ref[...] = w1f_ref[0].astype(BF16)
    w2b_ref[...] = w2f_ref[0].astype(BF16)


def _even_forward(xp, xs, rtab, ctab, mod, ng, w, w1f, w2f, layer):
    state_blk = (1, 1, GLA_HEADS, GLA_DK, GLA_DV)
    cast_in, cast_out, cast_shapes = _cast_specs(layer)

    def mix_tile(i):
        return jnp.maximum(i - 1, 0)

    return pl.pallas_call(
        functools.partial(_ev1_kernel, layer=layer),
        out_shape=(jax.ShapeDtypeStruct((NT * TM, EV_MID), F32),
                   jax.ShapeDtypeStruct((NT * TM, EV_MID_B), BF16),
                   jax.ShapeDtypeStruct((NT * SUBLANES, QK), F32),
                   jax.ShapeDtypeStruct((NSEQ, 1) + state_blk[2:], F32), *cast_shapes),
        grid=(NT + 1,),
        in_specs=[
            pl.BlockSpec((TM, D), lambda i: (jnp.minimum(i, NTP - 1), 0)),
            pl.BlockSpec((TM, D), lambda i: (jnp.maximum(jnp.minimum(i, NT - 1) - NTP, 0), 0)),
            _const_spec((GRID_W, D // 2)), _const_spec((GRID_W, D // 2)),
            _const_spec((DEPTH, NCOND, 6 * D)), _const_spec((DEPTH, 4, D)),
            _const_spec((D, EV_IN)), _const_spec((2 * GLA_RANK, 2 * QK)),
            _const_spec((2, QK)),
            _const_spec((1, SGU_WIDTH)), _const_spec((1, SGU_WIDTH)),
            _const_spec((SGU_GROUPS, SGU_CHUNK, SGU_CHUNK)), _const_spec((SGU_CHUNK, SGU_GROUPS)),
            _const_spec((TM, TM)), _const_spec((TM, TM)),
            pl.BlockSpec(state_blk, lambda i: (_seq_of(mix_tile(i)), 0, 0, 0, 0)),
            *cast_in,
        ],
        out_specs=(pl.BlockSpec((TM, EV_MID), lambda i: (mix_tile(i), 0)),
                   pl.BlockSpec((TM, EV_MID_B), lambda i: (mix_tile(i), 0)),
                   pl.BlockSpec((SUBLANES, QK), lambda i: (mix_tile(i), 0)),
                   pl.BlockSpec(state_blk, lambda i: (_seq_of(mix_tile(i)), 0, 0, 0, 0)), *cast_out),
        scratch_shapes=[pltpu.VMEM((GLA_HEADS // 2, 2 * GLA_DK, 2 * GLA_DV), F32),
                        pltpu.VMEM((TM, 2 * QK + 2 * VW), F32), pltpu.VMEM((TM, 2 * SGU_WIDTH), F32),
                        pltpu.VMEM((TM, 2 * QK), F32)],
        compiler_params=_params(),
        name="even_forward",
    )(xp, xs, rtab, ctab, mod, ng, w["win"], w["gmat"], w["gb"], w["lng"], w["lnb"],
      w["ws"], w["bst"], _cum_mask(False), _cum_mask(True), w["p0"], w1f, w2f)


def _ev2_kernel(xp_ref, xs_ref, rtab_ref, ctab_ref, mod_ref, ng_ref, mid_ref, midb_ref, decb_ref, gn_ref, wout_ref,
                p0_ref,
                x1_ref, pfin_ref, p_scr, cat_scr, *, layer):
    i = pl.program_id(0)
    mod_ref, ng_ref = mod_ref.at[layer], ng_ref.at[layer]
    live = i < NT
    _, _, _, first, last, _ = _tile_info(NT - 1 - jnp.minimum(i, NT - 1))
    is_p, within, cidx, _, _, _ = _tile_info(NT - 1 - jnp.maximum(i - 1, 0))

    @pl.when(i == 0)
    def _():
        cat_scr[...] = jnp.zeros(cat_scr.shape, BF16)

    @pl.when(jnp.logical_and(last, live))
    def _():
        _gla_state_load(p0_ref, p_scr)

    nblk = D // SLAB
    ys = [None] * nblk

    def project(j):
        ys[j] = _dot(cat_scr[...], wout_ref[:, j * SLAB:(j + 1) * SLAB])

    v_bf = midb_ref[:, 3 * QK:3 * QK + VW]
    dec = jnp.concatenate([jnp.broadcast_to(decb_ref[c:c + 1, :], (GLA_CHUNK, QK)) for c in range(TM // GLA_CHUNK)],
                          axis=0)
    prep = (midb_ref[:, 0:QK], midb_ref[:, QK:2 * QK], midb_ref[:, 2 * QK:3 * QK], dec)
    project(0)
    project(1)
    prods = _gla_products(prep, v_bf, reverse=True)
    project(2)
    o = mid_ref[:, VW:2 * VW] + _gla_outputs(prep, prods, v_bf, p_scr, reverse=True)
    project(3)
    heads = []
    for h in range(GLA_HEADS):
        oh = o[:, h * GLA_DV:(h + 1) * GLA_DV]
        heads.append(oh * lax.rsqrt(jnp.mean(oh * oh, axis=-1, keepdims=True) + EPS))
    on = jnp.concatenate(heads, axis=1) * gn_ref[...] * mid_ref[:, 2 * VW:3 * VW]
    cat_scr[...] = jnp.concatenate([mid_ref[:, 0:VW], on], axis=1).astype(BF16)

    x = _load_x0(is_p, within, xp_ref, xs_ref, rtab_ref, ctab_ref)
    _, _, g1, _, _, _ = _mod_rows(mod_ref, cidx)
    x1_ref[...] = x + g1 * _rms(jnp.concatenate(ys, axis=1), ng_ref[1:2, :])

    @pl.when(jnp.logical_and(first, live))
    def _():
        _gla_state_store(p_scr, pfin_ref)


def _even_reverse(xp, xs, rtab, ctab, mod, ng, mid, midb, decb, w, layer):
    state_blk = (1, 1, GLA_HEADS, GLA_DK, GLA_DV)
    assert D // SLAB == 4

    def mix_tile(i):
        return NT - 1 - jnp.minimum(i, NT - 1)

    def out_tile(i):
        return NT - 1 - jnp.maximum(i - 1, 0)

    return pl.pallas_call(
        functools.partial(_ev2_kernel, layer=layer),
        out_shape=(jax.ShapeDtypeStruct((NT * TM, D), F32),
                   jax.ShapeDtypeStruct((NSEQ, 1) + state_blk[2:], F32)),
        grid=(NT + 1,),
        in_specs=[
            pl.BlockSpec((TM, D), lambda i: (jnp.minimum(out_tile(i), NTP - 1), 0)),
            pl.BlockSpec((TM, D), lambda i: (jnp.maximum(out_tile(i) - NTP, 0), 0)),
            _const_spec((GRID_W, D // 2)), _const_spec((GRID_W, D // 2)),
            _const_spec((DEPTH, NCOND, 6 * D)), _const_spec((DEPTH, 4, D)),
            pl.BlockSpec((TM, EV_MID), lambda i: (mix_tile(i), 0)),
            pl.BlockSpec((TM, EV_MID_B), lambda i: (mix_tile(i), 0)),
            pl.BlockSpec((SUBLANES, QK), lambda i: (mix_tile(i), 0)),
            _const_spec((1, VW)), _const_spec((2 * VW, D)),
            pl.BlockSpec(state_blk, lambda i: (_seq_of(mix_tile(i)), 1, 0, 0, 0)),
        ],
        out_specs=(pl.BlockSpec((TM, D), lambda i: (out_tile(i), 0)),
                   pl.BlockSpec(state_blk, lambda i: (_seq_of(mix_tile(i)), 0, 0, 0, 0))),
        scratch_shapes=[pltpu.VMEM((GLA_HEADS // 2, 2 * GLA_DK, 2 * GLA_DV), F32),
                        pltpu.VMEM((TM, D), BF16)],
        compiler_params=_params(),
        name="even_reverse",
    )(xp, xs, rtab, ctab, mod, ng, mid, midb, decb, w["gn"], w["wout"], w["p0"])


FF_CHUNK = 1024
MLP_TILES = 2
TMM = MLP_TILES * TM


def _mlp_kernel(x_ref, mod_ref, ng_ref, w1_ref, b1_ref, w2_ref, b2_ref, o_ref, *, tile0, layer):
    mod_ref, ng_ref = mod_ref.at[layer], ng_ref.at[layer]
    b1_ref, b2_ref = b1_ref.at[layer:layer + 1], b2_ref.at[layer:layer + 1]
    _, _, cidx, _, _, _ = _tile_info(tile0 + pl.program_id(0) * MLP_TILES)
    _, _, _, sh2, sc2, g2 = _mod_rows(mod_ref, cidx)
    x = x_ref[...]
    hff = (_rms(x, ng_ref[2:3, :]) * (1.0 + sc2) + sh2).astype(BF16)
    acc = b2_ref[...]
    for j in range(D_FF // FF_CHUNK):
        cs = slice(j * FF_CHUNK, (j + 1) * FF_CHUNK)
        h = _dot(hff, w1_ref[:, cs]) + b1_ref[:, cs]
        h = jnp.square(jnp.maximum(h, 0.0)).astype(BF16)
        acc = acc + _dot(h, w2_ref[cs, :])
    o_ref[...] = x + g2 * _rms(acc, ng_ref[3:4, :])


def _mlp(x, mod, ng, w1, b1, w2, b2, layer, tile0, ntiles):
    assert tile0 % MLP_TILES == 0 and ntiles % MLP_TILES == 0 and TPS % MLP_TILES == 0 and NTP % MLP_TILES == 0
    b0 = tile0 // MLP_TILES
    return pl.pallas_call(
        functools.partial(_mlp_kernel, tile0=tile0, layer=layer),
        out_shape=jax.ShapeDtypeStruct((ntiles * TM, D), F32),
        grid=(ntiles // MLP_TILES,),
        in_specs=[
            pl.BlockSpec((TMM, D), lambda i: (b0 + i, 0)),
            _const_spec((DEPTH, NCOND, 6 * D)), _const_spec((DEPTH, 4, D)),
            _const_spec((D, D_FF)), _const_spec((DEPTH, D_FF)), _const_spec((D_FF, D)), _const_spec((DEPTH, D)),
        ],
        out_specs=pl.BlockSpec((TMM, D), lambda i: (i, 0)),
        compiler_params=_params(), name="mlp",
    )(x, mod, ng, w1, b1, w2, b2)


def _rg_gates(xc, wg_ref, ba_ref, bx_ref, lam_ref, a_scr, b_scr, slabs):
    xcb = xc.astype(BF16)
    sp = RG_C * _softplus(-lam_ref[...])
    for s in slabs:
        cs = slice(s * SLAB, (s + 1) * SLAB)
        pre = _dot(xcb[:, cs], wg_ref[s])
        r = _sigmoid(pre[:, 0:SLAB] + ba_ref[:, cs])
        i = _sigmoid(pre[:, SLAB:2 * SLAB] + bx_ref[:, cs])
        z = r * sp[:, cs]
        a = jnp.exp(-z)
        a_scr[:, cs] = a
        u = jnp.tanh(z) * (1.0 + a * a)
        b_scr[:, cs] = jnp.where(u > 0.0, u * lax.rsqrt(u), 0.0) * (i * xc[:, cs])


SEG = TM // SUBLANES


def _seg_perm(transpose):
    rr, cc = np.indices((TM, TM))
    if transpose:
        rr, cc = cc, rr
    return jnp.asarray(np.logical_and(cc // SEG == rr % SUBLANES, cc % SEG == rr // SUBLANES), BF16)


def _rg_scan(a_scr, b_scr, h_scr, c_scr, h0, reverse):
    def body(i, carry):
        hh, cc = carry
        grp = (SEG - 1 - i) if reverse else i
        r0 = pl.multiple_of(grp * SUBLANES, SUBLANES)
        a = a_scr[pl.ds(r0, SUBLANES), :]
        hh = a * hh + b_scr[pl.ds(r0, SUBLANES), :]
        cc = a * cc
        h_scr[pl.ds(r0, SUBLANES), :] = hh
        c_scr[pl.ds(r0, SUBLANES), :] = cc
        return hh, cc

    init = (jnp.zeros((SUBLANES, D_RNN), F32), jnp.ones((SUBLANES, D_RNN), F32))
    b, a = lax.fori_loop(0, SEG, body, init, unroll=4 if reverse else True)
    row = lax.broadcasted_iota(jnp.int32, (SUBLANES, D_RNN), 0)
    for s in (1, 2, 4):
        shift = (SUBLANES - s) if reverse else s
        valid = (row < SUBLANES - s) if reverse else (row >= s)
        a_s = pltpu.roll(a, shift, 0)
        b_s = pltpu.roll(b, shift, 0)
        b = jnp.where(valid, a * b_s + b, b)
        a = jnp.where(valid, a * a_s, a)
    after = a * h0 + b
    edge = (row == SUBLANES - 1) if reverse else (row == 0)
    enter = jnp.where(edge, h0, pltpu.roll(after, (SUBLANES - 1) if reverse else 1, 0))
    h = h_scr[...] + c_scr[...] * jnp.concatenate([enter] * SEG, axis=0)
    out = after[0:1, :] if reverse else after[SUBLANES - 1:SUBLANES, :]
    return h, out


def _od1_kernel(x_ref, xn_ref, mod_ref, ng_ref, win_ref, cw_ref, cb_ref, wg_ref, ba_ref, bx_ref, lam_ref,
                perm_ref, s0_ref, w1f_ref, w2f_ref, mid_ref, sfin_ref, w1b_ref, w2b_ref,
                proj_scr, xb_scr, a_scr, b_scr, h_scr, c_scr, hc_scr, tail_scr, *, layer):
    i = pl.program_id(0)
    mod_ref, ng_ref = mod_ref.at[layer], ng_ref.at[layer]
    ba_ref, bx_ref, lam_ref = ba_ref.at[0:1], bx_ref.at[0:1], lam_ref.at[0:1]
    _, _, cidx, _, _, _ = _tile_info(jnp.minimum(i, NT - 1))
    _, _, _, first, last, _ = _tile_info(jnp.maximum(i - 1, 0))
    _cast_slabs(w1f_ref, w2f_ref, w1b_ref, w2b_ref)

    @pl.when(i == 0)
    def _():
        proj_scr[...] = jnp.zeros(proj_scr.shape, F32)

    @pl.when(first)
    def _():
        tail_scr[...] = jnp.zeros((2 * SUBLANES, D_RNN), F32)
        hc_scr[...] = s0_ref[0, 0]

    nslab = D_RNN // SLAB
    pw = 2 * D_RNN // nslab

    def project(lhs, s):
        proj_scr[:, s * pw:(s + 1) * pw] = _dot(lhs, win_ref[:, s * pw:(s + 1) * pw])

    sh1, sc1, _, _, _, _ = _mod_rows(mod_ref, cidx)
    xe = jnp.concatenate([x_ref[...], xn_ref[...]], axis=0)
    hm = (_rms(xe, ng_ref[0:1, :]) * (1.0 + sc1) + sh1).astype(BF16)
    hmp = _dot(perm_ref[...], hm[0:TM, :]).astype(BF16)
    lhs = jnp.concatenate([hmp, hm[TM:TM + HALO, :]], axis=0)

    row = lax.broadcasted_iota(jnp.int32, (SUBLANES, D_RNN), 0)
    g30 = proj_scr[TM - 2 * SUBLANES:TM - SUBLANES, 0:D_RNN]
    g31 = proj_scr[TM - SUBLANES:TM, 0:D_RNN]
    g0 = proj_scr[0:SUBLANES, 0:D_RNN]
    nxt = jnp.where(last, 0.0, proj_scr[TM:TM + 1, 0:D_RNN])
    xb_scr[0:SUBLANES, :] = jnp.where(row == 0, pltpu.roll(tail_scr[0:SUBLANES, :], 1, 0), pltpu.roll(g30, 1, 0))
    xb_scr[SUBLANES:2 * SUBLANES, :] = jnp.where(
        row == 0, pltpu.roll(tail_scr[SUBLANES:2 * SUBLANES, :], 1, 0), pltpu.roll(g31, 1, 0))
    xb_scr[2 * SUBLANES:2 * SUBLANES + TM, :] = proj_scr[0:TM, 0:D_RNN]
    xb_scr[2 * SUBLANES + TM:3 * SUBLANES + TM, :] = jnp.where(
        row == SUBLANES - 1, nxt, pltpu.roll(g0, SUBLANES - 1, 0))
    tail_scr[0:SUBLANES, :] = g30
    tail_scr[SUBLANES:2 * SUBLANES, :] = g31

    xc = cb_ref[...]
    for j in range(4):
        xc = xc + cw_ref[j:j + 1, :] * xb_scr[j * SUBLANES:j * SUBLANES + TM, :]
    mid_ref[:, 2 * D_RNN:3 * D_RNN] = xc

    project(lhs, 0)
    mid_ref[:, D_RNN:2 * D_RNN] = _gelu(proj_scr[0:TM, D_RNN:2 * D_RNN])
    for s in range(nslab):
        if s + 1 < nslab:
            project(lhs, s + 1)
        _rg_gates(mid_ref[:, 2 * D_RNN:3 * D_RNN], wg_ref, ba_ref, bx_ref, lam_ref, a_scr, b_scr, range(s, s + 1))

    h, hc_scr[...] = _rg_scan(a_scr, b_scr, h_scr, c_scr, hc_scr[...], reverse=False)
    mid_ref[:, 0:D_RNN] = h

    @pl.when(last)
    def _():
        sfin_ref[0, 0] = hc_scr[...]


def _odd_forward(x, mod, ng, w, w1f, w2f, layer):
    nb16 = TM // HALO
    cast_in, cast_out, cast_shapes = _cast_specs(layer)

    def proj_tile(i):
        return jnp.minimum(i, NT - 1)

    def scan_tile(i):
        return jnp.maximum(i - 1, 0)

    return pl.pallas_call(
        functools.partial(_od1_kernel, layer=layer),
        out_shape=(jax.ShapeDtypeStruct((NT * TM, OD_MID), F32),
                   jax.ShapeDtypeStruct((NSEQ, 1, 1, D_RNN), F32), *cast_shapes),
        grid=(NT + 1,),
        in_specs=[
            pl.BlockSpec((TM, D), lambda i: (proj_tile(i), 0)),
            pl.BlockSpec((HALO, D), lambda i: (jnp.minimum(proj_tile(i) + 1, NT - 1) * nb16, 0)),
            _const_spec((DEPTH, NCOND, 6 * D)), _const_spec((DEPTH, 4, D)),
            _const_spec((D, 2 * D_RNN)), _const_spec((4, D_RNN)), _const_spec((1, D_RNN)),
            _const_spec((D_RNN // SLAB, SLAB, 2 * SLAB)),
            _const_spec((2, D_RNN)), _const_spec((2, D_RNN)), _const_spec((2, D_RNN)),
            _const_spec((TM, TM)),
            pl.BlockSpec((1, 1, 1, D_RNN), lambda i: (_seq_of(scan_tile(i)), 0, 0, 0)),
            *cast_in,
        ],
        out_specs=(pl.BlockSpec((TM, OD_MID), lambda i: (scan_tile(i), 0)),
                   pl.BlockSpec((1, 1, 1, D_RNN), lambda i: (_seq_of(scan_tile(i)), 0, 0, 0)), *cast_out),
        scratch_shapes=[
            pltpu.VMEM((TM + HALO, 2 * D_RNN), F32),
            pltpu.VMEM((3 * SUBLANES + TM, D_RNN), F32),
            pltpu.VMEM((TM, D_RNN), F32), pltpu.VMEM((TM, D_RNN), F32), pltpu.VMEM((TM, D_RNN), F32),
            pltpu.VMEM((TM, D_RNN), F32),
            pltpu.VMEM((1, D_RNN), F32), pltpu.VMEM((2 * SUBLANES, D_RNN), F32),
        ],
        compiler_params=_params(),
        name="odd_forward",
    )(x, x, mod, ng, w["win"], w["cw"], w["cb"], w["wg"][0], w["ba"], w["bx"], w["lam"],
      _seg_perm(False), w["s0"], w1f, w2f)


def _od2_kernel(x_ref, mod_ref, ng_ref, mid_ref, wg_ref, ba_ref, bx_ref, lam_ref, wout_ref, perm_ref, s0_ref,
                x1_ref, sfin_ref, a_scr, b_scr, h_scr, c_scr, hc_scr, zp_scr, *, layer):
    i = pl.program_id(0)
    mod_ref, ng_ref = mod_ref.at[layer], ng_ref.at[layer]
    ba_ref, bx_ref, lam_ref = ba_ref.at[1:2], bx_ref.at[1:2], lam_ref.at[1:2]
    live = i < NT
    _, _, _, first, last, _ = _tile_info(NT - 1 - jnp.minimum(i, NT - 1))
    _, _, cidx_prev, _, _, _ = _tile_info(NT - 1 - jnp.maximum(i - 1, 0))
    _, _, g1, _, _, _ = _mod_rows(mod_ref, cidx_prev)

    @pl.when(i == 0)
    def _():
        zp_scr[...] = jnp.zeros(zp_scr.shape, BF16)

    @pl.when(jnp.logical_and(last, live))
    def _():
        hc_scr[...] = s0_ref[0, 0]

    xc = mid_ref[:, 2 * D_RNN:3 * D_RNN]
    nslab = D_RNN // SLAB
    z = _dot(perm_ref[...], zp_scr[...]).astype(BF16)
    ys = []
    for s in range(nslab):
        ys.append(_dot(z, wout_ref[:, s * SLAB:(s + 1) * SLAB]))
        _rg_gates(xc, wg_ref, ba_ref, bx_ref, lam_ref, a_scr, b_scr, range(s, s + 1))
    x1_ref[...] = x_ref[...] + g1 * _rms(jnp.concatenate(ys, axis=1), ng_ref[1:2, :])
    h_b, hc_scr[...] = _rg_scan(a_scr, b_scr, h_scr, c_scr, hc_scr[...], reverse=True)
    zp_scr[...] = ((mid_ref[:, 0:D_RNN] + h_b) * mid_ref[:, D_RNN:2 * D_RNN]).astype(BF16)

    @pl.when(jnp.logical_and(first, live))
    def _():
        sfin_ref[0, 0] = hc_scr[...]


def _odd_reverse(x, mod, ng, mid, w, layer):
    def scan_tile(i):
        return NT - 1 - jnp.minimum(i, NT - 1)

    def out_tile(i):
        return NT - 1 - jnp.maximum(i - 1, 0)

    return pl.pallas_call(
        functools.partial(_od2_kernel, layer=layer),
        out_shape=(jax.ShapeDtypeStruct((NT * TM, D), F32),
                   jax.ShapeDtypeStruct((NSEQ, 1, 1, D_RNN), F32)),
        grid=(NT + 1,),
        in_specs=[
            pl.BlockSpec((TM, D), lambda i: (out_tile(i), 0)),
            _const_spec((DEPTH, NCOND, 6 * D)), _const_spec((DEPTH, 4, D)),
            pl.BlockSpec((TM, OD_MID), lambda i: (scan_tile(i), 0)),
            _const_spec((D_RNN // SLAB, SLAB, 2 * SLAB)),
            _const_spec((2, D_RNN)), _const_spec((2, D_RNN)), _const_spec((2, D_RNN)),
            _const_spec((D_RNN, D)), _const_spec((TM, TM)),
            pl.BlockSpec((1, 1, 1, D_RNN), lambda i: (_seq_of(scan_tile(i)), 1, 0, 0)),
        ],
        out_specs=(pl.BlockSpec((TM, D), lambda i: (out_tile(i), 0)),
                   pl.BlockSpec((1, 1, 1, D_RNN), lambda i: (_seq_of(scan_tile(i)), 0, 0, 0))),
        scratch_shapes=[
            pltpu.VMEM((TM, D_RNN), F32), pltpu.VMEM((TM, D_RNN), F32), pltpu.VMEM((TM, D_RNN), F32),
            pltpu.VMEM((TM, D_RNN), F32),
            pltpu.VMEM((1, D_RNN), F32), pltpu.VMEM((TM, D_RNN), BF16),
        ],
        compiler_params=_params(),
        name="odd_reverse",
    )(x, mod, ng, mid, w["wg"][1], w["ba"], w["bx"], w["lam"], w["wout"], _seg_perm(True), w["s0"])


def _pos_tables():
    n = D // 4
    omega = 1.0 / (10000.0 ** (jnp.arange(n, dtype=F32) / n))
    idx = jnp.arange(GRID_W, dtype=F32)[:, None] * omega
    tab = jnp.concatenate([jnp.sin(idx), jnp.cos(idx)], axis=-1)
    return tab, tab


def _block_diag_slabs(w):
    per = SLAB // RG_BS
    rows = jnp.tile(w.reshape(D_RNN // SLAB, SLAB, RG_BS), (1, 1, per))
    ri, ci = np.indices((SLAB, SLAB))
    return jnp.where(jnp.asarray(ri // RG_BS == ci // RG_BS), rows, 0.0)


def kernel(x_prompt, x_sample, c, state_gla, state_rglru, c_ctx, mod_w, mod_b, norm_g, mlp_w1, mlp_b1, mlp_w2,
           mlp_b2, ev_w_in, ev_w_out, sgu_ln_g, sgu_ln_b, sgu_ws, sgu_bs, gla_gate_w2, gla_gate_b, gla_norm_g,
           rg_w_in, rg_conv_w, rg_conv_b, rg_wa, rg_ba, rg_wx, rg_bx, rg_L, rg_w_out):
    assert x_prompt.shape == (BATCH, SEQ, D) and x_sample.shape == (DEC_BATCH, DEC_SEQ, D)
    assert SEQ == TM and DEC_SEQ % TM == 0 and DEPTH == 2
    xp = x_prompt.reshape(NTP * TM, D)
    xs = x_sample.reshape(NTS * TM, D)
    cond8 = jnp.concatenate([c_ctx[None, :], c, jnp.zeros((NCOND - 1 - DEC_BATCH, D), F32)], axis=0)
    mods = _modulation(cond8, mod_w, mod_b)
    rtab, ctab = _pos_tables()

    gz = jnp.zeros((GLA_RANK, QK), F32)
    gmat = jnp.concatenate([jnp.concatenate([gla_gate_w2[0, 0], gz], axis=1),
                            jnp.concatenate([gz, gla_gate_w2[0, 1]], axis=1)], axis=0)
    s0_gla = jnp.concatenate([jnp.zeros((BATCH,) + state_gla.shape[2:], F32), state_gla[:, 0]], axis=0)
    ev = {
        "win": ev_w_in[0].astype(BF16),
        "gmat": gmat.astype(BF16),
        "gb": gla_gate_b[0],
        "lng": sgu_ln_g[0].reshape(1, SGU_WIDTH),
        "lnb": sgu_ln_b[0].reshape(1, SGU_WIDTH),
        "ws": sgu_ws[0].astype(BF16),
        "bst": sgu_bs[0].T,
        "gn": gla_norm_g[0].reshape(1, VW),
        "wout": ev_w_out[0].astype(BF16),
        "p0": s0_gla,
    }
    mid, midb, decb, pf, w1b, w2b = _even_forward(xp, xs, rtab, ctab, mods, norm_g, ev, mlp_w1, mlp_w2, 0)
    x1, pb = _even_reverse(xp, xs, rtab, ctab, mods, norm_g, mid, midb, decb, ev, 0)
    x2 = _mlp(x1, mods, norm_g, w1b, mlp_b1, w2b, mlp_b2, 0, 0, NT)
    new_gla = jnp.stack([pf[:BATCH, 0], pb[:BATCH, 0]], axis=1)[:, None]

    s0_rg = jnp.concatenate([jnp.zeros((BATCH, 2, D_RNN), F32), state_rglru[:, 0]], axis=0)
    od = {
        "win": rg_w_in[0].astype(BF16),
        "cw": rg_conv_w[0],
        "cb": rg_conv_b[0].reshape(1, D_RNN),
        "wg": jnp.stack([
            jnp.concatenate([_block_diag_slabs(rg_wa[0, d]), _block_diag_slabs(rg_wx[0, d])], axis=-1)
            for d in range(2)], axis=0).astype(BF16),
        "ba": rg_ba[0], "bx": rg_bx[0], "lam": rg_L[0],
        "wout": rg_w_out[0].astype(BF16),
        "s0": s0_rg.reshape(NSEQ, 2, 1, D_RNN),
    }
    mid1, sf, w1b, w2b = _odd_forward(x2, mods, norm_g, od, mlp_w1, mlp_w2, 1)
    x3, sb = _odd_reverse(x2, mods, norm_g, mid1, od, 1)
    mlp1 = (mods, norm_g, w1b, mlp_b1, w2b, mlp_b2, 1)
    y_prompt = _mlp(x3, *mlp1, 0, NTP)
    y_sample = _mlp(x3, *mlp1, NTP, NTS)
    new_rg = jnp.stack([sf[:BATCH, 0, 0], sb[:BATCH, 0, 0]], axis=1)[:, None]
    return (y_prompt.reshape(BATCH, SEQ, D), y_sample.reshape(DEC_BATCH, DEC_SEQ, D), new_gla, new_rg)
```

```python
import functools

import jax
import jax.numpy as jnp
import numpy as np
from jax import lax
from jax.experimental import pallas as pl
from jax.experimental.pallas import tpu as pltpu

D = 1024
BATCH = 16
SEQ = 256
DEPTH = 2
DEC_BATCH = 4
DEC_SEQ = 4096
GRID_W = 64
D_FF = 4 * D
EPS = 1e-6
SGU_CHUNK = 128
SGU_GROUPS = 4
SGU_WIDTH = D // 2
GLA_HEADS = 4
GLA_DV = 128
GLA_DK = 64
GLA_RANK = 16
GLA_NORMALIZER = 16.0
GLA_CHUNK = 64
QK = GLA_HEADS * GLA_DK
VW = GLA_HEADS * GLA_DV
EV_MAIN = 2 * SGU_WIDTH + 2 * QK + 2 * VW
EV_IN = EV_MAIN + 2 * GLA_RANK
D_RNN = D
RG_BLOCKS = 16
RG_BS = D_RNN // RG_BLOCKS
RG_C = 8.0
LANES = 128
SUBLANES = 8
SLAB = 256

TM = 256
NTP = BATCH * SEQ // TM
TPS = DEC_SEQ // TM
NTS = DEC_BATCH * TPS
NT = NTP + NTS
NSEQ = BATCH + DEC_BATCH
NCOND = 8
HALO = 16
EV_MID = 3 * VW
EV_MID_B = 3 * QK + VW
OD_MID = 3 * D_RNN
VMEM_LIMIT = 56 * 1024 * 1024

F32 = jnp.float32
BF16 = jnp.bfloat16


def _tile_info(t):
    is_p = t < NTP
    ts = jnp.maximum(t - NTP, 0)
    sq = ts // TPS
    within = ts % TPS
    cidx = jnp.where(is_p, 0, 1 + sq)
    first = jnp.logical_or(is_p, within == 0)
    last = jnp.logical_or(is_p, within == TPS - 1)
    seq = jnp.where(is_p, t, BATCH + sq)
    return is_p, within, cidx, first, last, seq


def _seq_of(t):
    return jnp.where(t < NTP, t, BATCH + jnp.maximum(t - NTP, 0) // TPS)


def _rms(x, g):
    return x * lax.rsqrt(jnp.mean(x * x, axis=-1, keepdims=True) + EPS) * g


def _dot(a, b):
    return jnp.dot(a, b, preferred_element_type=F32)


def _dot_nt(a, b):
    return lax.dot_general(a, b, (((1,), (1,)), ((), ())), preferred_element_type=F32)


def _dot_tn(a, b):
    return lax.dot_general(a, b, (((0,), (0,)), ((), ())), preferred_element_type=F32)


def _split3(x):
    hi = x.astype(BF16)
    r1 = x - hi.astype(F32)
    mid = r1.astype(BF16)
    lo = (r1 - mid.astype(F32)).astype(BF16)
    return hi, mid, lo


def _dot_exact_lhs(m, parts):
    return _dot(m, parts[0]) + _dot(m, parts[1]) + _dot(m, parts[2])


def _log_sigmoid(x):
    return jnp.minimum(x, 0.0) - jnp.log(1.0 + jnp.exp(-jnp.abs(x)))


def _softplus(x):
    return jnp.maximum(x, 0.0) + jnp.log(1.0 + jnp.exp(-jnp.abs(x)))


LOG2E = 1.4426950408889634
_GELU_C = 0.7978845608028654 * LOG2E


def _sigmoid(x):
    return 1.0 / (1.0 + jnp.exp2(x * (-LOG2E)))


def _gelu(x):
    return x / (1.0 + jnp.exp2(x * ((-2.0 * _GELU_C) + (-2.0 * 0.044715 * _GELU_C) * (x * x))))


def _mod_rows(mod_ref, cidx):
    m = mod_ref[pl.ds(cidx, 1), :]
    return [m[:, j * D:(j + 1) * D] for j in range(6)]


def _load_x0(is_p, within, xp_ref, xs_ref, rtab_ref, ctab_ref):
    rows_per_tile = TM // GRID_W
    r0 = within * rows_per_tile
    posr = jnp.concatenate(
        [jnp.broadcast_to(rtab_ref[pl.ds(r0 + j, 1), :], (GRID_W, D // 2)) for j in range(rows_per_tile)],
        axis=0)
    posc = jnp.concatenate([ctab_ref[...]] * rows_per_tile, axis=0)
    pos = jnp.concatenate([posr, posc], axis=1)
    return jnp.where(is_p, xp_ref[...], xs_ref[...] + pos)


def _mod_kernel(cond_ref, w_ref, b_ref, o_ref):
    c = cond_ref[...]
    sc = (c * _sigmoid(c)).astype(BF16)
    o_ref[0] = _dot(sc, w_ref[0].astype(BF16)) + b_ref[pl.ds(pl.program_id(0), 1), :]


def _modulation(cond8, mod_w, mod_b):
    wn = 2 * D
    return pl.pallas_call(
        _mod_kernel,
        out_shape=jax.ShapeDtypeStruct((DEPTH, NCOND, 6 * D), F32),
        grid=(DEPTH, 6 * D // wn),
        in_specs=[
            pl.BlockSpec((NCOND, D), lambda l, j: (0, 0)),
            pl.BlockSpec((1, D, wn), lambda l, j: (l, 0, j)),
            pl.BlockSpec((DEPTH, wn), lambda l, j: (0, j)),
        ],
        out_specs=pl.BlockSpec((1, NCOND, wn), lambda l, j: (l, 0, j)),
        compiler_params=pltpu.CompilerParams(
            dimension_semantics=("arbitrary", "arbitrary"), vmem_limit_bytes=VMEM_LIMIT),
        name="modulation",
    )(cond8, mod_w, mod_b)


def _chunk_rows(c):
    return slice(c * GLA_CHUNK, (c + 1) * GLA_CHUNK)


def _pair_lanes(p):
    return slice(p * LANES, (p + 1) * LANES)


def _gla_order(reverse):
    n = TM // GLA_CHUNK
    return list(reversed(range(n))) if reverse else list(range(n))


def _cum_mask(reverse):
    ri, ci = np.indices((TM, TM))
    order = (ci >= ri) if reverse else (ci <= ri)
    return jnp.asarray(np.logical_and(ri // GLA_CHUNK == ci // GLA_CHUNK, order), BF16)


def _gla_prep(qs, k, la, cum_m, reverse):
    n = TM // GLA_CHUNK
    b = _dot_exact_lhs(cum_m, _split3(la))
    ends = [c * GLA_CHUNK if reverse else (c + 1) * GLA_CHUNK - 1 for c in range(n)]
    btot = jnp.concatenate([jnp.broadcast_to(b[e:e + 1, :], (GLA_CHUNK, QK)) for e in ends], axis=0)
    qe = (qs * jnp.exp(b)).astype(BF16)
    ke = (k * jnp.exp(-b)).astype(BF16)
    kd = (k * jnp.exp(btot - b)).astype(BF16)
    dec = jnp.exp(btot)
    return qe, ke, kd, dec


def _gla_products(prep, v_bf, reverse):
    qe, ke, kd, _ = prep
    n = TM // GLA_CHUNK
    lane = lax.broadcasted_iota(jnp.int32, (TM, QK), 1) % LANES
    zero = jnp.zeros_like(ke)
    ke_h = (jnp.where(lane < GLA_DK, ke, zero), jnp.where(lane >= GLA_DK, ke, zero))
    cr = lax.broadcasted_iota(jnp.int32, (GLA_CHUNK, 2 * GLA_CHUNK), 0)
    cc = lax.broadcasted_iota(jnp.int32, (GLA_CHUNK, 2 * GLA_CHUNK), 1) % GLA_CHUNK
    cmask = (cc >= cr) if reverse else (cc <= cr)
    br = lax.broadcasted_iota(jnp.int32, (2 * GLA_DK, 2 * GLA_DV), 0)
    bc = lax.broadcasted_iota(jnp.int32, (2 * GLA_DK, 2 * GLA_DV), 1)
    diag = (br < GLA_DK) == (bc < GLA_DV)
    att = [[None] * n for _ in range(GLA_HEADS // 2)]
    ds = [[None] * n for _ in range(GLA_HEADS // 2)]
    for p in range(GLA_HEADS // 2):
        vp = v_bf[:, p * 2 * GLA_DV:(p + 1) * 2 * GLA_DV]
        for c in _gla_order(reverse):
            rs, ls = _chunk_rows(c), _pair_lanes(p)
            kk = jnp.concatenate([ke_h[0][rs, ls], ke_h[1][rs, ls]], axis=0)
            att[p][c] = jnp.where(cmask, _dot_nt(qe[rs, ls], kk), 0.0).astype(BF16)
            ds[p][c] = jnp.where(diag, _dot_tn(kd[rs, ls], vp[rs, :]), 0.0)
    return att, ds


def _gla_outputs(prep, prods, v_bf, p_scr, reverse):
    qe, _, _, dec = prep
    att, ds = prods
    n = TM // GLA_CHUNK
    half = TM // 2
    lane_v = lax.broadcasted_iota(jnp.int32, (TM, 2 * GLA_DV), 1)
    rows = [[None] * (GLA_HEADS // 2) for _ in range(n)]
    for p in range(GLA_HEADS // 2):
        ls = _pair_lanes(p)
        vp = v_bf[:, p * 2 * GLA_DV:(p + 1) * 2 * GLA_DV]
        vl = jnp.where(lane_v < GLA_DV, vp, jnp.zeros_like(vp))
        vr = jnp.where(lane_v >= GLA_DV, vp, jnp.zeros_like(vp))
        dec_t = (dec[0:half, ls].T, dec[half:TM, ls].T)
        s = p_scr[p]
        for c in _gla_order(reverse):
            rs = _chunk_rows(c)
            rhs = jnp.concatenate([vl[rs, :], vr[rs, :], s.astype(BF16)], axis=0)
            rows[c][p] = _dot(jnp.concatenate([att[p][c], qe[rs, ls]], axis=1), rhs)
            col = (c % 2) * GLA_CHUNK
            s = dec_t[(c * GLA_CHUNK) // half][:, col:col + 1] * s + ds[p][c]
        p_scr[p] = s
    return jnp.concatenate([jnp.concatenate(r, axis=1) for r in rows], axis=0)


def _gla_state_load(s_ref, p_scr):
    p_scr[...] = jnp.zeros(p_scr.shape, F32)
    for h in range(GLA_HEADS):
        r0, c0 = (h % 2) * GLA_DK, (h % 2) * GLA_DV
        p_scr[h // 2, r0:r0 + GLA_DK, c0:c0 + GLA_DV] = s_ref[0, 0, h]


def _gla_state_store(p_scr, s_ref):
    for h in range(GLA_HEADS):
        r0, c0 = (h % 2) * GLA_DK, (h % 2) * GLA_DV
        s_ref[0, 0, h] = p_scr[h // 2, r0:r0 + GLA_DK, c0:c0 + GLA_DV]


def _ev1_kernel(xp_ref, xs_ref, rtab_ref, ctab_ref, mod_ref, ng_ref, win_ref, gmat_ref, gb_ref,
                lng_ref, lnb_ref, ws_ref, bst_ref, cum_ref, cumr_ref, p0_ref, w1f_ref, w2f_ref,
                mid_ref, midb_ref, decb_ref, pfin_ref, w1b_ref, w2b_ref, p_scr, pg_scr, ps_scr, la_scr, *, layer):
    i = pl.program_id(0)
    mod_ref, ng_ref = mod_ref.at[layer], ng_ref.at[layer]
    is_p, within, cidx, _, _, _ = _tile_info(jnp.minimum(i, NT - 1))
    _, _, _, first, last, _ = _tile_info(jnp.maximum(i - 1, 0))
    _cast_slabs(w1f_ref, w2f_ref, w1b_ref, w2b_ref)

    @pl.when(i == 0)
    def _():
        pg_scr[...] = jnp.zeros(pg_scr.shape, F32)
        ps_scr[...] = jnp.zeros(ps_scr.shape, F32)
        la_scr[...] = jnp.zeros(la_scr.shape, F32)

    @pl.when(first)
    def _():
        _gla_state_load(p0_ref, p_scr)

    x = _load_x0(is_p, within, xp_ref, xs_ref, rtab_ref, ctab_ref)
    sh1, sc1, _, _, _, _ = _mod_rows(mod_ref, cidx)
    hm = (_rms(x, ng_ref[0:1, :]) * (1.0 + sc1) + sh1).astype(BF16)
    o0 = 2 * SGU_WIDTH
    lr = _dot(hm, win_ref[:, EV_MAIN:EV_IN]).astype(BF16)

    qs = pg_scr[:, 0:QK] * (GLA_DK ** -0.5)
    k = pg_scr[:, QK:2 * QK]
    v = pg_scr[:, 2 * QK:2 * QK + VW]
    g = pg_scr[:, 2 * QK + VW:2 * QK + 2 * VW]
    v_bf = v.astype(BF16)
    mid_ref[:, 2 * VW:3 * VW] = g * _sigmoid(g)
    midb_ref[:, 3 * QK:3 * QK + VW] = v_bf
    prep = _gla_prep(qs, k, la_scr[:, 0:QK], cum_ref[...], reverse=False)
    la_b = la_scr[:, QK:2 * QK]

    pg_scr[...] = _dot(hm, win_ref[:, o0:EV_MAIN])
    u = _gelu(ps_scr[:, 0:SGU_WIDTH])
    vg = _gelu(ps_scr[:, SGU_WIDTH:2 * SGU_WIDTH])
    mu = jnp.mean(vg, axis=-1, keepdims=True)
    vc = vg - mu
    vn = (vc * lax.rsqrt(jnp.mean(vc * vc, axis=-1, keepdims=True) + EPS) * lng_ref[...] + lnb_ref[...]).astype(BF16)
    gb = jnp.concatenate([gb_ref[0:1, :], gb_ref[1:2, :]], axis=1)
    la_scr[...] = _log_sigmoid(_dot(lr, gmat_ref[...]) + gb) * (1.0 / GLA_NORMALIZER)
    prods = _gla_products(prep, v_bf, reverse=False)
    ps_scr[...] = _dot(hm, win_ref[:, 0:o0])

    gd = SGU_WIDTH // SGU_GROUPS
    nch = TM // SGU_CHUNK
    sv_cols = []
    for grp in range(SGU_GROUPS):
        vcat = jnp.concatenate(
            [vn[c * SGU_CHUNK:(c + 1) * SGU_CHUNK, grp * gd:(grp + 1) * gd] for c in range(nch)], axis=1)
        sg = _dot(ws_ref[grp], vcat) + bst_ref[:, grp:grp + 1]
        sv_cols.append(jnp.concatenate([sg[:, c * gd:(c + 1) * gd] for c in range(nch)], axis=0))
    mid_ref[:, 0:VW] = u * jnp.concatenate(sv_cols, axis=1)
    mid_ref[:, VW:2 * VW] = _gla_outputs(prep, prods, v_bf, p_scr, reverse=False)

    qe_b, ke_b, kd_b, dec_b = _gla_prep(qs, k, la_b, cumr_ref[...], reverse=True)
    midb_ref[:, 0:QK] = qe_b
    midb_ref[:, QK:2 * QK] = ke_b
    midb_ref[:, 2 * QK:3 * QK] = kd_b
    nchunk = TM // GLA_CHUNK
    decb_ref[...] = jnp.concatenate(
        [dec_b[c * GLA_CHUNK:c * GLA_CHUNK + 1, :] for c in range(nchunk)]
        + [jnp.zeros((SUBLANES - nchunk, QK), F32)], axis=0)

    @pl.when(last)
    def _():
        _gla_state_store(p_scr, pfin_ref)


def _const_spec(shape):
    nd = len(shape)
    return pl.BlockSpec(shape, lambda i, _nd=nd: (0,) * _nd, pipeline_mode=pl.Buffered(1))


def _params():
    return pltpu.CompilerParams(dimension_semantics=("arbitrary",), vmem_limit_bytes=VMEM_LIMIT)


CAST_STEPS = 64


def _cast_specs(layer):
    r1, r2 = D // CAST_STEPS, D_FF // CAST_STEPS
    in_specs = [pl.BlockSpec((1, r1, D_FF), lambda i: (layer, jnp.minimum(i, CAST_STEPS - 1), 0)),
                pl.BlockSpec((1, r2, D), lambda i: (layer, jnp.minimum(i, CAST_STEPS - 1), 0))]
    out_specs = [pl.BlockSpec((r1, D_FF), lambda i: (jnp.minimum(i, CAST_STEPS - 1), 0)),
                 pl.BlockSpec((r2, D), lambda i: (jnp.minimum(i, CAST_STEPS - 1), 0))]
    shapes = [jax.ShapeDtypeStruct((D, D_FF), BF16), jax.ShapeDtypeStruct((D_FF, D), BF16)]
    return in_specs, out_specs, shapes


def _cast_slabs(w1f_ref, w2f_ref, w1b_ref, w2b_ref):
    w1b_ref[...] = w1f_ref[0].astype(BF16)
    w2b_ref[...] = w2f_ref[0].astype(BF16)


def _even_forward(xp, xs, rtab, ctab, mod, ng, w, w1f, w2f, layer):
    state_blk = (1, 1, GLA_HEADS, GLA_DK, GLA_DV)
    cast_in, cast_out, cast_shapes = _cast_specs(layer)

    def mix_tile(i):
        return jnp.maximum(i - 1, 0)

    return pl.pallas_call(
        functools.partial(_ev1_kernel, layer=layer),
        out_shape=(jax.ShapeDtypeStruct((NT * TM, EV_MID), F32),
                   jax.ShapeDtypeStruct((NT * TM, EV_MID_B), BF16),
                   jax.ShapeDtypeStruct((NT * SUBLANES, QK), F32),
                   jax.ShapeDtypeStruct((NSEQ, 1) + state_blk[2:], F32), *cast_shapes),
        grid=(NT + 1,),
        in_specs=[
            pl.BlockSpec((TM, D), lambda i: (jnp.minimum(i, NTP - 1), 0)),
            pl.BlockSpec((TM, D), lambda i: (jnp.maximum(jnp.minimum(i, NT - 1) - NTP, 0), 0)),
            _const_spec((GRID_W, D // 2)), _const_spec((GRID_W, D // 2)),
            _const_spec((DEPTH, NCOND, 6 * D)), _const_spec((DEPTH, 4, D)),
            _const_spec((D, EV_IN)), _const_spec((2 * GLA_RANK, 2 * QK)),
            _const_spec((2, QK)),
            _const_spec((1, SGU_WIDTH)), _const_spec((1, SGU_WIDTH)),
            _const_spec((SGU_GROUPS, SGU_CHUNK, SGU_CHUNK)), _const_spec((SGU_CHUNK, SGU_GROUPS)),
            _const_spec((TM, TM)), _const_spec((TM, TM)),
            pl.BlockSpec(state_blk, lambda i: (_seq_of(mix_tile(i)), 0, 0, 0, 0)),
            *cast_in,
        ],
        out_specs=(pl.BlockSpec((TM, EV_MID), lambda i: (mix_tile(i), 0)),
                   pl.BlockSpec((TM, EV_MID_B), lambda i: (mix_tile(i), 0)),
                   pl.BlockSpec((SUBLANES, QK), lambda i: (mix_tile(i), 0)),
                   pl.BlockSpec(state_blk, lambda i: (_seq_of(mix_tile(i)), 0, 0, 0, 0)), *cast_out),
        scratch_shapes=[pltpu.VMEM((GLA_HEADS // 2, 2 * GLA_DK, 2 * GLA_DV), F32),
                        pltpu.VMEM((TM, 2 * QK + 2 * VW), F32), pltpu.VMEM((TM, 2 * SGU_WIDTH), F32),
                        pltpu.VMEM((TM, 2 * QK), F32)],
        compiler_params=_params(),
        name="even_forward",
    )(xp, xs, rtab, ctab, mod, ng, w["win"], w["gmat"], w["gb"], w["lng"], w["lnb"],
      w["ws"], w["bst"], _cum_mask(False), _cum_mask(True), w["p0"], w1f, w2f)


def _ev2_kernel(xp_ref, xs_ref, rtab_ref, ctab_ref, mod_ref, ng_ref, mid_ref, midb_ref, decb_ref, gn_ref, wout_ref,
                p0_ref,
                x1_ref, pfin_ref, p_scr, cat_scr, *, layer):
    i = pl.program_id(0)
    mod_ref, ng_ref = mod_ref.at[layer], ng_ref.at[layer]
    live = i < NT
    _, _, _, first, last, _ = _tile_info(NT - 1 - jnp.minimum(i, NT - 1))
    is_p, within, cidx, _, _, _ = _tile_info(NT - 1 - jnp.maximum(i - 1, 0))

    @pl.when(i == 0)
    def _():
        cat_scr[...] = jnp.zeros(cat_scr.shape, BF16)

    @pl.when(jnp.logical_and(last, live))
    def _():
        _gla_state_load(p0_ref, p_scr)

    nblk = D // SLAB
    ys = [None] * nblk

    def project(j):
        ys[j] = _dot(cat_scr[...], wout_ref[:, j * SLAB:(j + 1) * SLAB])

    v_bf = midb_ref[:, 3 * QK:3 * QK + VW]
    dec = jnp.concatenate([jnp.broadcast_to(decb_ref[c:c + 1, :], (GLA_CHUNK, QK)) for c in range(TM // GLA_CHUNK)],
                          axis=0)
    prep = (midb_ref[:, 0:QK], midb_ref[:, QK:2 * QK], midb_ref[:, 2 * QK:3 * QK], dec)
    prods = _gla_products(prep, v_bf, reverse=True)
    project(0)
    project(1)
    o = mid_ref[:, VW:2 * VW] + _gla_outputs(prep, prods, v_bf, p_scr, reverse=True)
    project(2)
    project(3)
    heads = []
    for h in range(GLA_HEADS):
        oh = o[:, h * GLA_DV:(h + 1) * GLA_DV]
        heads.append(oh * lax.rsqrt(jnp.mean(oh * oh, axis=-1, keepdims=True) + EPS))
    on = jnp.concatenate(heads, axis=1) * gn_ref[...] * mid_ref[:, 2 * VW:3 * VW]
    cat_scr[...] = jnp.concatenate([mid_ref[:, 0:VW], on], axis=1).astype(BF16)

    x = _load_x0(is_p, within, xp_ref, xs_ref, rtab_ref, ctab_ref)
    _, _, g1, _, _, _ = _mod_rows(mod_ref, cidx)
    x1_ref[...] = x + g1 * _rms(jnp.concatenate(ys, axis=1), ng_ref[1:2, :])

    @pl.when(jnp.logical_and(first, live))
    def _():
        _gla_state_store(p_scr, pfin_ref)


def _even_reverse(xp, xs, rtab, ctab, mod, ng, mid, midb, decb, w, layer):
    state_blk = (1, 1, GLA_HEADS, GLA_DK, GLA_DV)
    assert D // SLAB == 4

    def mix_tile(i):
        return NT - 1 - jnp.minimum(i, NT - 1)

    def out_tile(i):
        return NT - 1 - jnp.maximum(i - 1, 0)

    return pl.pallas_call(
        functools.partial(_ev2_kernel, layer=layer),
        out_shape=(jax.ShapeDtypeStruct((NT * TM, D), F32),
                   jax.ShapeDtypeStruct((NSEQ, 1) + state_blk[2:], F32)),
        grid=(NT + 1,),
        in_specs=[
            pl.BlockSpec((TM, D), lambda i: (jnp.minimum(out_tile(i), NTP - 1), 0)),
            pl.BlockSpec((TM, D), lambda i: (jnp.maximum(out_tile(i) - NTP, 0), 0)),
            _const_spec((GRID_W, D // 2)), _const_spec((GRID_W, D // 2)),
            _const_spec((DEPTH, NCOND, 6 * D)), _const_spec((DEPTH, 4, D)),
            pl.BlockSpec((TM, EV_MID), lambda i: (mix_tile(i), 0)),
            pl.BlockSpec((TM, EV_MID_B), lambda i: (mix_tile(i), 0)),
            pl.BlockSpec((SUBLANES, QK), lambda i: (mix_tile(i), 0)),
            _const_spec((1, VW)), _const_spec((2 * VW, D)),
            pl.BlockSpec(state_blk, lambda i: (_seq_of(mix_tile(i)), 1, 0, 0, 0)),
        ],
        out_specs=(pl.BlockSpec((TM, D), lambda i: (out_tile(i), 0)),
                   pl.BlockSpec(state_blk, lambda i: (_seq_of(mix_tile(i)), 0, 0, 0, 0))),
        scratch_shapes=[pltpu.VMEM((GLA_HEADS // 2, 2 * GLA_DK, 2 * GLA_DV), F32),
                        pltpu.VMEM((TM, D), BF16)],
        compiler_params=_params(),
        name="even_reverse",
    )(xp, xs, rtab, ctab, mod, ng, mid, midb, decb, w["gn"], w["wout"], w["p0"])


FF_CHUNK = 1024
MLP_TILES = 2
TMM = MLP_TILES * TM


def _mlp_kernel(x_ref, mod_ref, ng_ref, w1_ref, b1_ref, w2_ref, b2_ref, o_ref, *, tile0, layer):
    mod_ref, ng_ref = mod_ref.at[layer], ng_ref.at[layer]
    b1_ref, b2_ref = b1_ref.at[layer:layer + 1], b2_ref.at[layer:layer + 1]
    _, _, cidx, _, _, _ = _tile_info(tile0 + pl.program_id(0) * MLP_TILES)
    _, _, _, sh2, sc2, g2 = _mod_rows(mod_ref, cidx)
    x = x_ref[...]
    hff = (_rms(x, ng_ref[2:3, :]) * (1.0 + sc2) + sh2).astype(BF16)
    acc = b2_ref[...]
    for j in range(D_FF // FF_CHUNK):
        cs = slice(j * FF_CHUNK, (j + 1) * FF_CHUNK)
        h = _dot(hff, w1_ref[:, cs]) + b1_ref[:, cs]
        h = jnp.square(jnp.maximum(h, 0.0)).astype(BF16)
        acc = acc + _dot(h, w2_ref[cs, :])
    o_ref[...] = x + g2 * _rms(acc, ng_ref[3:4, :])


def _mlp(x, mod, ng, w1, b1, w2, b2, layer, tile0, ntiles):
    assert tile0 % MLP_TILES == 0 and ntiles % MLP_TILES == 0 and TPS % MLP_TILES == 0 and NTP % MLP_TILES == 0
    b0 = tile0 // MLP_TILES
    return pl.pallas_call(
        functools.partial(_mlp_kernel, tile0=tile0, layer=layer),
        out_shape=jax.ShapeDtypeStruct((ntiles * TM, D), F32),
        grid=(ntiles // MLP_TILES,),
        in_specs=[
            pl.BlockSpec((TMM, D), lambda i: (b0 + i, 0)),
            _const_spec((DEPTH, NCOND, 6 * D)), _const_spec((DEPTH, 4, D)),
            _const_spec((D, D_FF)), _const_spec((DEPTH, D_FF)), _const_spec((D_FF, D)), _const_spec((DEPTH, D)),
        ],
        out_specs=pl.BlockSpec((TMM, D), lambda i: (i, 0)),
        compiler_params=_params(), name="mlp",
    )(x, mod, ng, w1, b1, w2, b2)


def _rg_gates(xc, wg_ref, ba_ref, bx_ref, lam_ref, a_scr, b_scr, slabs):
    xcb = xc.astype(BF16)
    sp = RG_C * _softplus(-lam_ref[...])
    for s in slabs:
        cs = slice(s * SLAB, (s + 1) * SLAB)
        pre = _dot(xcb[:, cs], wg_ref[s])
        r = _sigmoid(pre[:, 0:SLAB] + ba_ref[:, cs])
        i = _sigmoid(pre[:, SLAB:2 * SLAB] + bx_ref[:, cs])
        z = r * sp[:, cs]
        a = jnp.exp(-z)
        a_scr[:, cs] = a
        u = jnp.tanh(z) * (1.0 + a * a)
        b_scr[:, cs] = jnp.where(u > 0.0, u * lax.rsqrt(u), 0.0) * (i * xc[:, cs])


SEG = TM // SUBLANES


def _seg_perm(transpose):
    rr, cc = np.indices((TM, TM))
    if transpose:
        rr, cc = cc, rr
    return jnp.asarray(np.logical_and(cc // SEG == rr % SUBLANES, cc % SEG == rr // SUBLANES), BF16)


def _rg_scan(a_scr, b_scr, h_scr, c_scr, h0, reverse):
    def body(i, carry):
        hh, cc = carry
        grp = (SEG - 1 - i) if reverse else i
        r0 = pl.multiple_of(grp * SUBLANES, SUBLANES)
        a = a_scr[pl.ds(r0, SUBLANES), :]
        hh = a * hh + b_scr[pl.ds(r0, SUBLANES), :]
        cc = a * cc
        h_scr[pl.ds(r0, SUBLANES), :] = hh
        c_scr[pl.ds(r0, SUBLANES), :] = cc
        return hh, cc

    init = (jnp.zeros((SUBLANES, D_RNN), F32), jnp.ones((SUBLANES, D_RNN), F32))
    b, a = lax.fori_loop(0, SEG, body, init, unroll=4 if reverse else True)
    row = lax.broadcasted_iota(jnp.int32, (SUBLANES, D_RNN), 0)
    for s in (1, 2, 4):
        shift = (SUBLANES - s) if reverse else s
        valid = (row < SUBLANES - s) if reverse else (row >= s)
        a_s = pltpu.roll(a, shift, 0)
        b_s = pltpu.roll(b, shift, 0)
        b = jnp.where(valid, a * b_s + b, b)
        a = jnp.where(valid, a * a_s, a)
    after = a * h0 + b
    edge = (row == SUBLANES - 1) if reverse else (row == 0)
    enter = jnp.where(edge, h0, pltpu.roll(after, (SUBLANES - 1) if reverse else 1, 0))
    h = h_scr[...] + c_scr[...] * jnp.concatenate([enter] * SEG, axis=0)
    out = after[0:1, :] if reverse else after[SUBLANES - 1:SUBLANES, :]
    return h, out


def _od1_kernel(x_ref, xn_ref, mod_ref, ng_ref, win_ref, cw_ref, cb_ref, wg_ref, ba_ref, bx_ref, lam_ref,
                perm_ref, s0_ref, w1f_ref, w2f_ref, mid_ref, sfin_ref, w1b_ref, w2b_ref,
                proj_scr, xb_scr, a_scr, b_scr, h_scr, c_scr, hc_scr, tail_scr, *, layer):
    i = pl.program_id(0)
    mod_ref, ng_ref = mod_ref.at[layer], ng_ref.at[layer]
    ba_ref, bx_ref, lam_ref = ba_ref.at[0:1], bx_ref.at[0:1], lam_ref.at[0:1]
    _, _, cidx, _, _, _ = _tile_info(jnp.minimum(i, NT - 1))
    _, _, _, first, last, _ = _tile_info(jnp.maximum(i - 1, 0))
    _cast_slabs(w1f_ref, w2f_ref, w1b_ref, w2b_ref)

    @pl.when(i == 0)
    def _():
        proj_scr[...] = jnp.zeros(proj_scr.shape, F32)

    @pl.when(first)
    def _():
        tail_scr[...] = jnp.zeros((2 * SUBLANES, D_RNN), F32)
        hc_scr[...] = s0_ref[0, 0]

    nslab = D_RNN // SLAB
    pw = 2 * D_RNN // nslab

    def project(lhs, s):
        proj_scr[:, s * pw:(s + 1) * pw] = _dot(lhs, win_ref[:, s * pw:(s + 1) * pw])

    sh1, sc1, _, _, _, _ = _mod_rows(mod_ref, cidx)
    xe = jnp.concatenate([x_ref[...], xn_ref[...]], axis=0)
    hm = (_rms(xe, ng_ref[0:1, :]) * (1.0 + sc1) + sh1).astype(BF16)
    hmp = _dot(perm_ref[...], hm[0:TM, :]).astype(BF16)
    lhs = jnp.concatenate([hmp, hm[TM:TM + HALO, :]], axis=0)

    row = lax.broadcasted_iota(jnp.int32, (SUBLANES, D_RNN), 0)
    g30 = proj_scr[TM - 2 * SUBLANES:TM - SUBLANES, 0:D_RNN]
    g31 = proj_scr[TM - SUBLANES:TM, 0:D_RNN]
    g0 = proj_scr[0:SUBLANES, 0:D_RNN]
    nxt = jnp.where(last, 0.0, proj_scr[TM:TM + 1, 0:D_RNN])
    xb_scr[0:SUBLANES, :] = jnp.where(row == 0, pltpu.roll(tail_scr[0:SUBLANES, :], 1, 0), pltpu.roll(g30, 1, 0))
    xb_scr[SUBLANES:2 * SUBLANES, :] = jnp.where(
        row == 0, pltpu.roll(tail_scr[SUBLANES:2 * SUBLANES, :], 1, 0), pltpu.roll(g31, 1, 0))
    xb_scr[2 * SUBLANES:2 * SUBLANES + TM, :] = proj_scr[0:TM, 0:D_RNN]
    xb_scr[2 * SUBLANES + TM:3 * SUBLANES + TM, :] = jnp.where(
        row == SUBLANES - 1, nxt, pltpu.roll(g0, SUBLANES - 1, 0))
    tail_scr[0:SUBLANES, :] = g30
    tail_scr[SUBLANES:2 * SUBLANES, :] = g31

    xc = cb_ref[...]
    for j in range(4):
        xc = xc + cw_ref[j:j + 1, :] * xb_scr[j * SUBLANES:j * SUBLANES + TM, :]
    mid_ref[:, 2 * D_RNN:3 * D_RNN] = xc

    project(lhs, 0)
    mid_ref[:, D_RNN:2 * D_RNN] = _gelu(proj_scr[0:TM, D_RNN:2 * D_RNN])
    for s in range(nslab):
        if s + 1 < nslab:
            project(lhs, s + 1)
        _rg_gates(mid_ref[:, 2 * D_RNN:3 * D_RNN], wg_ref, ba_ref, bx_ref, lam_ref, a_scr, b_scr, range(s, s + 1))

    h, hc_scr[...] = _rg_scan(a_scr, b_scr, h_scr, c_scr, hc_scr[...], reverse=False)
    mid_ref[:, 0:D_RNN] = h

    @pl.when(last)
    def _():
        sfin_ref[0, 0] = hc_scr[...]


def _odd_forward(x, mod, ng, w, w1f, w2f, layer):
    nb16 = TM // HALO
    cast_in, cast_out, cast_shapes = _cast_specs(layer)

    def proj_tile(i):
        return jnp.minimum(i, NT - 1)

    def scan_tile(i):
        return jnp.maximum(i - 1, 0)

    return pl.pallas_call(
        functools.partial(_od1_kernel, layer=layer),
        out_shape=(jax.ShapeDtypeStruct((NT * TM, OD_MID), F32),
                   jax.ShapeDtypeStruct((NSEQ, 1, 1, D_RNN), F32), *cast_shapes),
        grid=(NT + 1,),
        in_specs=[
            pl.BlockSpec((TM, D), lambda i: (proj_tile(i), 0)),
            pl.BlockSpec((HALO, D), lambda i: (jnp.minimum(proj_tile(i) + 1, NT - 1) * nb16, 0)),
            _const_spec((DEPTH, NCOND, 6 * D)), _const_spec((DEPTH, 4, D)),
            _const_spec((D, 2 * D_RNN)), _const_spec((4, D_RNN)), _const_spec((1, D_RNN)),
            _const_spec((D_RNN // SLAB, SLAB, 2 * SLAB)),
            _const_spec((2, D_RNN)), _const_spec((2, D_RNN)), _const_spec((2, D_RNN)),
            _const_spec((TM, TM)),
            pl.BlockSpec((1, 1, 1, D_RNN), lambda i: (_seq_of(scan_tile(i)), 0, 0, 0)),
            *cast_in,
        ],
        out_specs=(pl.BlockSpec((TM, OD_MID), lambda i: (scan_tile(i), 0)),
                   pl.BlockSpec((1, 1, 1, D_RNN), lambda i: (_seq_of(scan_tile(i)), 0, 0, 0)), *cast_out),
        scratch_shapes=[
            pltpu.VMEM((TM + HALO, 2 * D_RNN), F32),
            pltpu.VMEM((3 * SUBLANES + TM, D_RNN), F32),
            pltpu.VMEM((TM, D_RNN), F32), pltpu.VMEM((TM, D_RNN), F32), pltpu.VMEM((TM, D_RNN), F32),
            pltpu.VMEM((TM, D_RNN), F32),
            pltpu.VMEM((1, D_RNN), F32), pltpu.VMEM((2 * SUBLANES, D_RNN), F32),
        ],
        compiler_params=_params(),
        name="odd_forward",
    )(x, x, mod, ng, w["win"], w["cw"], w["cb"], w["wg"][0], w["ba"], w["bx"], w["lam"],
      _seg_perm(False), w["s0"], w1f, w2f)


def _od2_kernel(x_ref, mod_ref, ng_ref, mid_ref, wg_ref, ba_ref, bx_ref, lam_ref, wout_ref, perm_ref, s0_ref,
                x1_ref, sfin_ref, a_scr, b_scr, h_scr, c_scr, hc_scr, zp_scr, *, layer):
    i = pl.program_id(0)
    mod_ref, ng_ref = mod_ref.at[layer], ng_ref.at[layer]
    ba_ref, bx_ref, lam_ref = ba_ref.at[1:2], bx_ref.at[1:2], lam_ref.at[1:2]
    live = i < NT
    _, _, _, first, last, _ = _tile_info(NT - 1 - jnp.minimum(i, NT - 1))
    _, _, cidx_prev, _, _, _ = _tile_info(NT - 1 - jnp.maximum(i - 1, 0))
    _, _, g1, _, _, _ = _mod_rows(mod_ref, cidx_prev)

    @pl.when(i == 0)
    def _():
        zp_scr[...] = jnp.zeros(zp_scr.shape, BF16)

    @pl.when(jnp.logical_and(last, live))
    def _():
        hc_scr[...] = s0_ref[0, 0]

    xc = mid_ref[:, 2 * D_RNN:3 * D_RNN]
    nslab = D_RNN // SLAB
    z = _dot(perm_ref[...], zp_scr[...]).astype(BF16)
    ys = []
    for s in range(nslab):
        ys.append(_dot(z, wout_ref[:, s * SLAB:(s + 1) * SLAB]))
        _rg_gates(xc, wg_ref, ba_ref, bx_ref, lam_ref, a_scr, b_scr, range(s, s + 1))
    x1_ref[...] = x_ref[...] + g1 * _rms(jnp.concatenate(ys, axis=1), ng_ref[1:2, :])
    h_b, hc_scr[...] = _rg_scan(a_scr, b_scr, h_scr, c_scr, hc_scr[...], reverse=True)
    zp_scr[...] = ((mid_ref[:, 0:D_RNN] + h_b) * mid_ref[:, D_RNN:2 * D_RNN]).astype(BF16)

    @pl.when(jnp.logical_and(first, live))
    def _():
        sfin_ref[0, 0] = hc_scr[...]


def _odd_reverse(x, mod, ng, mid, w, layer):
    def scan_tile(i):
        return NT - 1 - jnp.minimum(i, NT - 1)

    def out_tile(i):
        return NT - 1 - jnp.maximum(i - 1, 0)

    return pl.pallas_call(
        functools.partial(_od2_kernel, layer=layer),
        out_shape=(jax.ShapeDtypeStruct((NT * TM, D), F32),
                   jax.ShapeDtypeStruct((NSEQ, 1, 1, D_RNN), F32)),
        grid=(NT + 1,),
        in_specs=[
            pl.BlockSpec((TM, D), lambda i: (out_tile(i), 0)),
            _const_spec((DEPTH, NCOND, 6 * D)), _const_spec((DEPTH, 4, D)),
            pl.BlockSpec((TM, OD_MID), lambda i: (scan_tile(i), 0)),
            _const_spec((D_RNN // SLAB, SLAB, 2 * SLAB)),
            _const_spec((2, D_RNN)), _const_spec((2, D_RNN)), _const_spec((2, D_RNN)),
            _const_spec((D_RNN, D)), _const_spec((TM, TM)),
            pl.BlockSpec((1, 1, 1, D_RNN), lambda i: (_seq_of(scan_tile(i)), 1, 0, 0)),
        ],
        out_specs=(pl.BlockSpec((TM, D), lambda i: (out_tile(i), 0)),
                   pl.BlockSpec((1, 1, 1, D_RNN), lambda i: (_seq_of(scan_tile(i)), 0, 0, 0))),
        scratch_shapes=[
            pltpu.VMEM((TM, D_RNN), F32), pltpu.VMEM((TM, D_RNN), F32), pltpu.VMEM((TM, D_RNN), F32),
            pltpu.VMEM((TM, D_RNN), F32),
            pltpu.VMEM((1, D_RNN), F32), pltpu.VMEM((TM, D_RNN), BF16),
        ],
        compiler_params=_params(),
        name="odd_reverse",
    )(x, mod, ng, mid, w["wg"][1], w["ba"], w["bx"], w["lam"], w["wout"], _seg_perm(True), w["s0"])


def _pos_tables():
    n = D // 4
    omega = 1.0 / (10000.0 ** (jnp.arange(n, dtype=F32) / n))
    idx = jnp.arange(GRID_W, dtype=F32)[:, None] * omega
    tab = jnp.concatenate([jnp.sin(idx), jnp.cos(idx)], axis=-1)
    return tab, tab


def _block_diag_slabs(w):
    per = SLAB // RG_BS
    rows = jnp.tile(w.reshape(D_RNN // SLAB, SLAB, RG_BS), (1, 1, per))
    ri, ci = np.indices((SLAB, SLAB))
    return jnp.where(jnp.asarray(ri // RG_BS == ci // RG_BS), rows, 0.0)


def kernel(x_prompt, x_sample, c, state_gla, state_rglru, c_ctx, mod_w, mod_b, norm_g, mlp_w1, mlp_b1, mlp_w2,
           mlp_b2, ev_w_in, ev_w_out, sgu_ln_g, sgu_ln_b, sgu_ws, sgu_bs, gla_gate_w2, gla_gate_b, gla_norm_g,
           rg_w_in, rg_conv_w, rg_conv_b, rg_wa, rg_ba, rg_wx, rg_bx, rg_L, rg_w_out):
    assert x_prompt.shape == (BATCH, SEQ, D) and x_sample.shape == (DEC_BATCH, DEC_SEQ, D)
    assert SEQ == TM and DEC_SEQ % TM == 0 and DEPTH == 2
    xp = x_prompt.reshape(NTP * TM, D)
    xs = x_sample.reshape(NTS * TM, D)
    cond8 = jnp.concatenate([c_ctx[None, :], c, jnp.zeros((NCOND - 1 - DEC_BATCH, D), F32)], axis=0)
    mods = _modulation(cond8, mod_w, mod_b)
    rtab, ctab = _pos_tables()

    gz = jnp.zeros((GLA_RANK, QK), F32)
    gmat = jnp.concatenate([jnp.concatenate([gla_gate_w2[0, 0], gz], axis=1),
                            jnp.concatenate([gz, gla_gate_w2[0, 1]], axis=1)], axis=0)
    s0_gla = jnp.concatenate([jnp.zeros((BATCH,) + state_gla.shape[2:], F32), state_gla[:, 0]], axis=0)
    ev = {
        "win": ev_w_in[0].astype(BF16),
        "gmat": gmat.astype(BF16),
        "gb": gla_gate_b[0],
        "lng": sgu_ln_g[0].reshape(1, SGU_WIDTH),
        "lnb": sgu_ln_b[0].reshape(1, SGU_WIDTH),
        "ws": sgu_ws[0].astype(BF16),
        "bst": sgu_bs[0].T,
        "gn": gla_norm_g[0].reshape(1, VW),
        "wout": ev_w_out[0].astype(BF16),
        "p0": s0_gla,
    }
    mid, midb, decb, pf, w1b, w2b = _even_forward(xp, xs, rtab, ctab, mods, norm_g, ev, mlp_w1, mlp_w2, 0)
    x1, pb = _even_reverse(xp, xs, rtab, ctab, mods, norm_g, mid, midb, decb, ev, 0)
    x2 = _mlp(x1, mods, norm_g, w1b, mlp_b1, w2b, mlp_b2, 0, 0, NT)
    new_gla = jnp.stack([pf[:BATCH, 0], pb[:BATCH, 0]], axis=1)[:, None]

    s0_rg = jnp.concatenate([jnp.zeros((BATCH, 2, D_RNN), F32), state_rglru[:, 0]], axis=0)
    od = {
        "win": rg_w_in[0].astype(BF16),
        "cw": rg_conv_w[0],
        "cb": rg_conv_b[0].reshape(1, D_RNN),
        "wg": jnp.stack([
            jnp.concatenate([_block_diag_slabs(rg_wa[0, d]), _block_diag_slabs(rg_wx[0, d])], axis=-1)
            for d in range(2)], axis=0).astype(BF16),
        "ba": rg_ba[0], "bx": rg_bx[0], "lam": rg_L[0],
        "wout": rg_w_out[0].astype(BF16),
        "s0": s0_rg.reshape(NSEQ, 2, 1, D_RNN),
    }
    mid1, sf, w1b, w2b = _odd_forward(x2, mods, norm_g, od, mlp_w1, mlp_w2, 1)
    x3, sb = _odd_reverse(x2, mods, norm_g, mid1, od, 1)
    mlp1 = (mods, norm_g, w1b, mlp_b1, w2b, mlp_b2, 1)
    y_prompt = _mlp(x3, *mlp1, 0, NTP)
    y_sample = _mlp(x3, *mlp1, NTP, NTS)
    new_rg = jnp.stack([sf[:BATCH, 0, 0], sb[:BATCH, 0, 0]], axis=1)[:, None]
    return (y_prompt.reshape(BATCH, SEQ, D), y_sample.reshape(DEC_BATCH, DEC_SEQ, D), new_gla, new_rg)
```

```python
import functools

import jax
import jax.numpy as jnp
import numpy as np
from jax import lax
from jax.experimental import pallas as pl
from jax.experimental.pallas import tpu as pltpu

D = 1024
BATCH = 16
SEQ = 256
DEPTH = 2
DEC_BATCH = 4
DEC_SEQ = 4096
GRID_W = 64
D_FF = 4 * D
EPS = 1e-6
SGU_CHUNK = 128
SGU_GROUPS = 4
SGU_WIDTH = D // 2
GLA_HEADS = 4
GLA_DV = 128
GLA_DK = 64
GLA_RANK = 16
GLA_NORMALIZER = 16.0
GLA_CHUNK = 64
QK = GLA_HEADS * GLA_DK
VW = GLA_HEADS * GLA_DV
EV_MAIN = 2 * SGU_WIDTH + 2 * QK + 2 * VW
EV_IN = EV_MAIN + 2 * GLA_RANK
D_RNN = D
RG_BLOCKS = 16
RG_BS = D_RNN // RG_BLOCKS
RG_C = 8.0
LANES = 128
SUBLANES = 8
SLAB = 256

TM = 256
NTP = BATCH * SEQ // TM
TPS = DEC_SEQ // TM
NTS = DEC_BATCH * TPS
NT = NTP + NTS
NSEQ = BATCH + DEC_BATCH
NCOND = 8
HALO = 16
EV_MID = 3 * VW
EV_MID_B = 3 * QK + VW
OD_MID = 3 * D_RNN
VMEM_LIMIT = 56 * 1024 * 1024

F32 = jnp.float32
BF16 = jnp.bfloat16


def _tile_info(t):
    is_p = t < NTP
    ts = jnp.maximum(t - NTP, 0)
    sq = ts // TPS
    within = ts % TPS
    cidx = jnp.where(is_p, 0, 1 + sq)
    first = jnp.logical_or(is_p, within == 0)
    last = jnp.logical_or(is_p, within == TPS - 1)
    seq = jnp.where(is_p, t, BATCH + sq)
    return is_p, within, cidx, first, last, seq


def _seq_of(t):
    return jnp.where(t < NTP, t, BATCH + jnp.maximum(t - NTP, 0) // TPS)


def _rms(x, g):
    return x * lax.rsqrt(jnp.mean(x * x, axis=-1, keepdims=True) + EPS) * g


def _dot(a, b):
    return jnp.dot(a, b, preferred_element_type=F32)


def _dot_nt(a, b):
    return lax.dot_general(a, b, (((1,), (1,)), ((), ())), preferred_element_type=F32)


def _dot_tn(a, b):
    return lax.dot_general(a, b, (((0,), (0,)), ((), ())), preferred_element_type=F32)


def _split3(x):
    hi = x.astype(BF16)
    r1 = x - hi.astype(F32)
    mid = r1.astype(BF16)
    lo = (r1 - mid.astype(F32)).astype(BF16)
    return hi, mid, lo


def _dot_exact_lhs(m, parts):
    return _dot(m, parts[0]) + _dot(m, parts[1]) + _dot(m, parts[2])


def _log_sigmoid(x):
    return jnp.minimum(x, 0.0) - jnp.log(1.0 + jnp.exp(-jnp.abs(x)))


def _softplus(x):
    return jnp.maximum(x, 0.0) + jnp.log(1.0 + jnp.exp(-jnp.abs(x)))


LOG2E = 1.4426950408889634
_GELU_C = 0.7978845608028654 * LOG2E


def _sigmoid(x):
    return 1.0 / (1.0 + jnp.exp2(x * (-LOG2E)))


def _gelu(x):
    return x / (1.0 + jnp.exp2(x * ((-2.0 * _GELU_C) + (-2.0 * 0.044715 * _GELU_C) * (x * x))))


def _mod_rows(mod_ref, cidx):
    m = mod_ref[pl.ds(cidx, 1), :]
    return [m[:, j * D:(j + 1) * D] for j in range(6)]


def _load_x0(is_p, within, xp_ref, xs_ref, rtab_ref, ctab_ref):
    rows_per_tile = TM // GRID_W
    r0 = within * rows_per_tile
    posr = jnp.concatenate(
        [jnp.broadcast_to(rtab_ref[pl.ds(r0 + j, 1), :], (GRID_W, D // 2)) for j in range(rows_per_tile)],
        axis=0)
    posc = jnp.concatenate([ctab_ref[...]] * rows_per_tile, axis=0)
    pos = jnp.concatenate([posr, posc], axis=1)
    return jnp.where(is_p, xp_ref[...], xs_ref[...] + pos)


def _mod_kernel(cond_ref, w_ref, b_ref, o_ref):
    c = cond_ref[...]
    sc = (c * _sigmoid(c)).astype(BF16)
    o_ref[0] = _dot(sc, w_ref[0].astype(BF16)) + b_ref[pl.ds(pl.program_id(0), 1), :]


def _modulation(cond8, mod_w, mod_b):
    wn = 2 * D
    return pl.pallas_call(
        _mod_kernel,
        out_shape=jax.ShapeDtypeStruct((DEPTH, NCOND, 6 * D), F32),
        grid=(DEPTH, 6 * D // wn),
        in_specs=[
            pl.BlockSpec((NCOND, D), lambda l, j: (0, 0)),
            pl.BlockSpec((1, D, wn), lambda l, j: (l, 0, j)),
            pl.BlockSpec((DEPTH, wn), lambda l, j: (0, j)),
        ],
        out_specs=pl.BlockSpec((1, NCOND, wn), lambda l, j: (l, 0, j)),
        compiler_params=pltpu.CompilerParams(
            dimension_semantics=("arbitrary", "arbitrary"), vmem_limit_bytes=VMEM_LIMIT),
        name="modulation",
    )(cond8, mod_w, mod_b)


def _chunk_rows(c):
    return slice(c * GLA_CHUNK, (c + 1) * GLA_CHUNK)


def _pair_lanes(p):
    return slice(p * LANES, (p + 1) * LANES)


def _gla_order(reverse):
    n = TM // GLA_CHUNK
    return list(reversed(range(n))) if reverse else list(range(n))


def _cum_mask(reverse):
    ri, ci = np.indices((TM, TM))
    order = (ci >= ri) if reverse else (ci <= ri)
    return jnp.asarray(np.logical_and(ri // GLA_CHUNK == ci // GLA_CHUNK, order), BF16)


def _gla_prep(qs, k, la, cum_m, reverse):
    n = TM // GLA_CHUNK
    b = _dot_exact_lhs(cum_m, _split3(la))
    ends = [c * GLA_CHUNK if reverse else (c + 1) * GLA_CHUNK - 1 for c in range(n)]
    btot = jnp.concatenate([jnp.broadcast_to(b[e:e + 1, :], (GLA_CHUNK, QK)) for e in ends], axis=0)
    qe = (qs * jnp.exp(b)).astype(BF16)
    ke = (k * jnp.exp(-b)).astype(BF16)
    kd = (k * jnp.exp(btot - b)).astype(BF16)
    dec = jnp.exp(btot)
    return qe, ke, kd, dec


def _gla_products(prep, v_bf, reverse):
    qe, ke, kd, _ = prep
    n = TM // GLA_CHUNK
    lane = lax.broadcasted_iota(jnp.int32, (TM, QK), 1) % LANES
    zero = jnp.zeros_like(ke)
    ke_h = (jnp.where(lane < GLA_DK, ke, zero), jnp.where(lane >= GLA_DK, ke, zero))
    cr = lax.broadcasted_iota(jnp.int32, (GLA_CHUNK, 2 * GLA_CHUNK), 0)
    cc = lax.broadcasted_iota(jnp.int32, (GLA_CHUNK, 2 * GLA_CHUNK), 1) % GLA_CHUNK
    cmask = (cc >= cr) if reverse else (cc <= cr)
    br = lax.broadcasted_iota(jnp.int32, (2 * GLA_DK, 2 * GLA_DV), 0)
    bc = lax.broadcasted_iota(jnp.int32, (2 * GLA_DK, 2 * GLA_DV), 1)
    diag = (br < GLA_DK) == (bc < GLA_DV)
    att = [[None] * n for _ in range(GLA_HEADS // 2)]
    ds = [[None] * n for _ in range(GLA_HEADS // 2)]
    for p in range(GLA_HEADS // 2):
        vp = v_bf[:, p * 2 * GLA_DV:(p + 1) * 2 * GLA_DV]
        for c in _gla_order(reverse):
            rs, ls = _chunk_rows(c), _pair_lanes(p)
            kk = jnp.concatenate([ke_h[0][rs, ls], ke_h[1][rs, ls]], axis=0)
            att[p][c] = jnp.where(cmask, _dot_nt(qe[rs, ls], kk), 0.0).astype(BF16)
            ds[p][c] = jnp.where(diag, _dot_tn(kd[rs, ls], vp[rs, :]), 0.0)
    return att, ds


def _gla_outputs(prep, prods, v_bf, p_scr, reverse):
    qe, _, _, dec = prep
    att, ds = prods
    n = TM // GLA_CHUNK
    half = TM // 2
    lane_v = lax.broadcasted_iota(jnp.int32, (TM, 2 * GLA_DV), 1)
    rows = [[None] * (GLA_HEADS // 2) for _ in range(n)]
    for p in range(GLA_HEADS // 2):
        ls = _pair_lanes(p)
        vp = v_bf[:, p * 2 * GLA_DV:(p + 1) * 2 * GLA_DV]
        vl = jnp.where(lane_v < GLA_DV, vp, jnp.zeros_like(vp))
        vr = jnp.where(lane_v >= GLA_DV, vp, jnp.zeros_like(vp))
        dec_t = (dec[0:half, ls].T, dec[half:TM, ls].T)
        s = p_scr[p]
        for c in _gla_order(reverse):
            rs = _chunk_rows(c)
            rhs = jnp.concatenate([vl[rs, :], vr[rs, :], s.astype(BF16)], axis=0)
            rows[c][p] = _dot(jnp.concatenate([att[p][c], qe[rs, ls]], axis=1), rhs)
            col = (c % 2) * GLA_CHUNK
            s = dec_t[(c * GLA_CHUNK) // half][:, col:col + 1] * s + ds[p][c]
        p_scr[p] = s
    return jnp.concatenate([jnp.concatenate(r, axis=1) for r in rows], axis=0)


def _gla_state_load(s_ref, p_scr):
    p_scr[...] = jnp.zeros(p_scr.shape, F32)
    for h in range(GLA_HEADS):
        r0, c0 = (h % 2) * GLA_DK, (h % 2) * GLA_DV
        p_scr[h // 2, r0:r0 + GLA_DK, c0:c0 + GLA_DV] = s_ref[0, 0, h]


def _gla_state_store(p_scr, s_ref):
    for h in range(GLA_HEADS):
        r0, c0 = (h % 2) * GLA_DK, (h % 2) * GLA_DV
        s_ref[0, 0, h] = p_scr[h // 2, r0:r0 + GLA_DK, c0:c0 + GLA_DV]


def _ev1_kernel(xp_ref, xs_ref, rtab_ref, ctab_ref, mod_ref, ng_ref, win_ref, gmat_ref, gb_ref,
                lng_ref, lnb_ref, ws_ref, bst_ref, cum_ref, cumr_ref, p0_ref, w1f_ref, w2f_ref,
                wa_f, wb_f, wc_f, mid_ref, midb_ref, decb_ref, pfin_ref, w1b_ref, w2b_ref, wa_b, wb_b, wc_b,
                p_scr, pg_scr, ps_scr, la_scr, *, layer):
    i = pl.program_id(0)
    mod_ref, ng_ref = mod_ref.at[layer], ng_ref.at[layer]
    is_p, within, cidx, _, _, _ = _tile_info(jnp.minimum(i, NT - 1))
    _, _, _, first, last, _ = _tile_info(jnp.maximum(i - 1, 0))
    _cast_slabs(w1f_ref, w2f_ref, w1b_ref, w2b_ref)
    for f_ref, b_ref in ((wa_f, wa_b), (wb_f, wb_b), (wc_f, wc_b)):
        b_ref[...] = f_ref[0].astype(BF16)

    @pl.when(i == 0)
    def _():
        pg_scr[...] = jnp.zeros(pg_scr.shape, F32)
        ps_scr[...] = jnp.zeros(ps_scr.shape, F32)
        la_scr[...] = jnp.zeros(la_scr.shape, F32)

    @pl.when(first)
    def _():
        _gla_state_load(p0_ref, p_scr)

    x = _load_x0(is_p, within, xp_ref, xs_ref, rtab_ref, ctab_ref)
    sh1, sc1, _, _, _, _ = _mod_rows(mod_ref, cidx)
    hm = (_rms(x, ng_ref[0:1, :]) * (1.0 + sc1) + sh1).astype(BF16)
    o0 = 2 * SGU_WIDTH
    lr = _dot(hm, win_ref[:, EV_MAIN:EV_IN]).astype(BF16)

    qs = pg_scr[:, 0:QK] * (GLA_DK ** -0.5)
    k = pg_scr[:, QK:2 * QK]
    v = pg_scr[:, 2 * QK:2 * QK + VW]
    g = pg_scr[:, 2 * QK + VW:2 * QK + 2 * VW]
    v_bf = v.astype(BF16)
    mid_ref[:, 2 * VW:3 * VW] = g * _sigmoid(g)
    midb_ref[:, 3 * QK:3 * QK + VW] = v_bf
    prep = _gla_prep(qs, k, la_scr[:, 0:QK], cum_ref[...], reverse=False)
    la_b = la_scr[:, QK:2 * QK]

    pg_scr[...] = _dot(hm, win_ref[:, o0:EV_MAIN])
    u = _gelu(ps_scr[:, 0:SGU_WIDTH])
    vg = _gelu(ps_scr[:, SGU_WIDTH:2 * SGU_WIDTH])
    mu = jnp.mean(vg, axis=-1, keepdims=True)
    vc = vg - mu
    vn = (vc * lax.rsqrt(jnp.mean(vc * vc, axis=-1, keepdims=True) + EPS) * lng_ref[...] + lnb_ref[...]).astype(BF16)
    gb = jnp.concatenate([gb_ref[0:1, :], gb_ref[1:2, :]], axis=1)
    la_scr[...] = _log_sigmoid(_dot(lr, gmat_ref[...]) + gb) * (1.0 / GLA_NORMALIZER)
    prods = _gla_products(prep, v_bf, reverse=False)
    ps_scr[...] = _dot(hm, win_ref[:, 0:o0])

    gd = SGU_WIDTH // SGU_GROUPS
    nch = TM // SGU_CHUNK
    sv_cols = []
    for grp in range(SGU_GROUPS):
        vcat = jnp.concatenate(
            [vn[c * SGU_CHUNK:(c + 1) * SGU_CHUNK, grp * gd:(grp + 1) * gd] for c in range(nch)], axis=1)
        sg = _dot(ws_ref[grp], vcat) + bst_ref[:, grp:grp + 1]
        sv_cols.append(jnp.concatenate([sg[:, c * gd:(c + 1) * gd] for c in range(nch)], axis=0))
    mid_ref[:, 0:VW] = u * jnp.concatenate(sv_cols, axis=1)
    mid_ref[:, VW:2 * VW] = _gla_outputs(prep, prods, v_bf, p_scr, reverse=False)

    qe_b, ke_b, kd_b, dec_b = _gla_prep(qs, k, la_b, cumr_ref[...], reverse=True)
    midb_ref[:, 0:QK] = qe_b
    midb_ref[:, QK:2 * QK] = ke_b
    midb_ref[:, 2 * QK:3 * QK] = kd_b
    nchunk = TM // GLA_CHUNK
    decb_ref[...] = jnp.concatenate(
        [dec_b[c * GLA_CHUNK:c * GLA_CHUNK + 1, :] for c in range(nchunk)]
        + [jnp.zeros((SUBLANES - nchunk, QK), F32)], axis=0)

    @pl.when(last)
    def _():
        _gla_state_store(p_scr, pfin_ref)


def _const_spec(shape):
    nd = len(shape)
    return pl.BlockSpec(shape, lambda i, _nd=nd: (0,) * _nd, pipeline_mode=pl.Buffered(1))


def _params():
    return pltpu.CompilerParams(dimension_semantics=("arbitrary",), vmem_limit_bytes=VMEM_LIMIT)


CAST_STEPS = 64


def _cast_specs(layer):
    r1, r2 = D // CAST_STEPS, D_FF // CAST_STEPS
    in_specs = [pl.BlockSpec((1, r1, D_FF), lambda i: (layer, jnp.minimum(i, CAST_STEPS - 1), 0)),
                pl.BlockSpec((1, r2, D), lambda i: (layer, jnp.minimum(i, CAST_STEPS - 1), 0))]
    out_specs = [pl.BlockSpec((r1, D_FF), lambda i: (jnp.minimum(i, CAST_STEPS - 1), 0)),
                 pl.BlockSpec((r2, D), lambda i: (jnp.minimum(i, CAST_STEPS - 1), 0))]
    shapes = [jax.ShapeDtypeStruct((D, D_FF), BF16), jax.ShapeDtypeStruct((D_FF, D), BF16)]
    return in_specs, out_specs, shapes


def _cast_slabs(w1f_ref, w2f_ref, w1b_ref, w2b_ref):
    w1b_ref[...] = w1f_ref[0].astype(BF16)
    w2b_ref[...] = w2f_ref[0].astype(BF16)


def _later_cast_specs(widths):
    r = D // CAST_STEPS
    in_specs = [pl.BlockSpec((1, r, w), lambda i: (0, jnp.minimum(i, CAST_STEPS - 1), 0)) for w in widths]
    out_specs = [pl.BlockSpec((r, w), lambda i: (jnp.minimum(i, CAST_STEPS - 1), 0)) for w in widths]
    shapes = [jax.ShapeDtypeStruct((D, w), BF16) for w in widths]
    return in_specs, out_specs, shapes


def _even_forward(xp, xs, rtab, ctab, mod, ng, w, w1f, w2f, later, layer):
    state_blk = (1, 1, GLA_HEADS, GLA_DK, GLA_DV)
    cast_in, cast_out, cast_shapes = _cast_specs(layer)
    later_in, later_out, later_shapes = _later_cast_specs([a.shape[-1] for a in later])
    cast_in, cast_out, cast_shapes = cast_in + later_in, cast_out + later_out, cast_shapes + later_shapes

    def mix_tile(i):
        return jnp.maximum(i - 1, 0)

    return pl.pallas_call(
        functools.partial(_ev1_kernel, layer=layer),
        out_shape=(jax.ShapeDtypeStruct((NT * TM, EV_MID), F32),
                   jax.ShapeDtypeStruct((NT * TM, EV_MID_B), BF16),
                   jax.ShapeDtypeStruct((NT * SUBLANES, QK), F32),
                   jax.ShapeDtypeStruct((NSEQ, 1) + state_blk[2:], F32), *cast_shapes),
        grid=(NT + 1,),
        in_specs=[
            pl.BlockSpec((TM, D), lambda i: (jnp.minimum(i, NTP - 1), 0)),
            pl.BlockSpec((TM, D), lambda i: (jnp.maximum(jnp.minimum(i, NT - 1) - NTP, 0), 0)),
            _const_spec((GRID_W, D // 2)), _const_spec((GRID_W, D // 2)),
            _const_spec((DEPTH, NCOND, 6 * D)), _const_spec((DEPTH, 4, D)),
            _const_spec((D, EV_IN)), _const_spec((2 * GLA_RANK, 2 * QK)),
            _const_spec((2, QK)),
            _const_spec((1, SGU_WIDTH)), _const_spec((1, SGU_WIDTH)),
            _const_spec((SGU_GROUPS, SGU_CHUNK, SGU_CHUNK)), _const_spec((SGU_CHUNK, SGU_GROUPS)),
            _const_spec((TM, TM)), _const_spec((TM, TM)),
            pl.BlockSpec(state_blk, lambda i: (_seq_of(mix_tile(i)), 0, 0, 0, 0)),
            *cast_in,
        ],
        out_specs=(pl.BlockSpec((TM, EV_MID), lambda i: (mix_tile(i), 0)),
                   pl.BlockSpec((TM, EV_MID_B), lambda i: (mix_tile(i), 0)),
                   pl.BlockSpec((SUBLANES, QK), lambda i: (mix_tile(i), 0)),
                   pl.BlockSpec(state_blk, lambda i: (_seq_of(mix_tile(i)), 0, 0, 0, 0)), *cast_out),
        scratch_shapes=[pltpu.VMEM((GLA_HEADS // 2, 2 * GLA_DK, 2 * GLA_DV), F32),
                        pltpu.VMEM((TM, 2 * QK + 2 * VW), F32), pltpu.VMEM((TM, 2 * SGU_WIDTH), F32),
                        pltpu.VMEM((TM, 2 * QK), F32)],
        compiler_params=_params(),
        name="even_forward",
    )(xp, xs, rtab, ctab, mod, ng, w["win"], w["gmat"], w["gb"], w["lng"], w["lnb"],
      w["ws"], w["bst"], _cum_mask(False), _cum_mask(True), w["p0"], w1f, w2f, *later)


def _ev2_kernel(xp_ref, xs_ref, rtab_ref, ctab_ref, mod_ref, ng_ref, mid_ref, midb_ref, decb_ref, gn_ref, wout_ref,
                p0_ref,
                x1_ref, pfin_ref, p_scr, cat_scr, *, layer):
    i = pl.program_id(0)
    mod_ref, ng_ref = mod_ref.at[layer], ng_ref.at[layer]
    live = i < NT
    _, _, _, first, last, _ = _tile_info(NT - 1 - jnp.minimum(i, NT - 1))
    is_p, within, cidx, _, _, _ = _tile_info(NT - 1 - jnp.maximum(i - 1, 0))

    @pl.when(i == 0)
    def _():
        cat_scr[...] = jnp.zeros(cat_scr.shape, BF16)

    @pl.when(jnp.logical_and(last, live))
    def _():
        _gla_state_load(p0_ref, p_scr)

    nblk = D // SLAB
    ys = [None] * nblk

    def project(j):
        ys[j] = _dot(cat_scr[...], wout_ref[:, j * SLAB:(j + 1) * SLAB])

    v_bf = midb_ref[:, 3 * QK:3 * QK + VW]
    dec = jnp.concatenate([jnp.broadcast_to(decb_ref[c:c + 1, :], (GLA_CHUNK, QK)) for c in range(TM // GLA_CHUNK)],
                          axis=0)
    prep = (midb_ref[:, 0:QK], midb_ref[:, QK:2 * QK], midb_ref[:, 2 * QK:3 * QK], dec)
    project(0)
    project(1)
    prods = _gla_products(prep, v_bf, reverse=True)
    project(2)
    o = mid_ref[:, VW:2 * VW] + _gla_outputs(prep, prods, v_bf, p_scr, reverse=True)
    project(3)
    heads = []
    for h in range(GLA_HEADS):
        oh = o[:, h * GLA_DV:(h + 1) * GLA_DV]
        heads.append(oh * lax.rsqrt(jnp.mean(oh * oh, axis=-1, keepdims=True) + EPS))
    on = jnp.concatenate(heads, axis=1) * gn_ref[...] * mid_ref[:, 2 * VW:3 * VW]
    cat_scr[...] = jnp.concatenate([mid_ref[:, 0:VW], on], axis=1).astype(BF16)

    x = _load_x0(is_p, within, xp_ref, xs_ref, rtab_ref, ctab_ref)
    _, _, g1, _, _, _ = _mod_rows(mod_ref, cidx)
    x1_ref[...] = x + g1 * _rms(jnp.concatenate(ys, axis=1), ng_ref[1:2, :])

    @pl.when(jnp.logical_and(first, live))
    def _():
        _gla_state_store(p_scr, pfin_ref)


def _even_reverse(xp, xs, rtab, ctab, mod, ng, mid, midb, decb, w, layer):
    state_blk = (1, 1, GLA_HEADS, GLA_DK, GLA_DV)
    assert D // SLAB == 4

    def mix_tile(i):
        return NT - 1 - jnp.minimum(i, NT - 1)

    def out_tile(i):
        return NT - 1 - jnp.maximum(i - 1, 0)

    return pl.pallas_call(
        functools.partial(_ev2_kernel, layer=layer),
        out_shape=(jax.ShapeDtypeStruct((NT * TM, D), F32),
                   jax.ShapeDtypeStruct((NSEQ, 1) + state_blk[2:], F32)),
        grid=(NT + 1,),
        in_specs=[
            pl.BlockSpec((TM, D), lambda i: (jnp.minimum(out_tile(i), NTP - 1), 0)),
            pl.BlockSpec((TM, D), lambda i: (jnp.maximum(out_tile(i) - NTP, 0), 0)),
            _const_spec((GRID_W, D // 2)), _const_spec((GRID_W, D // 2)),
            _const_spec((DEPTH, NCOND, 6 * D)), _const_spec((DEPTH, 4, D)),
            pl.BlockSpec((TM, EV_MID), lambda i: (mix_tile(i), 0)),
            pl.BlockSpec((TM, EV_MID_B), lambda i: (mix_tile(i), 0)),
            pl.BlockSpec((SUBLANES, QK), lambda i: (mix_tile(i), 0)),
            _const_spec((1, VW)), _const_spec((2 * VW, D)),
            pl.BlockSpec(state_blk, lambda i: (_seq_of(mix_tile(i)), 1, 0, 0, 0)),
        ],
        out_specs=(pl.BlockSpec((TM, D), lambda i: (out_tile(i), 0)),
                   pl.BlockSpec(state_blk, lambda i: (_seq_of(mix_tile(i)), 0, 0, 0, 0))),
        scratch_shapes=[pltpu.VMEM((GLA_HEADS // 2, 2 * GLA_DK, 2 * GLA_DV), F32),
                        pltpu.VMEM((TM, D), BF16)],
        compiler_params=_params(),
        name="even_reverse",
    )(xp, xs, rtab, ctab, mod, ng, mid, midb, decb, w["gn"], w["wout"], w["p0"])


FF_CHUNK = 1024
MLP_TILES = 2
TMM = MLP_TILES * TM


def _mlp_kernel(x_ref, mod_ref, ng_ref, w1_ref, b1_ref, w2_ref, b2_ref, o_ref, *, tile0, layer):
    mod_ref, ng_ref = mod_ref.at[layer], ng_ref.at[layer]
    b1_ref, b2_ref = b1_ref.at[layer:layer + 1], b2_ref.at[layer:layer + 1]
    _, _, cidx, _, _, _ = _tile_info(tile0 + pl.program_id(0) * MLP_TILES)
    _, _, _, sh2, sc2, g2 = _mod_rows(mod_ref, cidx)
    x = x_ref[...]
    hff = (_rms(x, ng_ref[2:3, :]) * (1.0 + sc2) + sh2).astype(BF16)
    acc = b2_ref[...]
    for j in range(D_FF // FF_CHUNK):
        cs = slice(j * FF_CHUNK, (j + 1) * FF_CHUNK)
        h = _dot(hff, w1_ref[:, cs]) + b1_ref[:, cs]
        h = jnp.square(jnp.maximum(h, 0.0)).astype(BF16)
        acc = acc + _dot(h, w2_ref[cs, :])
    o_ref[...] = x + g2 * _rms(acc, ng_ref[3:4, :])


def _mlp(x, mod, ng, w1, b1, w2, b2, layer, tile0, ntiles):
    assert tile0 % MLP_TILES == 0 and ntiles % MLP_TILES == 0 and TPS % MLP_TILES == 0 and NTP % MLP_TILES == 0
    b0 = tile0 // MLP_TILES
    return pl.pallas_call(
        functools.partial(_mlp_kernel, tile0=tile0, layer=layer),
        out_shape=jax.ShapeDtypeStruct((ntiles * TM, D), F32),
        grid=(ntiles // MLP_TILES,),
        in_specs=[
            pl.BlockSpec((TMM, D), lambda i: (b0 + i, 0)),
            _const_spec((DEPTH, NCOND, 6 * D)), _const_spec((DEPTH, 4, D)),
            _const_spec((D, D_FF)), _const_spec((DEPTH, D_FF)), _const_spec((D_FF, D)), _const_spec((DEPTH, D)),
        ],
        out_specs=pl.BlockSpec((TMM, D), lambda i: (i, 0)),
        compiler_params=_params(), name="mlp",
    )(x, mod, ng, w1, b1, w2, b2)


def _rg_gates(xc, wg_ref, ba_ref, bx_ref, lam_ref, a_scr, b_scr, slabs):
    xcb = xc.astype(BF16)
    sp = RG_C * _softplus(-lam_ref[...])
    for s in slabs:
        cs = slice(s * SLAB, (s + 1) * SLAB)
        pre = _dot(xcb[:, cs], wg_ref[s])
        r = _sigmoid(pre[:, 0:SLAB] + ba_ref[:, cs])
        i = _sigmoid(pre[:, SLAB:2 * SLAB] + bx_ref[:, cs])
        z = r * sp[:, cs]
        a = jnp.exp(-z)
        a_scr[:, cs] = a
        u = jnp.tanh(z) * (1.0 + a * a)
        b_scr[:, cs] = jnp.where(u > 0.0, u * lax.rsqrt(u), 0.0) * (i * xc[:, cs])


SEG = TM // SUBLANES


def _seg_perm(transpose):
    rr, cc = np.indices((TM, TM))
    if transpose:
        rr, cc = cc, rr
    return jnp.asarray(np.logical_and(cc // SEG == rr % SUBLANES, cc % SEG == rr // SUBLANES), BF16)


def _rg_scan(a_scr, b_scr, h_scr, c_scr, h0, reverse):
    def body(i, carry):
        hh, cc = carry
        grp = (SEG - 1 - i) if reverse else i
        r0 = pl.multiple_of(grp * SUBLANES, SUBLANES)
        a = a_scr[pl.ds(r0, SUBLANES), :]
        hh = a * hh + b_scr[pl.ds(r0, SUBLANES), :]
        cc = a * cc
        h_scr[pl.ds(r0, SUBLANES), :] = hh
        c_scr[pl.ds(r0, SUBLANES), :] = cc
        return hh, cc

    init = (jnp.zeros((SUBLANES, D_RNN), F32), jnp.ones((SUBLANES, D_RNN), F32))
    b, a = lax.fori_loop(0, SEG, body, init, unroll=4 if reverse else True)
    row = lax.broadcasted_iota(jnp.int32, (SUBLANES, D_RNN), 0)
    for s in (1, 2, 4):
        shift = (SUBLANES - s) if reverse else s
        valid = (row < SUBLANES - s) if reverse else (row >= s)
        a_s = pltpu.roll(a, shift, 0)
        b_s = pltpu.roll(b, shift, 0)
        b = jnp.where(valid, a * b_s + b, b)
        a = jnp.where(valid, a * a_s, a)
    after = a * h0 + b
    edge = (row == SUBLANES - 1) if reverse else (row == 0)
    enter = jnp.where(edge, h0, pltpu.roll(after, (SUBLANES - 1) if reverse else 1, 0))
    h = h_scr[...] + c_scr[...] * jnp.concatenate([enter] * SEG, axis=0)
    out = after[0:1, :] if reverse else after[SUBLANES - 1:SUBLANES, :]
    return h, out


def _od1_kernel(x_ref, xn_ref, mod_ref, ng_ref, win_ref, cw_ref, cb_ref, wg_ref, ba_ref, bx_ref, lam_ref,
                perm_ref, s0_ref, w1f_ref, w2f_ref, mid_ref, sfin_ref, w1b_ref, w2b_ref,
                proj_scr, xb_scr, a_scr, b_scr, h_scr, c_scr, hc_scr, tail_scr, *, layer):
    i = pl.program_id(0)
    mod_ref, ng_ref = mod_ref.at[layer], ng_ref.at[layer]
    ba_ref, bx_ref, lam_ref = ba_ref.at[0:1], bx_ref.at[0:1], lam_ref.at[0:1]
    _, _, cidx, _, _, _ = _tile_info(jnp.minimum(i, NT - 1))
    _, _, _, first, last, _ = _tile_info(jnp.maximum(i - 1, 0))
    _cast_slabs(w1f_ref, w2f_ref, w1b_ref, w2b_ref)

    @pl.when(i == 0)
    def _():
        proj_scr[...] = jnp.zeros(proj_scr.shape, F32)

    @pl.when(first)
    def _():
        tail_scr[...] = jnp.zeros((2 * SUBLANES, D_RNN), F32)
        hc_scr[...] = s0_ref[0, 0]

    nslab = D_RNN // SLAB
    pw = 2 * D_RNN // nslab

    def project(lhs, s):
        proj_scr[:, s * pw:(s + 1) * pw] = _dot(lhs, win_ref[:, s * pw:(s + 1) * pw])

    sh1, sc1, _, _, _, _ = _mod_rows(mod_ref, cidx)
    xe = jnp.concatenate([x_ref[...], xn_ref[...]], axis=0)
    hm = (_rms(xe, ng_ref[0:1, :]) * (1.0 + sc1) + sh1).astype(BF16)
    hmp = _dot(perm_ref[...], hm[0:TM, :]).astype(BF16)
    lhs = jnp.concatenate([hmp, hm[TM:TM + HALO, :]], axis=0)

    row = lax.broadcasted_iota(jnp.int32, (SUBLANES, D_RNN), 0)
    g30 = proj_scr[TM - 2 * SUBLANES:TM - SUBLANES, 0:D_RNN]
    g31 = proj_scr[TM - SUBLANES:TM, 0:D_RNN]
    g0 = proj_scr[0:SUBLANES, 0:D_RNN]
    nxt = jnp.where(last, 0.0, proj_scr[TM:TM + 1, 0:D_RNN])
    xb_scr[0:SUBLANES, :] = jnp.where(row == 0, pltpu.roll(tail_scr[0:SUBLANES, :], 1, 0), pltpu.roll(g30, 1, 0))
    xb_scr[SUBLANES:2 * SUBLANES, :] = jnp.where(
        row == 0, pltpu.roll(tail_scr[SUBLANES:2 * SUBLANES, :], 1, 0), pltpu.roll(g31, 1, 0))
    xb_scr[2 * SUBLANES:2 * SUBLANES + TM, :] = proj_scr[0:TM, 0:D_RNN]
    xb_scr[2 * SUBLANES + TM:3 * SUBLANES + TM, :] = jnp.where(
        row == SUBLANES - 1, nxt, pltpu.roll(g0, SUBLANES - 1, 0))
    tail_scr[0:SUBLANES, :] = g30
    tail_scr[SUBLANES:2 * SUBLANES, :] = g31

    xc = cb_ref[...]
    for j in range(4):
        xc = xc + cw_ref[j:j + 1, :] * xb_scr[j * SUBLANES:j * SUBLANES + TM, :]
    mid_ref[:, 2 * D_RNN:3 * D_RNN] = xc

    project(lhs, 0)
    mid_ref[:, D_RNN:2 * D_RNN] = _gelu(proj_scr[0:TM, D_RNN:2 * D_RNN])
    for s in range(nslab):
        if s + 1 < nslab:
            project(lhs, s + 1)
        _rg_gates(mid_ref[:, 2 * D_RNN:3 * D_RNN], wg_ref, ba_ref, bx_ref, lam_ref, a_scr, b_scr, range(s, s + 1))

    h, hc_scr[...] = _rg_scan(a_scr, b_scr, h_scr, c_scr, hc_scr[...], reverse=False)
    mid_ref[:, 0:D_RNN] = h

    @pl.when(last)
    def _():
        sfin_ref[0, 0] = hc_scr[...]


def _odd_forward(x, mod, ng, w, w1f, w2f, layer):
    nb16 = TM // HALO
    cast_in, cast_out, cast_shapes = _cast_specs(layer)

    def proj_tile(i):
        return jnp.minimum(i, NT - 1)

    def scan_tile(i):
        return jnp.maximum(i - 1, 0)

    return pl.pallas_call(
        functools.partial(_od1_kernel, layer=layer),
        out_shape=(jax.ShapeDtypeStruct((NT * TM, OD_MID), F32),
                   jax.ShapeDtypeStruct((NSEQ, 1, 1, D_RNN), F32), *cast_shapes),
        grid=(NT + 1,),
        in_specs=[
            pl.BlockSpec((TM, D), lambda i: (proj_tile(i), 0)),
            pl.BlockSpec((HALO, D), lambda i: (jnp.minimum(proj_tile(i) + 1, NT - 1) * nb16, 0)),
            _const_spec((DEPTH, NCOND, 6 * D)), _const_spec((DEPTH, 4, D)),
            _const_spec((D, 2 * D_RNN)), _const_spec((4, D_RNN)), _const_spec((1, D_RNN)),
            _const_spec((D_RNN // SLAB, SLAB, 2 * SLAB)),
            _const_spec((2, D_RNN)), _const_spec((2, D_RNN)), _const_spec((2, D_RNN)),
            _const_spec((TM, TM)),
            pl.BlockSpec((1, 1, 1, D_RNN), lambda i: (_seq_of(scan_tile(i)), 0, 0, 0)),
            *cast_in,
        ],
        out_specs=(pl.BlockSpec((TM, OD_MID), lambda i: (scan_tile(i), 0)),
                   pl.BlockSpec((1, 1, 1, D_RNN), lambda i: (_seq_of(scan_tile(i)), 0, 0, 0)), *cast_out),
        scratch_shapes=[
            pltpu.VMEM((TM + HALO, 2 * D_RNN), F32),
            pltpu.VMEM((3 * SUBLANES + TM, D_RNN), F32),
            pltpu.VMEM((TM, D_RNN), F32), pltpu.VMEM((TM, D_RNN), F32), pltpu.VMEM((TM, D_RNN), F32),
            pltpu.VMEM((TM, D_RNN), F32),
            pltpu.VMEM((1, D_RNN), F32), pltpu.VMEM((2 * SUBLANES, D_RNN), F32),
        ],
        compiler_params=_params(),
        name="odd_forward",
    )(x, x, mod, ng, w["win"], w["cw"], w["cb"], w["wg"][0], w["ba"], w["bx"], w["lam"],
      _seg_perm(False), w["s0"], w1f, w2f)


def _od2_kernel(x_ref, mod_ref, ng_ref, mid_ref, wg_ref, ba_ref, bx_ref, lam_ref, wout_ref, perm_ref, s0_ref,
                x1_ref, sfin_ref, a_scr, b_scr, h_scr, c_scr, hc_scr, zp_scr, *, layer):
    i = pl.program_id(0)
    mod_ref, ng_ref = mod_ref.at[layer], ng_ref.at[layer]
    ba_ref, bx_ref, lam_ref = ba_ref.at[1:2], bx_ref.at[1:2], lam_ref.at[1:2]
    live = i < NT
    _, _, _, first, last, _ = _tile_info(NT - 1 - jnp.minimum(i, NT - 1))
    _, _, cidx_prev, _, _, _ = _tile_info(NT - 1 - jnp.maximum(i - 1, 0))
    _, _, g1, _, _, _ = _mod_rows(mod_ref, cidx_prev)

    @pl.when(i == 0)
    def _():
        zp_scr[...] = jnp.zeros(zp_scr.shape, BF16)

    @pl.when(jnp.logical_and(last, live))
    def _():
        hc_scr[...] = s0_ref[0, 0]

    xc = mid_ref[:, 2 * D_RNN:3 * D_RNN]
    nslab = D_RNN // SLAB
    z = _dot(perm_ref[...], zp_scr[...]).astype(BF16)
    ys = []
    for s in range(nslab):
        ys.append(_dot(z, wout_ref[:, s * SLAB:(s + 1) * SLAB]))
        _rg_gates(xc, wg_ref, ba_ref, bx_ref, lam_ref, a_scr, b_scr, range(s, s + 1))
    x1_ref[...] = x_ref[...] + g1 * _rms(jnp.concatenate(ys, axis=1), ng_ref[1:2, :])
    h_b, hc_scr[...] = _rg_scan(a_scr, b_scr, h_scr, c_scr, hc_scr[...], reverse=True)
    zp_scr[...] = ((mid_ref[:, 0:D_RNN] + h_b) * mid_ref[:, D_RNN:2 * D_RNN]).astype(BF16)

    @pl.when(jnp.logical_and(first, live))
    def _():
        sfin_ref[0, 0] = hc_scr[...]


def _odd_reverse(x, mod, ng, mid, w, layer):
    def scan_tile(i):
        return NT - 1 - jnp.minimum(i, NT - 1)

    def out_tile(i):
        return NT - 1 - jnp.maximum(i - 1, 0)

    return pl.pallas_call(
        functools.partial(_od2_kernel, layer=layer),
        out_shape=(jax.ShapeDtypeStruct((NT * TM, D), F32),
                   jax.ShapeDtypeStruct((NSEQ, 1, 1, D_RNN), F32)),
        grid=(NT + 1,),
        in_specs=[
            pl.BlockSpec((TM, D), lambda i: (out_tile(i), 0)),
            _const_spec((DEPTH, NCOND, 6 * D)), _const_spec((DEPTH, 4, D)),
            pl.BlockSpec((TM, OD_MID), lambda i: (scan_tile(i), 0)),
            _const_spec((D_RNN // SLAB, SLAB, 2 * SLAB)),
            _const_spec((2, D_RNN)), _const_spec((2, D_RNN)), _const_spec((2, D_RNN)),
            _const_spec((D_RNN, D)), _const_spec((TM, TM)),
            pl.BlockSpec((1, 1, 1, D_RNN), lambda i: (_seq_of(scan_tile(i)), 1, 0, 0)),
        ],
        out_specs=(pl.BlockSpec((TM, D), lambda i: (out_tile(i), 0)),
                   pl.BlockSpec((1, 1, 1, D_RNN), lambda i: (_seq_of(scan_tile(i)), 0, 0, 0))),
        scratch_shapes=[
            pltpu.VMEM((TM, D_RNN), F32), pltpu.VMEM((TM, D_RNN), F32), pltpu.VMEM((TM, D_RNN), F32),
            pltpu.VMEM((TM, D_RNN), F32),
            pltpu.VMEM((1, D_RNN), F32), pltpu.VMEM((TM, D_RNN), BF16),
        ],
        compiler_params=_params(),
        name="odd_reverse",
    )(x, mod, ng, mid, w["wg"][1], w["ba"], w["bx"], w["lam"], w["wout"], _seg_perm(True), w["s0"])


def _pos_tables():
    n = D // 4
    omega = 1.0 / (10000.0 ** (jnp.arange(n, dtype=F32) / n))
    idx = jnp.arange(GRID_W, dtype=F32)[:, None] * omega
    tab = jnp.concatenate([jnp.sin(idx), jnp.cos(idx)], axis=-1)
    return tab, tab


def _block_diag_slabs(w):
    per = SLAB // RG_BS
    rows = jnp.tile(w.reshape(D_RNN // SLAB, SLAB, RG_BS), (1, 1, per))
    ri, ci = np.indices((SLAB, SLAB))
    return jnp.where(jnp.asarray(ri // RG_BS == ci // RG_BS), rows, 0.0)


def kernel(x_prompt, x_sample, c, state_gla, state_rglru, c_ctx, mod_w, mod_b, norm_g, mlp_w1, mlp_b1, mlp_w2,
           mlp_b2, ev_w_in, ev_w_out, sgu_ln_g, sgu_ln_b, sgu_ws, sgu_bs, gla_gate_w2, gla_gate_b, gla_norm_g,
           rg_w_in, rg_conv_w, rg_conv_b, rg_wa, rg_ba, rg_wx, rg_bx, rg_L, rg_w_out):
    assert x_prompt.shape == (BATCH, SEQ, D) and x_sample.shape == (DEC_BATCH, DEC_SEQ, D)
    assert SEQ == TM and DEC_SEQ % TM == 0 and DEPTH == 2
    xp = x_prompt.reshape(NTP * TM, D)
    xs = x_sample.reshape(NTS * TM, D)
    cond8 = jnp.concatenate([c_ctx[None, :], c, jnp.zeros((NCOND - 1 - DEC_BATCH, D), F32)], axis=0)
    mods = _modulation(cond8, mod_w, mod_b)
    rtab, ctab = _pos_tables()

    gz = jnp.zeros((GLA_RANK, QK), F32)
    gmat = jnp.concatenate([jnp.concatenate([gla_gate_w2[0, 0], gz], axis=1),
                            jnp.concatenate([gz, gla_gate_w2[0, 1]], axis=1)], axis=0)
    s0_gla = jnp.concatenate([jnp.zeros((BATCH,) + state_gla.shape[2:], F32), state_gla[:, 0]], axis=0)
    ev = {
        "win": ev_w_in[0].astype(BF16),
        "gmat": gmat.astype(BF16),
        "gb": gla_gate_b[0],
        "lng": sgu_ln_g[0].reshape(1, SGU_WIDTH),
        "lnb": sgu_ln_b[0].reshape(1, SGU_WIDTH),
        "ws": sgu_ws[0].astype(BF16),
        "bst": sgu_bs[0].T,
        "gn": gla_norm_g[0].reshape(1, VW),
        "p0": s0_gla,
    }
    mid, midb, decb, pf, w1b, w2b, ev["wout"], rg_win_b, rg_wout_b = _even_forward(
        xp, xs, rtab, ctab, mods, norm_g, ev, mlp_w1, mlp_w2, (ev_w_out, rg_w_in, rg_w_out), 0)
    x1, pb = _even_reverse(xp, xs, rtab, ctab, mods, norm_g, mid, midb, decb, ev, 0)
    x2 = _mlp(x1, mods, norm_g, w1b, mlp_b1, w2b, mlp_b2, 0, 0, NT)
    new_gla = jnp.stack([pf[:BATCH, 0], pb[:BATCH, 0]], axis=1)[:, None]

    s0_rg = jnp.concatenate([jnp.zeros((BATCH, 2, D_RNN), F32), state_rglru[:, 0]], axis=0)
    od = {
        "win": rg_win_b,
        "cw": rg_conv_w[0],
        "cb": rg_conv_b[0].reshape(1, D_RNN),
        "wg": jnp.stack([
            jnp.concatenate([_block_diag_slabs(rg_wa[0, d]), _block_diag_slabs(rg_wx[0, d])], axis=-1)
            for d in range(2)], axis=0).astype(BF16),
        "ba": rg_ba[0], "bx": rg_bx[0], "lam": rg_L[0],
        "wout": rg_wout_b,
        "s0": s0_rg.reshape(NSEQ, 2, 1, D_RNN),
    }
    mid1, sf, w1b, w2b = _odd_forward(x2, mods, norm_g, od, mlp_w1, mlp_w2, 1)
    x3, sb = _odd_reverse(x2, mods, norm_g, mid1, od, 1)
    mlp1 = (mods, norm_g, w1b, mlp_b1, w2b, mlp_b2, 1)
    y_prompt = _mlp(x3, *mlp1, 0, NTP)
    y_sample = _mlp(x3, *mlp1, NTP, NTS)
    new_rg = jnp.stack([sf[:BATCH, 0, 0], sb[:BATCH, 0, 0]], axis=1)[:, None]
    return (y_prompt.reshape(BATCH, SEQ, D), y_sample.reshape(DEC_BATCH, DEC_SEQ, D), new_gla, new_rg)
```
